```python
import jax, jax.numpy as jnp
from jax import lax
import numpy as np

D_MODEL = 2048
BATCH = 4
SEQ = 2048
DEPTH = 1
DEC_BATCH = 32
DEC_SEQ = 4
PAST_LEN = 8192
PAGE_SIZE = 128

SGU_WIDTH = D_MODEL
SGU_GROUPS = 8
SGU_GROUP_DIM = SGU_WIDTH // SGU_GROUPS
CHUNK = 128
HEAD_DIM = 128
ATT_HEADS = D_MODEL // HEAD_DIM
ATT_WIDTH = ATT_HEADS * HEAD_DIM
DIL_WINDOWS = (128, 512, 2048)
DIL_RATES = (1, 4, 16)
N_DIL = 3
ATT_BLOCK = 128
ATT_COLS = 3 * N_DIL * ATT_WIDTH
IN_COLS = 2 * SGU_WIDTH + ATT_COLS + 2 * D_MODEL
PEER_HEADS = 8
PEER_TOPK = 16
N_KEYS = 128
N_EXPERTS = N_KEYS * N_KEYS
PEER_QDIM = 256
PEER_HALF = PEER_QDIM // 2
PEER_BLOCK = 128
NORM_EPS = 1e-6
MASK_VALUE = -1e30

kernel_name = "hybrid_sgu_dilated_attn_peer_step"


def rmsnorm(x, g):
    xf = x.astype(jnp.float32)
    y = xf * lax.rsqrt(jnp.mean(xf * xf, axis=-1, keepdims=True) + NORM_EPS)
    return (y * g.astype(jnp.float32)).astype(x.dtype)


def layernorm_gain(x, g):
    xf = x.astype(jnp.float32)
    xc = xf - jnp.mean(xf, axis=-1, keepdims=True)
    y = xc * lax.rsqrt(jnp.mean(xc * xc, axis=-1, keepdims=True) + NORM_EPS)
    return (y * g.astype(jnp.float32)).astype(x.dtype)


def alibi_slopes():
    n = N_DIL * ATT_HEADS
    e = jnp.arange(1, n + 1, dtype=jnp.float32)
    return jnp.exp2(-8.0 * e / n).reshape(N_DIL, ATT_HEADS)


def mixer_projections(x, norm_g, w_in):
    lead = x.shape[:-1]
    p = rmsnorm(x, norm_g) @ w_in
    u = jax.nn.gelu(p[..., :SGU_WIDTH])
    v = jax.nn.gelu(p[..., SGU_WIDTH:2 * SGU_WIDTH])
    att = p[..., 2 * SGU_WIDTH:2 * SGU_WIDTH + ATT_COLS].reshape(*lead, N_DIL, 3, ATT_HEADS, HEAD_DIM)
    gates = jax.nn.sigmoid(p[..., 2 * SGU_WIDTH + ATT_COLS:]).reshape(*lead, 2, D_MODEL)
    return u, v, att, gates


def sgu_mix(u, vn, sgu_w, sgu_b):
    n, length, _ = u.shape
    nc = length // CHUNK
    causal = jnp.tril(jnp.ones((CHUNK, CHUNK), dtype=sgu_w.dtype))
    vc = vn.reshape(n, nc, CHUNK, SGU_GROUPS, SGU_GROUP_DIM)
    mix = jnp.einsum("gts,ncsgd->nctgd", sgu_w * causal, vc) + sgu_b.T[None, None, :, :, None]
    return u * mix.reshape(n, length, SGU_WIDTH)


def dilated_band_attention(q, k, v, slopes, dil, window):
    b, s, h, dh = q.shape
    steps = window // dil
    length = s // dil

    def to_streams(t):
        return t.reshape(b, length, dil, h, dh).transpose(0, 2, 1, 3, 4).reshape(b * dil, length, h, dh)

    qs, ks, vs = to_streams(q), to_streams(k), to_streams(v)
    n = b * dil
    nb = -(-length // ATT_BLOCK)
    lp = nb * ATT_BLOCK
    qb = jnp.pad(qs, ((0, 0), (0, lp - length), (0, 0), (0, 0))).reshape(n, nb, ATT_BLOCK, h, dh)

    def band(t):
        tp = jnp.pad(t, ((0, 0), (ATT_BLOCK, lp - length), (0, 0), (0, 0)))
        prev = tp[:, :lp].reshape(n, nb, ATT_BLOCK, h, dh)
        cur = tp[:, ATT_BLOCK:].reshape(n, nb, ATT_BLOCK, h, dh)
        return jnp.concatenate([prev, cur], axis=2)

    kb, vb = band(ks), band(vs)
    sc = jnp.einsum("nbqhd,nbkhd->nbhqk", qb, kb, preferred_element_type=jnp.float32) * (HEAD_DIM ** -0.5)
    qi = jnp.arange(ATT_BLOCK)[:, None]
    ki = jnp.arange(2 * ATT_BLOCK)[None, :]
    delta = qi + ATT_BLOCK - ki
    k_abs = jnp.arange(nb)[:, None, None] * ATT_BLOCK + ki[None] - ATT_BLOCK
    valid = (delta >= 0) & (delta <= steps) & (k_abs >= 0)
    bias = -slopes[:, None, None] * (dil * delta).astype(jnp.float32)[None]
    sc = jnp.where(valid[None, :, None], sc + bias[None, None], MASK_VALUE)
    mx = jnp.max(sc, axis=-1, keepdims=True)
    p = jnp.exp(sc - mx)
    l = jnp.sum(p, axis=-1)
    o = jnp.einsum("nbhqk,nbkhd->nbqhd", p, vb.astype(jnp.float32)) / l.transpose(0, 1, 3, 2)[..., None]
    lse = (mx[..., 0] + jnp.log(l)).transpose(0, 1, 3, 2)
    o = o.reshape(n, lp, h, dh)[:, :length]
    lse = lse.reshape(n, lp, h)[:, :length]
    o = o.reshape(b, dil, length, h, dh).transpose(0, 2, 1, 3, 4).reshape(b, s, h, dh)
    lse = lse.reshape(b, dil, length, h).transpose(0, 2, 1, 3).reshape(b, s, h)
    return o, lse


def dilated_cached_attention(q, k_new, v_new, kv_cache, slopes, dil, window):
    lw = kv_cache.shape[1]
    ds = q.shape[1]
    k_all = jnp.concatenate([kv_cache[:, :, 0], k_new], axis=1)
    v_all = jnp.concatenate([kv_cache[:, :, 1], v_new], axis=1)
    nj = window // dil + 1
    j = jnp.arange(nj)
    idx = lw + jnp.arange(ds)[:, None] - dil * j[None, :]
    valid = idx >= 0
    idx = jnp.maximum(idx, 0)
    kg = k_all[:, idx]
    vg = v_all[:, idx]
    sc = jnp.einsum("bihd,bijhd->bhij", q, kg, preferred_element_type=jnp.float32) * (HEAD_DIM ** -0.5)
    bias = -slopes[:, None, None] * (dil * j).astype(jnp.float32)[None, None, :]
    sc = jnp.where(valid[None, None], sc + bias[None], MASK_VALUE)
    mx = jnp.max(sc, axis=-1, keepdims=True)
    p = jnp.exp(sc - mx)
    l = jnp.sum(p, axis=-1)
    o = jnp.einsum("bhij,bijhd->bihd", p, vg.astype(jnp.float32)) / l.transpose(0, 2, 1)[..., None]
    lse = (mx[..., 0] + jnp.log(l)).transpose(0, 2, 1)
    return o, lse


def combine_groups(outs, lses):
    w = jax.nn.softmax(jnp.stack(lses, axis=0), axis=0)
    o = jnp.sum(w[..., None] * jnp.stack(outs, axis=0), axis=0)
    return o.reshape(*o.shape[:-2], ATT_WIDTH)


def peer_ffn(x, w_q, sub_keys, u_tab, v_tab):
    lead = x.shape[:-1]
    xf = x.reshape(-1, D_MODEL)
    t = xf.shape[0]
    q = (xf @ w_q).reshape(t, PEER_HEADS, 2, PEER_HALF)
    s = jnp.einsum("thcd,ckd->thck", q, sub_keys, preferred_element_type=jnp.float32)
    s1, i1 = lax.top_k(s[:, :, 0], PEER_TOPK)
    s2, i2 = lax.top_k(s[:, :, 1], PEER_TOPK)
    cand = (s1[..., :, None] + s2[..., None, :]).reshape(t, PEER_HEADS, PEER_TOPK * PEER_TOPK)
    cidx = (i1[..., :, None] * N_KEYS + i2[..., None, :]).reshape(t, PEER_HEADS, PEER_TOPK * PEER_TOPK)
    top_s, pos = lax.top_k(cand, PEER_TOPK)
    eidx = jnp.take_along_axis(cidx, pos, axis=-1)
    gate = jax.nn.softmax(top_s, axis=-1)
    tp = -(-t // PEER_BLOCK) * PEER_BLOCK
    nblk = tp // PEER_BLOCK
    kk = PEER_HEADS * PEER_TOPK
    xb = jnp.pad(xf, ((0, tp - t), (0, 0))).reshape(nblk, PEER_BLOCK, D_MODEL)
    eb = jnp.pad(eidx.reshape(t, kk), ((0, tp - t), (0, 0))).reshape(nblk, PEER_BLOCK, kk)
    gb = jnp.pad(gate.reshape(t, kk), ((0, tp - t), (0, 0))).reshape(nblk, PEER_BLOCK, kk)

    def block_fn(args):
        xk, ek, gk = args
        hk = jnp.einsum("td,tkd->tk", xk, u_tab[ek], preferred_element_type=jnp.float32)
        act = jax.nn.gelu(hk) * gk
        return jnp.einsum("tk,tkd->td", act, v_tab[ek].astype(jnp.float32))

    out = lax.map(block_fn, (xb, eb, gb)).reshape(tp, D_MODEL)[:t]
    return out.reshape(*lead, D_MODEL).astype(x.dtype)


def block_output(x, a_out, b_out, gates, w_out, norm_ffn_g, peer_w_q, peer_sub_keys, peer_u, peer_v):
    merged = gates[..., 0, :] * a_out + gates[..., 1, :] * b_out.astype(x.dtype)
    h = x + merged @ w_out
    return h + peer_ffn(rmsnorm(h, norm_ffn_g), peer_w_q, peer_sub_keys, peer_u, peer_v)


def setup_inputs(seed: int = 0) -> dict:
    key = jax.random.key(seed)
    ks = jax.random.split(key, 17)
    f32 = jnp.float32

    def nrm(k, shape, scale):
        return jax.random.normal(k, shape, f32) * scale

    def gain(k, shape):
        return 1.0 + 0.05 * jax.random.normal(k, shape, f32)

    def kv_cache(k, window):
        return jax.random.normal(k, (DEPTH, DEC_BATCH, min(window, PAST_LEN), 2, ATT_HEADS, HEAD_DIM), f32)

    return {
        "x_prompt": nrm(ks[0], (BATCH, SEQ, D_MODEL), 1.0),
        "x_sample": nrm(ks[1], (DEC_BATCH, DEC_SEQ, D_MODEL), 1.0),
        "cache_kv_w128": kv_cache(ks[2], DIL_WINDOWS[0]),
        "cache_kv_w512": kv_cache(ks[3], DIL_WINDOWS[1]),
        "cache_kv_w2048": kv_cache(ks[4], DIL_WINDOWS[2]),
        "norm_mix_g": gain(ks[5], (DEPTH, D_MODEL)),
        "w_in": nrm(ks[6], (DEPTH, D_MODEL, IN_COLS), D_MODEL ** -0.5),
        "sgu_norm_g": gain(ks[7], (DEPTH, SGU_WIDTH)),
        "sgu_w": nrm(ks[8], (DEPTH, SGU_GROUPS, CHUNK, CHUNK), CHUNK ** -0.5),
        "sgu_b": 1.0 + nrm(ks[9], (DEPTH, SGU_GROUPS, CHUNK), 0.1),
        "w_out": nrm(ks[10], (DEPTH, D_MODEL, D_MODEL), D_MODEL ** -0.5),
        "norm_ffn_g": gain(ks[11], (DEPTH, D_MODEL)),
        "peer_w_q": nrm(ks[12], (DEPTH, D_MODEL, PEER_HEADS * PEER_QDIM), D_MODEL ** -0.5),
        "peer_sub_keys": nrm(ks[13], (DEPTH, 2, N_KEYS, PEER_HALF), PEER_HALF ** -0.5),
        "peer_u": nrm(ks[14], (DEPTH, N_EXPERTS, D_MODEL), D_MODEL ** -0.5),
        "peer_v": nrm(ks[15], (DEPTH, N_EXPERTS, D_MODEL), PEER_HEADS ** -0.5),
        "norm_final_g": gain(ks[16], (D_MODEL,)),
    }


def reference(x_prompt, x_sample, cache_kv_w128, cache_kv_w512, cache_kv_w2048, norm_mix_g, w_in, sgu_norm_g, sgu_w, sgu_b, w_out, norm_ffn_g, peer_w_q, peer_sub_keys, peer_u, peer_v, norm_final_g):
    slopes = alibi_slopes()
    caches = (cache_kv_w128, cache_kv_w512, cache_kv_w2048)
    yp, ys = x_prompt, x_sample
    kv_prompt = [[] for _ in range(N_DIL)]
    kv_sample = [[] for _ in range(N_DIL)]
    sgu_rows = []
    for layer in range(DEPTH):
        u, v, att, gates = mixer_projections(yp, norm_mix_g[layer], w_in[layer])
        vn = layernorm_gain(v, sgu_norm_g[layer])
        a_out = sgu_mix(u, vn, sgu_w[layer], sgu_b[layer])
        outs, lses = [], []
        for g in range(N_DIL):
            q, k, vv = att[..., g, 0, :, :], att[..., g, 1, :, :], att[..., g, 2, :, :]
            o, lse = dilated_band_attention(q, k, vv, slopes[g], DIL_RATES[g], DIL_WINDOWS[g])
            outs.append(o)
            lses.append(lse)
            lw = min(DIL_WINDOWS[g], yp.shape[1])
            kv_prompt[g].append(jnp.stack([k[:, -lw:], vv[:, -lw:]], axis=2))
        b_out = combine_groups(outs, lses)
        yp = block_output(yp, a_out, b_out, gates, w_out[layer], norm_ffn_g[layer], peer_w_q[layer], peer_sub_keys[layer], peer_u[layer], peer_v[layer])

        u, v, att, gates = mixer_projections(ys, norm_mix_g[layer], w_in[layer])
        vn = layernorm_gain(v, sgu_norm_g[layer])
        ds = ys.shape[1]
        lpad = -(-ds // CHUNK) * CHUNK - ds
        a_out = sgu_mix(jnp.pad(u, ((0, 0), (0, lpad), (0, 0))), jnp.pad(vn, ((0, 0), (0, lpad), (0, 0))), sgu_w[layer], sgu_b[layer])[:, :ds]
        sgu_rows.append(vn)
        outs, lses = [], []
        for g in range(N_DIL):
            q, k, vv = att[..., g, 0, :, :], att[..., g, 1, :, :], att[..., g, 2, :, :]
            o, lse = dilated_cached_attention(q, k, vv, caches[g][layer], slopes[g], DIL_RATES[g], DIL_WINDOWS[g])
            outs.append(o)
            lses.append(lse)
            kv_sample[g].append(jnp.stack([k, vv], axis=2))
        b_out = combine_groups(outs, lses)
        ys = block_output(ys, a_out, b_out, gates, w_out[layer], norm_ffn_g[layer], peer_w_q[layer], peer_sub_keys[layer], peer_u[layer], peer_v[layer])

    y_prompt = rmsnorm(yp, norm_final_g)
    y_sample = rmsnorm(ys, norm_final_g)
    kv_w128_prompt = jnp.stack(kv_prompt[0], axis=0)
    kv_w512_prompt = jnp.stack(kv_prompt[1], axis=0)
    kv_w2048_prompt = jnp.stack(kv_prompt[2], axis=0)
    kv_w128_sample = jnp.stack(kv_sample[0], axis=0)
    kv_w512_sample = jnp.stack(kv_sample[1], axis=0)
    kv_w2048_sample = jnp.stack(kv_sample[2], axis=0)
    sgu_v_sample = jnp.stack(sgu_rows, axis=0)
    return (y_prompt, y_sample, kv_w128_prompt, kv_w512_prompt, kv_w2048_prompt, kv_w128_sample, kv_w512_sample, kv_w2048_sample, sgu_v_sample)
```

```python
import functools

import numpy as np
import jax
import jax.numpy as jnp
from jax import lax
from jax.experimental import pallas as pl
from jax.experimental.pallas import tpu as pltpu

F32 = jnp.float32
BF16 = jnp.bfloat16

D_MODEL = 2048
HEAD_DIM = 128
ATT_HEADS = D_MODEL // HEAD_DIM
N_DIL = 3
DIL_WINDOWS = (128, 512, 2048)
DIL_RATES = (1, 4, 16)
ATT_BLOCK = 128
SGU_GROUPS = 8
SGU_GROUP_DIM = D_MODEL // SGU_GROUPS
CHUNK = 128
N_COL_BLOCKS = 13
COL_U, COL_V, COL_ATT, COL_GATE = 0, 1, 2, 11
PEER_HEADS = 8
PEER_TOPK = 16
N_KEYS = 128
N_EXPERTS = N_KEYS * N_KEYS
PEER_HALF = 128
NORM_EPS = 1e-6
MASK_VALUE = -1e30
NEG_INF = float("-inf")
VMEM_LIMIT = 56 * 1024 * 1024


def _alibi_slopes():
    n = N_DIL * ATT_HEADS
    e = np.arange(1, n + 1, dtype=np.float32)
    return np.exp2(np.float32(-8.0) * e / np.float32(n)).astype(np.float32).reshape(N_DIL, ATT_HEADS)


def _params(*sem):
    return pltpu.CompilerParams(dimension_semantics=sem, vmem_limit_bytes=VMEM_LIMIT)


def _rms(x, g):
    return x * lax.rsqrt(jnp.mean(x * x, axis=-1, keepdims=True) + NORM_EPS) * g


def _rmsnorm_kernel(x_ref, g_ref, o_ref):
    o_ref[...] = _rms(x_ref[...], g_ref[...]).astype(o_ref.dtype)


def _rmsnorm_bf16(x, g, tm):
    t = x.shape[0]
    return pl.pallas_call(
        _rmsnorm_kernel,
        grid=(t // tm,),
        in_specs=[pl.BlockSpec((tm, D_MODEL), lambda i: (i, 0)),
                  pl.BlockSpec((1, D_MODEL), lambda i: (0, 0))],
        out_specs=pl.BlockSpec((tm, D_MODEL), lambda i: (i, 0)),
        out_shape=jax.ShapeDtypeStruct((t, D_MODEL), BF16),
        compiler_params=_params("parallel"),
    )(x, g.reshape(1, D_MODEL))


def _inproj_kernel(x_ref, w_ref, o_ref, *, blocks_per_col):
    cb = pl.program_id(1) // blocks_per_col
    p = jnp.dot(x_ref[...], w_ref[...], preferred_element_type=F32)

    @pl.when(cb < COL_ATT)
    def _():
        o_ref[...] = jax.nn.gelu(p)

    @pl.when((cb >= COL_ATT) & (cb < COL_GATE))
    def _():
        o_ref[...] = p

    @pl.when(cb >= COL_GATE)
    def _():
        o_ref[...] = jax.nn.sigmoid(p)


def _inproj(xn, w, tm, tn):
    t = xn.shape[0]
    n = w.shape[1]
    return pl.pallas_call(
        functools.partial(_inproj_kernel, blocks_per_col=D_MODEL // tn),
        grid=(t // tm, n // tn),
        in_specs=[pl.BlockSpec((tm, D_MODEL), lambda i, j: (i, 0)),
                  pl.BlockSpec((D_MODEL, tn), lambda i, j: (0, j))],
        out_specs=pl.BlockSpec((tm, tn), lambda i, j: (i, j)),
        out_shape=jax.ShapeDtypeStruct((t, n), F32),
        compiler_params=_params("parallel", "arbitrary"),
    )(xn, w)


def _sgu_kernel(u_ref, v_ref, g_ref, w_ref, b_ref, a_ref, vn_ref):
    v = v_ref[...]
    vc = v - jnp.mean(v, axis=-1, keepdims=True)
    vn = vc * lax.rsqrt(jnp.mean(vc * vc, axis=-1, keepdims=True) + NORM_EPS) * g_ref[...]
    vn_ref[...] = vn
    for g in range(SGU_GROUPS):
        cs = slice(g * SGU_GROUP_DIM, (g + 1) * SGU_GROUP_DIM)
        mix = jnp.dot(w_ref[g], vn[:, cs].astype(BF16), preferred_element_type=F32) + b_ref[g]
        a_ref[:, cs] = (u_ref[:, cs] * mix).astype(a_ref.dtype)


def _sgu(p, sgu_norm_g, w_all, b_all, n_prompt_chunks):
    t = p.shape[0]
    sel = lambda c: jnp.where(c >= n_prompt_chunks, 1, 0)
    return pl.pallas_call(
        _sgu_kernel,
        grid=(t // CHUNK,),
        in_specs=[pl.BlockSpec((CHUNK, D_MODEL), lambda c: (c, COL_U)),
                  pl.BlockSpec((CHUNK, D_MODEL), lambda c: (c, COL_V)),
                  pl.BlockSpec((1, D_MODEL), lambda c: (0, 0)),
                  pl.BlockSpec((None, SGU_GROUPS, CHUNK, CHUNK), lambda c: (sel(c), 0, 0, 0)),
                  pl.BlockSpec((None, SGU_GROUPS, CHUNK, 1), lambda c: (sel(c), 0, 0, 0))],
        out_specs=[pl.BlockSpec((CHUNK, D_MODEL), lambda c: (c, 0)),
                   pl.BlockSpec((CHUNK, D_MODEL), lambda c: (0, 0))],
        out_shape=[jax.ShapeDtypeStruct((t, D_MODEL), BF16),
                   jax.ShapeDtypeStruct((CHUNK, D_MODEL), F32)],
        compiler_params=_params("arbitrary"),
    )(p, p, sgu_norm_g.reshape(1, D_MODEL), w_all, b_all)


def _band_attn_kernel(q_ref, kp_ref, kc_ref, vp_ref, vc_ref, o_ref, lse_ref, *, coefs):
    j = pl.program_id(2)
    qi = lax.broadcasted_iota(jnp.int32, (ATT_BLOCK, ATT_BLOCK), 0)
    ki = lax.broadcasted_iota(jnp.int32, (ATT_BLOCK, ATT_BLOCK), 1)
    valid_p = (ki >= qi) & (j > 0)
    valid_c = ki <= qi
    back_p = (qi + ATT_BLOCK - ki).astype(F32)
    back_c = (qi - ki).astype(F32)
    scale = HEAD_DIM ** -0.5
    nt = (((1,), (1,)), ((), ()))
    lse_ref[...] = jnp.zeros_like(lse_ref)
    for h in range(ATT_HEADS):
        hs = slice(h * HEAD_DIM, (h + 1) * HEAD_DIM)
        q = q_ref[:, hs].astype(BF16)
        s_p = lax.dot_general(q, kp_ref[:, hs].astype(BF16), nt, preferred_element_type=F32) * scale
        s_c = lax.dot_general(q, kc_ref[:, hs].astype(BF16), nt, preferred_element_type=F32) * scale
        s_p = jnp.where(valid_p, s_p - coefs[h] * back_p, MASK_VALUE)
        s_c = jnp.where(valid_c, s_c - coefs[h] * back_c, MASK_VALUE)
        m = jnp.maximum(jnp.max(s_p, axis=-1, keepdims=True), jnp.max(s_c, axis=-1, keepdims=True))
        p_p = jnp.exp(s_p - m)
        p_c = jnp.exp(s_c - m)
        l = jnp.sum(p_p, axis=-1, keepdims=True) + jnp.sum(p_c, axis=-1, keepdims=True)
        o = (jnp.dot(p_p.astype(BF16), vp_ref[:, hs].astype(BF16), preferred_element_type=F32)
             + jnp.dot(p_c.astype(BF16), vc_ref[:, hs].astype(BF16), preferred_element_type=F32))
        o_ref[:, hs] = (o / l).astype(o_ref.dtype)
        lse_ref[:, h:h + 1] = m + jnp.log(l)


def _band_attention(p, group, batch, seq):
    t = p.shape[0]
    dil = DIL_RATES[group]
    nb = seq // dil // ATT_BLOCK
    pv = p.reshape(t // dil, dil * N_COL_BLOCKS * D_MODEL)
    cq, ck, cv = (COL_ATT + 3 * group + i for i in range(3))
    coefs = [float(s) * dil for s in _alibi_slopes()[group]]

    def cur(col):
        return pl.BlockSpec((ATT_BLOCK, D_MODEL), lambda b, r, j: (b * nb + j, r * N_COL_BLOCKS + col))

    def prev(col):
        return pl.BlockSpec((ATT_BLOCK, D_MODEL),
                            lambda b, r, j: (b * nb + jnp.maximum(j - 1, 0), r * N_COL_BLOCKS + col))

    o, lse = pl.pallas_call(
        functools.partial(_band_attn_kernel, coefs=coefs),
        grid=(batch, dil, nb),
        in_specs=[cur(cq), prev(ck), cur(ck), prev(cv), cur(cv)],
        out_specs=[pl.BlockSpec((ATT_BLOCK, D_MODEL), lambda b, r, j: (b * nb + j, r)),
                   pl.BlockSpec((ATT_BLOCK, 128), lambda b, r, j: (b * nb + j, r))],
        out_shape=[jax.ShapeDtypeStruct((t // dil, dil * D_MODEL), BF16),
                   jax.ShapeDtypeStruct((t // dil, dil * 128), F32)],
        compiler_params=_params("parallel", "parallel", "arbitrary"),
    )(pv, pv, pv, pv, pv)
    return o.reshape(t, D_MODEL), lse.reshape(t, 128)


def _cached_attn_kernel(q_ref, k_ref, v_ref, c_ref, o_ref, lse_ref, *, coefs, dil, dec_seq):
    rows = q_ref.shape[0]
    kv_width = 2 * D_MODEL
    row = lax.broadcasted_iota(jnp.int32, (rows, ATT_BLOCK), 0)
    lane = lax.broadcasted_iota(jnp.int32, (rows, ATT_BLOCK), 1)
    row1 = row[:, :1]
    scale = HEAD_DIM ** -0.5
    nt = (((1,), (1,)), ((), ()))
    lse_ref[...] = jnp.zeros_like(lse_ref)
    for h in range(ATT_HEADS):
        hs = slice(h * HEAD_DIM, (h + 1) * HEAD_DIM)
        q = q_ref[:, hs]
        kn = k_ref[:, hs]
        vn = v_ref[:, hs]
        qb = q.astype(BF16)
        if dil == 1:
            s = lax.dot_general(qb, c_ref[:, hs].astype(BF16), nt, preferred_element_type=F32) * scale
            back = (ATT_BLOCK + row - lane).astype(F32)
            s = jnp.where(lane >= row, s - coefs[h] * back, MASK_VALUE)
            s_new = []
            for i2 in range(dec_seq):
                sn = jnp.sum(q * kn[i2:i2 + 1, :], axis=-1, keepdims=True) * scale
                back_n = (row1 - i2).astype(F32)
                s_new.append(jnp.where(row1 >= i2, sn - coefs[h] * back_n, MASK_VALUE))
        else:
            s = jnp.zeros((rows, ATT_BLOCK), F32)
            for i in range(dec_seq):
                ks = slice(i * kv_width + h * HEAD_DIM, i * kv_width + (h + 1) * HEAD_DIM)
                s_i = lax.dot_general(qb, c_ref[:, ks].astype(BF16), nt, preferred_element_type=F32) * scale
                s = jnp.where(row == i, s_i, s)
            s = s - coefs[h] * (ATT_BLOCK - lane).astype(F32)
            s_new = [jnp.sum(q * kn, axis=-1, keepdims=True) * scale]
        m = jnp.max(s, axis=-1, keepdims=True)
        for sn in s_new:
            m = jnp.maximum(m, sn)
        p = jnp.exp(s - m)
        p_new = [jnp.exp(sn - m) for sn in s_new]
        l = jnp.sum(p, axis=-1, keepdims=True)
        for pn in p_new:
            l = l + pn
        pb = p.astype(BF16)
        if dil == 1:
            vs = slice(D_MODEL + h * HEAD_DIM, D_MODEL + (h + 1) * HEAD_DIM)
            o = jnp.dot(pb, c_ref[:, vs].astype(BF16), preferred_element_type=F32)
            for i2 in range(dec_seq):
                o = o + p_new[i2] * vn[i2:i2 + 1, :]
        else:
            o = p_new[0] * vn
            for i in range(dec_seq):
                vs = slice(i * kv_width + D_MODEL + h * HEAD_DIM, i * kv_width + D_MODEL + (h + 1) * HEAD_DIM)
                o_i = jnp.dot(pb, c_ref[:, vs].astype(BF16), preferred_element_type=F32)
                o = o + jnp.where(row == i, o_i, 0.0)
        o_ref[:, hs] = o / l
        lse_ref[:, h:h + 1] = m + jnp.log(l)


def _cached_attention(ps, cache, group, dec_seq):
    db, rows, _ = ps.shape
    dil = DIL_RATES[group]
    window = DIL_WINDOWS[group]
    assert cache.shape[1] == window and window == dil * ATT_BLOCK
    assert dil == 1 or dec_seq <= dil
    kv_width = 2 * D_MODEL
    cv = cache.reshape(db, ATT_BLOCK, dil * kv_width)
    cw = kv_width if dil == 1 else dec_seq * kv_width
    cq, ck, cvv = (COL_ATT + 3 * group + i for i in range(3))
    coefs = [float(s) * dil for s in _alibi_slopes()[group]]
    new = lambda col: pl.BlockSpec((None, rows, D_MODEL), lambda b: (b, 0, col))
    return pl.pallas_call(
        functools.partial(_cached_attn_kernel, coefs=coefs, dil=dil, dec_seq=dec_seq),
        grid=(db,),
        in_specs=[new(cq), new(ck), new(cvv),
                  pl.BlockSpec((None, ATT_BLOCK, cw), lambda b: (b, 0, 0))],
        out_specs=[pl.BlockSpec((None, rows, D_MODEL), lambda b: (b, 0, 0)),
                   pl.BlockSpec((None, rows, 128), lambda b: (b, 0, 0))],
        out_shape=[jax.ShapeDtypeStruct((db, rows, D_MODEL), F32),
                   jax.ShapeDtypeStruct((db, rows, 128), F32)],
        compiler_params=_params("parallel"),
    )(ps, ps, ps, cv)


def _merge_kernel(a_ref, ga_ref, gb_ref, o0_ref, o1_ref, o2_ref, l0_ref, l1_ref, l2_ref,
                  x_ref, w_ref, g_ref, h_ref, hn_ref, merged_ref):
    l0, l1, l2 = l0_ref[...], l1_ref[...], l2_ref[...]
    mx = jnp.maximum(jnp.maximum(l0, l1), l2)
    e0, e1, e2 = jnp.exp(l0 - mx), jnp.exp(l1 - mx), jnp.exp(l2 - mx)
    den = e0 + e1 + e2
    w0, w1, w2 = e0 / den, e1 / den, e2 / den
    for h in range(ATT_HEADS):
        hs = slice(h * HEAD_DIM, (h + 1) * HEAD_DIM)
        b_out = (w0[:, h:h + 1] * o0_ref[:, hs].astype(F32)
                 + w1[:, h:h + 1] * o1_ref[:, hs].astype(F32)
                 + w2[:, h:h + 1] * o2_ref[:, hs].astype(F32))
        merged = ga_ref[:, hs] * a_ref[:, hs].astype(F32) + gb_ref[:, hs] * b_out
        merged_ref[:, hs] = merged.astype(merged_ref.dtype)
    h_new = x_ref[...] + jnp.dot(merged_ref[...], w_ref[...], preferred_element_type=F32)
    h_ref[...] = h_new
    hn_ref[...] = _rms(h_new, g_ref[...]).astype(hn_ref.dtype)


def _merge(a_out, p, outs, lses, x, w_out, norm_g, tm):
    t = x.shape[0]
    row = lambda col: pl.BlockSpec((tm, D_MODEL), lambda i: (i, col))
    lrow = pl.BlockSpec((tm, 128), lambda i: (i, 0))
    return pl.pallas_call(
        _merge_kernel,
        grid=(t // tm,),
        in_specs=[row(0), row(COL_GATE), row(COL_GATE + 1), row(0), row(0), row(0), lrow, lrow, lrow,
                  row(0), pl.BlockSpec((D_MODEL, D_MODEL), lambda i: (0, 0)),
                  pl.BlockSpec((1, D_MODEL), lambda i: (0, 0))],
        out_specs=[row(0), row(0)],
        out_shape=[jax.ShapeDtypeStruct((t, D_MODEL), F32), jax.ShapeDtypeStruct((t, D_MODEL), BF16)],
        scratch_shapes=[pltpu.VMEM((tm, D_MODEL), BF16)],
        compiler_params=_params("parallel"),
    )(a_out, p, p, outs[0], outs[1], outs[2], lses[0], lses[1], lses[2], x, w_out, norm_g.reshape(1, D_MODEL))


def _peer_score_kernel(hn_ref, wq_ref, keys_ref, s_ref):
    q = jnp.dot(hn_ref[...], wq_ref[...], preferred_element_type=F32).astype(BF16)
    nt = (((1,), (1,)), ((), ()))
    for hc in range(2 * PEER_HEADS):
        cs = slice(hc * PEER_HALF, (hc + 1) * PEER_HALF)
        s_ref[hc] = lax.dot_general(keys_ref[hc % 2], q[:, cs], nt, preferred_element_type=F32)


def _peer_scores(hn, w_q, sub_keys, tm):
    t = hn.shape[0]
    return pl.pallas_call(
        _peer_score_kernel,
        grid=(t // tm,),
        in_specs=[pl.BlockSpec((tm, D_MODEL), lambda i: (i, 0)),
                  pl.BlockSpec((D_MODEL, 2 * PEER_HEADS * PEER_HALF), lambda i: (0, 0)),
                  pl.BlockSpec((2, N_KEYS, PEER_HALF), lambda i: (0, 0, 0))],
        out_specs=pl.BlockSpec((2 * PEER_HEADS, N_KEYS, tm), lambda i: (0, 0, i)),
        out_shape=jax.ShapeDtypeStruct((2 * PEER_HEADS, N_KEYS, t), F32),
        compiler_params=_params("parallel"),
    )(hn, w_q, sub_keys)


def _take_top(s, count):
    rows = lax.broadcasted_iota(jnp.int32, s.shape, 0).astype(F32)
    vals, idxs = [], []
    for _ in range(count):
        m = jnp.max(s, axis=0, keepdims=True)
        pos = jnp.min(jnp.where(s == m, rows, float(s.shape[0])), axis=0, keepdims=True)
        vals.append(m)
        idxs.append(pos)
        s = jnp.where(rows == pos, NEG_INF, s)
    return vals, idxs


def _stack_rows(rows_list, krow):
    out = jnp.zeros(krow.shape, F32)
    for j, r in enumerate(rows_list):
        out = jnp.where(krow == float(j), r, out)
    return out


def _route_kernel(s_ref, e1_ref, e2_ref, gate_ref):
    k = PEER_TOPK
    v1, i1 = _take_top(s_ref[0], k)
    v2, i2 = _take_top(s_ref[1], k)
    krow = lax.broadcasted_iota(jnp.int32, (k, s_ref.shape[2]), 0).astype(F32)
    v2_all = _stack_rows(v2, krow)
    i1_all = _stack_rows(i1, krow)
    i2_all = _stack_rows(i2, krow)
    cand = jnp.concatenate([v1[a] + v2_all for a in range(k)], axis=0)
    top_s, pos = _take_top(cand, k)
    e1, e2 = [], []
    for j in range(k):
        a = jnp.floor(pos[j] * (1.0 / k))
        b = pos[j] - a * k
        e1.append(jnp.sum(jnp.where(krow == a, i1_all, 0.0), axis=0, keepdims=True))
        e2.append(jnp.sum(jnp.where(krow == b, i2_all, 0.0), axis=0, keepdims=True))
    ex = jnp.exp(_stack_rows(top_s, krow) - top_s[0])
    gate_ref[...] = ex / jnp.sum(ex, axis=0, keepdims=True)
    e1_ref[...] = _stack_rows(e1, krow)
    e2_ref[...] = _stack_rows(e2, krow)


def _route(scores_t, tl):
    t = scores_t.shape[2]
    kk = PEER_HEADS * PEER_TOPK
    out = pl.BlockSpec((PEER_TOPK, tl), lambda i, h: (h, i))
    return pl.pallas_call(
        _route_kernel,
        grid=(t // tl, PEER_HEADS),
        in_specs=[pl.BlockSpec((2, N_KEYS, tl), lambda i, h: (h, 0, i))],
        out_specs=[out, out, out],
        out_shape=[jax.ShapeDtypeStruct((kk, t), F32)] * 3,
        compiler_params=_params("parallel", "parallel"),
    )(scores_t)


def _expert_weight_kernel(e1_ref, e2_ref, gate_ref, g_ref, e1t_ref, e2t_ref, gt_ref):
    e1t_ref[...] = e1_ref[...].T
    e2t_ref[...] = e2_ref[...].T
    gt_ref[...] = gate_ref[...].T
    kk = e1_ref.shape[0]
    key = lax.broadcasted_iota(jnp.int32, (N_KEYS, kk), 0).astype(F32)
    nt = (((1,), (1,)), ((), ()))

    def body(t, carry):
        r1 = e1t_ref[pl.ds(t, 1), :]
        r2 = e2t_ref[pl.ds(t, 1), :]
        gr = gt_ref[pl.ds(t, 1), :]
        a_t = jnp.where(key == r1, 1.0, 0.0).astype(BF16)
        b_t = jnp.where(key == r2, gr, 0.0).astype(BF16)
        g_ref[t] = lax.dot_general(a_t, b_t, nt, preferred_element_type=F32).astype(g_ref.dtype)
        return carry

    lax.fori_loop(0, g_ref.shape[0], body, 0)


def _expert_weights(e1, e2, gate, tl):
    kk, t = e1.shape
    slot = pl.BlockSpec((kk, tl), lambda i: (0, i))
    return pl.pallas_call(
        _expert_weight_kernel,
        grid=(t // tl,),
        in_specs=[slot, slot, slot],
        out_specs=pl.BlockSpec((tl, N_KEYS, N_KEYS), lambda i: (i, 0, 0)),
        out_shape=jax.ShapeDtypeStruct((t, N_KEYS, N_KEYS), BF16),
        scratch_shapes=[pltpu.VMEM((tl, kk), F32)] * 3,
        compiler_params=_params("parallel"),
    )(e1, e2, gate)


def _peer_kernel(hn_ref, ut_ref, v_ref, g_ref, h_ref, gf_ref, y_ref, acc_ref):
    e = pl.program_id(1)

    @pl.when(e == 0)
    def _():
        acc_ref[...] = jnp.zeros_like(acc_ref)

    hk = jnp.dot(hn_ref[...], ut_ref[...], preferred_element_type=F32)
    act = (jax.nn.gelu(hk) * g_ref[...].astype(F32)).astype(BF16)
    acc_ref[...] += jnp.dot(act, v_ref[...], preferred_element_type=F32)

    @pl.when(e == pl.num_programs(1) - 1)
    def _():
        y_ref[...] = _rms(h_ref[...] + acc_ref[...], gf_ref[...])


def _peer(hn, u_t, v, g, h, norm_g, tm, te):
    t = hn.shape[0]
    row = pl.BlockSpec((tm, D_MODEL), lambda i, e: (i, 0))
    return pl.pallas_call(
        _peer_kernel,
        grid=(t // tm, N_EXPERTS // te),
        in_specs=[row,
                  pl.BlockSpec((D_MODEL, te), lambda i, e: (0, e)),
                  pl.BlockSpec((te, D_MODEL), lambda i, e: (e, 0)),
                  pl.BlockSpec((tm, te), lambda i, e: (i, e)),
                  row,
                  pl.BlockSpec((1, D_MODEL), lambda i, e: (0, 0))],
        out_specs=row,
        out_shape=jax.ShapeDtypeStruct((t, D_MODEL), F32),
        scratch_shapes=[pltpu.VMEM((tm, D_MODEL), F32)],
        compiler_params=_params("parallel", "arbitrary"),
    )(hn, u_t, v, g, h, norm_g.reshape(1, D_MODEL))


def _row_tile(t, candidates):
    for c in candidates:
        if t % c == 0:
            return c
    raise ValueError(f"no row tile for {t} tokens")


def kernel(x_prompt, x_sample, cache_kv_w128, cache_kv_w512, cache_kv_w2048, norm_mix_g, w_in, sgu_norm_g, sgu_w, sgu_b, w_out, norm_ffn_g, peer_w_q, peer_sub_keys, peer_u, peer_v, norm_final_g):
    batch, seq, _ = x_prompt.shape
    db, ds, _ = x_sample.shape
    depth = w_in.shape[0]
    assert depth == 1 and db * ds == CHUNK and seq % (DIL_RATES[-1] * ATT_BLOCK) == 0
    caches = (cache_kv_w128, cache_kv_w512, cache_kv_w2048)
    tp, ts = batch * seq, db * ds
    t = tp + ts
    rows = 8

    x = jnp.concatenate([x_prompt.reshape(tp, D_MODEL), x_sample.reshape(ts, D_MODEL)], axis=0)
    tm_big = _row_tile(t, (1040, 640, 128))
    tm_lane = _row_tile(t, (640, 128))

    xn = _rmsnorm_bf16(x, norm_mix_g[0], tm_big)
    p = _inproj(xn, w_in[0].astype(BF16), tm_big, 1024)

    w_tril = sgu_w[0] * jnp.tril(jnp.ones((CHUNK, CHUNK), F32))
    w_s = jnp.einsum("bc,gis->gbics", jnp.eye(db, dtype=F32), w_tril[:, :ds, :ds]).reshape(SGU_GROUPS, ts, ts)
    b_s = jnp.tile(sgu_b[0][:, :ds], (1, db))
    w_all = jnp.stack([w_tril, w_s]).astype(BF16)
    b_all = jnp.stack([sgu_b[0], b_s])[..., None]
    a_out, vn_s = _sgu(p, sgu_norm_g[0], w_all, b_all, tp // CHUNK)

    ps = p[tp:].reshape(db, ds, N_COL_BLOCKS * D_MODEL)
    ps_pad = jnp.pad(ps, ((0, 0), (0, rows - ds), (0, 0)))
    outs, lses = [], []
    for g in range(N_DIL):
        o_p, lse_p = _band_attention(p, g, batch, seq)
        o_s, lse_s = _cached_attention(ps_pad, caches[g][0], g, ds)
        o_s = o_s[:, :ds].reshape(ts, D_MODEL).astype(BF16)
        lse_s = lse_s[:, :ds].reshape(ts, 128)
        outs.append(lax.dynamic_update_slice(o_p, o_s, (tp, 0)))
        lses.append(lax.dynamic_update_slice(lse_p, lse_s, (tp, 0)))

    h, hn = _merge(a_out, p, outs, lses, x, w_out[0].astype(BF16), norm_ffn_g[0], 256 if t % 256 == 0 else 128)

    scores_t = _peer_scores(hn, peer_w_q[0].astype(BF16), peer_sub_keys[0].astype(BF16), tm_lane)
    e1, e2, gate = _route(scores_t, 128)
    g_dense = _expert_weights(e1, e2, gate, 128).reshape(t, N_EXPERTS)
    y = _peer(hn, peer_u[0].T.astype(BF16), peer_v[0].astype(BF16), g_dense, h, norm_final_g, tm_lane, 512)

    y_prompt = y[:tp].reshape(batch, seq, D_MODEL)
    y_sample = y[tp:].reshape(db, ds, D_MODEL)
    pp = p[:tp].reshape(batch, seq, N_COL_BLOCKS, ATT_HEADS, HEAD_DIM)
    psr = ps.reshape(db, ds, N_COL_BLOCKS, ATT_HEADS, HEAD_DIM)
    kv_prompt, kv_sample = [], []
    for g in range(N_DIL):
        ck = COL_ATT + 3 * g + 1
        lw = min(DIL_WINDOWS[g], seq)
        kv_prompt.append(pp[:, seq - lw:, ck:ck + 2][None])
        kv_sample.append(psr[:, :, ck:ck + 2][None])
    sgu_v_sample = vn_s.reshape(1, db, ds, D_MODEL)
    return (y_prompt, y_sample, kv_prompt[0], kv_prompt[1], kv_prompt[2],
            kv_sample[0], kv_sample[1], kv_sample[2], sgu_v_sample)
```

```python
import functools

import numpy as np
import jax
import jax.numpy as jnp
from jax import lax
from jax.experimental import pallas as pl
from jax.experimental.pallas import tpu as pltpu

F32 = jnp.float32
BF16 = jnp.bfloat16

D_MODEL = 2048
HEAD_DIM = 128
ATT_HEADS = D_MODEL // HEAD_DIM
N_DIL = 3
DIL_WINDOWS = (128, 512, 2048)
DIL_RATES = (1, 4, 16)
ATT_BLOCK = 128
SLABS = DIL_RATES[-1]
SLAB_ROWS = 128
SGU_GROUPS = 8
SGU_GROUP_DIM = D_MODEL // SGU_GROUPS
CHUNK = 128
N_COL_BLOCKS = 13
COL_U, COL_V, COL_ATT, COL_GATE = 0, 1, 2, 11
PEER_HEADS = 8
PEER_TOPK = 16
N_KEYS = 128
N_EXPERTS = N_KEYS * N_KEYS
PEER_HALF = 128
NORM_EPS = 1e-6
MASK_VALUE = -1e30
NEG_INF = float("-inf")
VMEM_LIMIT = 56 * 1024 * 1024

BLOCK_SHAPES = ((16, 8), (4, 32), (1, 128))


def _alibi_slopes():
    n = N_DIL * ATT_HEADS
    e = np.arange(1, n + 1, dtype=np.float32)
    return np.exp2(np.float32(-8.0) * e / np.float32(n)).astype(np.float32).reshape(N_DIL, ATT_HEADS)


def _block_steps(group):
    slabs, rows = BLOCK_SHAPES[group]
    n = np.arange(slabs * rows)
    s, j = n // rows, n % rows
    if group == 0:
        return j * 16 + (s % 4) * 4 + s // 4
    if group == 1:
        return j * 4 + s
    return j


def _params(*sem):
    return pltpu.CompilerParams(dimension_semantics=sem, vmem_limit_bytes=VMEM_LIMIT)


def _rms(x, g):
    return x * lax.rsqrt(jnp.mean(x * x, axis=-1, keepdims=True) + NORM_EPS) * g


def _permute_norm_kernel(x_ref, g_ref, xl_ref, xn_ref):
    for rr in range(8):
        x = x_ref[:, rr, :]
        xl_ref[rr % 4, rr // 4] = x
        xn_ref[rr % 4, rr // 4] = _rms(x, g_ref[...]).astype(xn_ref.dtype)


def _permute_norm(x_prompt, g):
    batch, seq, _ = x_prompt.shape
    x4 = x_prompt.reshape(batch, SLAB_ROWS, SLABS, D_MODEL)
    out = pl.BlockSpec((None, 4, 2, SLAB_ROWS, D_MODEL), lambda b, h: (b, 0, h, 0, 0))
    xl, xn = pl.pallas_call(
        _permute_norm_kernel,
        grid=(batch, 2),
        in_specs=[pl.BlockSpec((None, SLAB_ROWS, 8, D_MODEL), lambda b, h: (b, 0, h, 0)),
                  pl.BlockSpec((1, D_MODEL), lambda b, h: (0, 0))],
        out_specs=[out, out],
        out_shape=[jax.ShapeDtypeStruct((batch + 1, 4, 4, SLAB_ROWS, D_MODEL), F32),
                   jax.ShapeDtypeStruct((batch + 1, 4, 4, SLAB_ROWS, D_MODEL), BF16)],
        compiler_params=_params("parallel", "parallel"),
    )(x4, g.reshape(1, D_MODEL))
    rows = (batch + 1) * seq
    return xl.reshape(rows, D_MODEL), xn.reshape(rows, D_MODEL)


def _rmsnorm_kernel(x_ref, g_ref, o_ref):
    o_ref[...] = _rms(x_ref[...], g_ref[...]).astype(o_ref.dtype)


def _rmsnorm_bf16(x, g):
    return pl.pallas_call(
        _rmsnorm_kernel,
        out_shape=jax.ShapeDtypeStruct(x.shape, BF16),
    )(x, g.reshape(1, D_MODEL))


def _inproj_kernel(x_ref, w_ref, o_ref, *, blocks_per_col):
    cb = pl.program_id(1) // blocks_per_col
    p = jnp.dot(x_ref[...], w_ref[...], preferred_element_type=F32)

    @pl.when(cb < COL_ATT)
    def _():
        o_ref[...] = jax.nn.gelu(p)

    @pl.when((cb >= COL_ATT) & (cb < COL_GATE))
    def _():
        o_ref[...] = p

    @pl.when(cb >= COL_GATE)
    def _():
        o_ref[...] = jax.nn.sigmoid(p)


def _inproj(xn, w, t, tm, tn):
    n = w.shape[1]
    return pl.pallas_call(
        functools.partial(_inproj_kernel, blocks_per_col=D_MODEL // tn),
        grid=(t // tm, n // tn),
        in_specs=[pl.BlockSpec((tm, D_MODEL), lambda i, j: (i, 0)),
                  pl.BlockSpec((D_MODEL, tn), lambda i, j: (0, j))],
        out_specs=pl.BlockSpec((tm, tn), lambda i, j: (i, j)),
        out_shape=jax.ShapeDtypeStruct((t, n), F32),
        compiler_params=_params("parallel", "arbitrary"),
    )(xn, w)


def _sgu_kernel(u_ref, v_ref, g_ref, w_ref, b_ref, a_ref, vn_ref):
    v = v_ref[...].reshape(CHUNK, D_MODEL)
    u = u_ref[...].reshape(CHUNK, D_MODEL)
    vc = v - jnp.mean(v, axis=-1, keepdims=True)
    vn = vc * lax.rsqrt(jnp.mean(vc * vc, axis=-1, keepdims=True) + NORM_EPS) * g_ref[...]
    if vn_ref is not None:
        vn_ref[...] = vn
    cols = []
    for g in range(SGU_GROUPS):
        cs = slice(g * SGU_GROUP_DIM, (g + 1) * SGU_GROUP_DIM)
        mix = jnp.dot(w_ref[g], vn[:, cs].astype(BF16), preferred_element_type=F32) + b_ref[g]
        cols.append(u[:, cs] * mix)
    a_ref[...] = jnp.concatenate(cols, axis=-1).reshape(a_ref.shape)


def _sgu_prompt(p3, sgu_norm_g, w, b, batch):
    slabs, rows = BLOCK_SHAPES[0]
    n_chunks = SLAB_ROWS // rows
    blk = lambda col: pl.BlockSpec((slabs, rows, D_MODEL), lambda bb, c: (bb, c, col))
    return pl.pallas_call(
        lambda u, v, g, ww, bb, a: _sgu_kernel(u, v, g, ww, bb, a, None),
        grid=(batch, n_chunks),
        in_specs=[blk(COL_U), blk(COL_V),
                  pl.BlockSpec((1, D_MODEL), lambda bb, c: (0, 0)),
                  pl.BlockSpec((SGU_GROUPS, CHUNK, CHUNK), lambda bb, c: (0, 0, 0)),
                  pl.BlockSpec((SGU_GROUPS, CHUNK, 1), lambda bb, c: (0, 0, 0))],
        out_specs=blk(0),
        out_shape=jax.ShapeDtypeStruct((p3.shape[0], SLAB_ROWS, D_MODEL), F32),
        compiler_params=_params("parallel", "parallel"),
    )(p3, p3, sgu_norm_g.reshape(1, D_MODEL), w, b)


def _sgu_sample(p3, sgu_norm_g, w, b, slab):
    blk = lambda col: pl.BlockSpec((1, SLAB_ROWS, D_MODEL), lambda i: (slab, 0, col))
    return pl.pallas_call(
        _sgu_kernel,
        grid=(1,),
        in_specs=[blk(COL_U), blk(COL_V),
                  pl.BlockSpec((1, D_MODEL), lambda i: (0, 0)),
                  pl.BlockSpec((SGU_GROUPS, CHUNK, CHUNK), lambda i: (0, 0, 0)),
                  pl.BlockSpec((SGU_GROUPS, CHUNK, 1), lambda i: (0, 0, 0))],
        out_specs=[pl.BlockSpec((CHUNK, D_MODEL), lambda i: (0, 0)),
                   pl.BlockSpec((CHUNK, D_MODEL), lambda i: (0, 0))],
        out_shape=[jax.ShapeDtypeStruct((CHUNK, D_MODEL), F32), jax.ShapeDtypeStruct((CHUNK, D_MODEL), F32)],
        compiler_params=_params("arbitrary"),
    )(p3, p3, sgu_norm_g.reshape(1, D_MODEL), w, b)


def _band_attn_kernel(*refs, coefs, has_prev):
    if has_prev:
        back_ref, q_ref, kp_ref, kc_ref, vp_ref, vc_ref, o_ref, lse_ref = refs
    else:
        back_ref, q_ref, kc_ref, vc_ref, o_ref, lse_ref = refs
    n = ATT_BLOCK
    back_c = back_ref[...]
    valid_c = back_c >= 0.0
    if has_prev:
        back_p = back_c + float(n)
        valid_p = (back_p <= float(n)) & (pl.program_id(2) > 0)
    scale = HEAD_DIM ** -0.5
    nt = (((1,), (1,)), ((), ()))
    lane = lax.broadcasted_iota(jnp.int32, (n, 128), 1)
    lse = jnp.zeros((n, 128), F32)
    o_cols = []
    for h in range(ATT_HEADS):
        hs = slice(h * HEAD_DIM, (h + 1) * HEAD_DIM)
        ld = lambda ref: ref[:, :, hs].reshape(n, HEAD_DIM).astype(BF16)
        q = ld(q_ref)
        s_c = lax.dot_general(q, ld(kc_ref), nt, preferred_element_type=F32) * scale
        s_c = jnp.where(valid_c, s_c - coefs[h] * back_c, MASK_VALUE)
        m = jnp.max(s_c, axis=-1, keepdims=True)
        if has_prev:
            s_p = lax.dot_general(q, ld(kp_ref), nt, preferred_element_type=F32) * scale
            s_p = jnp.where(valid_p, s_p - coefs[h] * back_p, MASK_VALUE)
            m = jnp.maximum(m, jnp.max(s_p, axis=-1, keepdims=True))
        p_c = jnp.exp(s_c - m)
        l = jnp.sum(p_c, axis=-1, keepdims=True)
        o = jnp.dot(p_c.astype(BF16), ld(vc_ref), preferred_element_type=F32)
        if has_prev:
            p_p = jnp.exp(s_p - m)
            l = l + jnp.sum(p_p, axis=-1, keepdims=True)
            o = o + jnp.dot(p_p.astype(BF16), ld(vp_ref), preferred_element_type=F32)
        o_cols.append(o / l)
        lse = jnp.where(lane == h, m + jnp.log(l), lse)
    o_ref[...] = jnp.concatenate(o_cols, axis=-1).reshape(o_ref.shape)
    lse_ref[...] = lse.reshape(lse_ref.shape)


def _band_attention(p3, group, batch):
    slabs, rows = BLOCK_SHAPES[group]
    streams = SLABS // slabs
    nb = SLAB_ROWS // rows
    has_prev = nb > 1
    cq, ck, cv = (COL_ATT + 3 * group + i for i in range(3))
    coefs = [float(s) * DIL_RATES[group] for s in _alibi_slopes()[group]]
    steps = _block_steps(group)
    back = jnp.asarray((steps[:, None] - steps[None, :]).astype(np.float32))

    def cur(col, width=D_MODEL):
        return pl.BlockSpec((slabs, rows, width), lambda b, r, j: (b * streams + r, j, col))

    def prev(col):
        return pl.BlockSpec((slabs, rows, D_MODEL), lambda b, r, j: (b * streams + r, jnp.maximum(j - 1, 0), col))

    in_specs = [pl.BlockSpec((ATT_BLOCK, ATT_BLOCK), lambda b, r, j: (0, 0)), cur(cq)]
    in_specs += [prev(ck), cur(ck), prev(cv), cur(cv)] if has_prev else [cur(ck), cur(cv)]
    args = (back,) + (p3,) * (len(in_specs) - 1)
    return pl.pallas_call(
        functools.partial(_band_attn_kernel, coefs=coefs, has_prev=has_prev),
        grid=(batch, streams, nb),
        in_specs=in_specs,
        out_specs=[cur(0), cur(0, 128)],
        out_shape=[jax.ShapeDtypeStruct((p3.shape[0], SLAB_ROWS, D_MODEL), F32),
                   jax.ShapeDtypeStruct((p3.shape[0], SLAB_ROWS, 128), F32)],
        compiler_params=_params("parallel", "parallel", "arbitrary"),
    )(*args)


def _cached_attn_kernel(new_ref, c_ref, coef_ref, o_ref, lse_ref, *, shared, dec_seq):
    lane = lax.broadcasted_iota(jnp.int32, (ATT_HEADS, ATT_BLOCK), 1)
    coef = coef_ref[...]
    scale = HEAD_DIM ** -0.5
    qs = [new_ref[i, 0] for i in range(dec_seq)]
    res = lambda i: 0 if shared else i

    def qk_body(r, carry):
        out = []
        for i in range(dec_seq):
            col = jnp.sum(qs[i] * c_ref[r, res(i), 0], axis=-1, keepdims=True)
            out.append(jnp.where(lane == r, col, carry[i]))
        return tuple(out)

    zero = jnp.zeros((ATT_HEADS, ATT_BLOCK), F32)
    s_all = lax.fori_loop(0, ATT_BLOCK, qk_body, (zero,) * dec_seq, unroll=4)

    ps, p_news, ls, ms = [], [], [], []
    for i in range(dec_seq):
        if shared:
            back = (ATT_BLOCK + i - lane).astype(F32)
            s = jnp.where(lane >= i, s_all[i] * scale - coef * back, MASK_VALUE)
            s_new = [jnp.sum(qs[i] * new_ref[i2, 1], axis=-1, keepdims=True) * scale - coef[:, :1] * float(i - i2)
                     for i2 in range(i + 1)]
        else:
            s = s_all[i] * scale - coef * (ATT_BLOCK - lane).astype(F32)
            s_new = [jnp.sum(qs[i] * new_ref[i, 1], axis=-1, keepdims=True) * scale]
        m = jnp.max(s, axis=-1, keepdims=True)
        for sn in s_new:
            m = jnp.maximum(m, sn)
        p = jnp.exp(s - m)
        p_new = [jnp.exp(sn - m) for sn in s_new]
        l = jnp.sum(p, axis=-1, keepdims=True)
        for pn in p_new:
            l = l + pn
        ps.append(p)
        p_news.append(p_new)
        ls.append(l)
        ms.append(m)

    def pv_body(r, carry):
        out = []
        for i in range(dec_seq):
            col = jnp.sum(jnp.where(lane == r, ps[i], 0.0), axis=-1, keepdims=True)
            out.append(carry[i] + col * c_ref[r, res(i), 1])
        return tuple(out)

    zero_o = jnp.zeros((ATT_HEADS, HEAD_DIM), F32)
    o_all = lax.fori_loop(0, ATT_BLOCK, pv_body, (zero_o,) * dec_seq, unroll=4)
    for i in range(dec_seq):
        o = o_all[i]
        for i2, pn in enumerate(p_news[i]):
            o = o + pn * new_ref[i2 if shared else i, 2]
        o_ref[i] = o / ls[i]
        lse_ref[i] = jnp.broadcast_to(ms[i] + jnp.log(ls[i]), (ATT_HEADS, 128))


def _cached_attention(att_s, cache, group):
    db, dec_seq = att_s.shape[:2]
    dil = DIL_RATES[group]
    window = DIL_WINDOWS[group]
    assert cache.shape[1] == window and window == dil * ATT_BLOCK
    assert dil == 1 or dec_seq <= dil
    shared = dil == 1
    nres = 1 if shared else dec_seq
    cv = cache.reshape(db, ATT_BLOCK, dil, 2, ATT_HEADS, HEAD_DIM)
    coef = jnp.asarray(np.repeat((_alibi_slopes()[group] * dil)[:, None], 128, axis=1).astype(np.float32))
    out = lambda w: pl.BlockSpec((None, dec_seq, ATT_HEADS, w), lambda b: (b, 0, 0, 0))
    return pl.pallas_call(
        functools.partial(_cached_attn_kernel, shared=shared, dec_seq=dec_seq),
        grid=(db,),
        in_specs=[pl.BlockSpec((None, dec_seq, None, 3, ATT_HEADS, HEAD_DIM), lambda b: (b, 0, group, 0, 0, 0)),
                  pl.BlockSpec((None, ATT_BLOCK, nres, 2, ATT_HEADS, HEAD_DIM), lambda b: (b, 0, 0, 0, 0, 0)),
                  pl.BlockSpec((ATT_HEADS, 128), lambda b: (0, 0))],
        out_specs=[out(HEAD_DIM), out(128)],
        out_shape=[jax.ShapeDtypeStruct((db, dec_seq, ATT_HEADS, HEAD_DIM), F32),
                   jax.ShapeDtypeStruct((db, dec_seq, ATT_HEADS, 128), F32)],
        compiler_params=_params("parallel"),
    )(att_s, cv, coef)


def _merge_kernel(a_ref, ga_ref, gb_ref, o0_ref, o1_ref, o2_ref, l0_ref, l1_ref, l2_ref,
                  x_ref, w_ref, g_ref, h_ref, hn_ref, merged_ref):
    l0, l1, l2 = l0_ref[...], l1_ref[...], l2_ref[...]
    mx = jnp.maximum(jnp.maximum(l0, l1), l2)
    e0, e1, e2 = jnp.exp(l0 - mx), jnp.exp(l1 - mx), jnp.exp(l2 - mx)
    den = e0 + e1 + e2
    w0, w1, w2 = e0 / den, e1 / den, e2 / den
    for h in range(ATT_HEADS):
        hs = slice(h * HEAD_DIM, (h + 1) * HEAD_DIM)
        b_out = (w0[:, h:h + 1] * o0_ref[:, hs] + w1[:, h:h + 1] * o1_ref[:, hs] + w2[:, h:h + 1] * o2_ref[:, hs])
        merged = ga_ref[:, hs] * a_ref[:, hs] + gb_ref[:, hs] * b_out
        merged_ref[:, hs] = merged.astype(merged_ref.dtype)
    h_new = x_ref[...] + jnp.dot(merged_ref[...], w_ref[...], preferred_element_type=F32)
    h_ref[...] = h_new
    hn_ref[...] = _rms(h_new, g_ref[...]).astype(hn_ref.dtype)


def _merge(a_out, p, outs, lses, x, w_out, norm_g, t, tm):
    row = lambda col: pl.BlockSpec((tm, D_MODEL), lambda i: (i, col))
    lrow = pl.BlockSpec((tm, 128), lambda i: (i, 0))
    return pl.pallas_call(
        _merge_kernel,
        grid=(t // tm,),
        in_specs=[row(0), row(COL_GATE), row(COL_GATE + 1), row(0), row(0), row(0), lrow, lrow, lrow,
                  row(0), pl.BlockSpec((D_MODEL, D_MODEL), lambda i: (0, 0)),
                  pl.BlockSpec((1, D_MODEL), lambda i: (0, 0))],
        out_specs=[row(0), row(0)],
        out_shape=[jax.ShapeDtypeStruct((t, D_MODEL), F32), jax.ShapeDtypeStruct((t, D_MODEL), BF16)],
        scratch_shapes=[pltpu.VMEM((tm, D_MODEL), BF16)],
        compiler_params=_params("parallel"),
    )(a_out, p, p, outs[0], outs[1], outs[2], lses[0], lses[1], lses[2], x, w_out, norm_g.reshape(1, D_MODEL))


def _peer_score_kernel(hn_ref, wq_ref, keys_ref, s_ref):
    q = jnp.dot(hn_ref[...], wq_ref[...], preferred_element_type=F32).astype(BF16)
    nt = (((1,), (1,)), ((), ()))
    for hc in range(2 * PEER_HEADS):
        cs = slice(hc * PEER_HALF, (hc + 1) * PEER_HALF)
        s_ref[hc] = lax.dot_general(keys_ref[hc % 2], q[:, cs], nt, preferred_element_type=F32)


def _peer_scores(hn, w_q, sub_keys, tm):
    t = hn.shape[0]
    return pl.pallas_call(
        _peer_score_kernel,
        grid=(t // tm,),
        in_specs=[pl.BlockSpec((tm, D_MODEL), lambda i: (i, 0)),
                  pl.BlockSpec((D_MODEL, 2 * PEER_HEADS * PEER_HALF), lambda i: (0, 0)),
                  pl.BlockSpec((2, N_KEYS, PEER_HALF), lambda i: (0, 0, 0))],
        out_specs=pl.BlockSpec((2 * PEER_HEADS, N_KEYS, tm), lambda i: (0, 0, i)),
        out_shape=jax.ShapeDtypeStruct((2 * PEER_HEADS, N_KEYS, t), F32),
        compiler_params=_params("parallel"),
    )(hn, w_q, sub_keys)


def _take_top(s, count):
    rows = lax.broadcasted_iota(jnp.int32, s.shape, 0).astype(F32)
    vals, idxs = [], []
    for _ in range(count):
        m = jnp.max(s, axis=0, keepdims=True)
        pos = jnp.min(jnp.where(s == m, rows, float(s.shape[0])), axis=0, keepdims=True)
        vals.append(m)
        idxs.append(pos)
        s = jnp.where(rows == pos, NEG_INF, s)
    return vals, idxs


def _stack_rows(rows_list, krow):
    out = jnp.zeros(krow.shape, F32)
    for j, r in enumerate(rows_list):
        out = jnp.where(krow == float(j), r, out)
    return out


def _route_kernel(s_ref, e1_ref, e2_ref, gate_ref):
    k = PEER_TOPK
    v1, i1 = _take_top(s_ref[0], k)
    v2, i2 = _take_top(s_ref[1], k)
    krow = lax.broadcasted_iota(jnp.int32, (k, s_ref.shape[2]), 0).astype(F32)
    v2_all = _stack_rows(v2, krow)
    i1_all = _stack_rows(i1, krow)
    i2_all = _stack_rows(i2, krow)
    cand = jnp.concatenate([v1[a] + v2_all for a in range(k)], axis=0)
    top_s, pos = _take_top(cand, k)
    e1, e2 = [], []
    for j in range(k):
        a = jnp.floor(pos[j] * (1.0 / k))
        b = pos[j] - a * k
        e1.append(jnp.sum(jnp.where(krow == a, i1_all, 0.0), axis=0, keepdims=True))
        e2.append(jnp.sum(jnp.where(krow == b, i2_all, 0.0), axis=0, keepdims=True))
    ex = jnp.exp(_stack_rows(top_s, krow) - top_s[0])
    gate_ref[...] = ex / jnp.sum(ex, axis=0, keepdims=True)
    e1_ref[...] = _stack_rows(e1, krow)
    e2_ref[...] = _stack_rows(e2, krow)


def _route(scores_t, tl):
    t = scores_t.shape[2]
    kk = PEER_HEADS * PEER_TOPK
    out = pl.BlockSpec((PEER_TOPK, tl), lambda i, h: (h, i))
    return pl.pallas_call(
        _route_kernel,
        grid=(t // tl, PEER_HEADS),
        in_specs=[pl.BlockSpec((2, N_KEYS, tl), lambda i, h: (h, 0, i))],
        out_specs=[out, out, out],
        out_shape=[jax.ShapeDtypeStruct((kk, t), F32)] * 3,
        compiler_params=_params("parallel", "parallel"),
    )(scores_t)


def _expert_weight_kernel(e1_ref, e2_ref, gate_ref, g_ref, e1t_ref, e2t_ref, gt_ref):
    e1t_ref[...] = e1_ref[...].T
    e2t_ref[...] = e2_ref[...].T
    gt_ref[...] = gate_ref[...].T
    kk = e1_ref.shape[0]
    key = lax.broadcasted_iota(jnp.int32, (N_KEYS, kk), 0).astype(F32)
    nt = (((1,), (1,)), ((), ()))

    def body(t, carry):
        r1 = e1t_ref[pl.ds(t, 1), :]
        r2 = e2t_ref[pl.ds(t, 1), :]
        gr = gt_ref[pl.ds(t, 1), :]
        a_t = jnp.where(key == r1, 1.0, 0.0).astype(BF16)
        b_t = jnp.where(key == r2, gr, 0.0).astype(BF16)
        g_ref[t] = lax.dot_general(a_t, b_t, nt, preferred_element_type=F32).astype(g_ref.dtype)
        return carry

    lax.fori_loop(0, g_ref.shape[0], body, 0, unroll=8)


def _expert_weights(e1, e2, gate, tl):
    kk, t = e1.shape
    slot = pl.BlockSpec((kk, tl), lambda i: (0, i))
    return pl.pallas_call(
        _expert_weight_kernel,
        grid=(t // tl,),
        in_specs=[slot, slot, slot],
        out_specs=pl.BlockSpec((tl, N_KEYS, N_KEYS), lambda i: (i, 0, 0)),
        out_shape=jax.ShapeDtypeStruct((t, N_KEYS, N_KEYS), BF16),
        scratch_shapes=[pltpu.VMEM((tl, kk), F32)] * 3,
        compiler_params=_params("parallel"),
    )(e1, e2, gate)


def _peer_kernel(hn_ref, ut_ref, v_ref, g_ref, h_ref, gf_ref, y_ref, acc_ref):
    e = pl.program_id(1)

    @pl.when(e == 0)
    def _():
        acc_ref[...] = jnp.zeros_like(acc_ref)

    hk = jnp.dot(hn_ref[...], ut_ref[...], preferred_element_type=F32)
    act = (jax.nn.gelu(hk) * g_ref[...].astype(F32)).astype(BF16)
    acc_ref[...] += jnp.dot(act, v_ref[...], preferred_element_type=F32)

    @pl.when(e == pl.num_programs(1) - 1)
    def _():
        y_ref[...] = _rms(h_ref[...] + acc_ref[...], gf_ref[...])


def _peer(hn, u_t, v, g, h, norm_g, tm, te):
    t = hn.shape[0]
    row = pl.BlockSpec((tm, D_MODEL), lambda i, e: (i, 0))
    return pl.pallas_call(
        _peer_kernel,
        grid=(t // tm, N_EXPERTS // te),
        in_specs=[row,
                  pl.BlockSpec((D_MODEL, te), lambda i, e: (0, e)),
                  pl.BlockSpec((te, D_MODEL), lambda i, e: (e, 0)),
                  pl.BlockSpec((tm, te), lambda i, e: (i, e)),
                  row,
                  pl.BlockSpec((1, D_MODEL), lambda i, e: (0, 0))],
        out_specs=row,
        out_shape=jax.ShapeDtypeStruct((t, D_MODEL), F32),
        scratch_shapes=[pltpu.VMEM((tm, D_MODEL), F32)],
        compiler_params=_params("parallel", "arbitrary"),
    )(hn, u_t, v, g, h, norm_g.reshape(1, D_MODEL))


def _row_tile(t, candidates):
    for c in candidates:
        if t % c == 0:
            return c
    raise ValueError(f"no row tile for {t} tokens")


def _to_positions(a, batch):
    a = a.reshape((batch, 4, 4, SLAB_ROWS) + a.shape[1:])
    return jnp.swapaxes(a, 1, 3)


def kernel(x_prompt, x_sample, cache_kv_w128, cache_kv_w512, cache_kv_w2048, norm_mix_g, w_in, sgu_norm_g, sgu_w, sgu_b, w_out, norm_ffn_g, peer_w_q, peer_sub_keys, peer_u, peer_v, norm_final_g):
    batch, seq, _ = x_prompt.shape
    db, ds, _ = x_sample.shape
    assert w_in.shape[0] == 1 and db * ds == CHUNK and seq == SLABS * SLAB_ROWS
    caches = (cache_kv_w128, cache_kv_w512, cache_kv_w2048)
    tp, ts = batch * seq, db * ds
    t = tp + ts
    n_slabs = t // SLAB_ROWS
    tm_big = _row_tile(t, (1040, 640, 128))
    tm_lane = _row_tile(t, (640, 128))

    xs = x_sample.reshape(ts, D_MODEL)
    xl, xn = _permute_norm(x_prompt, norm_mix_g[0])
    xl = lax.dynamic_update_slice(xl, xs, (tp, 0))
    xn = lax.dynamic_update_slice(xn, _rmsnorm_bf16(xs, norm_mix_g[0]), (tp, 0))

    p = _inproj(xn, w_in[0].astype(BF16), t, tm_big, 1024)
    p3 = p.reshape(n_slabs, SLAB_ROWS, N_COL_BLOCKS * D_MODEL)

    w_tril = sgu_w[0] * jnp.tril(jnp.ones((CHUNK, CHUNK), F32))
    tau = _block_steps(0)
    w_p = w_tril[:, tau][:, :, tau].astype(BF16)
    b_p = sgu_b[0][:, tau][..., None]
    w_s = jnp.einsum("bc,gis->gbics", jnp.eye(db, dtype=F32), w_tril[:, :ds, :ds]).reshape(SGU_GROUPS, ts, ts)
    b_s = jnp.tile(sgu_b[0][:, :ds], (1, db))[..., None]
    a_out = _sgu_prompt(p3, sgu_norm_g[0], w_p, b_p, batch).reshape(t, D_MODEL)
    a_s, vn_s = _sgu_sample(p3, sgu_norm_g[0], w_s.astype(BF16), b_s, tp // SLAB_ROWS)
    a_out = lax.dynamic_update_slice(a_out, a_s, (tp, 0))

    att_s = p[tp:, COL_ATT * D_MODEL:COL_GATE * D_MODEL].reshape(db, ds, N_DIL, 3, ATT_HEADS, HEAD_DIM)
    outs, lses = [], []
    for g in range(N_DIL):
        o_p, lse_p = _band_attention(p3, g, batch)
        o_s, lse_s = _cached_attention(att_s, caches[g][0], g)
        lse_s = jnp.pad(lse_s[..., 0].reshape(ts, ATT_HEADS), ((0, 0), (0, 128 - ATT_HEADS)))
        outs.append(lax.dynamic_update_slice(o_p.reshape(t, D_MODEL), o_s.reshape(ts, D_MODEL), (tp, 0)))
        lses.append(lax.dynamic_update_slice(lse_p.reshape(t, 128), lse_s, (tp, 0)))

    h, hn = _merge(a_out, p, outs, lses, xl, w_out[0].astype(BF16), norm_ffn_g[0], t, 128)

    scores_t = _peer_scores(hn, peer_w_q[0].astype(BF16), peer_sub_keys[0].astype(BF16), tm_lane)
    e1, e2, gate = _route(scores_t, 128)
    g_dense = _expert_weights(e1, e2, gate, 128).reshape(t, N_EXPERTS)
    y = _peer(hn, peer_u[0].T.astype(BF16), peer_v[0].astype(BF16), g_dense, h, norm_final_g, tm_lane, 512)

    y_prompt = _to_positions(y[:tp], batch).reshape(batch, seq, D_MODEL)
    y_sample = y[tp:].reshape(db, ds, D_MODEL)
    kv_prompt, kv_sample = [], []
    for g in range(N_DIL):
        ck = (COL_ATT + 3 * g + 1) * D_MODEL
        lw = min(DIL_WINDOWS[g], seq)
        kv = _to_positions(p[:tp, ck:ck + 2 * D_MODEL], batch)[:, (seq - lw) // SLABS:]
        kv_prompt.append(kv.reshape(1, batch, lw, 2, ATT_HEADS, HEAD_DIM))
        kv_sample.append(att_s[:, :, g, 1:3][None])
    sgu_v_sample = vn_s.reshape(1, db, ds, D_MODEL)
    return (y_prompt, y_sample, kv_prompt[0], kv_prompt[1], kv_prompt[2],
            kv_sample[0], kv_sample[1], kv_sample[2], sgu_v_sample)
```

```python
import functools

import numpy as np
import jax
import jax.numpy as jnp
from jax import lax
from jax.experimental import pallas as pl
from jax.experimental.pallas import tpu as pltpu

F32 = jnp.float32
BF16 = jnp.bfloat16

D_MODEL = 2048
HEAD_DIM = 128
ATT_HEADS = D_MODEL // HEAD_DIM
N_DIL = 3
DIL_WINDOWS = (128, 512, 2048)
DIL_RATES = (1, 4, 16)
ATT_BLOCK = 128
SLABS = DIL_RATES[-1]
SLAB_ROWS = 128
SGU_GROUPS = 8
SGU_GROUP_DIM = D_MODEL // SGU_GROUPS
CHUNK = 128
N_COL_BLOCKS = 13
COL_U, COL_V, COL_ATT, COL_GATE = 0, 1, 2, 11
PEER_HEADS = 8
PEER_TOPK = 16
N_KEYS = 128
N_EXPERTS = N_KEYS * N_KEYS
PEER_HALF = 128
NORM_EPS = 1e-6
MASK_VALUE = -1e30
NEG_INF = float("-inf")
VMEM_LIMIT = 56 * 1024 * 1024

BLOCK_SHAPES = ((16, 8), (4, 32), (1, 128))


def _alibi_slopes():
    n = N_DIL * ATT_HEADS
    e = np.arange(1, n + 1, dtype=np.float32)
    return np.exp2(np.float32(-8.0) * e / np.float32(n)).astype(np.float32).reshape(N_DIL, ATT_HEADS)


def _block_steps(group):
    slabs, rows = BLOCK_SHAPES[group]
    n = np.arange(slabs * rows)
    s, j = n // rows, n % rows
    if group == 0:
        return j * 16 + (s % 4) * 4 + s // 4
    if group == 1:
        return j * 4 + s
    return j


def _params(*sem):
    return pltpu.CompilerParams(dimension_semantics=sem, vmem_limit_bytes=VMEM_LIMIT)


def _rms(x, g):
    return x * lax.rsqrt(jnp.mean(x * x, axis=-1, keepdims=True) + NORM_EPS) * g


def _permute_norm_kernel(x_ref, g_ref, xl_ref, xn_ref):
    for rr in range(8):
        x = x_ref[:, rr, :]
        xl_ref[rr % 4, rr // 4] = x
        xn_ref[rr % 4, rr // 4] = _rms(x, g_ref[...]).astype(xn_ref.dtype)


def _permute_norm(x_prompt, g):
    batch, seq, _ = x_prompt.shape
    x4 = x_prompt.reshape(batch, SLAB_ROWS, SLABS, D_MODEL)
    out = pl.BlockSpec((None, 4, 2, SLAB_ROWS, D_MODEL), lambda b, h: (b, 0, h, 0, 0))
    xl, xn = pl.pallas_call(
        _permute_norm_kernel,
        grid=(batch, 2),
        in_specs=[pl.BlockSpec((None, SLAB_ROWS, 8, D_MODEL), lambda b, h: (b, 0, h, 0)),
                  pl.BlockSpec((1, D_MODEL), lambda b, h: (0, 0))],
        out_specs=[out, out],
        out_shape=[jax.ShapeDtypeStruct((batch + 1, 4, 4, SLAB_ROWS, D_MODEL), F32),
                   jax.ShapeDtypeStruct((batch + 1, 4, 4, SLAB_ROWS, D_MODEL), BF16)],
        compiler_params=_params("parallel", "parallel"),
    )(x4, g.reshape(1, D_MODEL))
    rows = (batch + 1) * seq
    return xl.reshape(rows, D_MODEL), xn.reshape(rows, D_MODEL)


def _rmsnorm_kernel(x_ref, g_ref, o_ref):
    o_ref[...] = _rms(x_ref[...], g_ref[...]).astype(o_ref.dtype)


def _rmsnorm_bf16(x, g):
    return pl.pallas_call(
        _rmsnorm_kernel,
        out_shape=jax.ShapeDtypeStruct(x.shape, BF16),
    )(x, g.reshape(1, D_MODEL))


def _inproj_kernel(x_ref, w_ref, o_ref, *, blocks_per_col):
    cb = pl.program_id(1) // blocks_per_col
    p = jnp.dot(x_ref[...], w_ref[...], preferred_element_type=F32)

    @pl.when(cb < COL_ATT)
    def _():
        o_ref[...] = jax.nn.gelu(p)

    @pl.when((cb >= COL_ATT) & (cb < COL_GATE))
    def _():
        o_ref[...] = p

    @pl.when(cb >= COL_GATE)
    def _():
        o_ref[...] = jax.nn.sigmoid(p)


def _inproj(xn, w, t, tm, tn):
    n = w.shape[1]
    return pl.pallas_call(
        functools.partial(_inproj_kernel, blocks_per_col=D_MODEL // tn),
        grid=(t // tm, n // tn),
        in_specs=[pl.BlockSpec((tm, D_MODEL), lambda i, j: (i, 0)),
                  pl.BlockSpec((D_MODEL, tn), lambda i, j: (0, j))],
        out_specs=pl.BlockSpec((tm, tn), lambda i, j: (i, j)),
        out_shape=jax.ShapeDtypeStruct((t, n), F32),
        compiler_params=_params("parallel", "arbitrary"),
    )(xn, w)


def _sgu_kernel(u_ref, v_ref, g_ref, w_ref, b_ref, a_ref, vn_ref):
    v = v_ref[...].reshape(CHUNK, D_MODEL)
    u = u_ref[...].reshape(CHUNK, D_MODEL)
    vc = v - jnp.mean(v, axis=-1, keepdims=True)
    vn = vc * lax.rsqrt(jnp.mean(vc * vc, axis=-1, keepdims=True) + NORM_EPS) * g_ref[...]
    if vn_ref is not None:
        vn_ref[...] = vn
    cols = []
    for g in range(SGU_GROUPS):
        cs = slice(g * SGU_GROUP_DIM, (g + 1) * SGU_GROUP_DIM)
        mix = jnp.dot(w_ref[g], vn[:, cs].astype(BF16), preferred_element_type=F32) + b_ref[g]
        cols.append(u[:, cs] * mix)
    a_ref[...] = jnp.concatenate(cols, axis=-1).reshape(a_ref.shape)


def _sgu_prompt(p3, sgu_norm_g, w, b, batch):
    slabs, rows = BLOCK_SHAPES[0]
    n_chunks = SLAB_ROWS // rows
    blk = lambda col: pl.BlockSpec((slabs, rows, D_MODEL), lambda bb, c: (bb, c, col))
    return pl.pallas_call(
        lambda u, v, g, ww, bb, a: _sgu_kernel(u, v, g, ww, bb, a, None),
        grid=(batch, n_chunks),
        in_specs=[blk(COL_U), blk(COL_V),
                  pl.BlockSpec((1, D_MODEL), lambda bb, c: (0, 0)),
                  pl.BlockSpec((SGU_GROUPS, CHUNK, CHUNK), lambda bb, c: (0, 0, 0)),
                  pl.BlockSpec((SGU_GROUPS, CHUNK, 1), lambda bb, c: (0, 0, 0))],
        out_specs=blk(0),
        out_shape=jax.ShapeDtypeStruct((p3.shape[0], SLAB_ROWS, D_MODEL), F32),
        compiler_params=_params("parallel", "parallel"),
    )(p3, p3, sgu_norm_g.reshape(1, D_MODEL), w, b)


def _sgu_sample(p3, sgu_norm_g, w, b, slab):
    blk = lambda col: pl.BlockSpec((1, SLAB_ROWS, D_MODEL), lambda i: (slab, 0, col))
    return pl.pallas_call(
        _sgu_kernel,
        grid=(1,),
        in_specs=[blk(COL_U), blk(COL_V),
                  pl.BlockSpec((1, D_MODEL), lambda i: (0, 0)),
                  pl.BlockSpec((SGU_GROUPS, CHUNK, CHUNK), lambda i: (0, 0, 0)),
                  pl.BlockSpec((SGU_GROUPS, CHUNK, 1), lambda i: (0, 0, 0))],
        out_specs=[pl.BlockSpec((CHUNK, D_MODEL), lambda i: (0, 0)),
                   pl.BlockSpec((CHUNK, D_MODEL), lambda i: (0, 0))],
        out_shape=[jax.ShapeDtypeStruct((CHUNK, D_MODEL), F32), jax.ShapeDtypeStruct((CHUNK, D_MODEL), F32)],
        compiler_params=_params("arbitrary"),
    )(p3, p3, sgu_norm_g.reshape(1, D_MODEL), w, b)


def _band_attn_kernel(*refs, has_prev):
    if has_prev:
        bc_ref, bp_ref, q_ref, kp_ref, kc_ref, vp_ref, vc_ref, o_ref, lse_ref = refs
    else:
        bc_ref, q_ref, kc_ref, vc_ref, o_ref, lse_ref = refs
    n = ATT_BLOCK
    scale = HEAD_DIM ** -0.5
    nt = (((1,), (1,)), ((), ()))
    heads = range(ATT_HEADS)
    hs = [slice(h * HEAD_DIM, (h + 1) * HEAD_DIM) for h in heads]
    ld = lambda ref, h: ref[:, :, hs[h]].reshape(n, HEAD_DIM)
    q = [(ld(q_ref, h) * scale).astype(BF16) for h in heads]
    s_c = [lax.dot_general(q[h], ld(kc_ref, h).astype(BF16), nt, preferred_element_type=F32) + bc_ref[h]
           for h in heads]
    m = [jnp.max(s_c[h], axis=-1, keepdims=True) for h in heads]
    if has_prev:
        s_p = [lax.dot_general(q[h], ld(kp_ref, h).astype(BF16), nt, preferred_element_type=F32) + bp_ref[h]
               for h in heads]
        m = [jnp.maximum(m[h], jnp.max(s_p[h], axis=-1, keepdims=True)) for h in heads]
    p_c = [jnp.exp(s_c[h] - m[h]) for h in heads]
    l = [jnp.sum(p_c[h], axis=-1, keepdims=True) for h in heads]
    o = [jnp.dot(p_c[h].astype(BF16), ld(vc_ref, h).astype(BF16), preferred_element_type=F32) for h in heads]
    if has_prev:
        p_p = [jnp.exp(s_p[h] - m[h]) for h in heads]
        l = [l[h] + jnp.sum(p_p[h], axis=-1, keepdims=True) for h in heads]
        o = [o[h] + jnp.dot(p_p[h].astype(BF16), ld(vp_ref, h).astype(BF16), preferred_element_type=F32)
             for h in heads]
    lane = lax.broadcasted_iota(jnp.int32, (n, 128), 1)
    lse = jnp.zeros((n, 128), F32)
    for h in heads:
        lse = jnp.where(lane == h, m[h] + jnp.log(l[h]), lse)
    o_ref[...] = jnp.concatenate([o[h] / l[h] for h in heads], axis=-1).reshape(o_ref.shape)
    lse_ref[...] = lse.reshape(lse_ref.shape)


def _band_bias(group):
    steps = _block_steps(group)
    back = (steps[:, None] - steps[None, :]).astype(np.float32)
    coef = (_alibi_slopes()[group] * np.float32(DIL_RATES[group]))[:, None, None]
    cur = np.where(back >= 0, -(coef * back), np.float32(MASK_VALUE)).astype(np.float32)
    back_p = back + np.float32(ATT_BLOCK)
    prev = np.where(back_p <= ATT_BLOCK, -(coef * back_p), np.float32(MASK_VALUE)).astype(np.float32)
    return cur, np.stack([np.full_like(prev, MASK_VALUE), prev])


def _band_attention(p3, group, batch):
    slabs, rows = BLOCK_SHAPES[group]
    streams = SLABS // slabs
    nb = SLAB_ROWS // rows
    has_prev = nb > 1
    cq, ck, cv = (COL_ATT + 3 * group + i for i in range(3))
    bias_c, bias_p = _band_bias(group)

    def cur(col, width=D_MODEL):
        return pl.BlockSpec((slabs, rows, width), lambda b, r, j: (b * streams + r, j, col))

    def prev(col):
        return pl.BlockSpec((slabs, rows, D_MODEL), lambda b, r, j: (b * streams + r, jnp.maximum(j - 1, 0), col))

    table = (ATT_HEADS, ATT_BLOCK, ATT_BLOCK)
    in_specs = [pl.BlockSpec(table, lambda b, r, j: (0, 0, 0))]
    args = [jnp.asarray(bias_c)]
    if has_prev:
        in_specs += [pl.BlockSpec((None,) + table, lambda b, r, j: (jnp.minimum(j, 1), 0, 0, 0)),
                     cur(cq), prev(ck), cur(ck), prev(cv), cur(cv)]
        args += [jnp.asarray(bias_p)]
    else:
        in_specs += [cur(cq), cur(ck), cur(cv)]
    args += [p3] * (len(in_specs) - len(args))
    return pl.pallas_call(
        functools.partial(_band_attn_kernel, has_prev=has_prev),
        grid=(batch, streams, nb),
        in_specs=in_specs,
        out_specs=[cur(0), cur(0, 128)],
        out_shape=[jax.ShapeDtypeStruct((p3.shape[0], SLAB_ROWS, D_MODEL), F32),
                   jax.ShapeDtypeStruct((p3.shape[0], SLAB_ROWS, 128), F32)],
        compiler_params=_params("parallel", "parallel", "arbitrary"),
    )(*args)


def _cached_attn_kernel(new_ref, c_ref, coef_ref, o_ref, lse_ref, *, shared, dec_seq):
    lane = lax.broadcasted_iota(jnp.int32, (ATT_HEADS, ATT_BLOCK), 1)
    coef = coef_ref[...]
    scale = HEAD_DIM ** -0.5
    qs = [new_ref[i, 0] for i in range(dec_seq)]
    res = lambda i: 0 if shared else i

    def qk_body(r, carry):
        out = []
        for i in range(dec_seq):
            col = jnp.sum(qs[i] * c_ref[r, res(i), 0], axis=-1, keepdims=True)
            out.append(jnp.where(lane == r, col, carry[i]))
        return tuple(out)

    zero = jnp.zeros((ATT_HEADS, ATT_BLOCK), F32)
    s_all = lax.fori_loop(0, ATT_BLOCK, qk_body, (zero,) * dec_seq, unroll=32)

    ps, p_news, ls, ms = [], [], [], []
    for i in range(dec_seq):
        if shared:
            back = (ATT_BLOCK + i - lane).astype(F32)
            s = jnp.where(lane >= i, s_all[i] * scale - coef * back, MASK_VALUE)
            s_new = [jnp.sum(qs[i] * new_ref[i2, 1], axis=-1, keepdims=True) * scale - coef[:, :1] * float(i - i2)
                     for i2 in range(i + 1)]
        else:
            s = s_all[i] * scale - coef * (ATT_BLOCK - lane).astype(F32)
            s_new = [jnp.sum(qs[i] * new_ref[i, 1], axis=-1, keepdims=True) * scale]
        m = jnp.max(s, axis=-1, keepdims=True)
        for sn in s_new:
            m = jnp.maximum(m, sn)
        p = jnp.exp(s - m)
        p_new = [jnp.exp(sn - m) for sn in s_new]
        l = jnp.sum(p, axis=-1, keepdims=True)
        for pn in p_new:
            l = l + pn
        ps.append(p)
        p_news.append(p_new)
        ls.append(l)
        ms.append(m)

    def pv_body(r, carry):
        out = []
        for i in range(dec_seq):
            col = jnp.sum(jnp.where(lane == r, ps[i], 0.0), axis=-1, keepdims=True)
            out.append(carry[i] + col * c_ref[r, res(i), 1])
        return tuple(out)

    zero_o = jnp.zeros((ATT_HEADS, HEAD_DIM), F32)
    o_all = lax.fori_loop(0, ATT_BLOCK, pv_body, (zero_o,) * dec_seq, unroll=32)
    for i in range(dec_seq):
        o = o_all[i]
        for i2, pn in enumerate(p_news[i]):
            o = o + pn * new_ref[i2 if shared else i, 2]
        o_ref[i] = o / ls[i]
        lse_ref[i] = jnp.broadcast_to(ms[i] + jnp.log(ls[i]), (ATT_HEADS, 128))


def _cached_attention(att_s, cache, group):
    db, dec_seq = att_s.shape[:2]
    dil = DIL_RATES[group]
    window = DIL_WINDOWS[group]
    assert cache.shape[1] == window and window == dil * ATT_BLOCK
    assert dil == 1 or dec_seq <= dil
    shared = dil == 1
    nres = 1 if shared else dec_seq
    cv = cache.reshape(db, ATT_BLOCK, dil, 2, ATT_HEADS, HEAD_DIM)
    coef = jnp.asarray(np.repeat((_alibi_slopes()[group] * dil)[:, None], 128, axis=1).astype(np.float32))
    out = lambda w: pl.BlockSpec((None, dec_seq, ATT_HEADS, w), lambda b: (b, 0, 0, 0))
    return pl.pallas_call(
        functools.partial(_cached_attn_kernel, shared=shared, dec_seq=dec_seq),
        grid=(db,),
        in_specs=[pl.BlockSpec((None, dec_seq, None, 3, ATT_HEADS, HEAD_DIM), lambda b: (b, 0, group, 0, 0, 0)),
                  pl.BlockSpec((None, ATT_BLOCK, nres, 2, ATT_HEADS, HEAD_DIM), lambda b: (b, 0, 0, 0, 0, 0)),
                  pl.BlockSpec((ATT_HEADS, 128), lambda b: (0, 0))],
        out_specs=[out(HEAD_DIM), out(128)],
        out_shape=[jax.ShapeDtypeStruct((db, dec_seq, ATT_HEADS, HEAD_DIM), F32),
                   jax.ShapeDtypeStruct((db, dec_seq, ATT_HEADS, 128), F32)],
        compiler_params=_params("parallel"),
    )(att_s, cv, coef)


def _merge_kernel(a_ref, ga_ref, gb_ref, o0_ref, o1_ref, o2_ref, l0_ref, l1_ref, l2_ref,
                  x_ref, w_ref, g_ref, h_ref, hn_ref, merged_ref):
    l0, l1, l2 = l0_ref[...], l1_ref[...], l2_ref[...]
    mx = jnp.maximum(jnp.maximum(l0, l1), l2)
    e0, e1, e2 = jnp.exp(l0 - mx), jnp.exp(l1 - mx), jnp.exp(l2 - mx)
    den = e0 + e1 + e2
    w0, w1, w2 = e0 / den, e1 / den, e2 / den
    for h in range(ATT_HEADS):
        hs = slice(h * HEAD_DIM, (h + 1) * HEAD_DIM)
        b_out = (w0[:, h:h + 1] * o0_ref[:, hs] + w1[:, h:h + 1] * o1_ref[:, hs] + w2[:, h:h + 1] * o2_ref[:, hs])
        merged = ga_ref[:, hs] * a_ref[:, hs] + gb_ref[:, hs] * b_out
        merged_ref[:, hs] = merged.astype(merged_ref.dtype)
    h_new = x_ref[...] + jnp.dot(merged_ref[...], w_ref[...], preferred_element_type=F32)
    h_ref[...] = h_new
    hn_ref[...] = _rms(h_new, g_ref[...]).astype(hn_ref.dtype)


def _merge(a_out, p, outs, lses, x, w_out, norm_g, t, tm):
    row = lambda col: pl.BlockSpec((tm, D_MODEL), lambda i: (i, col))
    lrow = pl.BlockSpec((tm, 128), lambda i: (i, 0))
    return pl.pallas_call(
        _merge_kernel,
        grid=(t // tm,),
        in_specs=[row(0), row(COL_GATE), row(COL_GATE + 1), row(0), row(0), row(0), lrow, lrow, lrow,
                  row(0), pl.BlockSpec((D_MODEL, D_MODEL), lambda i: (0, 0)),
                  pl.BlockSpec((1, D_MODEL), lambda i: (0, 0))],
        out_specs=[row(0), row(0)],
        out_shape=[jax.ShapeDtypeStruct((x.shape[0], D_MODEL), F32), jax.ShapeDtypeStruct((t, D_MODEL), BF16)],
        scratch_shapes=[pltpu.VMEM((tm, D_MODEL), BF16)],
        compiler_params=_params("parallel"),
    )(a_out, p, p, outs[0], outs[1], outs[2], lses[0], lses[1], lses[2], x, w_out, norm_g.reshape(1, D_MODEL))


def _peer_score_kernel(hn_ref, wq_ref, keys_ref, s_ref):
    q = jnp.dot(hn_ref[...], wq_ref[...], preferred_element_type=F32).astype(BF16)
    nt = (((1,), (1,)), ((), ()))
    for hc in range(2 * PEER_HEADS):
        cs = slice(hc * PEER_HALF, (hc + 1) * PEER_HALF)
        s_ref[hc] = lax.dot_general(keys_ref[hc % 2], q[:, cs], nt, preferred_element_type=F32)


def _peer_scores(hn, w_q, sub_keys, tm):
    t = hn.shape[0]
    return pl.pallas_call(
        _peer_score_kernel,
        grid=(t // tm,),
        in_specs=[pl.BlockSpec((tm, D_MODEL), lambda i: (i, 0)),
                  pl.BlockSpec((D_MODEL, 2 * PEER_HEADS * PEER_HALF), lambda i: (0, 0)),
                  pl.BlockSpec((2, N_KEYS, PEER_HALF), lambda i: (0, 0, 0))],
        out_specs=pl.BlockSpec((2 * PEER_HEADS, N_KEYS, tm), lambda i: (0, 0, i)),
        out_shape=jax.ShapeDtypeStruct((2 * PEER_HEADS, N_KEYS, t), F32),
        compiler_params=_params("parallel"),
    )(hn, w_q, sub_keys)


def _take_top(s, count):
    rows = lax.broadcasted_iota(jnp.int32, s.shape, 0).astype(F32)
    vals, idxs = [], []
    for _ in range(count):
        m = jnp.max(s, axis=0, keepdims=True)
        pos = jnp.min(jnp.where(s == m, rows, float(s.shape[0])), axis=0, keepdims=True)
        vals.append(m)
        idxs.append(pos)
        s = jnp.where(rows == pos, NEG_INF, s)
    return vals, idxs


def _stack_rows(rows_list, krow):
    out = jnp.zeros(krow.shape, F32)
    for j, r in enumerate(rows_list):
        out = jnp.where(krow == float(j), r, out)
    return out


def _cand_layout():
    k = PEER_TOPK
    pieces = [("row_a", a, 16 if a == 0 else 8, 0, k // (a + 1)) for a in range(4)]
    pieces += [("col_b", 0, 16, 4, 16), ("col_b", 1, 8, 4, 8), ("col_b", 2, 8, 4, 5)]
    pos = []
    for kind, idx, rows, lo, hi in pieces:
        for r in range(rows):
            a, b = (idx, r) if kind == "row_a" else (r, idx)
            ok = lo <= r < hi and (a + 1) * (b + 1) <= k
            pos.append(a * k + b if ok else k * k)
    assert sorted(p for p in pos if p < k * k) == sorted(
        a * k + b for a in range(k) for b in range(k) if (a + 1) * (b + 1) <= k)
    return pieces, np.asarray(pos, np.float32)


def _route_kernel(s_ref, pos_ref, e1_ref, e2_ref, gate_ref):
    k = PEER_TOPK
    v1, i1 = _take_top(s_ref[0], k)
    v2, i2 = _take_top(s_ref[1], k)
    krow = lax.broadcasted_iota(jnp.int32, (k, s_ref.shape[2]), 0).astype(F32)
    v1_all = _stack_rows(v1, krow)
    v2_all = _stack_rows(v2, krow)
    i1_all = _stack_rows(i1, krow)
    i2_all = _stack_rows(i2, krow)
    parts = []
    for kind, idx, rows, _, _ in _cand_layout()[0]:
        parts.append(v1[idx] + v2_all[:rows] if kind == "row_a" else v1_all[:rows] + v2[idx])
    flat = pos_ref[...]
    cand = jnp.where(flat < float(k * k), jnp.concatenate(parts, axis=0), NEG_INF)
    top_s, pos = [], []
    for _ in range(k):
        m = jnp.max(cand, axis=0, keepdims=True)
        p = jnp.min(jnp.where(cand == m, flat, float(k * k)), axis=0, keepdims=True)
        top_s.append(m)
        pos.append(p)
        cand = jnp.where(flat == p, NEG_INF, cand)
    e1, e2 = [], []
    for j in range(k):
        a = jnp.floor(pos[j] * (1.0 / k))
        b = pos[j] - a * k
        e1.append(jnp.sum(jnp.where(krow == a, i1_all, 0.0), axis=0, keepdims=True))
        e2.append(jnp.sum(jnp.where(krow == b, i2_all, 0.0), axis=0, keepdims=True))
    ex = jnp.exp(_stack_rows(top_s, krow) - top_s[0])
    gate_ref[...] = ex / jnp.sum(ex, axis=0, keepdims=True)
    e1_ref[...] = _stack_rows(e1, krow)
    e2_ref[...] = _stack_rows(e2, krow)


def _route(scores_t, tl):
    t = scores_t.shape[2]
    kk = PEER_HEADS * PEER_TOPK
    out = pl.BlockSpec((PEER_TOPK, tl), lambda i, h: (h, i))
    flat = jnp.asarray(np.repeat(_cand_layout()[1][:, None], tl, axis=1))
    return pl.pallas_call(
        _route_kernel,
        grid=(t // tl, PEER_HEADS),
        in_specs=[pl.BlockSpec((2, N_KEYS, tl), lambda i, h: (h, 0, i)),
                  pl.BlockSpec(flat.shape, lambda i, h: (0, 0))],
        out_specs=[out, out, out],
        out_shape=[jax.ShapeDtypeStruct((kk, t), F32)] * 3,
        compiler_params=_params("parallel", "parallel"),
    )(scores_t, flat)


def _expert_weight_kernel(e1_ref, e2_ref, gate_ref, g_ref, e1t_ref, e2t_ref, gt_ref):
    e1t_ref[...] = e1_ref[...].T
    e2t_ref[...] = e2_ref[...].T
    gt_ref[...] = gate_ref[...].T
    kk = e1_ref.shape[0]
    key = lax.broadcasted_iota(jnp.int32, (N_KEYS, kk), 0).astype(F32)
    nt = (((1,), (1,)), ((), ()))

    def body(t, carry):
        r1 = e1t_ref[pl.ds(t, 1), :]
        r2 = e2t_ref[pl.ds(t, 1), :]
        gr = gt_ref[pl.ds(t, 1), :]
        a_t = jnp.where(key == r1, 1.0, 0.0).astype(BF16)
        b_t = jnp.where(key == r2, gr, 0.0).astype(BF16)
        g_ref[t] = lax.dot_general(a_t, b_t, nt, preferred_element_type=F32).astype(g_ref.dtype)
        return carry

    lax.fori_loop(0, g_ref.shape[0], body, 0, unroll=8)


def _expert_weights(e1, e2, gate, tl):
    kk, t = e1.shape
    slot = pl.BlockSpec((kk, tl), lambda i: (0, i))
    return pl.pallas_call(
        _expert_weight_kernel,
        grid=(t // tl,),
        in_specs=[slot, slot, slot],
        out_specs=pl.BlockSpec((tl, N_KEYS, N_KEYS), lambda i: (i, 0, 0)),
        out_shape=jax.ShapeDtypeStruct((t, N_KEYS, N_KEYS), F32),
        scratch_shapes=[pltpu.VMEM((tl, kk), F32)] * 3,
        compiler_params=_params("parallel"),
    )(e1, e2, gate)


def _peer_kernel(hn_ref, u_ref, v_ref, g_ref, y_ref):
    e = pl.program_id(1)
    nt = (((1,), (1,)), ((), ()))
    hk = lax.dot_general(hn_ref[...], u_ref[...], nt, preferred_element_type=F32)
    act = [(jax.nn.gelu(hk[:, a * N_KEYS:(a + 1) * N_KEYS]) * g_ref[:, a, :]).astype(BF16)
           for a in range(g_ref.shape[1])]
    out = jnp.dot(jnp.concatenate(act, axis=-1), v_ref[...], preferred_element_type=F32)

    @pl.when(e == 0)
    def _():
        y_ref[...] = out

    @pl.when(e > 0)
    def _():
        y_ref[...] += out


def _peer(hn, u, v, g, t, rows_out, tm, te):
    row = pl.BlockSpec((tm, D_MODEL), lambda i, e: (i, 0))
    tab = pl.BlockSpec((te, D_MODEL), lambda i, e: (e, 0))
    return pl.pallas_call(
        _peer_kernel,
        grid=(t // tm, N_EXPERTS // te),
        in_specs=[row, tab, tab, pl.BlockSpec((tm, te // N_KEYS, N_KEYS), lambda i, e: (i, e, 0))],
        out_specs=row,
        out_shape=jax.ShapeDtypeStruct((rows_out, D_MODEL), F32),
        compiler_params=_params("parallel", "arbitrary"),
    )(hn, u, v, g)


def _final_norm_prompt_kernel(h_ref, f_ref, g_ref, y_ref):
    for rr in range(8):
        y_ref[:, rr, :] = _rms(h_ref[rr % 4, rr // 4] + f_ref[rr % 4, rr // 4], g_ref[...])


def _final_norm_prompt(h, f, g, batch, seq):
    rows = 64
    h5 = h.reshape(batch + 1, 4, 4, SLAB_ROWS, D_MODEL)
    f5 = f.reshape(batch + 1, 4, 4, SLAB_ROWS, D_MODEL)
    blk = pl.BlockSpec((None, 4, 2, rows, D_MODEL), lambda b, hh, j: (b, 0, hh, j, 0))
    y = pl.pallas_call(
        _final_norm_prompt_kernel,
        grid=(batch, 2, SLAB_ROWS // rows),
        in_specs=[blk, blk, pl.BlockSpec((1, D_MODEL), lambda b, hh, j: (0, 0))],
        out_specs=pl.BlockSpec((None, rows, 8, D_MODEL), lambda b, hh, j: (b, j, hh, 0)),
        out_shape=jax.ShapeDtypeStruct((batch, SLAB_ROWS, SLABS, D_MODEL), F32),
        compiler_params=_params("parallel", "parallel", "parallel"),
    )(h5, f5, g.reshape(1, D_MODEL))
    return y.reshape(batch, seq, D_MODEL)


def _final_norm_rows_kernel(h_ref, f_ref, g_ref, y_ref):
    y_ref[...] = _rms(h_ref[...] + f_ref[...], g_ref[...])


def _final_norm_rows(h, f, g, row_block, rows):
    blk = pl.BlockSpec((rows, D_MODEL), lambda i: (row_block, 0))
    return pl.pallas_call(
        _final_norm_rows_kernel,
        grid=(1,),
        in_specs=[blk, blk, pl.BlockSpec((1, D_MODEL), lambda i: (0, 0))],
        out_specs=pl.BlockSpec((rows, D_MODEL), lambda i: (0, 0)),
        out_shape=jax.ShapeDtypeStruct((rows, D_MODEL), F32),
        compiler_params=_params("arbitrary"),
    )(h, f, g.reshape(1, D_MODEL))


def _kv_prompt_kernel(*refs):
    ins, outs = refs[:2 * N_DIL], refs[2 * N_DIL:]
    for g in range(N_DIL):
        for kv in range(2):
            src = ins[2 * g + kv]
            for h in range(ATT_HEADS):
                outs[g][:, kv, h, :] = src[0, :, h * HEAD_DIM:(h + 1) * HEAD_DIM]


def _kv_prompt(p3, batch, seq):
    in_specs, out_specs, out_shape = [], [], []
    residue = lambda s: (s % 4) * 4 + s // 4
    for g in range(N_DIL):
        steps = min(DIL_WINDOWS[g], seq) // SLABS
        last = SLAB_ROWS // steps - 1
        for kv in range(2):
            col = COL_ATT + 3 * g + 1 + kv
            in_specs.append(pl.BlockSpec((1, steps, D_MODEL),
                                         lambda b, s, last=last, col=col: (b * SLABS + s, last, col)))
        out_specs.append(pl.BlockSpec((None, steps, None, 2, ATT_HEADS, HEAD_DIM),
                                      lambda b, s: (b, 0, residue(s), 0, 0, 0)))
        out_shape.append(jax.ShapeDtypeStruct((batch, steps, SLABS, 2, ATT_HEADS, HEAD_DIM), F32))
    outs = pl.pallas_call(
        _kv_prompt_kernel,
        grid=(batch, SLABS),
        in_specs=in_specs,
        out_specs=out_specs,
        out_shape=out_shape,
        compiler_params=_params("parallel", "parallel"),
    )(*([p3] * len(in_specs)))
    return [o.reshape(1, batch, -1, 2, ATT_HEADS, HEAD_DIM) for o in outs]


def _row_tile(t, candidates):
    for c in candidates:
        if t % c == 0:
            return c
    raise ValueError(f"no row tile for {t} tokens")


def kernel(x_prompt, x_sample, cache_kv_w128, cache_kv_w512, cache_kv_w2048, norm_mix_g, w_in, sgu_norm_g, sgu_w, sgu_b, w_out, norm_ffn_g, peer_w_q, peer_sub_keys, peer_u, peer_v, norm_final_g):
    batch, seq, _ = x_prompt.shape
    db, ds, _ = x_sample.shape
    assert w_in.shape[0] == 1 and db * ds == CHUNK and seq == SLABS * SLAB_ROWS
    caches = (cache_kv_w128, cache_kv_w512, cache_kv_w2048)
    tp, ts = batch * seq, db * ds
    t = tp + ts
    n_slabs = t // SLAB_ROWS
    tm_big = _row_tile(t, (1040, 640, 128))
    tm_lane = _row_tile(t, (640, 128))

    xs = x_sample.reshape(ts, D_MODEL)
    xl, xn = _permute_norm(x_prompt, norm_mix_g[0])
    xl = lax.dynamic_update_slice(xl, xs, (tp, 0))
    xn = lax.dynamic_update_slice(xn, _rmsnorm_bf16(xs, norm_mix_g[0]), (tp, 0))

    p = _inproj(xn, w_in[0].astype(BF16), t, tm_big, 1024)
    p3 = p.reshape(n_slabs, SLAB_ROWS, N_COL_BLOCKS * D_MODEL)

    w_tril = sgu_w[0] * jnp.tril(jnp.ones((CHUNK, CHUNK), F32))
    tau = _block_steps(0)
    w_p = w_tril[:, tau][:, :, tau].astype(BF16)
    b_p = sgu_b[0][:, tau][..., None]
    w_s = jnp.einsum("bc,gis->gbics", jnp.eye(db, dtype=F32), w_tril[:, :ds, :ds]).reshape(SGU_GROUPS, ts, ts)
    b_s = jnp.tile(sgu_b[0][:, :ds], (1, db))[..., None]
    a_out = _sgu_prompt(p3, sgu_norm_g[0], w_p, b_p, batch).reshape(t, D_MODEL)
    a_s, vn_s = _sgu_sample(p3, sgu_norm_g[0], w_s.astype(BF16), b_s, tp // SLAB_ROWS)
    a_out = lax.dynamic_update_slice(a_out, a_s, (tp, 0))

    att_s = p[tp:, COL_ATT * D_MODEL:COL_GATE * D_MODEL].reshape(db, ds, N_DIL, 3, ATT_HEADS, HEAD_DIM)
    outs, lses = [], []
    for g in range(N_DIL):
        o_p, lse_p = _band_attention(p3, g, batch)
        o_s, lse_s = _cached_attention(att_s, caches[g][0], g)
        lse_s = jnp.pad(lse_s[..., 0].reshape(ts, ATT_HEADS), ((0, 0), (0, 128 - ATT_HEADS)))
        outs.append(lax.dynamic_update_slice(o_p.reshape(t, D_MODEL), o_s.reshape(ts, D_MODEL), (tp, 0)))
        lses.append(lax.dynamic_update_slice(lse_p.reshape(t, 128), lse_s, (tp, 0)))

    h, hn = _merge(a_out, p, outs, lses, xl, w_out[0].astype(BF16), norm_ffn_g[0], t, 128)

    scores_t = _peer_scores(hn, peer_w_q[0].astype(BF16), peer_sub_keys[0].astype(BF16), tm_lane)
    e1, e2, gate = _route(scores_t, 128)
    g_dense = _expert_weights(e1, e2, gate, 128)
    f = _peer(hn, peer_u[0].astype(BF16), peer_v[0].astype(BF16), g_dense, t, h.shape[0], tm_lane, 1024)

    y_prompt = _final_norm_prompt(h, f, norm_final_g, batch, seq)
    y_sample = _final_norm_rows(h, f, norm_final_g, tp // ts, ts).reshape(db, ds, D_MODEL)
    kv_prompt = _kv_prompt(p3, batch, seq)
    kv_sample = [att_s[:, :, g, 1:3][None] for g in range(N_DIL)]
    sgu_v_sample = vn_s.reshape(1, db, ds, D_MODEL)
    return (y_prompt, y_sample, kv_prompt[0], kv_prompt[1], kv_prompt[2],
            kv_sample[0], kv_sample[1], kv_sample[2], sgu_v_sample)
```

```python
import functools

import numpy as np
import jax
import jax.numpy as jnp
from jax import lax
from jax.experimental import pallas as pl
from jax.experimental.pallas import tpu as pltpu

F32 = jnp.float32
BF16 = jnp.bfloat16

D_MODEL = 2048
HEAD_DIM = 128
ATT_HEADS = D_MODEL // HEAD_DIM
N_DIL = 3
DIL_WINDOWS = (128, 512, 2048)
DIL_RATES = (1, 4, 16)
ATT_BLOCK = 128
SLABS = DIL_RATES[-1]
SLAB_ROWS = 128
SGU_GROUPS = 8
SGU_GROUP_DIM = D_MODEL // SGU_GROUPS
CHUNK = 128
N_COL_BLOCKS = 13
COL_U, COL_V, COL_ATT, COL_GATE = 0, 1, 2, 11
COL_ATT16 = COL_ATT + 3
PEER_HEADS = 8
PEER_TOPK = 16
N_KEYS = 128
N_EXPERTS = N_KEYS * N_KEYS
PEER_HALF = 128
NORM_EPS = 1e-6
MASK_VALUE = -1e30
NEG_INF = float("-inf")
VMEM_LIMIT = 56 * 1024 * 1024

BLOCK_SHAPES = ((16, 8), (4, 32), (1, 128))


def _alibi_slopes():
    n = N_DIL * ATT_HEADS
    e = np.arange(1, n + 1, dtype=np.float32)
    return np.exp2(np.float32(-8.0) * e / np.float32(n)).astype(np.float32).reshape(N_DIL, ATT_HEADS)


def _block_steps(group):
    slabs, rows = BLOCK_SHAPES[group]
    n = np.arange(slabs * rows)
    s, j = n // rows, n % rows
    if group == 0:
        return j * 16 + (s % 4) * 4 + s // 4
    if group == 1:
        return j * 4 + s
    return j


def _params(*sem):
    return pltpu.CompilerParams(dimension_semantics=sem, vmem_limit_bytes=VMEM_LIMIT)


def _rms(x, g):
    return x * lax.rsqrt(jnp.mean(x * x, axis=-1, keepdims=True) + NORM_EPS) * g


def _permute_norm_kernel(x_ref, g_ref, xl_ref, xn_ref):
    for rr in range(8):
        x = x_ref[:, rr, :]
        xl_ref[rr % 4, rr // 4] = x
        xn_ref[rr % 4, rr // 4] = _rms(x, g_ref[...]).astype(xn_ref.dtype)


def _permute_norm(x_prompt, g):
    batch, seq, _ = x_prompt.shape
    x4 = x_prompt.reshape(batch, SLAB_ROWS, SLABS, D_MODEL)
    out = pl.BlockSpec((None, 4, 2, SLAB_ROWS, D_MODEL), lambda b, h: (b, 0, h, 0, 0))
    xl, xn = pl.pallas_call(
        _permute_norm_kernel,
        grid=(batch, 2),
        in_specs=[pl.BlockSpec((None, SLAB_ROWS, 8, D_MODEL), lambda b, h: (b, 0, h, 0)),
                  pl.BlockSpec((1, D_MODEL), lambda b, h: (0, 0))],
        out_specs=[out, out],
        out_shape=[jax.ShapeDtypeStruct((batch + 1, 4, 4, SLAB_ROWS, D_MODEL), F32),
                   jax.ShapeDtypeStruct((batch + 1, 4, 4, SLAB_ROWS, D_MODEL), BF16)],
        compiler_params=_params("parallel", "parallel"),
    )(x4, g.reshape(1, D_MODEL))
    rows = (batch + 1) * seq
    return xl.reshape(rows, D_MODEL), xn.reshape(rows, D_MODEL)


def _rmsnorm_kernel(x_ref, g_ref, o_ref):
    o_ref[...] = _rms(x_ref[...], g_ref[...]).astype(o_ref.dtype)


def _rmsnorm_bf16(x, g):
    return pl.pallas_call(
        _rmsnorm_kernel,
        out_shape=jax.ShapeDtypeStruct(x.shape, BF16),
    )(x, g.reshape(1, D_MODEL))


def _cast_kernel(x_ref, o_ref):
    o_ref[...] = x_ref[...].astype(o_ref.dtype)


def _to_bf16(w, rows, cols):
    r, c = w.shape
    return pl.pallas_call(
        _cast_kernel,
        grid=(r // rows, c // cols),
        in_specs=[pl.BlockSpec((rows, cols), lambda i, j: (i, j))],
        out_specs=pl.BlockSpec((rows, cols), lambda i, j: (i, j)),
        out_shape=jax.ShapeDtypeStruct((r, c), BF16),
        compiler_params=_params("parallel", "parallel"),
    )(w)


def _inproj_kernel(x_ref, w_ref, o_ref, att_ref, *, blocks_per_col):
    cb = pl.program_id(1) // blocks_per_col
    p = jnp.dot(x_ref[...], w_ref[...], preferred_element_type=F32)

    @pl.when(cb < COL_ATT)
    def _():
        o_ref[...] = jax.nn.gelu(p)

    @pl.when((cb >= COL_ATT) & (cb < COL_ATT16))
    def _():
        o_ref[...] = p

    @pl.when((cb >= COL_ATT16) & (cb < COL_GATE))
    def _():
        o_ref[...] = p
        is_q = (cb - COL_ATT) % 3 == 0
        att_ref[...] = (p * jnp.where(is_q, HEAD_DIM ** -0.5, 1.0)).astype(att_ref.dtype)

    @pl.when(cb >= COL_GATE)
    def _():
        o_ref[...] = jax.nn.sigmoid(p)


def _inproj(xn, w, t, tm, tn):
    n = w.shape[1]
    per_col = D_MODEL // tn
    n_att = (COL_GATE - COL_ATT16) * per_col
    att_block = lambda j: jnp.clip(j - COL_ATT16 * per_col, 0, n_att - 1)
    return pl.pallas_call(
        functools.partial(_inproj_kernel, blocks_per_col=per_col),
        grid=(t // tm, n // tn),
        in_specs=[pl.BlockSpec((tm, D_MODEL), lambda i, j: (i, 0)),
                  pl.BlockSpec((D_MODEL, tn), lambda i, j: (0, j))],
        out_specs=[pl.BlockSpec((tm, tn), lambda i, j: (i, j)),
                   pl.BlockSpec((tm, tn), lambda i, j: (i, att_block(j)))],
        out_shape=[jax.ShapeDtypeStruct((t, n), F32),
                   jax.ShapeDtypeStruct((t, n_att * tn), BF16)],
        compiler_params=_params("parallel", "arbitrary"),
    )(xn, w)


def _sgu_kernel(u_ref, v_ref, g_ref, w_ref, b_ref, a_ref, vn_ref):
    v = v_ref[...].reshape(CHUNK, D_MODEL)
    u = u_ref[...].reshape(CHUNK, D_MODEL)
    vc = v - jnp.mean(v, axis=-1, keepdims=True)
    vn = vc * lax.rsqrt(jnp.mean(vc * vc, axis=-1, keepdims=True) + NORM_EPS) * g_ref[...]
    if vn_ref is not None:
        vn_ref[...] = vn
    cols = []
    for g in range(SGU_GROUPS):
        cs = slice(g * SGU_GROUP_DIM, (g + 1) * SGU_GROUP_DIM)
        mix = jnp.dot(w_ref[g], vn[:, cs].astype(BF16), preferred_element_type=F32) + b_ref[g]
        cols.append(u[:, cs] * mix)
    a_ref[...] = jnp.concatenate(cols, axis=-1).reshape(a_ref.shape)


def _sgu_prompt(p3, sgu_norm_g, w, b, batch):
    slabs, rows = BLOCK_SHAPES[0]
    n_chunks = SLAB_ROWS // rows
    blk = lambda col: pl.BlockSpec((slabs, rows, D_MODEL), lambda bb, c: (bb, c, col))
    return pl.pallas_call(
        lambda u, v, g, ww, bb, a: _sgu_kernel(u, v, g, ww, bb, a, None),
        grid=(batch, n_chunks),
        in_specs=[blk(COL_U), blk(COL_V),
                  pl.BlockSpec((1, D_MODEL), lambda bb, c: (0, 0)),
                  pl.BlockSpec((SGU_GROUPS, CHUNK, CHUNK), lambda bb, c: (0, 0, 0)),
                  pl.BlockSpec((SGU_GROUPS, CHUNK, 1), lambda bb, c: (0, 0, 0))],
        out_specs=blk(0),
        out_shape=jax.ShapeDtypeStruct((p3.shape[0], SLAB_ROWS, D_MODEL), F32),
        compiler_params=_params("parallel", "parallel"),
    )(p3, p3, sgu_norm_g.reshape(1, D_MODEL), w, b)


def _sgu_sample(p3, sgu_norm_g, w, b, slab):
    blk = lambda col: pl.BlockSpec((1, SLAB_ROWS, D_MODEL), lambda i: (slab, 0, col))
    return pl.pallas_call(
        _sgu_kernel,
        grid=(1,),
        in_specs=[blk(COL_U), blk(COL_V),
                  pl.BlockSpec((1, D_MODEL), lambda i: (0, 0)),
                  pl.BlockSpec((SGU_GROUPS, CHUNK, CHUNK), lambda i: (0, 0, 0)),
                  pl.BlockSpec((SGU_GROUPS, CHUNK, 1), lambda i: (0, 0, 0))],
        out_specs=[pl.BlockSpec((CHUNK, D_MODEL), lambda i: (0, 0)),
                   pl.BlockSpec((CHUNK, D_MODEL), lambda i: (0, 0))],
        out_shape=[jax.ShapeDtypeStruct((CHUNK, D_MODEL), F32), jax.ShapeDtypeStruct((CHUNK, D_MODEL), F32)],
        compiler_params=_params("arbitrary"),
    )(p3, p3, sgu_norm_g.reshape(1, D_MODEL), w, b)


def _band_attn_kernel(*refs, has_prev, prescaled):
    if has_prev:
        bc_ref, bp_ref, q_ref, kp_ref, kc_ref, vp_ref, vc_ref, o_ref, lse_ref = refs
    else:
        bc_ref, q_ref, kc_ref, vc_ref, o_ref, lse_ref = refs
    n = ATT_BLOCK
    scale = HEAD_DIM ** -0.5
    nt = (((1,), (1,)), ((), ()))
    heads = range(ATT_HEADS)
    hs = [slice(h * HEAD_DIM, (h + 1) * HEAD_DIM) for h in heads]
    ld = lambda ref, h: ref[:, :, hs[h]].reshape(n, HEAD_DIM)
    q = [ld(q_ref, h) if prescaled else (ld(q_ref, h) * scale).astype(BF16) for h in heads]
    s_c = [lax.dot_general(q[h], ld(kc_ref, h).astype(BF16), nt, preferred_element_type=F32) + bc_ref[h]
           for h in heads]
    m = [jnp.max(s_c[h], axis=-1, keepdims=True) for h in heads]
    if has_prev:
        s_p = [lax.dot_general(q[h], ld(kp_ref, h).astype(BF16), nt, preferred_element_type=F32) + bp_ref[h]
               for h in heads]
        m = [jnp.maximum(m[h], jnp.max(s_p[h], axis=-1, keepdims=True)) for h in heads]
    p_c = [jnp.exp(s_c[h] - m[h]) for h in heads]
    l = [jnp.sum(p_c[h], axis=-1, keepdims=True) for h in heads]
    o = [jnp.dot(p_c[h].astype(BF16), ld(vc_ref, h).astype(BF16), preferred_element_type=F32) for h in heads]
    if has_prev:
        p_p = [jnp.exp(s_p[h] - m[h]) for h in heads]
        l = [l[h] + jnp.sum(p_p[h], axis=-1, keepdims=True) for h in heads]
        o = [o[h] + jnp.dot(p_p[h].astype(BF16), ld(vp_ref, h).astype(BF16), preferred_element_type=F32)
             for h in heads]
    lane = lax.broadcasted_iota(jnp.int32, (n, 128), 1)
    lse = jnp.zeros((n, 128), F32)
    for h in heads:
        lse = jnp.where(lane == h, m[h] + jnp.log(l[h]), lse)
    o_ref[...] = jnp.concatenate([o[h] / l[h] for h in heads], axis=-1).reshape(o_ref.shape)
    lse_ref[...] = lse.reshape(lse_ref.shape)


def _band_bias(group):
    steps = _block_steps(group)
    back = (steps[:, None] - steps[None, :]).astype(np.float32)
    coef = (_alibi_slopes()[group] * np.float32(DIL_RATES[group]))[:, None, None]
    cur = np.where(back >= 0, -(coef * back), np.float32(MASK_VALUE)).astype(np.float32)
    back_p = back + np.float32(ATT_BLOCK)
    prev = np.where(back_p <= ATT_BLOCK, -(coef * back_p), np.float32(MASK_VALUE)).astype(np.float32)
    return cur, np.stack([np.full_like(prev, MASK_VALUE), prev])


def _band_attention(src, col_q, prescaled, group, batch):
    p3 = src
    slabs, rows = BLOCK_SHAPES[group]
    streams = SLABS // slabs
    nb = SLAB_ROWS // rows
    has_prev = nb > 1
    cq, ck, cv = col_q, col_q + 1, col_q + 2
    bias_c, bias_p = _band_bias(group)

    def cur(col, width=D_MODEL):
        return pl.BlockSpec((slabs, rows, width), lambda b, r, j: (b * streams + r, j, col))

    def prev(col):
        return pl.BlockSpec((slabs, rows, D_MODEL), lambda b, r, j: (b * streams + r, jnp.maximum(j - 1, 0), col))

    table = (ATT_HEADS, ATT_BLOCK, ATT_BLOCK)
    in_specs = [pl.BlockSpec(table, lambda b, r, j: (0, 0, 0))]
    args = [jnp.asarray(bias_c)]
    if has_prev:
        in_specs += [pl.BlockSpec((None,) + table, lambda b, r, j: (jnp.minimum(j, 1), 0, 0, 0)),
                     cur(cq), prev(ck), cur(ck), prev(cv), cur(cv)]
        args += [jnp.asarray(bias_p)]
    else:
        in_specs += [cur(cq), cur(ck), cur(cv)]
    args += [p3] * (len(in_specs) - len(args))
    return pl.pallas_call(
        functools.partial(_band_attn_kernel, has_prev=has_prev, prescaled=prescaled),
        grid=(batch, streams, nb),
        in_specs=in_specs,
        out_specs=[cur(0), cur(0, 128)],
        out_shape=[jax.ShapeDtypeStruct((p3.shape[0], SLAB_ROWS, D_MODEL), F32),
                   jax.ShapeDtypeStruct((p3.shape[0], SLAB_ROWS, 128), F32)],
        compiler_params=_params("parallel", "parallel", "arbitrary"),
    )(*args)


def _cached_attn_kernel(new_ref, c_ref, coef_ref, o_ref, lse_ref, *, shared, dec_seq):
    lane = lax.broadcasted_iota(jnp.int32, (ATT_HEADS, ATT_BLOCK), 1)
    coef = coef_ref[...]
    scale = HEAD_DIM ** -0.5
    qs = [new_ref[i, 0] for i in range(dec_seq)]
    res = lambda i: 0 if shared else i

    def qk_body(r, carry):
        out = []
        for i in range(dec_seq):
            col = jnp.sum(qs[i] * c_ref[r, res(i), 0], axis=-1, keepdims=True)
            out.append(jnp.where(lane == r, col, carry[i]))
        return tuple(out)

    zero = jnp.zeros((ATT_HEADS, ATT_BLOCK), F32)
    s_all = lax.fori_loop(0, ATT_BLOCK, qk_body, (zero,) * dec_seq, unroll=32)

    ps, p_news, ls, ms = [], [], [], []
    for i in range(dec_seq):
        if shared:
            back = (ATT_BLOCK + i - lane).astype(F32)
            s = jnp.where(lane >= i, s_all[i] * scale - coef * back, MASK_VALUE)
            s_new = [jnp.sum(qs[i] * new_ref[i2, 1], axis=-1, keepdims=True) * scale - coef[:, :1] * float(i - i2)
                     for i2 in range(i + 1)]
        else:
            s = s_all[i] * scale - coef * (ATT_BLOCK - lane).astype(F32)
            s_new = [jnp.sum(qs[i] * new_ref[i, 1], axis=-1, keepdims=True) * scale]
        m = jnp.max(s, axis=-1, keepdims=True)
        for sn in s_new:
            m = jnp.maximum(m, sn)
        p = jnp.exp(s - m)
        p_new = [jnp.exp(sn - m) for sn in s_new]
        l = jnp.sum(p, axis=-1, keepdims=True)
        for pn in p_new:
            l = l + pn
        ps.append(p)
        p_news.append(p_new)
        ls.append(l)
        ms.append(m)

    def pv_body(r, carry):
        out = []
        for i in range(dec_seq):
            col = jnp.sum(jnp.where(lane == r, ps[i], 0.0), axis=-1, keepdims=True)
            out.append(carry[i] + col * c_ref[r, res(i), 1])
        return tuple(out)

    zero_o = jnp.zeros((ATT_HEADS, HEAD_DIM), F32)
    o_all = lax.fori_loop(0, ATT_BLOCK, pv_body, (zero_o,) * dec_seq, unroll=32)
    for i in range(dec_seq):
        o = o_all[i]
        for i2, pn in enumerate(p_news[i]):
            o = o + pn * new_ref[i2 if shared else i, 2]
        o_ref[i] = o / ls[i]
        lse_ref[i] = jnp.broadcast_to(ms[i] + jnp.log(ls[i]), (ATT_HEADS, 128))


def _cached_attention(att_s, cache, group):
    db, dec_seq = att_s.shape[:2]
    dil = DIL_RATES[group]
    window = DIL_WINDOWS[group]
    assert cache.shape[1] == window and window == dil * ATT_BLOCK
    assert dil == 1 or dec_seq <= dil
    shared = dil == 1
    nres = 1 if shared else dec_seq
    cv = cache.reshape(db, ATT_BLOCK, dil, 2, ATT_HEADS, HEAD_DIM)
    coef = jnp.asarray(np.repeat((_alibi_slopes()[group] * dil)[:, None], 128, axis=1).astype(np.float32))
    out = lambda w: pl.BlockSpec((None, dec_seq, ATT_HEADS, w), lambda b: (b, 0, 0, 0))
    return pl.pallas_call(
        functools.partial(_cached_attn_kernel, shared=shared, dec_seq=dec_seq),
        grid=(db,),
        in_specs=[pl.BlockSpec((None, dec_seq, None, 3, ATT_HEADS, HEAD_DIM), lambda b: (b, 0, group, 0, 0, 0)),
                  pl.BlockSpec((None, ATT_BLOCK, nres, 2, ATT_HEADS, HEAD_DIM), lambda b: (b, 0, 0, 0, 0, 0)),
                  pl.BlockSpec((ATT_HEADS, 128), lambda b: (0, 0))],
        out_specs=[out(HEAD_DIM), out(128)],
        out_shape=[jax.ShapeDtypeStruct((db, dec_seq, ATT_HEADS, HEAD_DIM), F32),
                   jax.ShapeDtypeStruct((db, dec_seq, ATT_HEADS, 128), F32)],
        compiler_params=_params("parallel"),
    )(att_s, cv, coef)


def _merge_kernel(a_ref, ga_ref, gb_ref, o0_ref, o1_ref, o2_ref, l0_ref, l1_ref, l2_ref,
                  x_ref, w_ref, g_ref, h_ref, hn_ref, merged_ref):
    l0, l1, l2 = l0_ref[...], l1_ref[...], l2_ref[...]
    mx = jnp.maximum(jnp.maximum(l0, l1), l2)
    e0, e1, e2 = jnp.exp(l0 - mx), jnp.exp(l1 - mx), jnp.exp(l2 - mx)
    den = e0 + e1 + e2
    w0, w1, w2 = e0 / den, e1 / den, e2 / den
    for h in range(ATT_HEADS):
        hs = slice(h * HEAD_DIM, (h + 1) * HEAD_DIM)
        b_out = (w0[:, h:h + 1] * o0_ref[:, hs] + w1[:, h:h + 1] * o1_ref[:, hs] + w2[:, h:h + 1] * o2_ref[:, hs])
        merged = ga_ref[:, hs] * a_ref[:, hs] + gb_ref[:, hs] * b_out
        merged_ref[:, hs] = merged.astype(merged_ref.dtype)
    h_new = x_ref[...] + jnp.dot(merged_ref[...], w_ref[...], preferred_element_type=F32)
    h_ref[...] = h_new
    hn_ref[...] = _rms(h_new, g_ref[...]).astype(hn_ref.dtype)


def _merge(a_out, p, outs, lses, x, w_out, norm_g, t, tm):
    row = lambda col: pl.BlockSpec((tm, D_MODEL), lambda i: (i, col))
    lrow = pl.BlockSpec((tm, 128), lambda i: (i, 0))
    return pl.pallas_call(
        _merge_kernel,
        grid=(t // tm,),
        in_specs=[row(0), row(COL_GATE), row(COL_GATE + 1), row(0), row(0), row(0), lrow, lrow, lrow,
                  row(0), pl.BlockSpec((D_MODEL, D_MODEL), lambda i: (0, 0)),
                  pl.BlockSpec((1, D_MODEL), lambda i: (0, 0))],
        out_specs=[row(0), row(0)],
        out_shape=[jax.ShapeDtypeStruct((x.shape[0], D_MODEL), F32), jax.ShapeDtypeStruct((t, D_MODEL), BF16)],
        scratch_shapes=[pltpu.VMEM((tm, D_MODEL), BF16)],
        compiler_params=_params("parallel"),
    )(a_out, p, p, outs[0], outs[1], outs[2], lses[0], lses[1], lses[2], x, w_out, norm_g.reshape(1, D_MODEL))


def _peer_score_kernel(hn_ref, wq_ref, keys_ref, s_ref):
    q = jnp.dot(hn_ref[...], wq_ref[...], preferred_element_type=F32).astype(BF16)
    nt = (((1,), (1,)), ((), ()))
    for hc in range(2 * PEER_HEADS):
        cs = slice(hc * PEER_HALF, (hc + 1) * PEER_HALF)
        s_ref[hc] = lax.dot_general(keys_ref[hc % 2], q[:, cs], nt, preferred_element_type=F32)


def _peer_scores(hn, w_q, sub_keys, tm):
    t = hn.shape[0]
    return pl.pallas_call(
        _peer_score_kernel,
        grid=(t // tm,),
        in_specs=[pl.BlockSpec((tm, D_MODEL), lambda i: (i, 0)),
                  pl.BlockSpec((D_MODEL, 2 * PEER_HEADS * PEER_HALF), lambda i: (0, 0)),
                  pl.BlockSpec((2, N_KEYS, PEER_HALF), lambda i: (0, 0, 0))],
        out_specs=pl.BlockSpec((2 * PEER_HEADS, N_KEYS, tm), lambda i: (0, 0, i)),
        out_shape=jax.ShapeDtypeStruct((2 * PEER_HEADS, N_KEYS, t), F32),
        compiler_params=_params("parallel"),
    )(hn, w_q, sub_keys)


def _take_top(s, count):
    rows = lax.broadcasted_iota(jnp.int32, s.shape, 0).astype(F32)
    vals, idxs = [], []
    for _ in range(count):
        m = jnp.max(s, axis=0, keepdims=True)
        pos = jnp.min(jnp.where(s == m, rows, float(s.shape[0])), axis=0, keepdims=True)
        vals.append(m)
        idxs.append(pos)
        s = jnp.where(rows == pos, NEG_INF, s)
    return vals, idxs


def _stack_rows(rows_list, krow):
    out = jnp.zeros(krow.shape, F32)
    for j, r in enumerate(rows_list):
        out = jnp.where(krow == float(j), r, out)
    return out


def _cand_layout():
    k = PEER_TOPK
    pieces = [("row_a", a, 16 if a == 0 else 8, 0, k // (a + 1)) for a in range(4)]
    pieces += [("col_b", 0, 16, 4, 16), ("col_b", 1, 8, 4, 8), ("col_b", 2, 8, 4, 5)]
    pos = []
    for kind, idx, rows, lo, hi in pieces:
        for r in range(rows):
            a, b = (idx, r) if kind == "row_a" else (r, idx)
            ok = lo <= r < hi and (a + 1) * (b + 1) <= k
            pos.append(a * k + b if ok else k * k)
    assert sorted(p for p in pos if p < k * k) == sorted(
        a * k + b for a in range(k) for b in range(k) if (a + 1) * (b + 1) <= k)
    return pieces, np.asarray(pos, np.float32)


def _route_kernel(s_ref, pos_ref, e1_ref, e2_ref, gate_ref):
    k = PEER_TOPK
    v1, i1 = _take_top(s_ref[0], k)
    v2, i2 = _take_top(s_ref[1], k)
    krow = lax.broadcasted_iota(jnp.int32, (k, s_ref.shape[2]), 0).astype(F32)
    v1_all = _stack_rows(v1, krow)
    v2_all = _stack_rows(v2, krow)
    i1_all = _stack_rows(i1, krow)
    i2_all = _stack_rows(i2, krow)
    parts = []
    for kind, idx, rows, _, _ in _cand_layout()[0]:
        parts.append(v1[idx] + v2_all[:rows] if kind == "row_a" else v1_all[:rows] + v2[idx])
    flat = pos_ref[...]
    cand = jnp.where(flat < float(k * k), jnp.concatenate(parts, axis=0), NEG_INF)
    top_s, pos = [], []
    for _ in range(k):
        m = jnp.max(cand, axis=0, keepdims=True)
        p = jnp.min(jnp.where(cand == m, flat, float(k * k)), axis=0, keepdims=True)
        top_s.append(m)
        pos.append(p)
        cand = jnp.where(flat == p, NEG_INF, cand)
    e1, e2 = [], []
    for j in range(k):
        a = jnp.floor(pos[j] * (1.0 / k))
        b = pos[j] - a * k
        e1.append(jnp.sum(jnp.where(krow == a, i1_all, 0.0), axis=0, keepdims=True))
        e2.append(jnp.sum(jnp.where(krow == b, i2_all, 0.0), axis=0, keepdims=True))
    ex = jnp.exp(_stack_rows(top_s, krow) - top_s[0])
    gate_ref[...] = ex / jnp.sum(ex, axis=0, keepdims=True)
    e1_ref[...] = _stack_rows(e1, krow)
    e2_ref[...] = _stack_rows(e2, krow)


def _route(scores_t, tl):
    t = scores_t.shape[2]
    kk = PEER_HEADS * PEER_TOPK
    out = pl.BlockSpec((PEER_TOPK, tl), lambda i, h: (h, i))
    flat = jnp.asarray(np.repeat(_cand_layout()[1][:, None], tl, axis=1))
    return pl.pallas_call(
        _route_kernel,
        grid=(t // tl, PEER_HEADS),
        in_specs=[pl.BlockSpec((2, N_KEYS, tl), lambda i, h: (h, 0, i)),
                  pl.BlockSpec(flat.shape, lambda i, h: (0, 0))],
        out_specs=[out, out, out],
        out_shape=[jax.ShapeDtypeStruct((kk, t), F32)] * 3,
        compiler_params=_params("parallel", "parallel"),
    )(scores_t, flat)


def _expert_weight_kernel(e1_ref, e2_ref, gate_ref, g_ref, e1t_ref, e2t_ref, gt_ref):
    e1t_ref[...] = e1_ref[...].T
    e2t_ref[...] = e2_ref[...].T
    gt_ref[...] = gate_ref[...].T
    kk = e1_ref.shape[0]
    key = lax.broadcasted_iota(jnp.int32, (N_KEYS, kk), 0).astype(F32)
    nt = (((1,), (1,)), ((), ()))

    def body(t, carry):
        r1 = e1t_ref[pl.ds(t, 1), :]
        r2 = e2t_ref[pl.ds(t, 1), :]
        gr = gt_ref[pl.ds(t, 1), :]
        a_t = jnp.where(key == r1, 1.0, 0.0).astype(BF16)
        b_t = jnp.where(key == r2, gr, 0.0).astype(BF16)
        g_ref[t] = lax.dot_general(a_t, b_t, nt, preferred_element_type=F32).astype(g_ref.dtype)
        return carry

    lax.fori_loop(0, g_ref.shape[0], body, 0, unroll=32)


def _expert_weights(e1, e2, gate, tl):
    kk, t = e1.shape
    slot = pl.BlockSpec((kk, tl), lambda i: (0, i))
    return pl.pallas_call(
        _expert_weight_kernel,
        grid=(t // tl,),
        in_specs=[slot, slot, slot],
        out_specs=pl.BlockSpec((tl, N_KEYS, N_KEYS), lambda i: (i, 0, 0)),
        out_shape=jax.ShapeDtypeStruct((t, N_KEYS, N_KEYS), F32),
        scratch_shapes=[pltpu.VMEM((tl, kk), F32)] * 3,
        compiler_params=_params("parallel"),
    )(e1, e2, gate)


def _peer_kernel(hn_ref, u_ref, v_ref, g_ref, y_ref):
    e = pl.program_id(1)
    nt = (((1,), (1,)), ((), ()))
    hk = lax.dot_general(hn_ref[...], u_ref[...], nt, preferred_element_type=F32)
    g = jnp.swapaxes(g_ref[...], 0, 1)
    act = [(jax.nn.gelu(hk[:, a * N_KEYS:(a + 1) * N_KEYS]) * g[a]).astype(BF16) for a in range(g.shape[0])]
    out = jnp.dot(jnp.concatenate(act, axis=-1), v_ref[...], preferred_element_type=F32)

    @pl.when(e == 0)
    def _():
        y_ref[...] = out

    @pl.when(e > 0)
    def _():
        y_ref[...] += out


def _peer(hn, u, v, g, t, rows_out, tm, te):
    row = pl.BlockSpec((tm, D_MODEL), lambda i, e: (i, 0))
    tab = pl.BlockSpec((te, D_MODEL), lambda i, e: (e, 0))
    return pl.pallas_call(
        _peer_kernel,
        grid=(t // tm, N_EXPERTS // te),
        in_specs=[row, tab, tab, pl.BlockSpec((tm, te // N_KEYS, N_KEYS), lambda i, e: (i, e, 0))],
        out_specs=row,
        out_shape=jax.ShapeDtypeStruct((rows_out, D_MODEL), F32),
        compiler_params=_params("parallel", "arbitrary"),
    )(hn, u, v, g)


def _final_norm_prompt_kernel(h_ref, f_ref, g_ref, y_ref):
    for rr in range(8):
        y_ref[:, rr, :] = _rms(h_ref[rr % 4, rr // 4] + f_ref[rr % 4, rr // 4], g_ref[...])


def _final_norm_prompt(h, f, g, batch, seq):
    rows = 64
    h5 = h.reshape(batch + 1, 4, 4, SLAB_ROWS, D_MODEL)
    f5 = f.reshape(batch + 1, 4, 4, SLAB_ROWS, D_MODEL)
    blk = pl.BlockSpec((None, 4, 2, rows, D_MODEL), lambda b, hh, j: (b, 0, hh, j, 0))
    y = pl.pallas_call(
        _final_norm_prompt_kernel,
        grid=(batch, 2, SLAB_ROWS // rows),
        in_specs=[blk, blk, pl.BlockSpec((1, D_MODEL), lambda b, hh, j: (0, 0))],
        out_specs=pl.BlockSpec((None, rows, 8, D_MODEL), lambda b, hh, j: (b, j, hh, 0)),
        out_shape=jax.ShapeDtypeStruct((batch, SLAB_ROWS, SLABS, D_MODEL), F32),
        compiler_params=_params("parallel", "parallel", "parallel"),
    )(h5, f5, g.reshape(1, D_MODEL))
    return y.reshape(batch, seq, D_MODEL)


def _final_norm_rows_kernel(h_ref, f_ref, g_ref, y_ref):
    y_ref[...] = _rms(h_ref[...] + f_ref[...], g_ref[...])


def _final_norm_rows(h, f, g, row_block, rows):
    blk = pl.BlockSpec((rows, D_MODEL), lambda i: (row_block, 0))
    return pl.pallas_call(
        _final_norm_rows_kernel,
        grid=(1,),
        in_specs=[blk, blk, pl.BlockSpec((1, D_MODEL), lambda i: (0, 0))],
        out_specs=pl.BlockSpec((rows, D_MODEL), lambda i: (0, 0)),
        out_shape=jax.ShapeDtypeStruct((rows, D_MODEL), F32),
        compiler_params=_params("arbitrary"),
    )(h, f, g.reshape(1, D_MODEL))


def _kv_prompt_kernel(*refs):
    ins, outs = refs[:2 * N_DIL], refs[2 * N_DIL:]
    for g in range(N_DIL):
        for kv in range(2):
            src = ins[2 * g + kv]
            for h in range(ATT_HEADS):
                outs[g][:, kv, h, :] = src[0, :, h * HEAD_DIM:(h + 1) * HEAD_DIM]


def _kv_prompt(p3, batch, seq):
    in_specs, out_specs, out_shape = [], [], []
    residue = lambda s: (s % 4) * 4 + s // 4
    for g in range(N_DIL):
        steps = min(DIL_WINDOWS[g], seq) // SLABS
        last = SLAB_ROWS // steps - 1
        for kv in range(2):
            col = COL_ATT + 3 * g + 1 + kv
            in_specs.append(pl.BlockSpec((1, steps, D_MODEL),
                                         lambda b, s, last=last, col=col: (b * SLABS + s, last, col)))
        out_specs.append(pl.BlockSpec((None, steps, None, 2, ATT_HEADS, HEAD_DIM),
                                      lambda b, s: (b, 0, residue(s), 0, 0, 0)))
        out_shape.append(jax.ShapeDtypeStruct((batch, steps, SLABS, 2, ATT_HEADS, HEAD_DIM), F32))
    outs = pl.pallas_call(
        _kv_prompt_kernel,
        grid=(batch, SLABS),
        in_specs=in_specs,
        out_specs=out_specs,
        out_shape=out_shape,
        compiler_params=_params("parallel", "parallel"),
    )(*([p3] * len(in_specs)))
    return [o.reshape(1, batch, -1, 2, ATT_HEADS, HEAD_DIM) for o in outs]


def _row_tile(t, candidates):
    for c in candidates:
        if t % c == 0:
            return c
    raise ValueError(f"no row tile for {t} tokens")


def kernel(x_prompt, x_sample, cache_kv_w128, cache_kv_w512, cache_kv_w2048, norm_mix_g, w_in, sgu_norm_g, sgu_w, sgu_b, w_out, norm_ffn_g, peer_w_q, peer_sub_keys, peer_u, peer_v, norm_final_g):
    batch, seq, _ = x_prompt.shape
    db, ds, _ = x_sample.shape
    assert w_in.shape[0] == 1 and db * ds == CHUNK and seq == SLABS * SLAB_ROWS
    caches = (cache_kv_w128, cache_kv_w512, cache_kv_w2048)
    tp, ts = batch * seq, db * ds
    t = tp + ts
    n_slabs = t // SLAB_ROWS
    tm_big = _row_tile(t, (1040, 640, 128))
    tm_lane = _row_tile(t, (640, 128))

    xs = x_sample.reshape(ts, D_MODEL)
    xl, xn = _permute_norm(x_prompt, norm_mix_g[0])
    xl = lax.dynamic_update_slice(xl, xs, (tp, 0))
    xn = lax.dynamic_update_slice(xn, _rmsnorm_bf16(xs, norm_mix_g[0]), (tp, 0))

    p, att16 = _inproj(xn, _to_bf16(w_in[0], D_MODEL, 1024), t, tm_big, 1024)
    p3 = p.reshape(n_slabs, SLAB_ROWS, N_COL_BLOCKS * D_MODEL)
    att16_3 = att16.reshape(n_slabs, SLAB_ROWS, att16.shape[1])

    w_tril = sgu_w[0] * jnp.tril(jnp.ones((CHUNK, CHUNK), F32))
    tau = _block_steps(0)
    w_p = w_tril[:, tau][:, :, tau].astype(BF16)
    b_p = sgu_b[0][:, tau][..., None]
    w_s = jnp.einsum("bc,gis->gbics", jnp.eye(db, dtype=F32), w_tril[:, :ds, :ds]).reshape(SGU_GROUPS, ts, ts)
    b_s = jnp.tile(sgu_b[0][:, :ds], (1, db))[..., None]
    a_out = _sgu_prompt(p3, sgu_norm_g[0], w_p, b_p, batch).reshape(t, D_MODEL)
    a_s, vn_s = _sgu_sample(p3, sgu_norm_g[0], w_s.astype(BF16), b_s, tp // SLAB_ROWS)
    a_out = lax.dynamic_update_slice(a_out, a_s, (tp, 0))

    att_s = p[tp:, COL_ATT * D_MODEL:COL_GATE * D_MODEL].reshape(db, ds, N_DIL, 3, ATT_HEADS, HEAD_DIM)
    outs, lses = [], []
    for g in range(N_DIL):
        if g == 0:
            o_p, lse_p = _band_attention(p3, COL_ATT, False, g, batch)
        else:
            o_p, lse_p = _band_attention(att16_3, 3 * (g - 1), True, g, batch)
        o_s, lse_s = _cached_attention(att_s, caches[g][0], g)
        lse_s = jnp.pad(lse_s[..., 0].reshape(ts, ATT_HEADS), ((0, 0), (0, 128 - ATT_HEADS)))
        outs.append(lax.dynamic_update_slice(o_p.reshape(t, D_MODEL), o_s.reshape(ts, D_MODEL), (tp, 0)))
        lses.append(lax.dynamic_update_slice(lse_p.reshape(t, 128), lse_s, (tp, 0)))

    h, hn = _merge(a_out, p, outs, lses, xl, _to_bf16(w_out[0], 1024, D_MODEL), norm_ffn_g[0], t, 128)

    scores_t = _peer_scores(hn, _to_bf16(peer_w_q[0], 1024, D_MODEL), peer_sub_keys[0].astype(BF16), tm_lane)
    e1, e2, gate = _route(scores_t, 128)
    g_dense = _expert_weights(e1, e2, gate, 128)
    f = _peer(hn, _to_bf16(peer_u[0], 1024, D_MODEL), _to_bf16(peer_v[0], 1024, D_MODEL), g_dense,
              t, h.shape[0], tm_lane, 1024)

    y_prompt = _final_norm_prompt(h, f, norm_final_g, batch, seq)
    y_sample = _final_norm_rows(h, f, norm_final_g, tp // ts, ts).reshape(db, ds, D_MODEL)
    kv_prompt = _kv_prompt(p3, batch, seq)
    kv_sample = [att_s[:, :, g, 1:3][None] for g in range(N_DIL)]
    sgu_v_sample = vn_s.reshape(1, db, ds, D_MODEL)
    return (y_prompt, y_sample, kv_prompt[0], kv_prompt[1], kv_prompt[2],
            kv_sample[0], kv_sample[1], kv_sample[2], sgu_v_sample)
```

```python
import functools

import numpy as np
import jax
import jax.numpy as jnp
from jax import lax
from jax.experimental import pallas as pl
from jax.experimental.pallas import tpu as pltpu

F32 = jnp.float32
BF16 = jnp.bfloat16

D_MODEL = 2048
HEAD_DIM = 128
ATT_HEADS = D_MODEL // HEAD_DIM
N_DIL = 3
DIL_WINDOWS = (128, 512, 2048)
DIL_RATES = (1, 4, 16)
ATT_BLOCK = 128
SLABS = DIL_RATES[-1]
SLAB_ROWS = 128
SGU_GROUPS = 8
SGU_GROUP_DIM = D_MODEL // SGU_GROUPS
CHUNK = 128
N_COL_BLOCKS = 13
COL_U, COL_V, COL_ATT, COL_GATE = 0, 1, 2, 11
COL_ATT16 = COL_ATT + 3
PEER_HEADS = 8
PEER_TOPK = 16
N_KEYS = 128
N_EXPERTS = N_KEYS * N_KEYS
PEER_HALF = 128
NORM_EPS = 1e-6
MASK_VALUE = -1e30
NEG_INF = float("-inf")
VMEM_LIMIT = 56 * 1024 * 1024

BLOCK_SHAPES = ((16, 8), (4, 32), (1, 128))


def _alibi_slopes():
    n = N_DIL * ATT_HEADS
    e = np.arange(1, n + 1, dtype=np.float32)
    return np.exp2(np.float32(-8.0) * e / np.float32(n)).astype(np.float32).reshape(N_DIL, ATT_HEADS)


def _block_steps(group):
    slabs, rows = BLOCK_SHAPES[group]
    n = np.arange(slabs * rows)
    s, j = n // rows, n % rows
    if group == 0:
        return j * 16 + (s % 4) * 4 + s // 4
    if group == 1:
        return j * 4 + s
    return j


def _params(*sem):
    return pltpu.CompilerParams(dimension_semantics=sem, vmem_limit_bytes=VMEM_LIMIT)


def _rms(x, g):
    return x * lax.rsqrt(jnp.mean(x * x, axis=-1, keepdims=True) + NORM_EPS) * g


def _permute_norm_kernel(x_ref, g_ref, xl_ref, xn_ref):
    xs = jnp.swapaxes(x_ref[...], 0, 1)
    for rr in range(8):
        x = xs[rr]
        xl_ref[rr % 4, rr // 4] = x
        xn_ref[rr % 4, rr // 4] = _rms(x, g_ref[...]).astype(xn_ref.dtype)


def _permute_norm(x_prompt, g):
    batch, seq, _ = x_prompt.shape
    x4 = x_prompt.reshape(batch, SLAB_ROWS, SLABS, D_MODEL)
    out = pl.BlockSpec((None, 4, 2, SLAB_ROWS, D_MODEL), lambda b, h: (b, 0, h, 0, 0))
    xl, xn = pl.pallas_call(
        _permute_norm_kernel,
        grid=(batch, 2),
        in_specs=[pl.BlockSpec((None, SLAB_ROWS, 8, D_MODEL), lambda b, h: (b, 0, h, 0)),
                  pl.BlockSpec((1, D_MODEL), lambda b, h: (0, 0))],
        out_specs=[out, out],
        out_shape=[jax.ShapeDtypeStruct((batch + 1, 4, 4, SLAB_ROWS, D_MODEL), F32),
                   jax.ShapeDtypeStruct((batch + 1, 4, 4, SLAB_ROWS, D_MODEL), BF16)],
        compiler_params=_params("parallel", "parallel"),
    )(x4, g.reshape(1, D_MODEL))
    rows = (batch + 1) * seq
    return xl.reshape(rows, D_MODEL), xn.reshape(rows, D_MODEL)


def _rmsnorm_kernel(x_ref, g_ref, o_ref):
    o_ref[...] = _rms(x_ref[...], g_ref[...]).astype(o_ref.dtype)


def _rmsnorm_bf16(x, g):
    return pl.pallas_call(
        _rmsnorm_kernel,
        out_shape=jax.ShapeDtypeStruct(x.shape, BF16),
    )(x, g.reshape(1, D_MODEL))


def _cast_kernel(x_ref, o_ref):
    o_ref[...] = x_ref[...].astype(o_ref.dtype)


def _to_bf16(w, rows, cols):
    r, c = w.shape
    return pl.pallas_call(
        _cast_kernel,
        grid=(r // rows, c // cols),
        in_specs=[pl.BlockSpec((rows, cols), lambda i, j: (i, j))],
        out_specs=pl.BlockSpec((rows, cols), lambda i, j: (i, j)),
        out_shape=jax.ShapeDtypeStruct((r, c), BF16),
        compiler_params=_params("parallel", "parallel"),
    )(w)


def _inproj_kernel(x_ref, w_ref, o_ref, att_ref, *, blocks_per_col):
    cb = pl.program_id(1) // blocks_per_col
    p = jnp.dot(x_ref[...], w_ref[...], preferred_element_type=F32)

    @pl.when(cb < COL_ATT)
    def _():
        o_ref[...] = jax.nn.gelu(p)

    @pl.when((cb >= COL_ATT) & (cb < COL_ATT16))
    def _():
        o_ref[...] = p

    @pl.when((cb >= COL_ATT16) & (cb < COL_GATE))
    def _():
        o_ref[...] = p
        is_q = (cb - COL_ATT) % 3 == 0
        att_ref[...] = (p * jnp.where(is_q, HEAD_DIM ** -0.5, 1.0)).astype(att_ref.dtype)

    @pl.when(cb >= COL_GATE)
    def _():
        o_ref[...] = jax.nn.sigmoid(p)


def _inproj(xn, w, t, tm, tn):
    n = w.shape[1]
    per_col = D_MODEL // tn
    n_att = (COL_GATE - COL_ATT16) * per_col
    att_block = lambda j: jnp.clip(j - COL_ATT16 * per_col, 0, n_att - 1)
    return pl.pallas_call(
        functools.partial(_inproj_kernel, blocks_per_col=per_col),
        grid=(t // tm, n // tn),
        in_specs=[pl.BlockSpec((tm, D_MODEL), lambda i, j: (i, 0)),
                  pl.BlockSpec((D_MODEL, tn), lambda i, j: (0, j))],
        out_specs=[pl.BlockSpec((tm, tn), lambda i, j: (i, j)),
                   pl.BlockSpec((tm, tn), lambda i, j: (i, att_block(j)))],
        out_shape=[jax.ShapeDtypeStruct((t, n), F32),
                   jax.ShapeDtypeStruct((t, n_att * tn), BF16)],
        compiler_params=_params("parallel", "arbitrary"),
    )(xn, w)


def _sgu_kernel(u_ref, v_ref, g_ref, w_ref, b_ref, a_ref, vn_ref):
    v = v_ref[...].reshape(CHUNK, D_MODEL)
    u = u_ref[...].reshape(CHUNK, D_MODEL)
    vc = v - jnp.mean(v, axis=-1, keepdims=True)
    vn = vc * lax.rsqrt(jnp.mean(vc * vc, axis=-1, keepdims=True) + NORM_EPS) * g_ref[...]
    if vn_ref is not None:
        vn_ref[...] = vn
    cols = []
    for g in range(SGU_GROUPS):
        cs = slice(g * SGU_GROUP_DIM, (g + 1) * SGU_GROUP_DIM)
        mix = jnp.dot(w_ref[g], vn[:, cs].astype(BF16), preferred_element_type=F32) + b_ref[g]
        cols.append(u[:, cs] * mix)
    a_ref[...] = jnp.concatenate(cols, axis=-1).reshape(a_ref.shape)


def _sgu_prompt(p3, sgu_norm_g, w, b, batch):
    slabs, rows = BLOCK_SHAPES[0]
    n_chunks = SLAB_ROWS // rows
    blk = lambda col: pl.BlockSpec((slabs, rows, D_MODEL), lambda bb, c: (bb, c, col))
    return pl.pallas_call(
        lambda u, v, g, ww, bb, a: _sgu_kernel(u, v, g, ww, bb, a, None),
        grid=(batch, n_chunks),
        in_specs=[blk(COL_U), blk(COL_V),
                  pl.BlockSpec((1, D_MODEL), lambda bb, c: (0, 0)),
                  pl.BlockSpec((SGU_GROUPS, CHUNK, CHUNK), lambda bb, c: (0, 0, 0)),
                  pl.BlockSpec((SGU_GROUPS, CHUNK, 1), lambda bb, c: (0, 0, 0))],
        out_specs=blk(0),
        out_shape=jax.ShapeDtypeStruct((p3.shape[0], SLAB_ROWS, D_MODEL), F32),
        compiler_params=_params("parallel", "parallel"),
    )(p3, p3, sgu_norm_g.reshape(1, D_MODEL), w, b)


def _sgu_sample(p3, sgu_norm_g, w, b, slab):
    blk = lambda col: pl.BlockSpec((1, SLAB_ROWS, D_MODEL), lambda i: (slab, 0, col))
    return pl.pallas_call(
        _sgu_kernel,
        grid=(1,),
        in_specs=[blk(COL_U), blk(COL_V),
                  pl.BlockSpec((1, D_MODEL), lambda i: (0, 0)),
                  pl.BlockSpec((SGU_GROUPS, CHUNK, CHUNK), lambda i: (0, 0, 0)),
                  pl.BlockSpec((SGU_GROUPS, CHUNK, 1), lambda i: (0, 0, 0))],
        out_specs=[pl.BlockSpec((CHUNK, D_MODEL), lambda i: (0, 0)),
                   pl.BlockSpec((CHUNK, D_MODEL), lambda i: (0, 0))],
        out_shape=[jax.ShapeDtypeStruct((CHUNK, D_MODEL), F32), jax.ShapeDtypeStruct((CHUNK, D_MODEL), F32)],
        compiler_params=_params("arbitrary"),
    )(p3, p3, sgu_norm_g.reshape(1, D_MODEL), w, b)


def _band_attn_kernel(*refs, has_prev, prescaled):
    if has_prev:
        bc_ref, bp_ref, q_ref, kp_ref, kc_ref, vp_ref, vc_ref, o_ref, lse_ref = refs
    else:
        bc_ref, q_ref, kc_ref, vc_ref, o_ref, lse_ref = refs
    n = ATT_BLOCK
    scale = HEAD_DIM ** -0.5
    nt = (((1,), (1,)), ((), ()))
    heads = range(ATT_HEADS)
    hs = [slice(h * HEAD_DIM, (h + 1) * HEAD_DIM) for h in heads]
    ld = lambda ref, h: ref[:, :, hs[h]].reshape(n, HEAD_DIM)
    q = [ld(q_ref, h) if prescaled else (ld(q_ref, h) * scale).astype(BF16) for h in heads]
    s_c = [lax.dot_general(q[h], ld(kc_ref, h).astype(BF16), nt, preferred_element_type=F32) + bc_ref[h]
           for h in heads]
    m = [jnp.max(s_c[h], axis=-1, keepdims=True) for h in heads]
    if has_prev:
        s_p = [lax.dot_general(q[h], ld(kp_ref, h).astype(BF16), nt, preferred_element_type=F32) + bp_ref[h]
               for h in heads]
        m = [jnp.maximum(m[h], jnp.max(s_p[h], axis=-1, keepdims=True)) for h in heads]
    p_c = [jnp.exp(s_c[h] - m[h]) for h in heads]
    l = [jnp.sum(p_c[h], axis=-1, keepdims=True) for h in heads]
    o = [jnp.dot(p_c[h].astype(BF16), ld(vc_ref, h).astype(BF16), preferred_element_type=F32) for h in heads]
    if has_prev:
        p_p = [jnp.exp(s_p[h] - m[h]) for h in heads]
        l = [l[h] + jnp.sum(p_p[h], axis=-1, keepdims=True) for h in heads]
        o = [o[h] + jnp.dot(p_p[h].astype(BF16), ld(vp_ref, h).astype(BF16), preferred_element_type=F32)
             for h in heads]
    lane = lax.broadcasted_iota(jnp.int32, (n, 128), 1)
    lse = jnp.zeros((n, 128), F32)
    for h in heads:
        lse = jnp.where(lane == h, m[h] + jnp.log(l[h]), lse)
    o_ref[...] = jnp.concatenate([o[h] / l[h] for h in heads], axis=-1).reshape(o_ref.shape)
    lse_ref[...] = lse.reshape(lse_ref.shape)


def _band_bias(group):
    steps = _block_steps(group)
    back = (steps[:, None] - steps[None, :]).astype(np.float32)
    coef = (_alibi_slopes()[group] * np.float32(DIL_RATES[group]))[:, None, None]
    cur = np.where(back >= 0, -(coef * back), np.float32(MASK_VALUE)).astype(np.float32)
    back_p = back + np.float32(ATT_BLOCK)
    prev = np.where(back_p <= ATT_BLOCK, -(coef * back_p), np.float32(MASK_VALUE)).astype(np.float32)
    return cur, np.stack([np.full_like(prev, MASK_VALUE), prev])


def _band_attention(src, col_q, prescaled, group, batch):
    p3 = src
    slabs, rows = BLOCK_SHAPES[group]
    streams = SLABS // slabs
    nb = SLAB_ROWS // rows
    has_prev = nb > 1
    cq, ck, cv = col_q, col_q + 1, col_q + 2
    bias_c, bias_p = _band_bias(group)

    def cur(col, width=D_MODEL):
        return pl.BlockSpec((slabs, rows, width), lambda b, r, j: (b * streams + r, j, col))

    def prev(col):
        return pl.BlockSpec((slabs, rows, D_MODEL), lambda b, r, j: (b * streams + r, jnp.maximum(j - 1, 0), col))

    table = (ATT_HEADS, ATT_BLOCK, ATT_BLOCK)
    in_specs = [pl.BlockSpec(table, lambda b, r, j: (0, 0, 0))]
    args = [jnp.asarray(bias_c)]
    if has_prev:
        in_specs += [pl.BlockSpec((None,) + table, lambda b, r, j: (jnp.minimum(j, 1), 0, 0, 0)),
                     cur(cq), prev(ck), cur(ck), prev(cv), cur(cv)]
        args += [jnp.asarray(bias_p)]
    else:
        in_specs += [cur(cq), cur(ck), cur(cv)]
    args += [p3] * (len(in_specs) - len(args))
    return pl.pallas_call(
        functools.partial(_band_attn_kernel, has_prev=has_prev, prescaled=prescaled),
        grid=(batch, streams, nb),
        in_specs=in_specs,
        out_specs=[cur(0), cur(0, 128)],
        out_shape=[jax.ShapeDtypeStruct((p3.shape[0], SLAB_ROWS, D_MODEL), F32),
                   jax.ShapeDtypeStruct((p3.shape[0], SLAB_ROWS, 128), F32)],
        compiler_params=_params("parallel", "parallel", "arbitrary"),
    )(*args)


def _cached_attn_kernel(new_ref, c0_ref, c1_ref, c2_ref, coef_ref, o_ref, lse_ref, *, dec_seq):
    caches = (c0_ref, c1_ref, c2_ref)
    shared = [d == 1 for d in DIL_RATES]
    pairs = [(g, i) for g in range(N_DIL) for i in range(dec_seq)]
    res = lambda g, i: 0 if shared[g] else i
    lane = lax.broadcasted_iota(jnp.int32, (ATT_HEADS, ATT_BLOCK), 1)
    scale = HEAD_DIM ** -0.5
    q = {(g, i): new_ref[i, g, 0] for g, i in pairs}

    def qk_body(r, carry):
        out = []
        for n, (g, i) in enumerate(pairs):
            col = jnp.sum(q[g, i] * caches[g][r, res(g, i), 0], axis=-1, keepdims=True)
            out.append(jnp.where(lane == r, col, carry[n]))
        return tuple(out)

    zero = jnp.zeros((ATT_HEADS, ATT_BLOCK), F32)
    s_all = lax.fori_loop(0, ATT_BLOCK, qk_body, (zero,) * len(pairs), unroll=16)

    ps, p_news, ls, ms = [], [], [], []
    for n, (g, i) in enumerate(pairs):
        coef = coef_ref[g]
        if shared[g]:
            back = (ATT_BLOCK + i - lane).astype(F32)
            s = jnp.where(lane >= i, s_all[n] * scale - coef * back, MASK_VALUE)
            s_new = [jnp.sum(q[g, i] * new_ref[i2, g, 1], axis=-1, keepdims=True) * scale
                     - coef[:, :1] * float(i - i2) for i2 in range(i + 1)]
        else:
            s = s_all[n] * scale - coef * (ATT_BLOCK - lane).astype(F32)
            s_new = [jnp.sum(q[g, i] * new_ref[i, g, 1], axis=-1, keepdims=True) * scale]
        m = jnp.max(s, axis=-1, keepdims=True)
        for sn in s_new:
            m = jnp.maximum(m, sn)
        p = jnp.exp(s - m)
        p_new = [jnp.exp(sn - m) for sn in s_new]
        l = jnp.sum(p, axis=-1, keepdims=True)
        for pn in p_new:
            l = l + pn
        ps.append(p)
        p_news.append(p_new)
        ls.append(l)
        ms.append(m)

    def pv_body(r, carry):
        out = []
        for n, (g, i) in enumerate(pairs):
            col = jnp.sum(jnp.where(lane == r, ps[n], 0.0), axis=-1, keepdims=True)
            out.append(carry[n] + col * caches[g][r, res(g, i), 1])
        return tuple(out)

    zero_o = jnp.zeros((ATT_HEADS, HEAD_DIM), F32)
    o_all = lax.fori_loop(0, ATT_BLOCK, pv_body, (zero_o,) * len(pairs), unroll=16)
    for n, (g, i) in enumerate(pairs):
        o = o_all[n]
        for i2, pn in enumerate(p_news[n]):
            o = o + pn * new_ref[i2 if shared[g] else i, g, 2]
        o_ref[g, i] = o / ls[n]
        lse_ref[g, i] = jnp.broadcast_to(ms[n] + jnp.log(ls[n]), (ATT_HEADS, 128))


def _cached_attention(att_s, caches):
    db, dec_seq = att_s.shape[:2]
    views, specs = [], []
    for g, cache in enumerate(caches):
        dil, window = DIL_RATES[g], DIL_WINDOWS[g]
        assert cache.shape[1] == window and window == dil * ATT_BLOCK
        assert dil == 1 or dec_seq <= dil
        nres = 1 if dil == 1 else dec_seq
        views.append(cache.reshape(db, ATT_BLOCK, dil, 2, ATT_HEADS, HEAD_DIM))
        specs.append(pl.BlockSpec((None, ATT_BLOCK, nres, 2, ATT_HEADS, HEAD_DIM), lambda b: (b, 0, 0, 0, 0, 0)))
    coef = np.stack([np.repeat((_alibi_slopes()[g] * DIL_RATES[g])[:, None], 128, axis=1) for g in range(N_DIL)])
    out = lambda w: pl.BlockSpec((N_DIL, None, dec_seq, ATT_HEADS, w), lambda b: (0, b, 0, 0, 0))
    return pl.pallas_call(
        functools.partial(_cached_attn_kernel, dec_seq=dec_seq),
        grid=(db,),
        in_specs=[pl.BlockSpec((None, dec_seq, N_DIL, 3, ATT_HEADS, HEAD_DIM), lambda b: (b, 0, 0, 0, 0, 0))]
        + specs + [pl.BlockSpec((N_DIL, ATT_HEADS, 128), lambda b: (0, 0, 0))],
        out_specs=[out(HEAD_DIM), out(128)],
        out_shape=[jax.ShapeDtypeStruct((N_DIL, db, dec_seq, ATT_HEADS, HEAD_DIM), F32),
                   jax.ShapeDtypeStruct((N_DIL, db, dec_seq, ATT_HEADS, 128), F32)],
        compiler_params=_params("parallel"),
    )(att_s, *views, jnp.asarray(coef.astype(np.float32)))


def _merge_kernel(a_ref, ga_ref, gb_ref, o0_ref, o1_ref, o2_ref, l0_ref, l1_ref, l2_ref,
                  x_ref, w_ref, g_ref, h_ref, hn_ref, merged_ref):
    l0, l1, l2 = l0_ref[...], l1_ref[...], l2_ref[...]
    mx = jnp.maximum(jnp.maximum(l0, l1), l2)
    e0, e1, e2 = jnp.exp(l0 - mx), jnp.exp(l1 - mx), jnp.exp(l2 - mx)
    den = e0 + e1 + e2
    w0, w1, w2 = e0 / den, e1 / den, e2 / den
    for h in range(ATT_HEADS):
        hs = slice(h * HEAD_DIM, (h + 1) * HEAD_DIM)
        b_out = (w0[:, h:h + 1] * o0_ref[:, hs] + w1[:, h:h + 1] * o1_ref[:, hs] + w2[:, h:h + 1] * o2_ref[:, hs])
        merged = ga_ref[:, hs] * a_ref[:, hs] + gb_ref[:, hs] * b_out
        merged_ref[:, hs] = merged.astype(merged_ref.dtype)
    h_new = x_ref[...] + jnp.dot(merged_ref[...], w_ref[...], preferred_element_type=F32)
    h_ref[...] = h_new
    hn_ref[...] = _rms(h_new, g_ref[...]).astype(hn_ref.dtype)


def _merge(a_out, p, outs, lses, x, w_out, norm_g, t, tm):
    row = lambda col: pl.BlockSpec((tm, D_MODEL), lambda i: (i, col))
    lrow = pl.BlockSpec((tm, 128), lambda i: (i, 0))
    return pl.pallas_call(
        _merge_kernel,
        grid=(t // tm,),
        in_specs=[row(0), row(COL_GATE), row(COL_GATE + 1), row(0), row(0), row(0), lrow, lrow, lrow,
                  row(0), pl.BlockSpec((D_MODEL, D_MODEL), lambda i: (0, 0)),
                  pl.BlockSpec((1, D_MODEL), lambda i: (0, 0))],
        out_specs=[row(0), row(0)],
        out_shape=[jax.ShapeDtypeStruct((x.shape[0], D_MODEL), F32), jax.ShapeDtypeStruct((t, D_MODEL), BF16)],
        scratch_shapes=[pltpu.VMEM((tm, D_MODEL), BF16)],
        compiler_params=_params("parallel"),
    )(a_out, p, p, outs[0], outs[1], outs[2], lses[0], lses[1], lses[2], x, w_out, norm_g.reshape(1, D_MODEL))


def _peer_score_kernel(hn_ref, wq_ref, keys_ref, s_ref):
    q = jnp.dot(hn_ref[...], wq_ref[...], preferred_element_type=F32).astype(BF16)
    nt = (((1,), (1,)), ((), ()))
    for hc in range(2 * PEER_HEADS):
        cs = slice(hc * PEER_HALF, (hc + 1) * PEER_HALF)
        s_ref[hc] = lax.dot_general(keys_ref[hc % 2], q[:, cs], nt, preferred_element_type=F32)


def _peer_scores(hn, w_q, sub_keys, tm):
    t = hn.shape[0]
    return pl.pallas_call(
        _peer_score_kernel,
        grid=(t // tm,),
        in_specs=[pl.BlockSpec((tm, D_MODEL), lambda i: (i, 0)),
                  pl.BlockSpec((D_MODEL, 2 * PEER_HEADS * PEER_HALF), lambda i: (0, 0)),
                  pl.BlockSpec((2, N_KEYS, PEER_HALF), lambda i: (0, 0, 0))],
        out_specs=pl.BlockSpec((2 * PEER_HEADS, N_KEYS, tm), lambda i: (0, 0, i)),
        out_shape=jax.ShapeDtypeStruct((2 * PEER_HEADS, N_KEYS, t), F32),
        compiler_params=_params("parallel"),
    )(hn, w_q, sub_keys)


def _take_top(arrays, order, count, sentinel):
    arrays = list(arrays)
    vals = [[] for _ in arrays]
    idxs = [[] for _ in arrays]
    for _ in range(count):
        for n, s in enumerate(arrays):
            m = jnp.max(s, axis=0, keepdims=True)
            pos = jnp.min(jnp.where(s == m, order, sentinel), axis=0, keepdims=True)
            vals[n].append(m)
            idxs[n].append(pos)
            arrays[n] = jnp.where(order == pos, NEG_INF, s)
    return vals, idxs


def _stack_rows(rows_list, krow):
    out = jnp.zeros(krow.shape, F32)
    for j, r in enumerate(rows_list):
        out = jnp.where(krow == float(j), r, out)
    return out


def _cand_layout():
    k = PEER_TOPK
    pieces = [("row_a", a, 16 if a == 0 else 8, 0, k // (a + 1)) for a in range(4)]
    pieces += [("col_b", 0, 16, 4, 16), ("col_b", 1, 8, 4, 8), ("col_b", 2, 8, 4, 5)]
    pos = []
    for kind, idx, rows, lo, hi in pieces:
        for r in range(rows):
            a, b = (idx, r) if kind == "row_a" else (r, idx)
            ok = lo <= r < hi and (a + 1) * (b + 1) <= k
            pos.append(a * k + b if ok else k * k)
    assert sorted(p for p in pos if p < k * k) == sorted(
        a * k + b for a in range(k) for b in range(k) if (a + 1) * (b + 1) <= k)
    return pieces, np.asarray(pos, np.float32)


def _route_kernel(s_ref, pos_ref, e1_ref, e2_ref, gate_ref):
    k = PEER_TOPK
    heads = s_ref.shape[0] // 2
    lanes = s_ref.shape[2]
    key_rank = lax.broadcasted_iota(jnp.int32, (N_KEYS, lanes), 0).astype(F32)
    vals, idxs = _take_top([s_ref[n] for n in range(2 * heads)], key_rank, k, float(N_KEYS))
    krow = lax.broadcasted_iota(jnp.int32, (k, lanes), 0).astype(F32)
    flat = pos_ref[...]
    cands, i1_all, i2_all = [], [], []
    for hd in range(heads):
        v1, v2 = vals[2 * hd], vals[2 * hd + 1]
        v1_all = _stack_rows(v1, krow)
        v2_all = _stack_rows(v2, krow)
        i1_all.append(_stack_rows(idxs[2 * hd], krow))
        i2_all.append(_stack_rows(idxs[2 * hd + 1], krow))
        parts = [v1[idx] + v2_all[:rows] if kind == "row_a" else v1_all[:rows] + v2[idx]
                 for kind, idx, rows, _, _ in _cand_layout()[0]]
        cands.append(jnp.where(flat < float(k * k), jnp.concatenate(parts, axis=0), NEG_INF))
    top_s, pos = _take_top(cands, flat, k, float(k * k))
    for hd in range(heads):
        e1, e2 = [], []
        for j in range(k):
            a = jnp.floor(pos[hd][j] * (1.0 / k))
            b = pos[hd][j] - a * k
            e1.append(jnp.sum(jnp.where(krow == a, i1_all[hd], 0.0), axis=0, keepdims=True))
            e2.append(jnp.sum(jnp.where(krow == b, i2_all[hd], 0.0), axis=0, keepdims=True))
        ex = jnp.exp(_stack_rows(top_s[hd], krow) - top_s[hd][0])
        rows = slice(hd * k, (hd + 1) * k)
        gate_ref[rows, :] = ex / jnp.sum(ex, axis=0, keepdims=True)
        e1_ref[rows, :] = _stack_rows(e1, krow)
        e2_ref[rows, :] = _stack_rows(e2, krow)


def _route(scores_t, tl, heads_per_step):
    t = scores_t.shape[2]
    kk = PEER_HEADS * PEER_TOPK
    out = pl.BlockSpec((heads_per_step * PEER_TOPK, tl), lambda i, h: (h, i))
    flat = jnp.asarray(np.repeat(_cand_layout()[1][:, None], tl, axis=1))
    return pl.pallas_call(
        _route_kernel,
        grid=(t // tl, PEER_HEADS // heads_per_step),
        in_specs=[pl.BlockSpec((2 * heads_per_step, N_KEYS, tl), lambda i, h: (h, 0, i)),
                  pl.BlockSpec(flat.shape, lambda i, h: (0, 0))],
        out_specs=[out, out, out],
        out_shape=[jax.ShapeDtypeStruct((kk, t), F32)] * 3,
        compiler_params=_params("parallel", "parallel"),
    )(scores_t, flat)


def _expert_weight_kernel(e1_ref, e2_ref, gate_ref, g_ref, e1t_ref, e2t_ref, gt_ref):
    e1t_ref[...] = e1_ref[...].T
    e2t_ref[...] = e2_ref[...].T
    gt_ref[...] = gate_ref[...].T
    kk = e1_ref.shape[0]
    key = lax.broadcasted_iota(jnp.int32, (N_KEYS, kk), 0).astype(F32)
    nt = (((1,), (1,)), ((), ()))

    def body(t, carry):
        r1 = e1t_ref[pl.ds(t, 1), :]
        r2 = e2t_ref[pl.ds(t, 1), :]
        gr = gt_ref[pl.ds(t, 1), :]
        a_t = jnp.where(key == r1, 1.0, 0.0).astype(BF16)
        b_t = jnp.where(key == r2, gr, 0.0).astype(BF16)
        g_ref[t] = lax.dot_general(a_t, b_t, nt, preferred_element_type=F32).astype(g_ref.dtype)
        return carry

    lax.fori_loop(0, g_ref.shape[0], body, 0, unroll=32)


def _expert_weights(e1, e2, gate, tl):
    kk, t = e1.shape
    slot = pl.BlockSpec((kk, tl), lambda i: (0, i))
    return pl.pallas_call(
        _expert_weight_kernel,
        grid=(t // tl,),
        in_specs=[slot, slot, slot],
        out_specs=pl.BlockSpec((tl, N_KEYS, N_KEYS), lambda i: (i, 0, 0)),
        out_shape=jax.ShapeDtypeStruct((t, N_KEYS, N_KEYS), F32),
        scratch_shapes=[pltpu.VMEM((tl, kk), F32)] * 3,
        compiler_params=_params("parallel"),
    )(e1, e2, gate)


def _peer_kernel(hn_ref, u_ref, v_ref, g_ref, y_ref):
    e = pl.program_id(1)
    nt = (((1,), (1,)), ((), ()))
    hk = lax.dot_general(hn_ref[...], u_ref[...], nt, preferred_element_type=F32)
    g = jnp.swapaxes(g_ref[...], 0, 1)
    act = [(jax.nn.gelu(hk[:, a * N_KEYS:(a + 1) * N_KEYS]) * g[a]).astype(BF16) for a in range(g.shape[0])]
    out = jnp.dot(jnp.concatenate(act, axis=-1), v_ref[...], preferred_element_type=F32)

    @pl.when(e == 0)
    def _():
        y_ref[...] = out

    @pl.when(e > 0)
    def _():
        y_ref[...] += out


def _peer(hn, u, v, g, t, rows_out, tm, te):
    row = pl.BlockSpec((tm, D_MODEL), lambda i, e: (i, 0))
    tab = pl.BlockSpec((te, D_MODEL), lambda i, e: (e, 0))
    return pl.pallas_call(
        _peer_kernel,
        grid=(t // tm, N_EXPERTS // te),
        in_specs=[row, tab, tab, pl.BlockSpec((tm, te // N_KEYS, N_KEYS), lambda i, e: (i, e, 0))],
        out_specs=row,
        out_shape=jax.ShapeDtypeStruct((rows_out, D_MODEL), F32),
        compiler_params=_params("parallel", "arbitrary"),
    )(hn, u, v, g)


def _final_norm_prompt_kernel(h_ref, f_ref, g_ref, y_ref):
    y = [_rms(h_ref[rr % 4, rr // 4] + f_ref[rr % 4, rr // 4], g_ref[...]) for rr in range(8)]
    y_ref[...] = jnp.swapaxes(jnp.stack(y, axis=0), 0, 1)


def _final_norm_prompt(h, f, g, batch, seq):
    rows = 64
    h5 = h.reshape(batch + 1, 4, 4, SLAB_ROWS, D_MODEL)
    f5 = f.reshape(batch + 1, 4, 4, SLAB_ROWS, D_MODEL)
    blk = pl.BlockSpec((None, 4, 2, rows, D_MODEL), lambda b, hh, j: (b, 0, hh, j, 0))
    y = pl.pallas_call(
        _final_norm_prompt_kernel,
        grid=(batch, 2, SLAB_ROWS // rows),
        in_specs=[blk, blk, pl.BlockSpec((1, D_MODEL), lambda b, hh, j: (0, 0))],
        out_specs=pl.BlockSpec((None, rows, 8, D_MODEL), lambda b, hh, j: (b, j, hh, 0)),
        out_shape=jax.ShapeDtypeStruct((batch, SLAB_ROWS, SLABS, D_MODEL), F32),
        compiler_params=_params("parallel", "parallel", "parallel"),
    )(h5, f5, g.reshape(1, D_MODEL))
    return y.reshape(batch, seq, D_MODEL)


def _final_norm_rows_kernel(h_ref, f_ref, g_ref, y_ref):
    y_ref[...] = _rms(h_ref[...] + f_ref[...], g_ref[...])


def _final_norm_rows(h, f, g, row_block, rows):
    blk = pl.BlockSpec((rows, D_MODEL), lambda i: (row_block, 0))
    return pl.pallas_call(
        _final_norm_rows_kernel,
        grid=(1,),
        in_specs=[blk, blk, pl.BlockSpec((1, D_MODEL), lambda i: (0, 0))],
        out_specs=pl.BlockSpec((rows, D_MODEL), lambda i: (0, 0)),
        out_shape=jax.ShapeDtypeStruct((rows, D_MODEL), F32),
        compiler_params=_params("arbitrary"),
    )(h, f, g.reshape(1, D_MODEL))


def _kv_prompt_kernel(*refs):
    ins, outs = refs[:2 * N_DIL], refs[2 * N_DIL:]
    for g in range(N_DIL):
        for kv in range(2):
            src = ins[2 * g + kv]
            heads = [src[0, :, h * HEAD_DIM:(h + 1) * HEAD_DIM] for h in range(ATT_HEADS)]
            outs[g][:, kv, :, :] = jnp.swapaxes(jnp.stack(heads, axis=0), 0, 1)


def _kv_prompt(p3, batch, seq):
    in_specs, out_specs, out_shape = [], [], []
    residue = lambda s: (s % 4) * 4 + s // 4
    for g in range(N_DIL):
        steps = min(DIL_WINDOWS[g], seq) // SLABS
        last = SLAB_ROWS // steps - 1
        for kv in range(2):
            col = COL_ATT + 3 * g + 1 + kv
            in_specs.append(pl.BlockSpec((1, steps, D_MODEL),
                                         lambda b, s, last=last, col=col: (b * SLABS + s, last, col)))
        out_specs.append(pl.BlockSpec((None, steps, None, 2, ATT_HEADS, HEAD_DIM),
                                      lambda b, s: (b, 0, residue(s), 0, 0, 0)))
        out_shape.append(jax.ShapeDtypeStruct((batch, steps, SLABS, 2, ATT_HEADS, HEAD_DIM), F32))
    outs = pl.pallas_call(
        _kv_prompt_kernel,
        grid=(batch, SLABS),
        in_specs=in_specs,
        out_specs=out_specs,
        out_shape=out_shape,
        compiler_params=_params("parallel", "parallel"),
    )(*([p3] * len(in_specs)))
    return [o.reshape(1, batch, -1, 2, ATT_HEADS, HEAD_DIM) for o in outs]


def _row_tile(t, candidates):
    for c in candidates:
        if t % c == 0:
            return c
    raise ValueError(f"no row tile for {t} tokens")


def kernel(x_prompt, x_sample, cache_kv_w128, cache_kv_w512, cache_kv_w2048, norm_mix_g, w_in, sgu_norm_g, sgu_w, sgu_b, w_out, norm_ffn_g, peer_w_q, peer_sub_keys, peer_u, peer_v, norm_final_g):
    batch, seq, _ = x_prompt.shape
    db, ds, _ = x_sample.shape
    assert w_in.shape[0] == 1 and db * ds == CHUNK and seq == SLABS * SLAB_ROWS
    caches = (cache_kv_w128, cache_kv_w512, cache_kv_w2048)
    tp, ts = batch * seq, db * ds
    t = tp + ts
    n_slabs = t // SLAB_ROWS
    tm_big = _row_tile(t, (1040, 640, 128))
    tm_lane = _row_tile(t, (640, 128))

    xs = x_sample.reshape(ts, D_MODEL)
    xl, xn = _permute_norm(x_prompt, norm_mix_g[0])
    xl = lax.dynamic_update_slice(xl, xs, (tp, 0))
    xn = lax.dynamic_update_slice(xn, _rmsnorm_bf16(xs, norm_mix_g[0]), (tp, 0))

    p, att16 = _inproj(xn, _to_bf16(w_in[0], D_MODEL, 1024), t, tm_big, 1024)
    p3 = p.reshape(n_slabs, SLAB_ROWS, N_COL_BLOCKS * D_MODEL)
    att16_3 = att16.reshape(n_slabs, SLAB_ROWS, att16.shape[1])

    w_tril = sgu_w[0] * jnp.tril(jnp.ones((CHUNK, CHUNK), F32))
    tau = _block_steps(0)
    w_p = w_tril[:, tau][:, :, tau].astype(BF16)
    b_p = sgu_b[0][:, tau][..., None]
    w_s = jnp.einsum("bc,gis->gbics", jnp.eye(db, dtype=F32), w_tril[:, :ds, :ds]).reshape(SGU_GROUPS, ts, ts)
    b_s = jnp.tile(sgu_b[0][:, :ds], (1, db))[..., None]
    a_out = _sgu_prompt(p3, sgu_norm_g[0], w_p, b_p, batch).reshape(t, D_MODEL)
    a_s, vn_s = _sgu_sample(p3, sgu_norm_g[0], w_s.astype(BF16), b_s, tp // SLAB_ROWS)
    a_out = lax.dynamic_update_slice(a_out, a_s, (tp, 0))

    att_s = p[tp:, COL_ATT * D_MODEL:COL_GATE * D_MODEL].reshape(db, ds, N_DIL, 3, ATT_HEADS, HEAD_DIM)
    outs, lses = [], []
    o_s, lse_s = _cached_attention(att_s, [c[0] for c in caches])
    for g in range(N_DIL):
        if g == 0:
            o_p, lse_p = _band_attention(p3, COL_ATT, False, g, batch)
        else:
            o_p, lse_p = _band_attention(att16_3, 3 * (g - 1), True, g, batch)
        lse_g = jnp.pad(lse_s[g, ..., 0].reshape(ts, ATT_HEADS), ((0, 0), (0, 128 - ATT_HEADS)))
        outs.append(lax.dynamic_update_slice(o_p.reshape(t, D_MODEL), o_s[g].reshape(ts, D_MODEL), (tp, 0)))
        lses.append(lax.dynamic_update_slice(lse_p.reshape(t, 128), lse_g, (tp, 0)))

    h, hn = _merge(a_out, p, outs, lses, xl, _to_bf16(w_out[0], 1024, D_MODEL), norm_ffn_g[0], t, 128)

    scores_t = _peer_scores(hn, _to_bf16(peer_w_q[0], 1024, D_MODEL), peer_sub_keys[0].astype(BF16), tm_lane)
    e1, e2, gate = _route(scores_t, 128, 2)
    g_dense = _expert_weights(e1, e2, gate, 128)
    f = _peer(hn, _to_bf16(peer_u[0], 1024, D_MODEL), _to_bf16(peer_v[0], 1024, D_MODEL), g_dense,
              t, h.shape[0], tm_lane, 1024)

    y_prompt = _final_norm_prompt(h, f, norm_final_g, batch, seq)
    y_sample = _final_norm_rows(h, f, norm_final_g, tp // ts, ts).reshape(db, ds, D_MODEL)
    kv_prompt = _kv_prompt(p3, batch, seq)
    kv_sample = [att_s[:, :, g, 1:3][None] for g in range(N_DIL)]
    sgu_v_sample = vn_s.reshape(1, db, ds, D_MODEL)
    return (y_prompt, y_sample, kv_prompt[0], kv_prompt[1], kv_prompt[2],
            kv_sample[0], kv_sample[1], kv_sample[2], sgu_v_sample)
```

```python
import functools

import numpy as np
import jax
import jax.numpy as jnp
from jax import lax
from jax.experimental import pallas as pl
from jax.experimental.pallas import tpu as pltpu

F32 = jnp.float32
BF16 = jnp.bfloat16

D_MODEL = 2048
HEAD_DIM = 128
ATT_HEADS = D_MODEL // HEAD_DIM
N_DIL = 3
DIL_WINDOWS = (128, 512, 2048)
DIL_RATES = (1, 4, 16)
ATT_BLOCK = 128
SLABS = DIL_RATES[-1]
SLAB_ROWS = 128
SGU_GROUPS = 8
SGU_GROUP_DIM = D_MODEL // SGU_GROUPS
CHUNK = 128
N_COL_BLOCKS = 13
COL_U, COL_V, COL_ATT, COL_GATE = 0, 1, 2, 11
COL_ATT16 = COL_ATT + 3
PEER_HEADS = 8
PEER_TOPK = 16
N_KEYS = 128
N_EXPERTS = N_KEYS * N_KEYS
PEER_HALF = 128
NORM_EPS = 1e-6
MASK_VALUE = -1e30
NEG_INF = float("-inf")
VMEM_LIMIT = 56 * 1024 * 1024

BLOCK_SHAPES = ((16, 8), (4, 32), (1, 128))


def _alibi_slopes():
    n = N_DIL * ATT_HEADS
    e = np.arange(1, n + 1, dtype=np.float32)
    return np.exp2(np.float32(-8.0) * e / np.float32(n)).astype(np.float32).reshape(N_DIL, ATT_HEADS)


def _block_steps(group):
    slabs, rows = BLOCK_SHAPES[group]
    n = np.arange(slabs * rows)
    s, j = n // rows, n % rows
    if group == 0:
        return j * 16 + (s % 4) * 4 + s // 4
    if group == 1:
        return j * 4 + s
    return j


def _params(*sem):
    return pltpu.CompilerParams(dimension_semantics=sem, vmem_limit_bytes=VMEM_LIMIT)


def _rms(x, g):
    return x * lax.rsqrt(jnp.mean(x * x, axis=-1, keepdims=True) + NORM_EPS) * g


def _permute_norm_kernel(x_ref, g_ref, xl_ref, xn_ref):
    xs = jnp.swapaxes(x_ref[...], 0, 1)
    for rr in range(8):
        x = xs[rr]
        xl_ref[rr % 4, rr // 4] = x
        xn_ref[rr % 4, rr // 4] = _rms(x, g_ref[...]).astype(xn_ref.dtype)


def _permute_norm(x_prompt, g):
    batch, seq, _ = x_prompt.shape
    x4 = x_prompt.reshape(batch, SLAB_ROWS, SLABS, D_MODEL)
    out = pl.BlockSpec((None, 4, 2, SLAB_ROWS, D_MODEL), lambda b, h: (b, 0, h, 0, 0))
    xl, xn = pl.pallas_call(
        _permute_norm_kernel,
        grid=(batch, 2),
        in_specs=[pl.BlockSpec((None, SLAB_ROWS, 8, D_MODEL), lambda b, h: (b, 0, h, 0)),
                  pl.BlockSpec((1, D_MODEL), lambda b, h: (0, 0))],
        out_specs=[out, out],
        out_shape=[jax.ShapeDtypeStruct((batch + 1, 4, 4, SLAB_ROWS, D_MODEL), F32),
                   jax.ShapeDtypeStruct((batch + 1, 4, 4, SLAB_ROWS, D_MODEL), BF16)],
        compiler_params=_params("parallel", "parallel"),
    )(x4, g.reshape(1, D_MODEL))
    rows = (batch + 1) * seq
    return xl.reshape(rows, D_MODEL), xn.reshape(rows, D_MODEL)


def _rmsnorm_kernel(x_ref, g_ref, o_ref):
    o_ref[...] = _rms(x_ref[...], g_ref[...]).astype(o_ref.dtype)


def _rmsnorm_bf16(x, g):
    return pl.pallas_call(
        _rmsnorm_kernel,
        out_shape=jax.ShapeDtypeStruct(x.shape, BF16),
    )(x, g.reshape(1, D_MODEL))


def _cast_kernel(x_ref, o_ref):
    o_ref[...] = x_ref[...].astype(o_ref.dtype)


def _to_bf16(w, rows, cols):
    r, c = w.shape
    return pl.pallas_call(
        _cast_kernel,
        grid=(r // rows, c // cols),
        in_specs=[pl.BlockSpec((rows, cols), lambda i, j: (i, j))],
        out_specs=pl.BlockSpec((rows, cols), lambda i, j: (i, j)),
        out_shape=jax.ShapeDtypeStruct((r, c), BF16),
        compiler_params=_params("parallel", "parallel"),
    )(w)


def _inproj_kernel(x_ref, w_ref, o_ref, att_ref, *, blocks_per_col):
    cb = pl.program_id(1) // blocks_per_col
    p = jnp.dot(x_ref[...], w_ref[...], preferred_element_type=F32)

    @pl.when(cb < COL_ATT)
    def _():
        o_ref[...] = jax.nn.gelu(p)

    @pl.when((cb >= COL_ATT) & (cb < COL_ATT16))
    def _():
        o_ref[...] = p

    @pl.when((cb >= COL_ATT16) & (cb < COL_GATE))
    def _():
        o_ref[...] = p
        is_q = (cb - COL_ATT) % 3 == 0
        att_ref[...] = (p * jnp.where(is_q, HEAD_DIM ** -0.5, 1.0)).astype(att_ref.dtype)

    @pl.when(cb >= COL_GATE)
    def _():
        att_ref[...] = jax.nn.sigmoid(p).astype(att_ref.dtype)


def _inproj(xn, w, t, tm, tn):
    n = w.shape[1]
    per_col = D_MODEL // tn
    n_f32 = COL_GATE * per_col
    n_b16 = (N_COL_BLOCKS - COL_ATT16) * per_col
    f32_block = lambda j: jnp.minimum(j, n_f32 - 1)
    b16_block = lambda j: jnp.maximum(j - COL_ATT16 * per_col, 0)
    return pl.pallas_call(
        functools.partial(_inproj_kernel, blocks_per_col=per_col),
        grid=(t // tm, n // tn),
        in_specs=[pl.BlockSpec((tm, D_MODEL), lambda i, j: (i, 0)),
                  pl.BlockSpec((D_MODEL, tn), lambda i, j: (0, j))],
        out_specs=[pl.BlockSpec((tm, tn), lambda i, j: (i, f32_block(j))),
                   pl.BlockSpec((tm, tn), lambda i, j: (i, b16_block(j)))],
        out_shape=[jax.ShapeDtypeStruct((t, n_f32 * tn), F32),
                   jax.ShapeDtypeStruct((t, n_b16 * tn), BF16)],
        compiler_params=_params("parallel", "arbitrary"),
    )(xn, w)


def _sgu_kernel(u_ref, v_ref, g_ref, w_ref, b_ref, a_ref, vn_ref):
    v = v_ref[...].reshape(CHUNK, D_MODEL)
    u = u_ref[...].reshape(CHUNK, D_MODEL)
    vc = v - jnp.mean(v, axis=-1, keepdims=True)
    vn = vc * lax.rsqrt(jnp.mean(vc * vc, axis=-1, keepdims=True) + NORM_EPS) * g_ref[...]
    if vn_ref is not None:
        vn_ref[...] = vn
    cols = []
    for g in range(SGU_GROUPS):
        cs = slice(g * SGU_GROUP_DIM, (g + 1) * SGU_GROUP_DIM)
        mix = jnp.dot(w_ref[g], vn[:, cs].astype(BF16), preferred_element_type=F32) + b_ref[g]
        cols.append(u[:, cs] * mix)
    a_ref[...] = jnp.concatenate(cols, axis=-1).reshape(a_ref.shape)


def _sgu_prompt(p3, sgu_norm_g, w, b, batch):
    slabs, rows = BLOCK_SHAPES[0]
    n_chunks = SLAB_ROWS // rows
    blk = lambda col: pl.BlockSpec((slabs, rows, D_MODEL), lambda bb, c: (bb, c, col))
    return pl.pallas_call(
        lambda u, v, g, ww, bb, a: _sgu_kernel(u, v, g, ww, bb, a, None),
        grid=(batch, n_chunks),
        in_specs=[blk(COL_U), blk(COL_V),
                  pl.BlockSpec((1, D_MODEL), lambda bb, c: (0, 0)),
                  pl.BlockSpec((SGU_GROUPS, CHUNK, CHUNK), lambda bb, c: (0, 0, 0)),
                  pl.BlockSpec((SGU_GROUPS, CHUNK, 1), lambda bb, c: (0, 0, 0))],
        out_specs=blk(0),
        out_shape=jax.ShapeDtypeStruct((p3.shape[0], SLAB_ROWS, D_MODEL), F32),
        compiler_params=_params("parallel", "parallel"),
    )(p3, p3, sgu_norm_g.reshape(1, D_MODEL), w, b)


def _sgu_sample(p3, sgu_norm_g, w, b, slab):
    blk = lambda col: pl.BlockSpec((1, SLAB_ROWS, D_MODEL), lambda i: (slab, 0, col))
    return pl.pallas_call(
        _sgu_kernel,
        grid=(1,),
        in_specs=[blk(COL_U), blk(COL_V),
                  pl.BlockSpec((1, D_MODEL), lambda i: (0, 0)),
                  pl.BlockSpec((SGU_GROUPS, CHUNK, CHUNK), lambda i: (0, 0, 0)),
                  pl.BlockSpec((SGU_GROUPS, CHUNK, 1), lambda i: (0, 0, 0))],
        out_specs=[pl.BlockSpec((CHUNK, D_MODEL), lambda i: (0, 0)),
                   pl.BlockSpec((CHUNK, D_MODEL), lambda i: (0, 0))],
        out_shape=[jax.ShapeDtypeStruct((CHUNK, D_MODEL), F32), jax.ShapeDtypeStruct((CHUNK, D_MODEL), F32)],
        compiler_params=_params("arbitrary"),
    )(p3, p3, sgu_norm_g.reshape(1, D_MODEL), w, b)


def _band_attn_kernel(*refs, has_prev, prescaled):
    if has_prev:
        bc_ref, bp_ref, q_ref, kp_ref, kc_ref, vp_ref, vc_ref, o_ref, lse_ref = refs
    else:
        bc_ref, q_ref, kc_ref, vc_ref, o_ref, lse_ref = refs
    n = ATT_BLOCK
    scale = HEAD_DIM ** -0.5
    nt = (((1,), (1,)), ((), ()))
    heads = range(ATT_HEADS)
    hs = [slice(h * HEAD_DIM, (h + 1) * HEAD_DIM) for h in heads]
    ld = lambda ref, h: ref[:, :, hs[h]].reshape(n, HEAD_DIM)
    q = [ld(q_ref, h) if prescaled else (ld(q_ref, h) * scale).astype(BF16) for h in heads]
    s_c = [lax.dot_general(q[h], ld(kc_ref, h).astype(BF16), nt, preferred_element_type=F32) + bc_ref[h]
           for h in heads]
    m = [jnp.max(s_c[h], axis=-1, keepdims=True) for h in heads]
    if has_prev:
        s_p = [lax.dot_general(q[h], ld(kp_ref, h).astype(BF16), nt, preferred_element_type=F32) + bp_ref[h]
               for h in heads]
        m = [jnp.maximum(m[h], jnp.max(s_p[h], axis=-1, keepdims=True)) for h in heads]
    p_c = [jnp.exp(s_c[h] - m[h]) for h in heads]
    l = [jnp.sum(p_c[h], axis=-1, keepdims=True) for h in heads]
    o = [jnp.dot(p_c[h].astype(BF16), ld(vc_ref, h).astype(BF16), preferred_element_type=F32) for h in heads]
    if has_prev:
        p_p = [jnp.exp(s_p[h] - m[h]) for h in heads]
        l = [l[h] + jnp.sum(p_p[h], axis=-1, keepdims=True) for h in heads]
        o = [o[h] + jnp.dot(p_p[h].astype(BF16), ld(vp_ref, h).astype(BF16), preferred_element_type=F32)
             for h in heads]
    lane = lax.broadcasted_iota(jnp.int32, (n, 128), 1)
    lse = jnp.zeros((n, 128), F32)
    for h in heads:
        lse = jnp.where(lane == h, m[h] + jnp.log(l[h]), lse)
    o_ref[...] = jnp.concatenate([o[h] / l[h] for h in heads], axis=-1).reshape(o_ref.shape).astype(o_ref.dtype)
    lse_ref[...] = lse.reshape(lse_ref.shape)


def _band_bias(group):
    steps = _block_steps(group)
    back = (steps[:, None] - steps[None, :]).astype(np.float32)
    coef = (_alibi_slopes()[group] * np.float32(DIL_RATES[group]))[:, None, None]
    cur = np.where(back >= 0, -(coef * back), np.float32(MASK_VALUE)).astype(np.float32)
    back_p = back + np.float32(ATT_BLOCK)
    prev = np.where(back_p <= ATT_BLOCK, -(coef * back_p), np.float32(MASK_VALUE)).astype(np.float32)
    return cur, np.stack([np.full_like(prev, MASK_VALUE), prev])


def _band_attention(src, col_q, prescaled, group, batch):
    p3 = src
    slabs, rows = BLOCK_SHAPES[group]
    streams = SLABS // slabs
    nb = SLAB_ROWS // rows
    has_prev = nb > 1
    cq, ck, cv = col_q, col_q + 1, col_q + 2
    bias_c, bias_p = _band_bias(group)

    def cur(col, width=D_MODEL):
        return pl.BlockSpec((slabs, rows, width), lambda b, r, j: (b * streams + r, j, col))

    def prev(col):
        return pl.BlockSpec((slabs, rows, D_MODEL), lambda b, r, j: (b * streams + r, jnp.maximum(j - 1, 0), col))

    table = (ATT_HEADS, ATT_BLOCK, ATT_BLOCK)
    in_specs = [pl.BlockSpec(table, lambda b, r, j: (0, 0, 0))]
    args = [jnp.asarray(bias_c)]
    if has_prev:
        in_specs += [pl.BlockSpec((None,) + table, lambda b, r, j: (jnp.minimum(j, 1), 0, 0, 0)),
                     cur(cq), prev(ck), cur(ck), prev(cv), cur(cv)]
        args += [jnp.asarray(bias_p)]
    else:
        in_specs += [cur(cq), cur(ck), cur(cv)]
    args += [p3] * (len(in_specs) - len(args))
    return pl.pallas_call(
        functools.partial(_band_attn_kernel, has_prev=has_prev, prescaled=prescaled),
        grid=(batch, streams, nb),
        in_specs=in_specs,
        out_specs=[cur(0), cur(0, 128)],
        out_shape=[jax.ShapeDtypeStruct((p3.shape[0], SLAB_ROWS, D_MODEL), BF16 if prescaled else F32),
                   jax.ShapeDtypeStruct((p3.shape[0], SLAB_ROWS, 128), F32)],
        compiler_params=_params("parallel", "parallel", "arbitrary"),
    )(*args)


def _cached_attn_kernel(new_ref, c0_ref, c1_ref, c2_ref, coef_ref, o_ref, lse_ref, *, dec_seq):
    caches = (c0_ref, c1_ref, c2_ref)
    shared = [d == 1 for d in DIL_RATES]
    pairs = [(g, i) for g in range(N_DIL) for i in range(dec_seq)]
    res = lambda g, i: 0 if shared[g] else i
    lane = lax.broadcasted_iota(jnp.int32, (ATT_HEADS, ATT_BLOCK), 1)
    scale = HEAD_DIM ** -0.5
    q = {(g, i): new_ref[i, g, 0] for g, i in pairs}

    def qk_body(r, carry):
        out = []
        for n, (g, i) in enumerate(pairs):
            col = jnp.sum(q[g, i] * caches[g][r, res(g, i), 0], axis=-1, keepdims=True)
            out.append(jnp.where(lane == r, col, carry[n]))
        return tuple(out)

    zero = jnp.zeros((ATT_HEADS, ATT_BLOCK), F32)
    s_all = lax.fori_loop(0, ATT_BLOCK, qk_body, (zero,) * len(pairs), unroll=16)

    ps, p_news, ls, ms = [], [], [], []
    for n, (g, i) in enumerate(pairs):
        coef = coef_ref[g]
        if shared[g]:
            back = (ATT_BLOCK + i - lane).astype(F32)
            s = jnp.where(lane >= i, s_all[n] * scale - coef * back, MASK_VALUE)
            s_new = [jnp.sum(q[g, i] * new_ref[i2, g, 1], axis=-1, keepdims=True) * scale
                     - coef[:, :1] * float(i - i2) for i2 in range(i + 1)]
        else:
            s = s_all[n] * scale - coef * (ATT_BLOCK - lane).astype(F32)
            s_new = [jnp.sum(q[g, i] * new_ref[i, g, 1], axis=-1, keepdims=True) * scale]
        m = jnp.max(s, axis=-1, keepdims=True)
        for sn in s_new:
            m = jnp.maximum(m, sn)
        p = jnp.exp(s - m)
        p_new = [jnp.exp(sn - m) for sn in s_new]
        l = jnp.sum(p, axis=-1, keepdims=True)
        for pn in p_new:
            l = l + pn
        ps.append(p)
        p_news.append(p_new)
        ls.append(l)
        ms.append(m)

    def pv_body(r, carry):
        out = []
        for n, (g, i) in enumerate(pairs):
            col = jnp.sum(jnp.where(lane == r, ps[n], 0.0), axis=-1, keepdims=True)
            out.append(carry[n] + col * caches[g][r, res(g, i), 1])
        return tuple(out)

    zero_o = jnp.zeros((ATT_HEADS, HEAD_DIM), F32)
    o_all = lax.fori_loop(0, ATT_BLOCK, pv_body, (zero_o,) * len(pairs), unroll=16)
    for n, (g, i) in enumerate(pairs):
        o = o_all[n]
        for i2, pn in enumerate(p_news[n]):
            o = o + pn * new_ref[i2 if shared[g] else i, g, 2]
        o_ref[g, i] = o / ls[n]
        lse_ref[g, i] = jnp.broadcast_to(ms[n] + jnp.log(ls[n]), (ATT_HEADS, 128))


def _cached_attention(att_s, caches):
    db, dec_seq = att_s.shape[:2]
    views, specs = [], []
    for g, cache in enumerate(caches):
        dil, window = DIL_RATES[g], DIL_WINDOWS[g]
        assert cache.shape[1] == window and window == dil * ATT_BLOCK
        assert dil == 1 or dec_seq <= dil
        nres = 1 if dil == 1 else dec_seq
        views.append(cache.reshape(db, ATT_BLOCK, dil, 2, ATT_HEADS, HEAD_DIM))
        specs.append(pl.BlockSpec((None, ATT_BLOCK, nres, 2, ATT_HEADS, HEAD_DIM), lambda b: (b, 0, 0, 0, 0, 0)))
    coef = np.stack([np.repeat((_alibi_slopes()[g] * DIL_RATES[g])[:, None], 128, axis=1) for g in range(N_DIL)])
    out = lambda w: pl.BlockSpec((N_DIL, None, dec_seq, ATT_HEADS, w), lambda b: (0, b, 0, 0, 0))
    return pl.pallas_call(
        functools.partial(_cached_attn_kernel, dec_seq=dec_seq),
        grid=(db,),
        in_specs=[pl.BlockSpec((None, dec_seq, N_DIL, 3, ATT_HEADS, HEAD_DIM), lambda b: (b, 0, 0, 0, 0, 0))]
        + specs + [pl.BlockSpec((N_DIL, ATT_HEADS, 128), lambda b: (0, 0, 0))],
        out_specs=[out(HEAD_DIM), out(128)],
        out_shape=[jax.ShapeDtypeStruct((N_DIL, db, dec_seq, ATT_HEADS, HEAD_DIM), F32),
                   jax.ShapeDtypeStruct((N_DIL, db, dec_seq, ATT_HEADS, 128), F32)],
        compiler_params=_params("parallel"),
    )(att_s, *views, jnp.asarray(coef.astype(np.float32)))


def _merge_kernel(a_ref, ga_ref, gb_ref, o0_ref, o1_ref, o2_ref, l0_ref, l1_ref, l2_ref,
                  x_ref, w_ref, g_ref, h_ref, hn_ref, merged_ref):
    l0, l1, l2 = l0_ref[...], l1_ref[...], l2_ref[...]
    mx = jnp.maximum(jnp.maximum(l0, l1), l2)
    e0, e1, e2 = jnp.exp(l0 - mx), jnp.exp(l1 - mx), jnp.exp(l2 - mx)
    den = e0 + e1 + e2
    w0, w1, w2 = e0 / den, e1 / den, e2 / den
    for h in range(ATT_HEADS):
        hs = slice(h * HEAD_DIM, (h + 1) * HEAD_DIM)
        b_out = (w0[:, h:h + 1] * o0_ref[:, hs] + w1[:, h:h + 1] * o1_ref[:, hs].astype(F32)
                 + w2[:, h:h + 1] * o2_ref[:, hs].astype(F32))
        merged = ga_ref[:, hs].astype(F32) * a_ref[:, hs] + gb_ref[:, hs].astype(F32) * b_out
        merged_ref[:, hs] = merged.astype(merged_ref.dtype)
    h_new = x_ref[...] + jnp.dot(merged_ref[...], w_ref[...], preferred_element_type=F32)
    h_ref[...] = h_new
    hn_ref[...] = _rms(h_new, g_ref[...]).astype(hn_ref.dtype)


def _merge(a_out, gates, gate_col, outs, lses, x, w_out, norm_g, t, tm):
    row = lambda col: pl.BlockSpec((tm, D_MODEL), lambda i: (i, col))
    lrow = pl.BlockSpec((tm, 128), lambda i: (i, 0))
    p = gates
    return pl.pallas_call(
        _merge_kernel,
        grid=(t // tm,),
        in_specs=[row(0), row(gate_col), row(gate_col + 1), row(0), row(0), row(0), lrow, lrow, lrow,
                  row(0), pl.BlockSpec((D_MODEL, D_MODEL), lambda i: (0, 0)),
                  pl.BlockSpec((1, D_MODEL), lambda i: (0, 0))],
        out_specs=[row(0), row(0)],
        out_shape=[jax.ShapeDtypeStruct((x.shape[0], D_MODEL), F32), jax.ShapeDtypeStruct((t, D_MODEL), BF16)],
        scratch_shapes=[pltpu.VMEM((tm, D_MODEL), BF16)],
        compiler_params=_params("parallel"),
    )(a_out, p, p, outs[0], outs[1], outs[2], lses[0], lses[1], lses[2], x, w_out, norm_g.reshape(1, D_MODEL))


def _peer_score_kernel(hn_ref, wq_ref, keys_ref, s_ref):
    q = jnp.dot(hn_ref[...], wq_ref[...], preferred_element_type=F32).astype(BF16)
    nt = (((1,), (1,)), ((), ()))
    for hc in range(2 * PEER_HEADS):
        cs = slice(hc * PEER_HALF, (hc + 1) * PEER_HALF)
        s_ref[hc] = lax.dot_general(keys_ref[hc % 2], q[:, cs], nt, preferred_element_type=F32)


def _peer_scores(hn, w_q, sub_keys, tm):
    t = hn.shape[0]
    return pl.pallas_call(
        _peer_score_kernel,
        grid=(t // tm,),
        in_specs=[pl.BlockSpec((tm, D_MODEL), lambda i: (i, 0)),
                  pl.BlockSpec((D_MODEL, 2 * PEER_HEADS * PEER_HALF), lambda i: (0, 0)),
                  pl.BlockSpec((2, N_KEYS, PEER_HALF), lambda i: (0, 0, 0))],
        out_specs=pl.BlockSpec((2 * PEER_HEADS, N_KEYS, tm), lambda i: (0, 0, i)),
        out_shape=jax.ShapeDtypeStruct((2 * PEER_HEADS, N_KEYS, t), F32),
        compiler_params=_params("parallel"),
    )(hn, w_q, sub_keys)


def _take_top(arrays, order, count, sentinel):
    arrays = list(arrays)
    vals = [[] for _ in arrays]
    idxs = [[] for _ in arrays]
    for _ in range(count):
        for n, s in enumerate(arrays):
            m = jnp.max(s, axis=0, keepdims=True)
            pos = jnp.min(jnp.where(s == m, order, sentinel), axis=0, keepdims=True)
            vals[n].append(m)
            idxs[n].append(pos)
            arrays[n] = jnp.where(order == pos, NEG_INF, s)
    return vals, idxs


def _stack_rows(rows_list, krow):
    out = jnp.zeros(krow.shape, F32)
    for j, r in enumerate(rows_list):
        out = jnp.where(krow == float(j), r, out)
    return out


def _cand_layout():
    k = PEER_TOPK
    pieces = [("row_a", a, 16 if a == 0 else 8, 0, k // (a + 1)) for a in range(4)]
    pieces += [("col_b", 0, 16, 4, 16), ("col_b", 1, 8, 4, 8), ("col_b", 2, 8, 4, 5)]
    pos = []
    for kind, idx, rows, lo, hi in pieces:
        for r in range(rows):
            a, b = (idx, r) if kind == "row_a" else (r, idx)
            ok = lo <= r < hi and (a + 1) * (b + 1) <= k
            pos.append(a * k + b if ok else k * k)
    assert sorted(p for p in pos if p < k * k) == sorted(
        a * k + b for a in range(k) for b in range(k) if (a + 1) * (b + 1) <= k)
    return pieces, np.asarray(pos, np.float32)


def _route_kernel(s_ref, pos_ref, e1_ref, e2_ref, gate_ref):
    k = PEER_TOPK
    heads = s_ref.shape[0] // 2
    lanes = s_ref.shape[2]
    key_rank = lax.broadcasted_iota(jnp.int32, (N_KEYS, lanes), 0).astype(F32)
    vals, idxs = _take_top([s_ref[n] for n in range(2 * heads)], key_rank, k, float(N_KEYS))
    krow = lax.broadcasted_iota(jnp.int32, (k, lanes), 0).astype(F32)
    flat = pos_ref[...]
    cands, i1_all, i2_all = [], [], []
    for hd in range(heads):
        v1, v2 = vals[2 * hd], vals[2 * hd + 1]
        v1_all = _stack_rows(v1, krow)
        v2_all = _stack_rows(v2, krow)
        i1_all.append(_stack_rows(idxs[2 * hd], krow))
        i2_all.append(_stack_rows(idxs[2 * hd + 1], krow))
        parts = [v1[idx] + v2_all[:rows] if kind == "row_a" else v1_all[:rows] + v2[idx]
                 for kind, idx, rows, _, _ in _cand_layout()[0]]
        cands.append(jnp.where(flat < float(k * k), jnp.concatenate(parts, axis=0), NEG_INF))
    top_s, pos = _take_top(cands, flat, k, float(k * k))
    for hd in range(heads):
        e1, e2 = [], []
        for j in range(k):
            a = jnp.floor(pos[hd][j] * (1.0 / k))
            b = pos[hd][j] - a * k
            e1.append(jnp.sum(jnp.where(krow == a, i1_all[hd], 0.0), axis=0, keepdims=True))
            e2.append(jnp.sum(jnp.where(krow == b, i2_all[hd], 0.0), axis=0, keepdims=True))
        ex = jnp.exp(_stack_rows(top_s[hd], krow) - top_s[hd][0])
        rows = slice(hd * k, (hd + 1) * k)
        gate_ref[rows, :] = ex / jnp.sum(ex, axis=0, keepdims=True)
        e1_ref[rows, :] = _stack_rows(e1, krow)
        e2_ref[rows, :] = _stack_rows(e2, krow)


def _route(scores_t, tl, heads_per_step):
    t = scores_t.shape[2]
    kk = PEER_HEADS * PEER_TOPK
    out = pl.BlockSpec((heads_per_step * PEER_TOPK, tl), lambda i, h: (h, i))
    flat = jnp.asarray(np.repeat(_cand_layout()[1][:, None], tl, axis=1))
    return pl.pallas_call(
        _route_kernel,
        grid=(t // tl, PEER_HEADS // heads_per_step),
        in_specs=[pl.BlockSpec((2 * heads_per_step, N_KEYS, tl), lambda i, h: (h, 0, i)),
                  pl.BlockSpec(flat.shape, lambda i, h: (0, 0))],
        out_specs=[out, out, out],
        out_shape=[jax.ShapeDtypeStruct((kk, t), F32)] * 3,
        compiler_params=_params("parallel", "parallel"),
    )(scores_t, flat)


def _expert_weight_kernel(e1_ref, e2_ref, gate_ref, g_ref, e1t_ref, e2t_ref, gt_ref):
    e1t_ref[...] = e1_ref[...].T
    e2t_ref[...] = e2_ref[...].T
    gt_ref[...] = gate_ref[...].T
    kk = e1_ref.shape[0]
    key = lax.broadcasted_iota(jnp.int32, (N_KEYS, kk), 0).astype(F32)
    nt = (((1,), (1,)), ((), ()))

    def body(t, carry):
        r1 = e1t_ref[pl.ds(t, 1), :]
        r2 = e2t_ref[pl.ds(t, 1), :]
        gr = gt_ref[pl.ds(t, 1), :]
        a_t = jnp.where(key == r1, 1.0, 0.0).astype(BF16)
        b_t = jnp.where(key == r2, gr, 0.0).astype(BF16)
        g_ref[t] = lax.dot_general(a_t, b_t, nt, preferred_element_type=F32).astype(g_ref.dtype)
        return carry

    lax.fori_loop(0, g_ref.shape[0], body, 0, unroll=32)


def _expert_weights(e1, e2, gate, tl):
    kk, t = e1.shape
    slot = pl.BlockSpec((kk, tl), lambda i: (0, i))
    return pl.pallas_call(
        _expert_weight_kernel,
        grid=(t // tl,),
        in_specs=[slot, slot, slot],
        out_specs=pl.BlockSpec((tl, N_KEYS, N_KEYS), lambda i: (i, 0, 0)),
        out_shape=jax.ShapeDtypeStruct((t, N_KEYS, N_KEYS), F32),
        scratch_shapes=[pltpu.VMEM((tl, kk), F32)] * 3,
        compiler_params=_params("parallel"),
    )(e1, e2, gate)


def _peer_kernel(hn_ref, u_ref, v_ref, g_ref, y_ref):
    e = pl.program_id(1)
    nt = (((1,), (1,)), ((), ()))
    hk = lax.dot_general(hn_ref[...], u_ref[...], nt, preferred_element_type=F32)
    g = jnp.swapaxes(g_ref[...], 0, 1)
    act = [(jax.nn.gelu(hk[:, a * N_KEYS:(a + 1) * N_KEYS]) * g[a]).astype(BF16) for a in range(g.shape[0])]
    out = jnp.dot(jnp.concatenate(act, axis=-1), v_ref[...], preferred_element_type=F32)

    @pl.when(e == 0)
    def _():
        y_ref[...] = out

    @pl.when(e > 0)
    def _():
        y_ref[...] += out


def _peer(hn, u, v, g, t, rows_out, tm, te):
    row = pl.BlockSpec((tm, D_MODEL), lambda i, e: (i, 0))
    tab = pl.BlockSpec((te, D_MODEL), lambda i, e: (e, 0))
    return pl.pallas_call(
        _peer_kernel,
        grid=(t // tm, N_EXPERTS // te),
        in_specs=[row, tab, tab, pl.BlockSpec((tm, te // N_KEYS, N_KEYS), lambda i, e: (i, e, 0))],
        out_specs=row,
        out_shape=jax.ShapeDtypeStruct((rows_out, D_MODEL), F32),
        compiler_params=_params("parallel", "arbitrary"),
    )(hn, u, v, g)


def _final_norm_prompt_kernel(h_ref, f_ref, g_ref, y_ref):
    y = [_rms(h_ref[rr % 4, rr // 4] + f_ref[rr % 4, rr // 4], g_ref[...]) for rr in range(8)]
    y_ref[...] = jnp.swapaxes(jnp.stack(y, axis=0), 0, 1)


def _final_norm_prompt(h, f, g, batch, seq):
    rows = 64
    h5 = h.reshape(batch + 1, 4, 4, SLAB_ROWS, D_MODEL)
    f5 = f.reshape(batch + 1, 4, 4, SLAB_ROWS, D_MODEL)
    blk = pl.BlockSpec((None, 4, 2, rows, D_MODEL), lambda b, hh, j: (b, 0, hh, j, 0))
    y = pl.pallas_call(
        _final_norm_prompt_kernel,
        grid=(batch, 2, SLAB_ROWS // rows),
        in_specs=[blk, blk, pl.BlockSpec((1, D_MODEL), lambda b, hh, j: (0, 0))],
        out_specs=pl.BlockSpec((None, rows, 8, D_MODEL), lambda b, hh, j: (b, j, hh, 0)),
        out_shape=jax.ShapeDtypeStruct((batch, SLAB_ROWS, SLABS, D_MODEL), F32),
        compiler_params=_params("parallel", "parallel", "parallel"),
    )(h5, f5, g.reshape(1, D_MODEL))
    return y.reshape(batch, seq, D_MODEL)


def _final_norm_rows_kernel(h_ref, f_ref, g_ref, y_ref):
    y_ref[...] = _rms(h_ref[...] + f_ref[...], g_ref[...])


def _final_norm_rows(h, f, g, row_block, rows):
    blk = pl.BlockSpec((rows, D_MODEL), lambda i: (row_block, 0))
    return pl.pallas_call(
        _final_norm_rows_kernel,
        grid=(1,),
        in_specs=[blk, blk, pl.BlockSpec((1, D_MODEL), lambda i: (0, 0))],
        out_specs=pl.BlockSpec((rows, D_MODEL), lambda i: (0, 0)),
        out_shape=jax.ShapeDtypeStruct((rows, D_MODEL), F32),
        compiler_params=_params("arbitrary"),
    )(h, f, g.reshape(1, D_MODEL))


def _kv_prompt_kernel(*refs):
    ins, outs = refs[:2 * N_DIL], refs[2 * N_DIL:]
    for g in range(N_DIL):
        for kv in range(2):
            src = ins[2 * g + kv]
            heads = [src[0, :, h * HEAD_DIM:(h + 1) * HEAD_DIM] for h in range(ATT_HEADS)]
            outs[g][:, kv, :, :] = jnp.swapaxes(jnp.stack(heads, axis=0), 0, 1)


def _kv_prompt(p3, batch, seq):
    in_specs, out_specs, out_shape = [], [], []
    residue = lambda s: (s % 4) * 4 + s // 4
    for g in range(N_DIL):
        steps = min(DIL_WINDOWS[g], seq) // SLABS
        last = SLAB_ROWS // steps - 1
        for kv in range(2):
            col = COL_ATT + 3 * g + 1 + kv
            in_specs.append(pl.BlockSpec((1, steps, D_MODEL),
                                         lambda b, s, last=last, col=col: (b * SLABS + s, last, col)))
        out_specs.append(pl.BlockSpec((None, steps, None, 2, ATT_HEADS, HEAD_DIM),
                                      lambda b, s: (b, 0, residue(s), 0, 0, 0)))
        out_shape.append(jax.ShapeDtypeStruct((batch, steps, SLABS, 2, ATT_HEADS, HEAD_DIM), F32))
    outs = pl.pallas_call(
        _kv_prompt_kernel,
        grid=(batch, SLABS),
        in_specs=in_specs,
        out_specs=out_specs,
        out_shape=out_shape,
        compiler_params=_params("parallel", "parallel"),
    )(*([p3] * len(in_specs)))
    return [o.reshape(1, batch, -1, 2, ATT_HEADS, HEAD_DIM) for o in outs]


def _row_tile(t, candidates):
    for c in candidates:
        if t % c == 0:
            return c
    raise ValueError(f"no row tile for {t} tokens")


def kernel(x_prompt, x_sample, cache_kv_w128, cache_kv_w512, cache_kv_w2048, norm_mix_g, w_in, sgu_norm_g, sgu_w, sgu_b, w_out, norm_ffn_g, peer_w_q, peer_sub_keys, peer_u, peer_v, norm_final_g):
    batch, seq, _ = x_prompt.shape
    db, ds, _ = x_sample.shape
    assert w_in.shape[0] == 1 and db * ds == CHUNK and seq == SLABS * SLAB_ROWS
    caches = (cache_kv_w128, cache_kv_w512, cache_kv_w2048)
    tp, ts = batch * seq, db * ds
    t = tp + ts
    n_slabs = t // SLAB_ROWS
    tm_big = _row_tile(t, (1040, 640, 128))
    tm_lane = _row_tile(t, (640, 128))

    xs = x_sample.reshape(ts, D_MODEL)
    xl, xn = _permute_norm(x_prompt, norm_mix_g[0])
    xl = lax.dynamic_update_slice(xl, xs, (tp, 0))
    xn = lax.dynamic_update_slice(xn, _rmsnorm_bf16(xs, norm_mix_g[0]), (tp, 0))

    p, att16 = _inproj(xn, _to_bf16(w_in[0], D_MODEL, 1024), t, tm_big, 1024)
    p3 = p.reshape(n_slabs, SLAB_ROWS, p.shape[1])
    att16_3 = att16.reshape(n_slabs, SLAB_ROWS, att16.shape[1])

    w_tril = sgu_w[0] * jnp.tril(jnp.ones((CHUNK, CHUNK), F32))
    tau = _block_steps(0)
    w_p = w_tril[:, tau][:, :, tau].astype(BF16)
    b_p = sgu_b[0][:, tau][..., None]
    w_s = jnp.einsum("bc,gis->gbics", jnp.eye(db, dtype=F32), w_tril[:, :ds, :ds]).reshape(SGU_GROUPS, ts, ts)
    b_s = jnp.tile(sgu_b[0][:, :ds], (1, db))[..., None]
    a_out = _sgu_prompt(p3, sgu_norm_g[0], w_p, b_p, batch).reshape(t, D_MODEL)
    a_s, vn_s = _sgu_sample(p3, sgu_norm_g[0], w_s.astype(BF16), b_s, tp // SLAB_ROWS)
    a_out = lax.dynamic_update_slice(a_out, a_s, (tp, 0))

    att_s = p[tp:, COL_ATT * D_MODEL:COL_GATE * D_MODEL].reshape(db, ds, N_DIL, 3, ATT_HEADS, HEAD_DIM)
    outs, lses = [], []
    o_s, lse_s = _cached_attention(att_s, [c[0] for c in caches])
    for g in range(N_DIL):
        if g == 0:
            o_p, lse_p = _band_attention(p3, COL_ATT, False, g, batch)
        else:
            o_p, lse_p = _band_attention(att16_3, 3 * (g - 1), True, g, batch)
        lse_g = jnp.pad(lse_s[g, ..., 0].reshape(ts, ATT_HEADS), ((0, 0), (0, 128 - ATT_HEADS)))
        o_g = o_s[g].reshape(ts, D_MODEL).astype(o_p.dtype)
        outs.append(lax.dynamic_update_slice(o_p.reshape(t, D_MODEL), o_g, (tp, 0)))
        lses.append(lax.dynamic_update_slice(lse_p.reshape(t, 128), lse_g, (tp, 0)))

    h, hn = _merge(a_out, att16, COL_GATE - COL_ATT16, outs, lses, xl, _to_bf16(w_out[0], 1024, D_MODEL),
                   norm_ffn_g[0], t, 128)

    scores_t = _peer_scores(hn, _to_bf16(peer_w_q[0], 1024, D_MODEL), peer_sub_keys[0].astype(BF16), tm_lane)
    e1, e2, gate = _route(scores_t, 128, 2)
    g_dense = _expert_weights(e1, e2, gate, 128)
    f = _peer(hn, _to_bf16(peer_u[0], 1024, D_MODEL), _to_bf16(peer_v[0], 1024, D_MODEL), g_dense,
              t, h.shape[0], tm_lane, 1024)

    y_prompt = _final_norm_prompt(h, f, norm_final_g, batch, seq)
    y_sample = _final_norm_rows(h, f, norm_final_g, tp // ts, ts).reshape(db, ds, D_MODEL)
    kv_prompt = _kv_prompt(p3, batch, seq)
    kv_sample = [att_s[:, :, g, 1:3][None] for g in range(N_DIL)]
    sgu_v_sample = vn_s.reshape(1, db, ds, D_MODEL)
    return (y_prompt, y_sample, kv_prompt[0], kv_prompt[1], kv_prompt[2],
            kv_sample[0], kv_sample[1], kv_sample[2], sgu_v_sample)
```

```python
import functools

import numpy as np
import jax
import jax.numpy as jnp
from jax import lax
from jax.experimental import pallas as pl
from jax.experimental.pallas import tpu as pltpu

F32 = jnp.float32
BF16 = jnp.bfloat16

D_MODEL = 2048
HEAD_DIM = 128
ATT_HEADS = D_MODEL // HEAD_DIM
N_DIL = 3
DIL_WINDOWS = (128, 512, 2048)
DIL_RATES = (1, 4, 16)
ATT_BLOCK = 128
SLABS = DIL_RATES[-1]
SLAB_ROWS = 128
SGU_GROUPS = 8
SGU_GROUP_DIM = D_MODEL // SGU_GROUPS
CHUNK = 128
N_COL_BLOCKS = 13
COL_U, COL_V, COL_ATT, COL_GATE = 0, 1, 2, 11
COL_ATT16 = COL_ATT + 3
PEER_HEADS = 8
PEER_TOPK = 16
N_KEYS = 128
N_EXPERTS = N_KEYS * N_KEYS
PEER_HALF = 128
NORM_EPS = 1e-6
MASK_VALUE = -1e30
NEG_INF = float("-inf")
VMEM_LIMIT = 56 * 1024 * 1024

BLOCK_SHAPES = ((16, 8), (4, 32), (1, 128))


def _alibi_slopes():
    n = N_DIL * ATT_HEADS
    e = np.arange(1, n + 1, dtype=np.float32)
    return np.exp2(np.float32(-8.0) * e / np.float32(n)).astype(np.float32).reshape(N_DIL, ATT_HEADS)


def _block_steps(group):
    slabs, rows = BLOCK_SHAPES[group]
    n = np.arange(slabs * rows)
    s, j = n // rows, n % rows
    if group == 0:
        return j * 16 + (s % 4) * 4 + s // 4
    if group == 1:
        return j * 4 + s
    return j


def _params(*sem):
    return pltpu.CompilerParams(dimension_semantics=sem, vmem_limit_bytes=VMEM_LIMIT)


def _rms(x, g):
    return x * lax.rsqrt(jnp.mean(x * x, axis=-1, keepdims=True) + NORM_EPS) * g


def _permute_norm_kernel(x_ref, g_ref, xl_ref, xn_ref):
    xs = jnp.swapaxes(x_ref[...], 0, 1)
    for rr in range(8):
        x = xs[rr]
        xl_ref[rr % 4, rr // 4] = x
        xn_ref[rr % 4, rr // 4] = _rms(x, g_ref[...]).astype(xn_ref.dtype)


def _permute_norm(x_prompt, g):
    batch, seq, _ = x_prompt.shape
    x4 = x_prompt.reshape(batch, SLAB_ROWS, SLABS, D_MODEL)
    out = pl.BlockSpec((None, 4, 2, SLAB_ROWS, D_MODEL), lambda b, h: (b, 0, h, 0, 0))
    xl, xn = pl.pallas_call(
        _permute_norm_kernel,
        grid=(batch, 2),
        in_specs=[pl.BlockSpec((None, SLAB_ROWS, 8, D_MODEL), lambda b, h: (b, 0, h, 0)),
                  pl.BlockSpec((1, D_MODEL), lambda b, h: (0, 0))],
        out_specs=[out, out],
        out_shape=[jax.ShapeDtypeStruct((batch + 1, 4, 4, SLAB_ROWS, D_MODEL), F32),
                   jax.ShapeDtypeStruct((batch + 1, 4, 4, SLAB_ROWS, D_MODEL), BF16)],
        compiler_params=_params("parallel", "parallel"),
    )(x4, g.reshape(1, D_MODEL))
    rows = (batch + 1) * seq
    return xl.reshape(rows, D_MODEL), xn.reshape(rows, D_MODEL)


def _rmsnorm_kernel(x_ref, g_ref, o_ref):
    o_ref[...] = _rms(x_ref[...], g_ref[...]).astype(o_ref.dtype)


def _rmsnorm_bf16(x, g):
    return pl.pallas_call(
        _rmsnorm_kernel,
        out_shape=jax.ShapeDtypeStruct(x.shape, BF16),
    )(x, g.reshape(1, D_MODEL))


def _cast_kernel(x_ref, o_ref):
    o_ref[...] = x_ref[...].astype(o_ref.dtype)


def _to_bf16(w, rows, cols):
    r, c = w.shape
    return pl.pallas_call(
        _cast_kernel,
        grid=(r // rows, c // cols),
        in_specs=[pl.BlockSpec((rows, cols), lambda i, j: (i, j))],
        out_specs=pl.BlockSpec((rows, cols), lambda i, j: (i, j)),
        out_shape=jax.ShapeDtypeStruct((r, c), BF16),
        compiler_params=_params("parallel", "parallel"),
    )(w)


def _inproj_kernel(x_ref, w_ref, o_ref, att_ref, *, blocks_per_col):
    cb = pl.program_id(1) // blocks_per_col
    p = jnp.dot(x_ref[...], w_ref[...], preferred_element_type=F32)

    @pl.when(cb < COL_ATT)
    def _():
        o_ref[...] = jax.nn.gelu(p)

    @pl.when((cb >= COL_ATT) & (cb < COL_ATT16))
    def _():
        o_ref[...] = p

    @pl.when((cb >= COL_ATT16) & (cb < COL_GATE))
    def _():
        o_ref[...] = p
        is_q = (cb - COL_ATT) % 3 == 0
        att_ref[...] = (p * jnp.where(is_q, HEAD_DIM ** -0.5, 1.0)).astype(att_ref.dtype)

    @pl.when(cb >= COL_GATE)
    def _():
        att_ref[...] = jax.nn.sigmoid(p).astype(att_ref.dtype)


def _inproj(xn, w, t, tm, tn):
    n = w.shape[1]
    per_col = D_MODEL // tn
    n_f32 = COL_GATE * per_col
    n_b16 = (N_COL_BLOCKS - COL_ATT16) * per_col
    f32_block = lambda j: jnp.minimum(j, n_f32 - 1)
    b16_block = lambda j: jnp.maximum(j - COL_ATT16 * per_col, 0)
    return pl.pallas_call(
        functools.partial(_inproj_kernel, blocks_per_col=per_col),
        grid=(t // tm, n // tn),
        in_specs=[pl.BlockSpec((tm, D_MODEL), lambda i, j: (i, 0)),
                  pl.BlockSpec((D_MODEL, tn), lambda i, j: (0, j))],
        out_specs=[pl.BlockSpec((tm, tn), lambda i, j: (i, f32_block(j))),
                   pl.BlockSpec((tm, tn), lambda i, j: (i, b16_block(j)))],
        out_shape=[jax.ShapeDtypeStruct((t, n_f32 * tn), F32),
                   jax.ShapeDtypeStruct((t, n_b16 * tn), BF16)],
        compiler_params=_params("parallel", "arbitrary"),
    )(xn, w)


def _sgu_kernel(u_ref, v_ref, g_ref, w_ref, b_ref, a_ref, vn_ref):
    v = v_ref[...].reshape(CHUNK, D_MODEL)
    u = u_ref[...].reshape(CHUNK, D_MODEL)
    vc = v - jnp.mean(v, axis=-1, keepdims=True)
    vn = vc * lax.rsqrt(jnp.mean(vc * vc, axis=-1, keepdims=True) + NORM_EPS) * g_ref[...]
    if vn_ref is not None:
        vn_ref[...] = vn
    cols = []
    for g in range(SGU_GROUPS):
        cs = slice(g * SGU_GROUP_DIM, (g + 1) * SGU_GROUP_DIM)
        mix = jnp.dot(w_ref[g], vn[:, cs].astype(BF16), preferred_element_type=F32) + b_ref[g]
        cols.append(u[:, cs] * mix)
    a_ref[...] = jnp.concatenate(cols, axis=-1).reshape(a_ref.shape)


def _sgu_prompt(p3, sgu_norm_g, w, b, batch):
    slabs, rows = BLOCK_SHAPES[0]
    n_chunks = SLAB_ROWS // rows
    blk = lambda col: pl.BlockSpec((slabs, rows, D_MODEL), lambda bb, c: (bb, c, col))
    return pl.pallas_call(
        lambda u, v, g, ww, bb, a: _sgu_kernel(u, v, g, ww, bb, a, None),
        grid=(batch, n_chunks),
        in_specs=[blk(COL_U), blk(COL_V),
                  pl.BlockSpec((1, D_MODEL), lambda bb, c: (0, 0)),
                  pl.BlockSpec((SGU_GROUPS, CHUNK, CHUNK), lambda bb, c: (0, 0, 0)),
                  pl.BlockSpec((SGU_GROUPS, CHUNK, 1), lambda bb, c: (0, 0, 0))],
        out_specs=blk(0),
        out_shape=jax.ShapeDtypeStruct((p3.shape[0], SLAB_ROWS, D_MODEL), F32),
        compiler_params=_params("parallel", "parallel"),
    )(p3, p3, sgu_norm_g.reshape(1, D_MODEL), w, b)


def _sgu_sample(p3, sgu_norm_g, w, b, slab):
    blk = lambda col: pl.BlockSpec((1, SLAB_ROWS, D_MODEL), lambda i: (slab, 0, col))
    return pl.pallas_call(
        _sgu_kernel,
        grid=(1,),
        in_specs=[blk(COL_U), blk(COL_V),
                  pl.BlockSpec((1, D_MODEL), lambda i: (0, 0)),
                  pl.BlockSpec((SGU_GROUPS, CHUNK, CHUNK), lambda i: (0, 0, 0)),
                  pl.BlockSpec((SGU_GROUPS, CHUNK, 1), lambda i: (0, 0, 0))],
        out_specs=[pl.BlockSpec((CHUNK, D_MODEL), lambda i: (0, 0)),
                   pl.BlockSpec((CHUNK, D_MODEL), lambda i: (0, 0))],
        out_shape=[jax.ShapeDtypeStruct((CHUNK, D_MODEL), F32), jax.ShapeDtypeStruct((CHUNK, D_MODEL), F32)],
        compiler_params=_params("arbitrary"),
    )(p3, p3, sgu_norm_g.reshape(1, D_MODEL), w, b)


def _band_attn_kernel(*refs, has_prev, prescaled):
    if has_prev:
        bc_ref, bp_ref, q_ref, kc_ref, vc_ref, o_ref, lse_ref, kp_ref, vp_ref = refs

        @pl.when(pl.program_id(2) == 0)
        def _():
            kp_ref[...] = jnp.zeros_like(kp_ref)
            vp_ref[...] = jnp.zeros_like(vp_ref)
    else:
        bc_ref, q_ref, kc_ref, vc_ref, o_ref, lse_ref = refs
    n = ATT_BLOCK
    scale = HEAD_DIM ** -0.5
    nt = (((1,), (1,)), ((), ()))
    heads = range(ATT_HEADS)
    hs = [slice(h * HEAD_DIM, (h + 1) * HEAD_DIM) for h in heads]
    ld = lambda ref, h: ref[:, :, hs[h]].reshape(n, HEAD_DIM)
    q = [ld(q_ref, h) if prescaled else (ld(q_ref, h) * scale).astype(BF16) for h in heads]
    s_c = [lax.dot_general(q[h], ld(kc_ref, h).astype(BF16), nt, preferred_element_type=F32) + bc_ref[h]
           for h in heads]
    m = [jnp.max(s_c[h], axis=-1, keepdims=True) for h in heads]
    if has_prev:
        s_p = [lax.dot_general(q[h], ld(kp_ref, h).astype(BF16), nt, preferred_element_type=F32) + bp_ref[h]
               for h in heads]
        m = [jnp.maximum(m[h], jnp.max(s_p[h], axis=-1, keepdims=True)) for h in heads]
    p_c = [jnp.exp(s_c[h] - m[h]) for h in heads]
    l = [jnp.sum(p_c[h], axis=-1, keepdims=True) for h in heads]
    o = [jnp.dot(p_c[h].astype(BF16), ld(vc_ref, h).astype(BF16), preferred_element_type=F32) for h in heads]
    if has_prev:
        p_p = [jnp.exp(s_p[h] - m[h]) for h in heads]
        l = [l[h] + jnp.sum(p_p[h], axis=-1, keepdims=True) for h in heads]
        o = [o[h] + jnp.dot(p_p[h].astype(BF16), ld(vp_ref, h).astype(BF16), preferred_element_type=F32)
             for h in heads]
    lane = lax.broadcasted_iota(jnp.int32, (n, 128), 1)
    lse = jnp.zeros((n, 128), F32)
    for h in heads:
        lse = jnp.where(lane == h, m[h] + jnp.log(l[h]), lse)
    o_ref[...] = jnp.concatenate([o[h] / l[h] for h in heads], axis=-1).reshape(o_ref.shape).astype(o_ref.dtype)
    lse_ref[...] = lse.reshape(lse_ref.shape)
    if has_prev:
        kp_ref[...] = kc_ref[...]
        vp_ref[...] = vc_ref[...]


def _band_bias(group):
    steps = _block_steps(group)
    back = (steps[:, None] - steps[None, :]).astype(np.float32)
    coef = (_alibi_slopes()[group] * np.float32(DIL_RATES[group]))[:, None, None]
    cur = np.where(back >= 0, -(coef * back), np.float32(MASK_VALUE)).astype(np.float32)
    back_p = back + np.float32(ATT_BLOCK)
    prev = np.where(back_p <= ATT_BLOCK, -(coef * back_p), np.float32(MASK_VALUE)).astype(np.float32)
    return cur, np.stack([np.full_like(prev, MASK_VALUE), prev])


def _band_attention(src, col_q, prescaled, group, batch):
    p3 = src
    slabs, rows = BLOCK_SHAPES[group]
    streams = SLABS // slabs
    nb = SLAB_ROWS // rows
    has_prev = nb > 1
    cq, ck, cv = col_q, col_q + 1, col_q + 2
    bias_c, bias_p = _band_bias(group)

    def cur(col, width=D_MODEL):
        return pl.BlockSpec((slabs, rows, width), lambda b, r, j: (b * streams + r, j, col))

    table = (ATT_HEADS, ATT_BLOCK, ATT_BLOCK)
    in_specs = [pl.BlockSpec(table, lambda b, r, j: (0, 0, 0))]
    args = [jnp.asarray(bias_c)]
    scratch = []
    if has_prev:
        in_specs += [pl.BlockSpec((None,) + table, lambda b, r, j: (jnp.minimum(j, 1), 0, 0, 0))]
        args += [jnp.asarray(bias_p)]
        scratch = [pltpu.VMEM((slabs, rows, D_MODEL), p3.dtype)] * 2
    in_specs += [cur(cq), cur(ck), cur(cv)]
    args += [p3] * (len(in_specs) - len(args))
    return pl.pallas_call(
        functools.partial(_band_attn_kernel, has_prev=has_prev, prescaled=prescaled),
        grid=(batch, streams, nb),
        in_specs=in_specs,
        out_specs=[cur(0), cur(0, 128)],
        out_shape=[jax.ShapeDtypeStruct((p3.shape[0], SLAB_ROWS, D_MODEL), BF16 if prescaled else F32),
                   jax.ShapeDtypeStruct((p3.shape[0], SLAB_ROWS, 128), F32)],
        scratch_shapes=scratch,
        compiler_params=_params("parallel", "parallel", "arbitrary"),
    )(*args)


def _cached_attn_kernel(new_ref, c0_ref, c1_ref, c2_ref, coef_ref, o_ref, lse_ref, *, dec_seq):
    caches = (c0_ref, c1_ref, c2_ref)
    shared = [d == 1 for d in DIL_RATES]
    pairs = [(g, i) for g in range(N_DIL) for i in range(dec_seq)]
    res = lambda g, i: 0 if shared[g] else i
    lane = lax.broadcasted_iota(jnp.int32, (ATT_HEADS, ATT_BLOCK), 1)
    scale = HEAD_DIM ** -0.5
    q = {(g, i): new_ref[i, g, 0] for g, i in pairs}

    def qk_body(r, carry):
        out = []
        for n, (g, i) in enumerate(pairs):
            col = jnp.sum(q[g, i] * caches[g][r, res(g, i), 0], axis=-1, keepdims=True)
            out.append(jnp.where(lane == r, col, carry[n]))
        return tuple(out)

    zero = jnp.zeros((ATT_HEADS, ATT_BLOCK), F32)
    s_all = lax.fori_loop(0, ATT_BLOCK, qk_body, (zero,) * len(pairs), unroll=16)

    ps, p_news, ls, ms = [], [], [], []
    for n, (g, i) in enumerate(pairs):
        coef = coef_ref[g]
        if shared[g]:
            back = (ATT_BLOCK + i - lane).astype(F32)
            s = jnp.where(lane >= i, s_all[n] * scale - coef * back, MASK_VALUE)
            s_new = [jnp.sum(q[g, i] * new_ref[i2, g, 1], axis=-1, keepdims=True) * scale
                     - coef[:, :1] * float(i - i2) for i2 in range(i + 1)]
        else:
            s = s_all[n] * scale - coef * (ATT_BLOCK - lane).astype(F32)
            s_new = [jnp.sum(q[g, i] * new_ref[i, g, 1], axis=-1, keepdims=True) * scale]
        m = jnp.max(s, axis=-1, keepdims=True)
        for sn in s_new:
            m = jnp.maximum(m, sn)
        p = jnp.exp(s - m)
        p_new = [jnp.exp(sn - m) for sn in s_new]
        l = jnp.sum(p, axis=-1, keepdims=True)
        for pn in p_new:
            l = l + pn
        ps.append(p)
        p_news.append(p_new)
        ls.append(l)
        ms.append(m)

    def pv_body(r, carry):
        out = []
        for n, (g, i) in enumerate(pairs):
            col = jnp.sum(jnp.where(lane == r, ps[n], 0.0), axis=-1, keepdims=True)
            out.append(carry[n] + col * caches[g][r, res(g, i), 1])
        return tuple(out)

    zero_o = jnp.zeros((ATT_HEADS, HEAD_DIM), F32)
    o_all = lax.fori_loop(0, ATT_BLOCK, pv_body, (zero_o,) * len(pairs), unroll=16)
    for n, (g, i) in enumerate(pairs):
        o = o_all[n]
        for i2, pn in enumerate(p_news[n]):
            o = o + pn * new_ref[i2 if shared[g] else i, g, 2]
        o_ref[g, i] = o / ls[n]
        lse_ref[g, i] = jnp.broadcast_to(ms[n] + jnp.log(ls[n]), (ATT_HEADS, 128))


def _cached_attention(att_s, caches):
    db, dec_seq = att_s.shape[:2]
    views, specs = [], []
    for g, cache in enumerate(caches):
        dil, window = DIL_RATES[g], DIL_WINDOWS[g]
        assert cache.shape[1] == window and window == dil * ATT_BLOCK
        assert dil == 1 or dec_seq <= dil
        nres = 1 if dil == 1 else dec_seq
        views.append(cache.reshape(db, ATT_BLOCK, dil, 2, ATT_HEADS, HEAD_DIM))
        specs.append(pl.BlockSpec((None, ATT_BLOCK, nres, 2, ATT_HEADS, HEAD_DIM), lambda b: (b, 0, 0, 0, 0, 0)))
    coef = np.stack([np.repeat((_alibi_slopes()[g] * DIL_RATES[g])[:, None], 128, axis=1) for g in range(N_DIL)])
    out = lambda w: pl.BlockSpec((N_DIL, None, dec_seq, ATT_HEADS, w), lambda b: (0, b, 0, 0, 0))
    return pl.pallas_call(
        functools.partial(_cached_attn_kernel, dec_seq=dec_seq),
        grid=(db,),
        in_specs=[pl.BlockSpec((None, dec_seq, N_DIL, 3, ATT_HEADS, HEAD_DIM), lambda b: (b, 0, 0, 0, 0, 0))]
        + specs + [pl.BlockSpec((N_DIL, ATT_HEADS, 128), lambda b: (0, 0, 0))],
        out_specs=[out(HEAD_DIM), out(128)],
        out_shape=[jax.ShapeDtypeStruct((N_DIL, db, dec_seq, ATT_HEADS, HEAD_DIM), F32),
                   jax.ShapeDtypeStruct((N_DIL, db, dec_seq, ATT_HEADS, 128), F32)],
        compiler_params=_params("parallel"),
    )(att_s, *views, jnp.asarray(coef.astype(np.float32)))


def _merge_kernel(a_ref, ga_ref, gb_ref, o0_ref, o1_ref, o2_ref, l0_ref, l1_ref, l2_ref,
                  x_ref, w_ref, g_ref, h_ref, hn_ref, merged_ref):
    l0, l1, l2 = l0_ref[...], l1_ref[...], l2_ref[...]
    mx = jnp.maximum(jnp.maximum(l0, l1), l2)
    e0, e1, e2 = jnp.exp(l0 - mx), jnp.exp(l1 - mx), jnp.exp(l2 - mx)
    den = e0 + e1 + e2
    w0, w1, w2 = e0 / den, e1 / den, e2 / den
    for h in range(ATT_HEADS):
        hs = slice(h * HEAD_DIM, (h + 1) * HEAD_DIM)
        b_out = (w0[:, h:h + 1] * o0_ref[:, hs] + w1[:, h:h + 1] * o1_ref[:, hs].astype(F32)
                 + w2[:, h:h + 1] * o2_ref[:, hs].astype(F32))
        merged = ga_ref[:, hs].astype(F32) * a_ref[:, hs] + gb_ref[:, hs].astype(F32) * b_out
        merged_ref[:, hs] = merged.astype(merged_ref.dtype)
    h_new = x_ref[...] + jnp.dot(merged_ref[...], w_ref[...], preferred_element_type=F32)
    h_ref[...] = h_new
    hn_ref[...] = _rms(h_new, g_ref[...]).astype(hn_ref.dtype)


def _merge(a_out, gates, gate_col, outs, lses, x, w_out, norm_g, t, tm):
    row = lambda col: pl.BlockSpec((tm, D_MODEL), lambda i: (i, col))
    lrow = pl.BlockSpec((tm, 128), lambda i: (i, 0))
    p = gates
    return pl.pallas_call(
        _merge_kernel,
        grid=(t // tm,),
        in_specs=[row(0), row(gate_col), row(gate_col + 1), row(0), row(0), row(0), lrow, lrow, lrow,
                  row(0), pl.BlockSpec((D_MODEL, D_MODEL), lambda i: (0, 0)),
                  pl.BlockSpec((1, D_MODEL), lambda i: (0, 0))],
        out_specs=[row(0), row(0)],
        out_shape=[jax.ShapeDtypeStruct((x.shape[0], D_MODEL), F32), jax.ShapeDtypeStruct((t, D_MODEL), BF16)],
        scratch_shapes=[pltpu.VMEM((tm, D_MODEL), BF16)],
        compiler_params=_params("parallel"),
    )(a_out, p, p, outs[0], outs[1], outs[2], lses[0], lses[1], lses[2], x, w_out, norm_g.reshape(1, D_MODEL))


def _peer_score_kernel(hn_ref, wq_ref, keys_ref, s_ref):
    q = jnp.dot(hn_ref[...], wq_ref[...], preferred_element_type=F32).astype(BF16)
    nt = (((1,), (1,)), ((), ()))
    for hc in range(2 * PEER_HEADS):
        cs = slice(hc * PEER_HALF, (hc + 1) * PEER_HALF)
        s_ref[hc] = lax.dot_general(keys_ref[hc % 2], q[:, cs], nt, preferred_element_type=F32)


def _peer_scores(hn, w_q, sub_keys, tm):
    t = hn.shape[0]
    return pl.pallas_call(
        _peer_score_kernel,
        grid=(t // tm,),
        in_specs=[pl.BlockSpec((tm, D_MODEL), lambda i: (i, 0)),
                  pl.BlockSpec((D_MODEL, 2 * PEER_HEADS * PEER_HALF), lambda i: (0, 0)),
                  pl.BlockSpec((2, N_KEYS, PEER_HALF), lambda i: (0, 0, 0))],
        out_specs=pl.BlockSpec((2 * PEER_HEADS, N_KEYS, tm), lambda i: (0, 0, i)),
        out_shape=jax.ShapeDtypeStruct((2 * PEER_HEADS, N_KEYS, t), F32),
        compiler_params=_params("parallel"),
    )(hn, w_q, sub_keys)


def _take_top(arrays, order, count, sentinel):
    arrays = list(arrays)
    vals = [[] for _ in arrays]
    idxs = [[] for _ in arrays]
    for _ in range(count):
        for n, s in enumerate(arrays):
            m = jnp.max(s, axis=0, keepdims=True)
            pos = jnp.min(jnp.where(s == m, order, sentinel), axis=0, keepdims=True)
            vals[n].append(m)
            idxs[n].append(pos)
            arrays[n] = jnp.where(order == pos, NEG_INF, s)
    return vals, idxs


def _stack_rows(rows_list, krow):
    out = jnp.zeros(krow.shape, F32)
    for j, r in enumerate(rows_list):
        out = jnp.where(krow == float(j), r, out)
    return out


def _cand_layout():
    k = PEER_TOPK
    pieces = [("row_a", a, 16 if a == 0 else 8, 0, k // (a + 1)) for a in range(4)]
    pieces += [("col_b", 0, 16, 4, 16), ("col_b", 1, 8, 4, 8), ("col_b", 2, 8, 4, 5)]
    pos = []
    for kind, idx, rows, lo, hi in pieces:
        for r in range(rows):
            a, b = (idx, r) if kind == "row_a" else (r, idx)
            ok = lo <= r < hi and (a + 1) * (b + 1) <= k
            pos.append(a * k + b if ok else k * k)
    assert sorted(p for p in pos if p < k * k) == sorted(
        a * k + b for a in range(k) for b in range(k) if (a + 1) * (b + 1) <= k)
    return pieces, np.asarray(pos, np.float32)


def _route_kernel(s_ref, pos_ref, u_ref, v_ref, e1_ref, e2_ref, gate_ref, ub_ref, vb_ref):
    ub_ref[...] = u_ref[...].astype(ub_ref.dtype)
    vb_ref[...] = v_ref[...].astype(vb_ref.dtype)
    k = PEER_TOPK
    heads = s_ref.shape[0] // 2
    lanes = s_ref.shape[2]
    key_rank = lax.broadcasted_iota(jnp.int32, (N_KEYS, lanes), 0).astype(F32)
    vals, idxs = _take_top([s_ref[n] for n in range(2 * heads)], key_rank, k, float(N_KEYS))
    krow = lax.broadcasted_iota(jnp.int32, (k, lanes), 0).astype(F32)
    flat = pos_ref[...]
    cands, i1_all, i2_all = [], [], []
    for hd in range(heads):
        v1, v2 = vals[2 * hd], vals[2 * hd + 1]
        v1_all = _stack_rows(v1, krow)
        v2_all = _stack_rows(v2, krow)
        i1_all.append(_stack_rows(idxs[2 * hd], krow))
        i2_all.append(_stack_rows(idxs[2 * hd + 1], krow))
        parts = [v1[idx] + v2_all[:rows] if kind == "row_a" else v1_all[:rows] + v2[idx]
                 for kind, idx, rows, _, _ in _cand_layout()[0]]
        cands.append(jnp.where(flat < float(k * k), jnp.concatenate(parts, axis=0), NEG_INF))
    top_s, pos = _take_top(cands, flat, k, float(k * k))
    for hd in range(heads):
        e1, e2 = [], []
        for j in range(k):
            a = jnp.floor(pos[hd][j] * (1.0 / k))
            b = pos[hd][j] - a * k
            e1.append(jnp.sum(jnp.where(krow == a, i1_all[hd], 0.0), axis=0, keepdims=True))
            e2.append(jnp.sum(jnp.where(krow == b, i2_all[hd], 0.0), axis=0, keepdims=True))
        ex = jnp.exp(_stack_rows(top_s[hd], krow) - top_s[hd][0])
        rows = slice(hd * k, (hd + 1) * k)
        gate_ref[rows, :] = ex / jnp.sum(ex, axis=0, keepdims=True)
        e1_ref[rows, :] = _stack_rows(e1, krow)
        e2_ref[rows, :] = _stack_rows(e2, krow)


def _route(scores_t, tl, heads_per_step, peer_u, peer_v):
    t = scores_t.shape[2]
    kk = PEER_HEADS * PEER_TOPK
    n_head_steps = PEER_HEADS // heads_per_step
    steps = (t // tl) * n_head_steps
    n_blocks = 1 << (steps.bit_length() - 1)
    tab_rows = N_EXPERTS // n_blocks
    tab = pl.BlockSpec((tab_rows, D_MODEL), lambda i, h: (jnp.minimum(i * n_head_steps + h, n_blocks - 1), 0))
    out = pl.BlockSpec((heads_per_step * PEER_TOPK, tl), lambda i, h: (h, i))
    flat = jnp.asarray(np.repeat(_cand_layout()[1][:, None], tl, axis=1))
    return pl.pallas_call(
        _route_kernel,
        grid=(t // tl, n_head_steps),
        in_specs=[pl.BlockSpec((2 * heads_per_step, N_KEYS, tl), lambda i, h: (h, 0, i)),
                  pl.BlockSpec(flat.shape, lambda i, h: (0, 0)), tab, tab],
        out_specs=[out, out, out, tab, tab],
        out_shape=[jax.ShapeDtypeStruct((kk, t), F32)] * 3 + [jax.ShapeDtypeStruct((N_EXPERTS, D_MODEL), BF16)] * 2,
        compiler_params=_params("arbitrary", "arbitrary"),
    )(scores_t, flat, peer_u, peer_v)


def _expert_weight_kernel(e1_ref, e2_ref, gate_ref, g_ref, e1t_ref, e2t_ref, gt_ref):
    e1t_ref[...] = e1_ref[...].T
    e2t_ref[...] = e2_ref[...].T
    gt_ref[...] = gate_ref[...].T
    kk = e1_ref.shape[0]
    key = lax.broadcasted_iota(jnp.int32, (N_KEYS, kk), 0).astype(F32)
    nt = (((1,), (1,)), ((), ()))

    def body(t, carry):
        r1 = e1t_ref[pl.ds(t, 1), :]
        r2 = e2t_ref[pl.ds(t, 1), :]
        gr = gt_ref[pl.ds(t, 1), :]
        a_t = jnp.where(key == r1, 1.0, 0.0).astype(BF16)
        b_t = jnp.where(key == r2, gr, 0.0).astype(BF16)
        g_ref[t] = lax.dot_general(a_t, b_t, nt, preferred_element_type=F32).astype(g_ref.dtype)
        return carry

    lax.fori_loop(0, g_ref.shape[0], body, 0, unroll=32)


def _expert_weights(e1, e2, gate, tl):
    kk, t = e1.shape
    slot = pl.BlockSpec((kk, tl), lambda i: (0, i))
    return pl.pallas_call(
        _expert_weight_kernel,
        grid=(t // tl,),
        in_specs=[slot, slot, slot],
        out_specs=pl.BlockSpec((tl, N_KEYS, N_KEYS), lambda i: (i, 0, 0)),
        out_shape=jax.ShapeDtypeStruct((t, N_KEYS, N_KEYS), F32),
        scratch_shapes=[pltpu.VMEM((tl, kk), F32)] * 3,
        compiler_params=_params("parallel"),
    )(e1, e2, gate)


def _peer_kernel(hn_ref, u_ref, v_ref, g_ref, y_ref):
    e = pl.program_id(1)
    nt = (((1,), (1,)), ((), ()))
    hk = lax.dot_general(hn_ref[...], u_ref[...], nt, preferred_element_type=F32)
    g = jnp.swapaxes(g_ref[...], 0, 1)
    act = [(jax.nn.gelu(hk[:, a * N_KEYS:(a + 1) * N_KEYS]) * g[a]).astype(BF16) for a in range(g.shape[0])]
    out = jnp.dot(jnp.concatenate(act, axis=-1), v_ref[...], preferred_element_type=F32)

    @pl.when(e == 0)
    def _():
        y_ref[...] = out

    @pl.when(e > 0)
    def _():
        y_ref[...] += out


def _peer(hn, u, v, g, t, rows_out, tm, te):
    row = pl.BlockSpec((tm, D_MODEL), lambda i, e: (i, 0))
    tab = pl.BlockSpec((te, D_MODEL), lambda i, e: (e, 0))
    return pl.pallas_call(
        _peer_kernel,
        grid=(t // tm, N_EXPERTS // te),
        in_specs=[row, tab, tab, pl.BlockSpec((tm, te // N_KEYS, N_KEYS), lambda i, e: (i, e, 0))],
        out_specs=row,
        out_shape=jax.ShapeDtypeStruct((rows_out, D_MODEL), F32),
        compiler_params=_params("parallel", "arbitrary"),
    )(hn, u, v, g)


def _final_norm_prompt_kernel(h_ref, f_ref, g_ref, y_ref):
    y = [_rms(h_ref[rr % 4, rr // 4] + f_ref[rr % 4, rr // 4], g_ref[...]) for rr in range(8)]
    y_ref[...] = jnp.swapaxes(jnp.stack(y, axis=0), 0, 1)


def _final_norm_prompt(h, f, g, batch, seq):
    rows = 64
    h5 = h.reshape(batch + 1, 4, 4, SLAB_ROWS, D_MODEL)
    f5 = f.reshape(batch + 1, 4, 4, SLAB_ROWS, D_MODEL)
    blk = pl.BlockSpec((None, 4, 2, rows, D_MODEL), lambda b, hh, j: (b, 0, hh, j, 0))
    y = pl.pallas_call(
        _final_norm_prompt_kernel,
        grid=(batch, 2, SLAB_ROWS // rows),
        in_specs=[blk, blk, pl.BlockSpec((1, D_MODEL), lambda b, hh, j: (0, 0))],
        out_specs=pl.BlockSpec((None, rows, 8, D_MODEL), lambda b, hh, j: (b, j, hh, 0)),
        out_shape=jax.ShapeDtypeStruct((batch, SLAB_ROWS, SLABS, D_MODEL), F32),
        compiler_params=_params("parallel", "parallel", "parallel"),
    )(h5, f5, g.reshape(1, D_MODEL))
    return y.reshape(batch, seq, D_MODEL)


def _final_norm_rows_kernel(h_ref, f_ref, g_ref, y_ref):
    y_ref[...] = _rms(h_ref[...] + f_ref[...], g_ref[...])


def _final_norm_rows(h, f, g, row_block, rows):
    blk = pl.BlockSpec((rows, D_MODEL), lambda i: (row_block, 0))
    return pl.pallas_call(
        _final_norm_rows_kernel,
        grid=(1,),
        in_specs=[blk, blk, pl.BlockSpec((1, D_MODEL), lambda i: (0, 0))],
        out_specs=pl.BlockSpec((rows, D_MODEL), lambda i: (0, 0)),
        out_shape=jax.ShapeDtypeStruct((rows, D_MODEL), F32),
        compiler_params=_params("arbitrary"),
    )(h, f, g.reshape(1, D_MODEL))


def _kv_prompt_kernel(*refs):
    ins, outs = refs[:2 * N_DIL], refs[2 * N_DIL:]
    for g in range(N_DIL):
        for kv in range(2):
            src = ins[2 * g + kv]
            heads = [src[0, :, h * HEAD_DIM:(h + 1) * HEAD_DIM] for h in range(ATT_HEADS)]
            outs[g][:, kv, :, :] = jnp.swapaxes(jnp.stack(heads, axis=0), 0, 1)


def _kv_prompt(p3, batch, seq):
    in_specs, out_specs, out_shape = [], [], []
    residue = lambda s: (s % 4) * 4 + s // 4
    for g in range(N_DIL):
        steps = min(DIL_WINDOWS[g], seq) // SLABS
        last = SLAB_ROWS // steps - 1
        for kv in range(2):
            col = COL_ATT + 3 * g + 1 + kv
            in_specs.append(pl.BlockSpec((1, steps, D_MODEL),
                                         lambda b, s, last=last, col=col: (b * SLABS + s, last, col)))
        out_specs.append(pl.BlockSpec((None, steps, None, 2, ATT_HEADS, HEAD_DIM),
                                      lambda b, s: (b, 0, residue(s), 0, 0, 0)))
        out_shape.append(jax.ShapeDtypeStruct((batch, steps, SLABS, 2, ATT_HEADS, HEAD_DIM), F32))
    outs = pl.pallas_call(
        _kv_prompt_kernel,
        grid=(batch, SLABS),
        in_specs=in_specs,
        out_specs=out_specs,
        out_shape=out_shape,
        compiler_params=_params("parallel", "parallel"),
    )(*([p3] * len(in_specs)))
    return [o.reshape(1, batch, -1, 2, ATT_HEADS, HEAD_DIM) for o in outs]


def _row_tile(t, candidates):
    for c in candidates:
        if t % c == 0:
            return c
    raise ValueError(f"no row tile for {t} tokens")


def kernel(x_prompt, x_sample, cache_kv_w128, cache_kv_w512, cache_kv_w2048, norm_mix_g, w_in, sgu_norm_g, sgu_w, sgu_b, w_out, norm_ffn_g, peer_w_q, peer_sub_keys, peer_u, peer_v, norm_final_g):
    batch, seq, _ = x_prompt.shape
    db, ds, _ = x_sample.shape
    assert w_in.shape[0] == 1 and db * ds == CHUNK and seq == SLABS * SLAB_ROWS
    caches = (cache_kv_w128, cache_kv_w512, cache_kv_w2048)
    tp, ts = batch * seq, db * ds
    t = tp + ts
    n_slabs = t // SLAB_ROWS
    tm_big = _row_tile(t, (1040, 640, 128))
    tm_lane = _row_tile(t, (640, 128))

    xs = x_sample.reshape(ts, D_MODEL)
    xl, xn = _permute_norm(x_prompt, norm_mix_g[0])
    xl = lax.dynamic_update_slice(xl, xs, (tp, 0))
    xn = lax.dynamic_update_slice(xn, _rmsnorm_bf16(xs, norm_mix_g[0]), (tp, 0))

    p, att16 = _inproj(xn, _to_bf16(w_in[0], D_MODEL, 1024), t, tm_big, 1024)
    p3 = p.reshape(n_slabs, SLAB_ROWS, p.shape[1])
    att16_3 = att16.reshape(n_slabs, SLAB_ROWS, att16.shape[1])

    w_tril = sgu_w[0] * jnp.tril(jnp.ones((CHUNK, CHUNK), F32))
    tau = _block_steps(0)
    w_p = w_tril[:, tau][:, :, tau].astype(BF16)
    b_p = sgu_b[0][:, tau][..., None]
    w_s = jnp.einsum("bc,gis->gbics", jnp.eye(db, dtype=F32), w_tril[:, :ds, :ds]).reshape(SGU_GROUPS, ts, ts)
    b_s = jnp.tile(sgu_b[0][:, :ds], (1, db))[..., None]
    a_out = _sgu_prompt(p3, sgu_norm_g[0], w_p, b_p, batch).reshape(t, D_MODEL)
    a_s, vn_s = _sgu_sample(p3, sgu_norm_g[0], w_s.astype(BF16), b_s, tp // SLAB_ROWS)
    a_out = lax.dynamic_update_slice(a_out, a_s, (tp, 0))

    att_s = p[tp:, COL_ATT * D_MODEL:COL_GATE * D_MODEL].reshape(db, ds, N_DIL, 3, ATT_HEADS, HEAD_DIM)
    outs, lses = [], []
    o_s, lse_s = _cached_attention(att_s, [c[0] for c in caches])
    for g in range(N_DIL):
        if g == 0:
            o_p, lse_p = _band_attention(p3, COL_ATT, False, g, batch)
        else:
            o_p, lse_p = _band_attention(att16_3, 3 * (g - 1), True, g, batch)
        lse_g = jnp.pad(lse_s[g, ..., 0].reshape(ts, ATT_HEADS), ((0, 0), (0, 128 - ATT_HEADS)))
        o_g = o_s[g].reshape(ts, D_MODEL).astype(o_p.dtype)
        outs.append(lax.dynamic_update_slice(o_p.reshape(t, D_MODEL), o_g, (tp, 0)))
        lses.append(lax.dynamic_update_slice(lse_p.reshape(t, 128), lse_g, (tp, 0)))

    h, hn = _merge(a_out, att16, COL_GATE - COL_ATT16, outs, lses, xl, _to_bf16(w_out[0], 1024, D_MODEL),
                   norm_ffn_g[0], t, 128)

    scores_t = _peer_scores(hn, _to_bf16(peer_w_q[0], 1024, D_MODEL), peer_sub_keys[0].astype(BF16), tm_lane)
    e1, e2, gate, u_b16, v_b16 = _route(scores_t, 128, 2, peer_u[0], peer_v[0])
    g_dense = _expert_weights(e1, e2, gate, 128)
    f = _peer(hn, u_b16, v_b16, g_dense, t, h.shape[0], tm_lane, 1024)

    y_prompt = _final_norm_prompt(h, f, norm_final_g, batch, seq)
    y_sample = _final_norm_rows(h, f, norm_final_g, tp // ts, ts).reshape(db, ds, D_MODEL)
    kv_prompt = _kv_prompt(p3, batch, seq)
    kv_sample = [att_s[:, :, g, 1:3][None] for g in range(N_DIL)]
    sgu_v_sample = vn_s.reshape(1, db, ds, D_MODEL)
    return (y_prompt, y_sample, kv_prompt[0], kv_prompt[1], kv_prompt[2],
            kv_sample[0], kv_sample[1], kv_sample[2], sgu_v_sample)
```

```python
import functools

import numpy as np
import jax
import jax.numpy as jnp
from jax import lax
from jax.experimental import pallas as pl
from jax.experimental.pallas import tpu as pltpu

F32 = jnp.float32
BF16 = jnp.bfloat16

D_MODEL = 2048
HEAD_DIM = 128
ATT_HEADS = D_MODEL // HEAD_DIM
N_DIL = 3
DIL_WINDOWS = (128, 512, 2048)
DIL_RATES = (1, 4, 16)
ATT_BLOCK = 128
SLABS = DIL_RATES[-1]
SLAB_ROWS = 128
SGU_GROUPS = 8
SGU_GROUP_DIM = D_MODEL // SGU_GROUPS
CHUNK = 128
N_COL_BLOCKS = 13
COL_U, COL_V, COL_ATT, COL_GATE = 0, 1, 2, 11
COL_ATT16 = COL_ATT + 3
PEER_HEADS = 8
PEER_TOPK = 16
N_KEYS = 128
N_EXPERTS = N_KEYS * N_KEYS
PEER_HALF = 128
NORM_EPS = 1e-6
MASK_VALUE = -1e30
NEG_INF = float("-inf")
VMEM_LIMIT = 56 * 1024 * 1024

BLOCK_SHAPES = ((16, 8), (4, 32), (1, 128))


def _alibi_slopes():
    n = N_DIL * ATT_HEADS
    e = np.arange(1, n + 1, dtype=np.float32)
    return np.exp2(np.float32(-8.0) * e / np.float32(n)).astype(np.float32).reshape(N_DIL, ATT_HEADS)


def _block_steps(group):
    slabs, rows = BLOCK_SHAPES[group]
    n = np.arange(slabs * rows)
    s, j = n // rows, n % rows
    if group == 0:
        return j * 16 + (s % 4) * 4 + s // 4
    if group == 1:
        return j * 4 + s
    return j


def _params(*sem):
    return pltpu.CompilerParams(dimension_semantics=sem, vmem_limit_bytes=VMEM_LIMIT)


def _rms(x, g):
    return x * lax.rsqrt(jnp.mean(x * x, axis=-1, keepdims=True) + NORM_EPS) * g


def _permute_norm_kernel(x_ref, g_ref, xl_ref, xn_ref):
    xs = jnp.swapaxes(x_ref[...], 0, 1)
    for rr in range(8):
        x = xs[rr]
        xl_ref[rr % 4, rr // 4] = x
        xn_ref[rr % 4, rr // 4] = _rms(x, g_ref[...]).astype(xn_ref.dtype)


def _permute_norm(x_prompt, g):
    batch, seq, _ = x_prompt.shape
    x4 = x_prompt.reshape(batch, SLAB_ROWS, SLABS, D_MODEL)
    out = pl.BlockSpec((None, 4, 2, SLAB_ROWS, D_MODEL), lambda b, h: (b, 0, h, 0, 0))
    xl, xn = pl.pallas_call(
        _permute_norm_kernel,
        grid=(batch, 2),
        in_specs=[pl.BlockSpec((None, SLAB_ROWS, 8, D_MODEL), lambda b, h: (b, 0, h, 0)),
                  pl.BlockSpec((1, D_MODEL), lambda b, h: (0, 0))],
        out_specs=[out, out],
        out_shape=[jax.ShapeDtypeStruct((batch + 1, 4, 4, SLAB_ROWS, D_MODEL), F32),
                   jax.ShapeDtypeStruct((batch + 1, 4, 4, SLAB_ROWS, D_MODEL), BF16)],
        compiler_params=_params("parallel", "parallel"),
    )(x4, g.reshape(1, D_MODEL))
    rows = (batch + 1) * seq
    return xl.reshape(rows, D_MODEL), xn.reshape(rows, D_MODEL)


def _rmsnorm_kernel(x_ref, g_ref, o_ref):
    o_ref[...] = _rms(x_ref[...], g_ref[...]).astype(o_ref.dtype)


def _rmsnorm_bf16(x, g):
    return pl.pallas_call(
        _rmsnorm_kernel,
        out_shape=jax.ShapeDtypeStruct(x.shape, BF16),
    )(x, g.reshape(1, D_MODEL))


def _cast_kernel(x_ref, o_ref):
    o_ref[...] = x_ref[...].astype(o_ref.dtype)


def _to_bf16(w, rows, cols):
    r, c = w.shape
    return pl.pallas_call(
        _cast_kernel,
        grid=(r // rows, c // cols),
        in_specs=[pl.BlockSpec((rows, cols), lambda i, j: (i, j))],
        out_specs=pl.BlockSpec((rows, cols), lambda i, j: (i, j)),
        out_shape=jax.ShapeDtypeStruct((r, c), BF16),
        compiler_params=_params("parallel", "parallel"),
    )(w)


def _inproj_kernel(x_ref, w_ref, o_ref, att_ref, *, blocks_per_col):
    cb = pl.program_id(1) // blocks_per_col
    p = jnp.dot(x_ref[...], w_ref[...], preferred_element_type=F32)

    @pl.when(cb < COL_ATT)
    def _():
        o_ref[...] = jax.nn.gelu(p)

    @pl.when((cb >= COL_ATT) & (cb < COL_ATT16))
    def _():
        o_ref[...] = p

    @pl.when((cb >= COL_ATT16) & (cb < COL_GATE))
    def _():
        o_ref[...] = p
        is_q = (cb - COL_ATT) % 3 == 0
        att_ref[...] = (p * jnp.where(is_q, HEAD_DIM ** -0.5, 1.0)).astype(att_ref.dtype)

    @pl.when(cb >= COL_GATE)
    def _():
        att_ref[...] = jax.nn.sigmoid(p).astype(att_ref.dtype)


def _inproj(xn, w, t, tm, tn):
    n = w.shape[1]
    per_col = D_MODEL // tn
    n_f32 = COL_GATE * per_col
    n_b16 = (N_COL_BLOCKS - COL_ATT16) * per_col
    f32_block = lambda j: jnp.minimum(j, n_f32 - 1)
    b16_block = lambda j: jnp.maximum(j - COL_ATT16 * per_col, 0)
    return pl.pallas_call(
        functools.partial(_inproj_kernel, blocks_per_col=per_col),
        grid=(t // tm, n // tn),
        in_specs=[pl.BlockSpec((tm, D_MODEL), lambda i, j: (i, 0)),
                  pl.BlockSpec((D_MODEL, tn), lambda i, j: (0, j))],
        out_specs=[pl.BlockSpec((tm, tn), lambda i, j: (i, f32_block(j))),
                   pl.BlockSpec((tm, tn), lambda i, j: (i, b16_block(j)))],
        out_shape=[jax.ShapeDtypeStruct((t, n_f32 * tn), F32),
                   jax.ShapeDtypeStruct((t, n_b16 * tn), BF16)],
        compiler_params=_params("parallel", "arbitrary"),
    )(xn, w)


def _sgu_kernel(u_ref, v_ref, g_ref, w_ref, b_ref, a_ref, vn_ref):
    v = v_ref[...].reshape(CHUNK, D_MODEL)
    u = u_ref[...].reshape(CHUNK, D_MODEL)
    vc = v - jnp.mean(v, axis=-1, keepdims=True)
    vn = vc * lax.rsqrt(jnp.mean(vc * vc, axis=-1, keepdims=True) + NORM_EPS) * g_ref[...]
    if vn_ref is not None:
        vn_ref[...] = vn
    cols = []
    for g in range(SGU_GROUPS):
        cs = slice(g * SGU_GROUP_DIM, (g + 1) * SGU_GROUP_DIM)
        mix = jnp.dot(w_ref[g], vn[:, cs].astype(BF16), preferred_element_type=F32) + b_ref[g]
        cols.append(u[:, cs] * mix)
    a_ref[...] = jnp.concatenate(cols, axis=-1).reshape(a_ref.shape)


def _sgu_prompt(p3, sgu_norm_g, w, b, batch):
    slabs, rows = BLOCK_SHAPES[0]
    n_chunks = SLAB_ROWS // rows
    blk = lambda col: pl.BlockSpec((slabs, rows, D_MODEL), lambda bb, c: (bb, c, col))
    return pl.pallas_call(
        lambda u, v, g, ww, bb, a: _sgu_kernel(u, v, g, ww, bb, a, None),
        grid=(batch, n_chunks),
        in_specs=[blk(COL_U), blk(COL_V),
                  pl.BlockSpec((1, D_MODEL), lambda bb, c: (0, 0)),
                  pl.BlockSpec((SGU_GROUPS, CHUNK, CHUNK), lambda bb, c: (0, 0, 0)),
                  pl.BlockSpec((SGU_GROUPS, CHUNK, 1), lambda bb, c: (0, 0, 0))],
        out_specs=blk(0),
        out_shape=jax.ShapeDtypeStruct((p3.shape[0], SLAB_ROWS, D_MODEL), F32),
        compiler_params=_params("parallel", "parallel"),
    )(p3, p3, sgu_norm_g.reshape(1, D_MODEL), w, b)


def _sgu_sample(p3, sgu_norm_g, w, b, slab):
    blk = lambda col: pl.BlockSpec((1, SLAB_ROWS, D_MODEL), lambda i: (slab, 0, col))
    return pl.pallas_call(
        _sgu_kernel,
        grid=(1,),
        in_specs=[blk(COL_U), blk(COL_V),
                  pl.BlockSpec((1, D_MODEL), lambda i: (0, 0)),
                  pl.BlockSpec((SGU_GROUPS, CHUNK, CHUNK), lambda i: (0, 0, 0)),
                  pl.BlockSpec((SGU_GROUPS, CHUNK, 1), lambda i: (0, 0, 0))],
        out_specs=[pl.BlockSpec((CHUNK, D_MODEL), lambda i: (0, 0)),
                   pl.BlockSpec((CHUNK, D_MODEL), lambda i: (0, 0))],
        out_shape=[jax.ShapeDtypeStruct((CHUNK, D_MODEL), F32), jax.ShapeDtypeStruct((CHUNK, D_MODEL), F32)],
        compiler_params=_params("arbitrary"),
    )(p3, p3, sgu_norm_g.reshape(1, D_MODEL), w, b)


def _band_attn_kernel(*refs, has_prev, prescaled):
    if has_prev:
        bc_ref, bp_ref, q_ref, kc_ref, vc_ref, o_ref, lse_ref, kp_ref, vp_ref = refs

        @pl.when(pl.program_id(2) == 0)
        def _():
            kp_ref[...] = jnp.zeros_like(kp_ref)
            vp_ref[...] = jnp.zeros_like(vp_ref)
    else:
        bc_ref, q_ref, kc_ref, vc_ref, o_ref, lse_ref = refs
    n = ATT_BLOCK
    scale = HEAD_DIM ** -0.5
    nt = (((1,), (1,)), ((), ()))
    heads = range(ATT_HEADS)
    hs = [slice(h * HEAD_DIM, (h + 1) * HEAD_DIM) for h in heads]
    ld = lambda ref, h: ref[:, :, hs[h]].reshape(n, HEAD_DIM)
    q = [ld(q_ref, h) if prescaled else (ld(q_ref, h) * scale).astype(BF16) for h in heads]
    s_c = [lax.dot_general(q[h], ld(kc_ref, h).astype(BF16), nt, preferred_element_type=F32) + bc_ref[h]
           for h in heads]
    m = [jnp.max(s_c[h], axis=-1, keepdims=True) for h in heads]
    if has_prev:
        s_p = [lax.dot_general(q[h], ld(kp_ref, h).astype(BF16), nt, preferred_element_type=F32) + bp_ref[h]
               for h in heads]
        m = [jnp.maximum(m[h], jnp.max(s_p[h], axis=-1, keepdims=True)) for h in heads]
    p_c = [jnp.exp(s_c[h] - m[h]) for h in heads]
    l = [jnp.sum(p_c[h], axis=-1, keepdims=True) for h in heads]
    o = [jnp.dot(p_c[h].astype(BF16), ld(vc_ref, h).astype(BF16), preferred_element_type=F32) for h in heads]
    if has_prev:
        p_p = [jnp.exp(s_p[h] - m[h]) for h in heads]
        l = [l[h] + jnp.sum(p_p[h], axis=-1, keepdims=True) for h in heads]
        o = [o[h] + jnp.dot(p_p[h].astype(BF16), ld(vp_ref, h).astype(BF16), preferred_element_type=F32)
             for h in heads]
    lane = lax.broadcasted_iota(jnp.int32, (n, 128), 1)
    lse = jnp.zeros((n, 128), F32)
    for h in heads:
        lse = jnp.where(lane == h, m[h] + jnp.log(l[h]), lse)
    o_ref[...] = jnp.concatenate([o[h] / l[h] for h in heads], axis=-1).reshape(o_ref.shape).astype(o_ref.dtype)
    lse_ref[...] = lse.reshape(lse_ref.shape)
    if has_prev:
        kp_ref[...] = kc_ref[...]
        vp_ref[...] = vc_ref[...]


def _band_bias(group):
    steps = _block_steps(group)
    back = (steps[:, None] - steps[None, :]).astype(np.float32)
    coef = (_alibi_slopes()[group] * np.float32(DIL_RATES[group]))[:, None, None]
    cur = np.where(back >= 0, -(coef * back), np.float32(MASK_VALUE)).astype(np.float32)
    back_p = back + np.float32(ATT_BLOCK)
    prev = np.where(back_p <= ATT_BLOCK, -(coef * back_p), np.float32(MASK_VALUE)).astype(np.float32)
    return cur, np.stack([np.full_like(prev, MASK_VALUE), prev])


def _band_attention(src, col_q, prescaled, group, batch):
    p3 = src
    slabs, rows = BLOCK_SHAPES[group]
    streams = SLABS // slabs
    nb = SLAB_ROWS // rows
    has_prev = nb > 1
    cq, ck, cv = col_q, col_q + 1, col_q + 2
    bias_c, bias_p = _band_bias(group)

    def cur(col, width=D_MODEL):
        return pl.BlockSpec((slabs, rows, width), lambda b, r, j: (b * streams + r, j, col))

    table = (ATT_HEADS, ATT_BLOCK, ATT_BLOCK)
    in_specs = [pl.BlockSpec(table, lambda b, r, j: (0, 0, 0))]
    args = [jnp.asarray(bias_c)]
    scratch = []
    if has_prev:
        in_specs += [pl.BlockSpec((None,) + table, lambda b, r, j: (jnp.minimum(j, 1), 0, 0, 0))]
        args += [jnp.asarray(bias_p)]
        scratch = [pltpu.VMEM((slabs, rows, D_MODEL), p3.dtype)] * 2
    in_specs += [cur(cq), cur(ck), cur(cv)]
    args += [p3] * (len(in_specs) - len(args))
    return pl.pallas_call(
        functools.partial(_band_attn_kernel, has_prev=has_prev, prescaled=prescaled),
        grid=(batch, streams, nb),
        in_specs=in_specs,
        out_specs=[cur(0), cur(0, 128)],
        out_shape=[jax.ShapeDtypeStruct((p3.shape[0], SLAB_ROWS, D_MODEL), BF16 if prescaled else F32),
                   jax.ShapeDtypeStruct((p3.shape[0], SLAB_ROWS, 128), F32)],
        scratch_shapes=scratch,
        compiler_params=_params("parallel", "parallel", "arbitrary"),
    )(*args)


def _cached_attn_kernel(new_ref, c0_ref, c1_ref, c2_ref, coef_ref, o_ref, lse_ref, *, dec_seq):
    caches = (c0_ref, c1_ref, c2_ref)
    shared = [d == 1 for d in DIL_RATES]
    pairs = [(g, i) for g in range(N_DIL) for i in range(dec_seq)]
    res = lambda g, i: 0 if shared[g] else i
    lane = lax.broadcasted_iota(jnp.int32, (ATT_HEADS, ATT_BLOCK), 1)
    scale = HEAD_DIM ** -0.5
    q = {(g, i): new_ref[i, g, 0] for g, i in pairs}

    def qk_body(r, carry):
        out = []
        for n, (g, i) in enumerate(pairs):
            col = jnp.sum(q[g, i] * caches[g][r, res(g, i), 0], axis=-1, keepdims=True)
            out.append(jnp.where(lane == r, col, carry[n]))
        return tuple(out)

    zero = jnp.zeros((ATT_HEADS, ATT_BLOCK), F32)
    s_all = lax.fori_loop(0, ATT_BLOCK, qk_body, (zero,) * len(pairs), unroll=16)

    ps, p_news, ls, ms = [], [], [], []
    for n, (g, i) in enumerate(pairs):
        coef = coef_ref[g]
        if shared[g]:
            back = (ATT_BLOCK + i - lane).astype(F32)
            s = jnp.where(lane >= i, s_all[n] * scale - coef * back, MASK_VALUE)
            s_new = [jnp.sum(q[g, i] * new_ref[i2, g, 1], axis=-1, keepdims=True) * scale
                     - coef[:, :1] * float(i - i2) for i2 in range(i + 1)]
        else:
            s = s_all[n] * scale - coef * (ATT_BLOCK - lane).astype(F32)
            s_new = [jnp.sum(q[g, i] * new_ref[i, g, 1], axis=-1, keepdims=True) * scale]
        m = jnp.max(s, axis=-1, keepdims=True)
        for sn in s_new:
            m = jnp.maximum(m, sn)
        p = jnp.exp(s - m)
        p_new = [jnp.exp(sn - m) for sn in s_new]
        l = jnp.sum(p, axis=-1, keepdims=True)
        for pn in p_new:
            l = l + pn
        ps.append(p)
        p_news.append(p_new)
        ls.append(l)
        ms.append(m)

    def pv_body(r, carry):
        out = []
        for n, (g, i) in enumerate(pairs):
            col = jnp.sum(jnp.where(lane == r, ps[n], 0.0), axis=-1, keepdims=True)
            out.append(carry[n] + col * caches[g][r, res(g, i), 1])
        return tuple(out)

    zero_o = jnp.zeros((ATT_HEADS, HEAD_DIM), F32)
    o_all = lax.fori_loop(0, ATT_BLOCK, pv_body, (zero_o,) * len(pairs), unroll=16)
    for n, (g, i) in enumerate(pairs):
        o = o_all[n]
        for i2, pn in enumerate(p_news[n]):
            o = o + pn * new_ref[i2 if shared[g] else i, g, 2]
        o_ref[g, i] = o / ls[n]
        lse_ref[g, i] = jnp.broadcast_to(ms[n] + jnp.log(ls[n]), (ATT_HEADS, 128))


def _cached_attention(att_s, caches):
    db, dec_seq = att_s.shape[:2]
    views, specs = [], []
    for g, cache in enumerate(caches):
        dil, window = DIL_RATES[g], DIL_WINDOWS[g]
        assert cache.shape[1] == window and window == dil * ATT_BLOCK
        assert dil == 1 or dec_seq <= dil
        nres = 1 if dil == 1 else dec_seq
        views.append(cache.reshape(db, ATT_BLOCK, dil, 2, ATT_HEADS, HEAD_DIM))
        specs.append(pl.BlockSpec((None, ATT_BLOCK, nres, 2, ATT_HEADS, HEAD_DIM), lambda b: (b, 0, 0, 0, 0, 0)))
    coef = np.stack([np.repeat((_alibi_slopes()[g] * DIL_RATES[g])[:, None], 128, axis=1) for g in range(N_DIL)])
    out = lambda w: pl.BlockSpec((N_DIL, None, dec_seq, ATT_HEADS, w), lambda b: (0, b, 0, 0, 0))
    return pl.pallas_call(
        functools.partial(_cached_attn_kernel, dec_seq=dec_seq),
        grid=(db,),
        in_specs=[pl.BlockSpec((None, dec_seq, N_DIL, 3, ATT_HEADS, HEAD_DIM), lambda b: (b, 0, 0, 0, 0, 0))]
        + specs + [pl.BlockSpec((N_DIL, ATT_HEADS, 128), lambda b: (0, 0, 0))],
        out_specs=[out(HEAD_DIM), out(128)],
        out_shape=[jax.ShapeDtypeStruct((N_DIL, db, dec_seq, ATT_HEADS, HEAD_DIM), F32),
                   jax.ShapeDtypeStruct((N_DIL, db, dec_seq, ATT_HEADS, 128), F32)],
        compiler_params=_params("parallel"),
    )(att_s, *views, jnp.asarray(coef.astype(np.float32)))


def _merge_kernel(a_ref, ga_ref, gb_ref, o0_ref, o1_ref, o2_ref, l0_ref, l1_ref, l2_ref,
                  x_ref, w_ref, g_ref, h_ref, hn_ref, merged_ref):
    l0, l1, l2 = l0_ref[...], l1_ref[...], l2_ref[...]
    mx = jnp.maximum(jnp.maximum(l0, l1), l2)
    e0, e1, e2 = jnp.exp(l0 - mx), jnp.exp(l1 - mx), jnp.exp(l2 - mx)
    den = e0 + e1 + e2
    w0, w1, w2 = e0 / den, e1 / den, e2 / den
    for h in range(ATT_HEADS):
        hs = slice(h * HEAD_DIM, (h + 1) * HEAD_DIM)
        b_out = (w0[:, h:h + 1] * o0_ref[:, hs] + w1[:, h:h + 1] * o1_ref[:, hs].astype(F32)
                 + w2[:, h:h + 1] * o2_ref[:, hs].astype(F32))
        merged = ga_ref[:, hs].astype(F32) * a_ref[:, hs] + gb_ref[:, hs].astype(F32) * b_out
        merged_ref[:, hs] = merged.astype(merged_ref.dtype)
    h_new = x_ref[...] + jnp.dot(merged_ref[...], w_ref[...], preferred_element_type=F32)
    h_ref[...] = h_new
    hn_ref[...] = _rms(h_new, g_ref[...]).astype(hn_ref.dtype)


def _merge(a_out, gates, gate_col, outs, lses, x, w_out, norm_g, t, tm):
    row = lambda col: pl.BlockSpec((tm, D_MODEL), lambda i: (i, col))
    lrow = pl.BlockSpec((tm, 128), lambda i: (i, 0))
    p = gates
    return pl.pallas_call(
        _merge_kernel,
        grid=(t // tm,),
        in_specs=[row(0), row(gate_col), row(gate_col + 1), row(0), row(0), row(0), lrow, lrow, lrow,
                  row(0), pl.BlockSpec((D_MODEL, D_MODEL), lambda i: (0, 0)),
                  pl.BlockSpec((1, D_MODEL), lambda i: (0, 0))],
        out_specs=[row(0), row(0)],
        out_shape=[jax.ShapeDtypeStruct((x.shape[0], D_MODEL), F32), jax.ShapeDtypeStruct((t, D_MODEL), BF16)],
        scratch_shapes=[pltpu.VMEM((tm, D_MODEL), BF16)],
        compiler_params=_params("parallel"),
    )(a_out, p, p, outs[0], outs[1], outs[2], lses[0], lses[1], lses[2], x, w_out, norm_g.reshape(1, D_MODEL))


def _peer_score_kernel(hn_ref, wq_ref, keys_ref, s_ref):
    q = jnp.dot(hn_ref[...], wq_ref[...], preferred_element_type=F32).astype(BF16)
    nt = (((1,), (1,)), ((), ()))
    for hc in range(2 * PEER_HEADS):
        cs = slice(hc * PEER_HALF, (hc + 1) * PEER_HALF)
        s_ref[hc] = lax.dot_general(keys_ref[hc % 2], q[:, cs], nt, preferred_element_type=F32)


def _peer_scores(hn, w_q, sub_keys, tm):
    t = hn.shape[0]
    return pl.pallas_call(
        _peer_score_kernel,
        grid=(t // tm,),
        in_specs=[pl.BlockSpec((tm, D_MODEL), lambda i: (i, 0)),
                  pl.BlockSpec((D_MODEL, 2 * PEER_HEADS * PEER_HALF), lambda i: (0, 0)),
                  pl.BlockSpec((2, N_KEYS, PEER_HALF), lambda i: (0, 0, 0))],
        out_specs=pl.BlockSpec((2 * PEER_HEADS, N_KEYS, tm), lambda i: (0, 0, i)),
        out_shape=jax.ShapeDtypeStruct((2 * PEER_HEADS, N_KEYS, t), F32),
        compiler_params=_params("parallel"),
    )(hn, w_q, sub_keys)


def _take_top(arrays, order, count, sentinel):
    arrays = list(arrays)
    vals = [[] for _ in arrays]
    idxs = [[] for _ in arrays]
    for _ in range(count):
        for n, s in enumerate(arrays):
            m = jnp.max(s, axis=0, keepdims=True)
            pos = jnp.min(jnp.where(s == m, order, sentinel), axis=0, keepdims=True)
            vals[n].append(m)
            idxs[n].append(pos)
            arrays[n] = jnp.where(order == pos, NEG_INF, s)
    return vals, idxs


def _sort_network(n):
    pairs, p = [], 1
    while p < n:
        k = p
        while k >= 1:
            for j in range(k % p, n - k, 2 * k):
                for i in range(min(k, n - j - k)):
                    if (i + j) // (2 * p) == (i + j + k) // (2 * p):
                        pairs.append((i + j, i + j + k))
            k //= 2
        p *= 2
    return pairs


def _take_top_sorted(arrays, order, count, sentinel):
    tiles = arrays[0].shape[0] // 8
    assert all(a.shape[0] == 8 * tiles for a in arrays)
    val = [[a[8 * v:8 * v + 8] for v in range(tiles)] for a in arrays]
    idx = [[order[8 * v:8 * v + 8] for v in range(tiles)] for _ in arrays]
    network = [(i, j) for i, j in _sort_network(1 << (tiles - 1).bit_length()) if j < tiles]
    for i, j in network:
        for n in range(len(arrays)):
            vi, vj, ri, rj = val[n][i], val[n][j], idx[n][i], idx[n][j]
            swap = (vj > vi) | ((vj == vi) & (rj < ri))
            val[n][i], val[n][j] = jnp.where(swap, vj, vi), jnp.where(swap, vi, vj)
            idx[n][i], idx[n][j] = jnp.where(swap, rj, ri), jnp.where(swap, ri, rj)
    vals = [[] for _ in arrays]
    idxs = [[] for _ in arrays]
    for t in range(count):
        for n in range(len(arrays)):
            head, rank = val[n][0], idx[n][0]
            m = jnp.max(head, axis=0, keepdims=True)
            pos = jnp.min(jnp.where(head == m, rank, sentinel), axis=0, keepdims=True)
            vals[n].append(m)
            idxs[n].append(pos)
            popped = rank == pos
            for d in range(min(tiles, count - t - 1)):
                if d + 1 < tiles:
                    val[n][d] = jnp.where(popped, val[n][d + 1], val[n][d])
                    idx[n][d] = jnp.where(popped, idx[n][d + 1], idx[n][d])
                else:
                    val[n][d] = jnp.where(popped, NEG_INF, val[n][d])
                    idx[n][d] = jnp.where(popped, sentinel, idx[n][d])
    return vals, idxs


def _stack_rows(rows_list, krow):
    out = jnp.zeros(krow.shape, F32)
    for j, r in enumerate(rows_list):
        out = jnp.where(krow == float(j), r, out)
    return out


def _cand_layout():
    k = PEER_TOPK
    pieces = [("row_a", a, 16 if a == 0 else 8, 0, k // (a + 1)) for a in range(4)]
    pieces += [("col_b", 0, 16, 4, 16), ("col_b", 1, 8, 4, 8), ("col_b", 2, 8, 4, 5)]
    pos = []
    for kind, idx, rows, lo, hi in pieces:
        for r in range(rows):
            a, b = (idx, r) if kind == "row_a" else (r, idx)
            ok = lo <= r < hi and (a + 1) * (b + 1) <= k
            pos.append(a * k + b if ok else k * k)
    assert sorted(p for p in pos if p < k * k) == sorted(
        a * k + b for a in range(k) for b in range(k) if (a + 1) * (b + 1) <= k)
    return pieces, np.asarray(pos, np.float32)


def _route_kernel(s_ref, pos_ref, u_ref, v_ref, e1_ref, e2_ref, gate_ref, ub_ref, vb_ref):
    ub_ref[...] = u_ref[...].astype(ub_ref.dtype)
    vb_ref[...] = v_ref[...].astype(vb_ref.dtype)
    k = PEER_TOPK
    heads = s_ref.shape[0] // 2
    lanes = s_ref.shape[2]
    key_rank = lax.broadcasted_iota(jnp.int32, (N_KEYS, lanes), 0).astype(F32)
    vals, idxs = _take_top_sorted([s_ref[n] for n in range(2 * heads)], key_rank, k, float(N_KEYS))
    krow = lax.broadcasted_iota(jnp.int32, (k, lanes), 0).astype(F32)
    flat = pos_ref[...]
    cands, i1_all, i2_all = [], [], []
    for hd in range(heads):
        v1, v2 = vals[2 * hd], vals[2 * hd + 1]
        v1_all = _stack_rows(v1, krow)
        v2_all = _stack_rows(v2, krow)
        i1_all.append(_stack_rows(idxs[2 * hd], krow))
        i2_all.append(_stack_rows(idxs[2 * hd + 1], krow))
        parts = [v1[idx] + v2_all[:rows] if kind == "row_a" else v1_all[:rows] + v2[idx]
                 for kind, idx, rows, _, _ in _cand_layout()[0]]
        cands.append(jnp.where(flat < float(k * k), jnp.concatenate(parts, axis=0), NEG_INF))
    top_s, pos = _take_top_sorted(cands, flat, k, float(k * k))
    for hd in range(heads):
        e1, e2 = [], []
        for j in range(k):
            a = jnp.floor(pos[hd][j] * (1.0 / k))
            b = pos[hd][j] - a * k
            e1.append(jnp.sum(jnp.where(krow == a, i1_all[hd], 0.0), axis=0, keepdims=True))
            e2.append(jnp.sum(jnp.where(krow == b, i2_all[hd], 0.0), axis=0, keepdims=True))
        ex = jnp.exp(_stack_rows(top_s[hd], krow) - top_s[hd][0])
        rows = slice(hd * k, (hd + 1) * k)
        gate_ref[rows, :] = ex / jnp.sum(ex, axis=0, keepdims=True)
        e1_ref[rows, :] = _stack_rows(e1, krow)
        e2_ref[rows, :] = _stack_rows(e2, krow)


def _route(scores_t, tl, heads_per_step, peer_u, peer_v):
    t = scores_t.shape[2]
    kk = PEER_HEADS * PEER_TOPK
    n_head_steps = PEER_HEADS // heads_per_step
    steps = (t // tl) * n_head_steps
    n_blocks = 1 << (steps.bit_length() - 1)
    tab_rows = N_EXPERTS // n_blocks
    tab = pl.BlockSpec((tab_rows, D_MODEL), lambda i, h: (jnp.minimum(i * n_head_steps + h, n_blocks - 1), 0))
    out = pl.BlockSpec((heads_per_step * PEER_TOPK, tl), lambda i, h: (h, i))
    flat = jnp.asarray(np.repeat(_cand_layout()[1][:, None], tl, axis=1))
    return pl.pallas_call(
        _route_kernel,
        grid=(t // tl, n_head_steps),
        in_specs=[pl.BlockSpec((2 * heads_per_step, N_KEYS, tl), lambda i, h: (h, 0, i)),
                  pl.BlockSpec(flat.shape, lambda i, h: (0, 0)), tab, tab],
        out_specs=[out, out, out, tab, tab],
        out_shape=[jax.ShapeDtypeStruct((kk, t), F32)] * 3 + [jax.ShapeDtypeStruct((N_EXPERTS, D_MODEL), BF16)] * 2,
        compiler_params=_params("arbitrary", "arbitrary"),
    )(scores_t, flat, peer_u, peer_v)


def _expert_weight_kernel(e1_ref, e2_ref, gate_ref, g_ref, e1t_ref, e2t_ref, gt_ref):
    e1t_ref[...] = e1_ref[...].T
    e2t_ref[...] = e2_ref[...].T
    gt_ref[...] = gate_ref[...].T
    kk = e1_ref.shape[0]
    key = lax.broadcasted_iota(jnp.int32, (N_KEYS, kk), 0).astype(F32)
    nt = (((1,), (1,)), ((), ()))

    def body(t, carry):
        r1 = e1t_ref[pl.ds(t, 1), :]
        r2 = e2t_ref[pl.ds(t, 1), :]
        gr = gt_ref[pl.ds(t, 1), :]
        a_t = jnp.where(key == r1, 1.0, 0.0).astype(BF16)
        b_t = jnp.where(key == r2, gr, 0.0).astype(BF16)
        g_ref[t] = lax.dot_general(a_t, b_t, nt, preferred_element_type=F32).astype(g_ref.dtype)
        return carry

    lax.fori_loop(0, g_ref.shape[0], body, 0, unroll=32)


def _expert_weights(e1, e2, gate, tl):
    kk, t = e1.shape
    slot = pl.BlockSpec((kk, tl), lambda i: (0, i))
    return pl.pallas_call(
        _expert_weight_kernel,
        grid=(t // tl,),
        in_specs=[slot, slot, slot],
        out_specs=pl.BlockSpec((tl, N_KEYS, N_KEYS), lambda i: (i, 0, 0)),
        out_shape=jax.ShapeDtypeStruct((t, N_KEYS, N_KEYS), F32),
        scratch_shapes=[pltpu.VMEM((tl, kk), F32)] * 3,
        compiler_params=_params("parallel"),
    )(e1, e2, gate)


def _peer_kernel(hn_ref, u_ref, v_ref, g_ref, y_ref):
    e = pl.program_id(1)
    nt = (((1,), (1,)), ((), ()))
    hk = lax.dot_general(hn_ref[...], u_ref[...], nt, preferred_element_type=F32)
    g = jnp.swapaxes(g_ref[...], 0, 1)
    act = [(jax.nn.gelu(hk[:, a * N_KEYS:(a + 1) * N_KEYS]) * g[a]).astype(BF16) for a in range(g.shape[0])]
    out = jnp.dot(jnp.concatenate(act, axis=-1), v_ref[...], preferred_element_type=F32)

    @pl.when(e == 0)
    def _():
        y_ref[...] = out

    @pl.when(e > 0)
    def _():
        y_ref[...] += out


def _peer(hn, u, v, g, t, rows_out, tm, te):
    row = pl.BlockSpec((tm, D_MODEL), lambda i, e: (i, 0))
    tab = pl.BlockSpec((te, D_MODEL), lambda i, e: (e, 0))
    return pl.pallas_call(
        _peer_kernel,
        grid=(t // tm, N_EXPERTS // te),
        in_specs=[row, tab, tab, pl.BlockSpec((tm, te // N_KEYS, N_KEYS), lambda i, e: (i, e, 0))],
        out_specs=row,
        out_shape=jax.ShapeDtypeStruct((rows_out, D_MODEL), F32),
        compiler_params=_params("parallel", "arbitrary"),
    )(hn, u, v, g)


def _final_norm_prompt_kernel(h_ref, f_ref, g_ref, y_ref):
    y = [_rms(h_ref[rr % 4, rr // 4] + f_ref[rr % 4, rr // 4], g_ref[...]) for rr in range(8)]
    y_ref[...] = jnp.swapaxes(jnp.stack(y, axis=0), 0, 1)


def _final_norm_prompt(h, f, g, batch, seq):
    rows = 64
    h5 = h.reshape(batch + 1, 4, 4, SLAB_ROWS, D_MODEL)
    f5 = f.reshape(batch + 1, 4, 4, SLAB_ROWS, D_MODEL)
    blk = pl.BlockSpec((None, 4, 2, rows, D_MODEL), lambda b, hh, j: (b, 0, hh, j, 0))
    y = pl.pallas_call(
        _final_norm_prompt_kernel,
        grid=(batch, 2, SLAB_ROWS // rows),
        in_specs=[blk, blk, pl.BlockSpec((1, D_MODEL), lambda b, hh, j: (0, 0))],
        out_specs=pl.BlockSpec((None, rows, 8, D_MODEL), lambda b, hh, j: (b, j, hh, 0)),
        out_shape=jax.ShapeDtypeStruct((batch, SLAB_ROWS, SLABS, D_MODEL), F32),
        compiler_params=_params("parallel", "parallel", "parallel"),
    )(h5, f5, g.reshape(1, D_MODEL))
    return y.reshape(batch, seq, D_MODEL)


def _final_norm_rows_kernel(h_ref, f_ref, g_ref, y_ref):
    y_ref[...] = _rms(h_ref[...] + f_ref[...], g_ref[...])


def _final_norm_rows(h, f, g, row_block, rows):
    blk = pl.BlockSpec((rows, D_MODEL), lambda i: (row_block, 0))
    return pl.pallas_call(
        _final_norm_rows_kernel,
        grid=(1,),
        in_specs=[blk, blk, pl.BlockSpec((1, D_MODEL), lambda i: (0, 0))],
        out_specs=pl.BlockSpec((rows, D_MODEL), lambda i: (0, 0)),
        out_shape=jax.ShapeDtypeStruct((rows, D_MODEL), F32),
        compiler_params=_params("arbitrary"),
    )(h, f, g.reshape(1, D_MODEL))


def _kv_prompt_kernel(*refs):
    ins, outs = refs[:2 * N_DIL], refs[2 * N_DIL:]
    for g in range(N_DIL):
        for kv in range(2):
            src = ins[2 * g + kv]
            heads = [src[0, :, h * HEAD_DIM:(h + 1) * HEAD_DIM] for h in range(ATT_HEADS)]
            outs[g][:, kv, :, :] = jnp.swapaxes(jnp.stack(heads, axis=0), 0, 1)


def _kv_prompt(p3, batch, seq):
    in_specs, out_specs, out_shape = [], [], []
    residue = lambda s: (s % 4) * 4 + s // 4
    for g in range(N_DIL):
        steps = min(DIL_WINDOWS[g], seq) // SLABS
        last = SLAB_ROWS // steps - 1
        for kv in range(2):
            col = COL_ATT + 3 * g + 1 + kv
            in_specs.append(pl.BlockSpec((1, steps, D_MODEL),
                                         lambda b, s, last=last, col=col: (b * SLABS + s, last, col)))
        out_specs.append(pl.BlockSpec((None, steps, None, 2, ATT_HEADS, HEAD_DIM),
                                      lambda b, s: (b, 0, residue(s), 0, 0, 0)))
        out_shape.append(jax.ShapeDtypeStruct((batch, steps, SLABS, 2, ATT_HEADS, HEAD_DIM), F32))
    outs = pl.pallas_call(
        _kv_prompt_kernel,
        grid=(batch, SLABS),
        in_specs=in_specs,
        out_specs=out_specs,
        out_shape=out_shape,
        compiler_params=_params("parallel", "parallel"),
    )(*([p3] * len(in_specs)))
    return [o.reshape(1, batch, -1, 2, ATT_HEADS, HEAD_DIM) for o in outs]


def _row_tile(t, candidates):
    for c in candidates:
        if t % c == 0:
            return c
    raise ValueError(f"no row tile for {t} tokens")


def kernel(x_prompt, x_sample, cache_kv_w128, cache_kv_w512, cache_kv_w2048, norm_mix_g, w_in, sgu_norm_g, sgu_w, sgu_b, w_out, norm_ffn_g, peer_w_q, peer_sub_keys, peer_u, peer_v, norm_final_g):
    batch, seq, _ = x_prompt.shape
    db, ds, _ = x_sample.shape
    assert w_in.shape[0] == 1 and db * ds == CHUNK and seq == SLABS * SLAB_ROWS
    caches = (cache_kv_w128, cache_kv_w512, cache_kv_w2048)
    tp, ts = batch * seq, db * ds
    t = tp + ts
    n_slabs = t // SLAB_ROWS
    tm_big = _row_tile(t, (1040, 640, 128))
    tm_lane = _row_tile(t, (640, 128))

    xs = x_sample.reshape(ts, D_MODEL)
    xl, xn = _permute_norm(x_prompt, norm_mix_g[0])
    xl = lax.dynamic_update_slice(xl, xs, (tp, 0))
    xn = lax.dynamic_update_slice(xn, _rmsnorm_bf16(xs, norm_mix_g[0]), (tp, 0))

    p, att16 = _inproj(xn, _to_bf16(w_in[0], D_MODEL, 1024), t, tm_big, 1024)
    p3 = p.reshape(n_slabs, SLAB_ROWS, p.shape[1])
    att16_3 = att16.reshape(n_slabs, SLAB_ROWS, att16.shape[1])

    w_tril = sgu_w[0] * jnp.tril(jnp.ones((CHUNK, CHUNK), F32))
    tau = _block_steps(0)
    w_p = w_tril[:, tau][:, :, tau].astype(BF16)
    b_p = sgu_b[0][:, tau][..., None]
    w_s = jnp.einsum("bc,gis->gbics", jnp.eye(db, dtype=F32), w_tril[:, :ds, :ds]).reshape(SGU_GROUPS, ts, ts)
    b_s = jnp.tile(sgu_b[0][:, :ds], (1, db))[..., None]
    a_out = _sgu_prompt(p3, sgu_norm_g[0], w_p, b_p, batch).reshape(t, D_MODEL)
    a_s, vn_s = _sgu_sample(p3, sgu_norm_g[0], w_s.astype(BF16), b_s, tp // SLAB_ROWS)
    a_out = lax.dynamic_update_slice(a_out, a_s, (tp, 0))

    att_s = p[tp:, COL_ATT * D_MODEL:COL_GATE * D_MODEL].reshape(db, ds, N_DIL, 3, ATT_HEADS, HEAD_DIM)
    outs, lses = [], []
    o_s, lse_s = _cached_attention(att_s, [c[0] for c in caches])
    for g in range(N_DIL):
        if g == 0:
            o_p, lse_p = _band_attention(p3, COL_ATT, False, g, batch)
        else:
            o_p, lse_p = _band_attention(att16_3, 3 * (g - 1), True, g, batch)
        lse_g = jnp.pad(lse_s[g, ..., 0].reshape(ts, ATT_HEADS), ((0, 0), (0, 128 - ATT_HEADS)))
        o_g = o_s[g].reshape(ts, D_MODEL).astype(o_p.dtype)
        outs.append(lax.dynamic_update_slice(o_p.reshape(t, D_MODEL), o_g, (tp, 0)))
        lses.append(lax.dynamic_update_slice(lse_p.reshape(t, 128), lse_g, (tp, 0)))

    h, hn = _merge(a_out, att16, COL_GATE - COL_ATT16, outs, lses, xl, _to_bf16(w_out[0], 1024, D_MODEL),
                   norm_ffn_g[0], t, _row_tile(t, (320, 128)))

    scores_t = _peer_scores(hn, _to_bf16(peer_w_q[0], 1024, D_MODEL), peer_sub_keys[0].astype(BF16), tm_lane)
    e1, e2, gate, u_b16, v_b16 = _route(scores_t, 128, 2, peer_u[0], peer_v[0])
    g_dense = _expert_weights(e1, e2, gate, 128)
    f = _peer(hn, u_b16, v_b16, g_dense, t, h.shape[0], tm_lane, 1024)

    y_prompt = _final_norm_prompt(h, f, norm_final_g, batch, seq)
    y_sample = _final_norm_rows(h, f, norm_final_g, tp // ts, ts).reshape(db, ds, D_MODEL)
    kv_prompt = _kv_prompt(p3, batch, seq)
    kv_sample = [att_s[:, :, g, 1:3][None] for g in range(N_DIL)]
    sgu_v_sample = vn_s.reshape(1, db, ds, D_MODEL)
    return (y_prompt, y_sample, kv_prompt[0], kv_prompt[1], kv_prompt[2],
            kv_sample[0], kv_sample[1], kv_sample[2], sgu_v_sample)
```

```python
import functools

import numpy as np
import jax
import jax.numpy as jnp
from jax import lax
from jax.experimental import pallas as pl
from jax.experimental.pallas import tpu as pltpu

F32 = jnp.float32
BF16 = jnp.bfloat16

D_MODEL = 2048
HEAD_DIM = 128
ATT_HEADS = D_MODEL // HEAD_DIM
N_DIL = 3
DIL_WINDOWS = (128, 512, 2048)
DIL_RATES = (1, 4, 16)
ATT_BLOCK = 128
SLABS = DIL_RATES[-1]
SLAB_ROWS = 128
SGU_GROUPS = 8
SGU_GROUP_DIM = D_MODEL // SGU_GROUPS
CHUNK = 128
N_COL_BLOCKS = 13
COL_U, COL_V, COL_ATT, COL_GATE = 0, 1, 2, 11
COL_ATT16 = COL_ATT + 3
PEER_HEADS = 8
PEER_TOPK = 16
N_KEYS = 128
N_EXPERTS = N_KEYS * N_KEYS
PEER_HALF = 128
NORM_EPS = 1e-6
MASK_VALUE = -1e30
NEG_INF = float("-inf")
VMEM_LIMIT = 56 * 1024 * 1024

BLOCK_SHAPES = ((16, 8), (4, 32), (1, 128))


def _alibi_slopes():
    n = N_DIL * ATT_HEADS
    e = np.arange(1, n + 1, dtype=np.float32)
    return np.exp2(np.float32(-8.0) * e / np.float32(n)).astype(np.float32).reshape(N_DIL, ATT_HEADS)


def _block_steps(group):
    slabs, rows = BLOCK_SHAPES[group]
    n = np.arange(slabs * rows)
    s, j = n // rows, n % rows
    if group == 0:
        return j * 16 + (s % 4) * 4 + s // 4
    if group == 1:
        return j * 4 + s
    return j


def _params(*sem):
    return pltpu.CompilerParams(dimension_semantics=sem, vmem_limit_bytes=VMEM_LIMIT)


def _rms(x, g):
    return x * lax.rsqrt(jnp.mean(x * x, axis=-1, keepdims=True) + NORM_EPS) * g


def _permute_norm_kernel(x_ref, g_ref, xl_ref, xn_ref):
    xs = jnp.swapaxes(x_ref[...], 0, 1)
    for rr in range(8):
        x = xs[rr]
        xl_ref[rr % 4, rr // 4] = x
        xn_ref[rr % 4, rr // 4] = _rms(x, g_ref[...]).astype(xn_ref.dtype)


def _permute_norm(x_prompt, g):
    batch, seq, _ = x_prompt.shape
    x4 = x_prompt.reshape(batch, SLAB_ROWS, SLABS, D_MODEL)
    out = pl.BlockSpec((None, 4, 2, SLAB_ROWS, D_MODEL), lambda b, h: (b, 0, h, 0, 0))
    xl, xn = pl.pallas_call(
        _permute_norm_kernel,
        grid=(batch, 2),
        in_specs=[pl.BlockSpec((None, SLAB_ROWS, 8, D_MODEL), lambda b, h: (b, 0, h, 0)),
                  pl.BlockSpec((1, D_MODEL), lambda b, h: (0, 0))],
        out_specs=[out, out],
        out_shape=[jax.ShapeDtypeStruct((batch + 1, 4, 4, SLAB_ROWS, D_MODEL), F32),
                   jax.ShapeDtypeStruct((batch + 1, 4, 4, SLAB_ROWS, D_MODEL), BF16)],
        compiler_params=_params("parallel", "parallel"),
    )(x4, g.reshape(1, D_MODEL))
    rows = (batch + 1) * seq
    return xl.reshape(rows, D_MODEL), xn.reshape(rows, D_MODEL)


def _rmsnorm_kernel(x_ref, g_ref, o_ref):
    o_ref[...] = _rms(x_ref[...], g_ref[...]).astype(o_ref.dtype)


def _rmsnorm_bf16(x, g):
    return pl.pallas_call(
        _rmsnorm_kernel,
        out_shape=jax.ShapeDtypeStruct(x.shape, BF16),
    )(x, g.reshape(1, D_MODEL))


def _cast_kernel(x_ref, o_ref):
    o_ref[...] = x_ref[...].astype(o_ref.dtype)


def _to_bf16(w, rows, cols):
    r, c = w.shape
    return pl.pallas_call(
        _cast_kernel,
        grid=(r // rows, c // cols),
        in_specs=[pl.BlockSpec((rows, cols), lambda i, j: (i, j))],
        out_specs=pl.BlockSpec((rows, cols), lambda i, j: (i, j)),
        out_shape=jax.ShapeDtypeStruct((r, c), BF16),
        compiler_params=_params("parallel", "parallel"),
    )(w)


def _inproj_kernel(x_ref, w_ref, o_ref, att_ref, w16_ref, *, blocks_per_col):
    cb = pl.program_id(0) // blocks_per_col

    @pl.when(pl.program_id(1) == 0)
    def _():
        w16_ref[...] = w_ref[...].astype(w16_ref.dtype)

    p = jnp.dot(x_ref[...], w16_ref[...], preferred_element_type=F32)

    @pl.when(cb < COL_ATT)
    def _():
        o_ref[...] = jax.nn.gelu(p)

    @pl.when((cb >= COL_ATT) & (cb < COL_ATT16))
    def _():
        o_ref[...] = p

    @pl.when((cb >= COL_ATT16) & (cb < COL_GATE))
    def _():
        o_ref[...] = p
        is_q = (cb - COL_ATT) % 3 == 0
        att_ref[...] = (p * jnp.where(is_q, HEAD_DIM ** -0.5, 1.0)).astype(att_ref.dtype)

    @pl.when(cb >= COL_GATE)
    def _():
        att_ref[...] = jax.nn.sigmoid(p).astype(att_ref.dtype)


def _inproj(xn, w, t, tm, tn):
    n = w.shape[1]
    per_col = D_MODEL // tn
    n_rows = t // tm
    n_f32 = COL_GATE * per_col
    first_b16 = COL_ATT16 * per_col
    n_b16 = (N_COL_BLOCKS - COL_ATT16) * per_col
    f32_block = lambda j, i: (jnp.where(j < n_f32, i, n_rows - 1), jnp.minimum(j, n_f32 - 1))
    b16_block = lambda j, i: (jnp.where(j >= first_b16, i, 0), jnp.maximum(j - first_b16, 0))
    return pl.pallas_call(
        functools.partial(_inproj_kernel, blocks_per_col=per_col),
        grid=(n // tn, n_rows),
        in_specs=[pl.BlockSpec((tm, D_MODEL), lambda j, i: (i, 0)),
                  pl.BlockSpec((D_MODEL, tn), lambda j, i: (0, j))],
        out_specs=[pl.BlockSpec((tm, tn), f32_block),
                   pl.BlockSpec((tm, tn), b16_block)],
        out_shape=[jax.ShapeDtypeStruct((t, n_f32 * tn), F32),
                   jax.ShapeDtypeStruct((t, n_b16 * tn), BF16)],
        scratch_shapes=[pltpu.VMEM((D_MODEL, tn), BF16)],
        compiler_params=_params("arbitrary", "arbitrary"),
    )(xn, w)


def _sgu_kernel(u_ref, v_ref, g_ref, w_ref, b_ref, a_ref, vn_ref):
    v = v_ref[...].reshape(CHUNK, D_MODEL)
    u = u_ref[...].reshape(CHUNK, D_MODEL)
    vc = v - jnp.mean(v, axis=-1, keepdims=True)
    vn = vc * lax.rsqrt(jnp.mean(vc * vc, axis=-1, keepdims=True) + NORM_EPS) * g_ref[...]
    if vn_ref is not None:
        vn_ref[...] = vn
    cols = []
    for g in range(SGU_GROUPS):
        cs = slice(g * SGU_GROUP_DIM, (g + 1) * SGU_GROUP_DIM)
        mix = jnp.dot(w_ref[g], vn[:, cs].astype(BF16), preferred_element_type=F32) + b_ref[g]
        cols.append(u[:, cs] * mix)
    a_ref[...] = jnp.concatenate(cols, axis=-1).reshape(a_ref.shape)


def _sgu_prompt(p3, sgu_norm_g, w, b, batch):
    slabs, rows = BLOCK_SHAPES[0]
    n_chunks = SLAB_ROWS // rows
    blk = lambda col: pl.BlockSpec((slabs, rows, D_MODEL), lambda bb, c: (bb, c, col))
    return pl.pallas_call(
        lambda u, v, g, ww, bb, a: _sgu_kernel(u, v, g, ww, bb, a, None),
        grid=(batch, n_chunks),
        in_specs=[blk(COL_U), blk(COL_V),
                  pl.BlockSpec((1, D_MODEL), lambda bb, c: (0, 0)),
                  pl.BlockSpec((SGU_GROUPS, CHUNK, CHUNK), lambda bb, c: (0, 0, 0)),
                  pl.BlockSpec((SGU_GROUPS, CHUNK, 1), lambda bb, c: (0, 0, 0))],
        out_specs=blk(0),
        out_shape=jax.ShapeDtypeStruct((p3.shape[0], SLAB_ROWS, D_MODEL), F32),
        compiler_params=_params("parallel", "parallel"),
    )(p3, p3, sgu_norm_g.reshape(1, D_MODEL), w, b)


def _sgu_sample(p3, sgu_norm_g, w, b, slab):
    blk = lambda col: pl.BlockSpec((1, SLAB_ROWS, D_MODEL), lambda i: (slab, 0, col))
    return pl.pallas_call(
        _sgu_kernel,
        grid=(1,),
        in_specs=[blk(COL_U), blk(COL_V),
                  pl.BlockSpec((1, D_MODEL), lambda i: (0, 0)),
                  pl.BlockSpec((SGU_GROUPS, CHUNK, CHUNK), lambda i: (0, 0, 0)),
                  pl.BlockSpec((SGU_GROUPS, CHUNK, 1), lambda i: (0, 0, 0))],
        out_specs=[pl.BlockSpec((CHUNK, D_MODEL), lambda i: (0, 0)),
                   pl.BlockSpec((CHUNK, D_MODEL), lambda i: (0, 0))],
        out_shape=[jax.ShapeDtypeStruct((CHUNK, D_MODEL), F32), jax.ShapeDtypeStruct((CHUNK, D_MODEL), F32)],
        compiler_params=_params("arbitrary"),
    )(p3, p3, sgu_norm_g.reshape(1, D_MODEL), w, b)


def _band_attn_kernel(*refs, has_prev, prescaled):
    if has_prev:
        bc_ref, bp_ref, q_ref, kc_ref, vc_ref, o_ref, lse_ref, kp_ref, vp_ref = refs

        @pl.when(pl.program_id(2) == 0)
        def _():
            kp_ref[...] = jnp.zeros_like(kp_ref)
            vp_ref[...] = jnp.zeros_like(vp_ref)
    else:
        bc_ref, q_ref, kc_ref, vc_ref, o_ref, lse_ref = refs
    n = ATT_BLOCK
    scale = HEAD_DIM ** -0.5
    nt = (((1,), (1,)), ((), ()))
    heads = range(ATT_HEADS)
    hs = [slice(h * HEAD_DIM, (h + 1) * HEAD_DIM) for h in heads]
    ld = lambda ref, h: ref[:, :, hs[h]].reshape(n, HEAD_DIM)
    q = [ld(q_ref, h) if prescaled else (ld(q_ref, h) * scale).astype(BF16) for h in heads]
    s_c = [lax.dot_general(q[h], ld(kc_ref, h).astype(BF16), nt, preferred_element_type=F32) + bc_ref[h]
           for h in heads]
    m = [jnp.max(s_c[h], axis=-1, keepdims=True) for h in heads]
    if has_prev:
        s_p = [lax.dot_general(q[h], ld(kp_ref, h).astype(BF16), nt, preferred_element_type=F32) + bp_ref[h]
               for h in heads]
        m = [jnp.maximum(m[h], jnp.max(s_p[h], axis=-1, keepdims=True)) for h in heads]
    p_c = [jnp.exp(s_c[h] - m[h]) for h in heads]
    l = [jnp.sum(p_c[h], axis=-1, keepdims=True) for h in heads]
    o = [jnp.dot(p_c[h].astype(BF16), ld(vc_ref, h).astype(BF16), preferred_element_type=F32) for h in heads]
    if has_prev:
        p_p = [jnp.exp(s_p[h] - m[h]) for h in heads]
        l = [l[h] + jnp.sum(p_p[h], axis=-1, keepdims=True) for h in heads]
        o = [o[h] + jnp.dot(p_p[h].astype(BF16), ld(vp_ref, h).astype(BF16), preferred_element_type=F32)
             for h in heads]
    lane = lax.broadcasted_iota(jnp.int32, (n, 128), 1)
    lse = jnp.zeros((n, 128), F32)
    for h in heads:
        lse = jnp.where(lane == h, m[h] + jnp.log(l[h]), lse)
    o_ref[...] = jnp.concatenate([o[h] / l[h] for h in heads], axis=-1).reshape(o_ref.shape).astype(o_ref.dtype)
    lse_ref[...] = lse.reshape(lse_ref.shape)
    if has_prev:
        kp_ref[...] = kc_ref[...]
        vp_ref[...] = vc_ref[...]


def _band_bias(group):
    steps = _block_steps(group)
    back = (steps[:, None] - steps[None, :]).astype(np.float32)
    coef = (_alibi_slopes()[group] * np.float32(DIL_RATES[group]))[:, None, None]
    cur = np.where(back >= 0, -(coef * back), np.float32(MASK_VALUE)).astype(np.float32)
    back_p = back + np.float32(ATT_BLOCK)
    prev = np.where(back_p <= ATT_BLOCK, -(coef * back_p), np.float32(MASK_VALUE)).astype(np.float32)
    return cur, np.stack([np.full_like(prev, MASK_VALUE), prev])


def _band_attention(src, col_q, prescaled, group, batch):
    p3 = src
    slabs, rows = BLOCK_SHAPES[group]
    streams = SLABS // slabs
    nb = SLAB_ROWS // rows
    has_prev = nb > 1
    cq, ck, cv = col_q, col_q + 1, col_q + 2
    bias_c, bias_p = _band_bias(group)

    def cur(col, width=D_MODEL):
        return pl.BlockSpec((slabs, rows, width), lambda b, r, j: (b * streams + r, j, col))

    table = (ATT_HEADS, ATT_BLOCK, ATT_BLOCK)
    in_specs = [pl.BlockSpec(table, lambda b, r, j: (0, 0, 0))]
    args = [jnp.asarray(bias_c)]
    scratch = []
    if has_prev:
        in_specs += [pl.BlockSpec((None,) + table, lambda b, r, j: (jnp.minimum(j, 1), 0, 0, 0))]
        args += [jnp.asarray(bias_p)]
        scratch = [pltpu.VMEM((slabs, rows, D_MODEL), p3.dtype)] * 2
    in_specs += [cur(cq), cur(ck), cur(cv)]
    args += [p3] * (len(in_specs) - len(args))
    return pl.pallas_call(
        functools.partial(_band_attn_kernel, has_prev=has_prev, prescaled=prescaled),
        grid=(batch, streams, nb),
        in_specs=in_specs,
        out_specs=[cur(0), cur(0, 128)],
        out_shape=[jax.ShapeDtypeStruct((p3.shape[0], SLAB_ROWS, D_MODEL), BF16 if prescaled else F32),
                   jax.ShapeDtypeStruct((p3.shape[0], SLAB_ROWS, 128), F32)],
        scratch_shapes=scratch,
        compiler_params=_params("parallel", "parallel", "arbitrary"),
    )(*args)


def _cached_attn_kernel(new_ref, c0_ref, c1_ref, c2_ref, coef_ref, o_ref, lse_ref, *, dec_seq):
    caches = (c0_ref, c1_ref, c2_ref)
    shared = [d == 1 for d in DIL_RATES]
    pairs = [(g, i) for g in range(N_DIL) for i in range(dec_seq)]
    res = lambda g, i: 0 if shared[g] else i
    lane = lax.broadcasted_iota(jnp.int32, (ATT_HEADS, ATT_BLOCK), 1)
    scale = HEAD_DIM ** -0.5
    q = {(g, i): new_ref[i, g, 0] for g, i in pairs}

    def qk_body(r, carry):
        out = []
        for n, (g, i) in enumerate(pairs):
            col = jnp.sum(q[g, i] * caches[g][r, res(g, i), 0], axis=-1, keepdims=True)
            out.append(jnp.where(lane == r, col, carry[n]))
        return tuple(out)

    zero = jnp.zeros((ATT_HEADS, ATT_BLOCK), F32)
    s_all = lax.fori_loop(0, ATT_BLOCK, qk_body, (zero,) * len(pairs), unroll=16)

    ps, p_news, ls, ms = [], [], [], []
    for n, (g, i) in enumerate(pairs):
        coef = coef_ref[g]
        if shared[g]:
            back = (ATT_BLOCK + i - lane).astype(F32)
            s = jnp.where(lane >= i, s_all[n] * scale - coef * back, MASK_VALUE)
            s_new = [jnp.sum(q[g, i] * new_ref[i2, g, 1], axis=-1, keepdims=True) * scale
                     - coef[:, :1] * float(i - i2) for i2 in range(i + 1)]
        else:
            s = s_all[n] * scale - coef * (ATT_BLOCK - lane).astype(F32)
            s_new = [jnp.sum(q[g, i] * new_ref[i, g, 1], axis=-1, keepdims=True) * scale]
        m = jnp.max(s, axis=-1, keepdims=True)
        for sn in s_new:
            m = jnp.maximum(m, sn)
        p = jnp.exp(s - m)
        p_new = [jnp.exp(sn - m) for sn in s_new]
        l = jnp.sum(p, axis=-1, keepdims=True)
        for pn in p_new:
            l = l + pn
        ps.append(p)
        p_news.append(p_new)
        ls.append(l)
        ms.append(m)

    def pv_body(r, carry):
        out = []
        for n, (g, i) in enumerate(pairs):
            col = jnp.sum(jnp.where(lane == r, ps[n], 0.0), axis=-1, keepdims=True)
            out.append(carry[n] + col * caches[g][r, res(g, i), 1])
        return tuple(out)

    zero_o = jnp.zeros((ATT_HEADS, HEAD_DIM), F32)
    o_all = lax.fori_loop(0, ATT_BLOCK, pv_body, (zero_o,) * len(pairs), unroll=16)
    for n, (g, i) in enumerate(pairs):
        o = o_all[n]
        for i2, pn in enumerate(p_news[n]):
            o = o + pn * new_ref[i2 if shared[g] else i, g, 2]
        o_ref[g, i] = o / ls[n]
        lse_ref[g, i] = jnp.broadcast_to(ms[n] + jnp.log(ls[n]), (ATT_HEADS, 128))


def _cached_attention(att_s, caches):
    db, dec_seq = att_s.shape[:2]
    views, specs = [], []
    for g, cache in enumerate(caches):
        dil, window = DIL_RATES[g], DIL_WINDOWS[g]
        assert cache.shape[1] == window and window == dil * ATT_BLOCK
        assert dil == 1 or dec_seq <= dil
        nres = 1 if dil == 1 else dec_seq
        views.append(cache.reshape(db, ATT_BLOCK, dil, 2, ATT_HEADS, HEAD_DIM))
        specs.append(pl.BlockSpec((None, ATT_BLOCK, nres, 2, ATT_HEADS, HEAD_DIM), lambda b: (b, 0, 0, 0, 0, 0)))
    coef = np.stack([np.repeat((_alibi_slopes()[g] * DIL_RATES[g])[:, None], 128, axis=1) for g in range(N_DIL)])
    out = lambda w: pl.BlockSpec((N_DIL, None, dec_seq, ATT_HEADS, w), lambda b: (0, b, 0, 0, 0))
    return pl.pallas_call(
        functools.partial(_cached_attn_kernel, dec_seq=dec_seq),
        grid=(db,),
        in_specs=[pl.BlockSpec((None, dec_seq, N_DIL, 3, ATT_HEADS, HEAD_DIM), lambda b: (b, 0, 0, 0, 0, 0))]
        + specs + [pl.BlockSpec((N_DIL, ATT_HEADS, 128), lambda b: (0, 0, 0))],
        out_specs=[out(HEAD_DIM), out(128)],
        out_shape=[jax.ShapeDtypeStruct((N_DIL, db, dec_seq, ATT_HEADS, HEAD_DIM), F32),
                   jax.ShapeDtypeStruct((N_DIL, db, dec_seq, ATT_HEADS, 128), F32)],
        compiler_params=_params("parallel"),
    )(att_s, *views, jnp.asarray(coef.astype(np.float32)))


def _merge_kernel(a_ref, ga_ref, gb_ref, o0_ref, o1_ref, o2_ref, l0_ref, l1_ref, l2_ref,
                  x_ref, w_ref, g_ref, h_ref, hn_ref, merged_ref):
    l0, l1, l2 = l0_ref[...], l1_ref[...], l2_ref[...]
    mx = jnp.maximum(jnp.maximum(l0, l1), l2)
    e0, e1, e2 = jnp.exp(l0 - mx), jnp.exp(l1 - mx), jnp.exp(l2 - mx)
    den = e0 + e1 + e2
    w0, w1, w2 = e0 / den, e1 / den, e2 / den
    for h in range(ATT_HEADS):
        hs = slice(h * HEAD_DIM, (h + 1) * HEAD_DIM)
        b_out = (w0[:, h:h + 1] * o0_ref[:, hs] + w1[:, h:h + 1] * o1_ref[:, hs].astype(F32)
                 + w2[:, h:h + 1] * o2_ref[:, hs].astype(F32))
        merged = ga_ref[:, hs].astype(F32) * a_ref[:, hs] + gb_ref[:, hs].astype(F32) * b_out
        merged_ref[:, hs] = merged.astype(merged_ref.dtype)
    h_new = x_ref[...] + jnp.dot(merged_ref[...], w_ref[...], preferred_element_type=F32)
    h_ref[...] = h_new
    hn_ref[...] = _rms(h_new, g_ref[...]).astype(hn_ref.dtype)


def _merge(a_out, gates, gate_col, outs, lses, x, w_out, norm_g, t, tm):
    row = lambda col: pl.BlockSpec((tm, D_MODEL), lambda i: (i, col))
    lrow = pl.BlockSpec((tm, 128), lambda i: (i, 0))
    p = gates
    return pl.pallas_call(
        _merge_kernel,
        grid=(t // tm,),
        in_specs=[row(0), row(gate_col), row(gate_col + 1), row(0), row(0), row(0), lrow, lrow, lrow,
                  row(0), pl.BlockSpec((D_MODEL, D_MODEL), lambda i: (0, 0)),
                  pl.BlockSpec((1, D_MODEL), lambda i: (0, 0))],
        out_specs=[row(0), row(0)],
        out_shape=[jax.ShapeDtypeStruct((x.shape[0], D_MODEL), F32), jax.ShapeDtypeStruct((t, D_MODEL), BF16)],
        scratch_shapes=[pltpu.VMEM((tm, D_MODEL), BF16)],
        compiler_params=_params("parallel"),
    )(a_out, p, p, outs[0], outs[1], outs[2], lses[0], lses[1], lses[2], x, w_out, norm_g.reshape(1, D_MODEL))


def _peer_score_kernel(hn_ref, wq_ref, keys_ref, s_ref):
    q = jnp.dot(hn_ref[...], wq_ref[...], preferred_element_type=F32).astype(BF16)
    nt = (((1,), (1,)), ((), ()))
    for hc in range(2 * PEER_HEADS):
        cs = slice(hc * PEER_HALF, (hc + 1) * PEER_HALF)
        s_ref[hc] = lax.dot_general(keys_ref[hc % 2], q[:, cs], nt, preferred_element_type=F32)


def _peer_scores(hn, w_q, sub_keys, tm):
    t = hn.shape[0]
    return pl.pallas_call(
        _peer_score_kernel,
        grid=(t // tm,),
        in_specs=[pl.BlockSpec((tm, D_MODEL), lambda i: (i, 0)),
                  pl.BlockSpec((D_MODEL, 2 * PEER_HEADS * PEER_HALF), lambda i: (0, 0)),
                  pl.BlockSpec((2, N_KEYS, PEER_HALF), lambda i: (0, 0, 0))],
        out_specs=pl.BlockSpec((2 * PEER_HEADS, N_KEYS, tm), lambda i: (0, 0, i)),
        out_shape=jax.ShapeDtypeStruct((2 * PEER_HEADS, N_KEYS, t), F32),
        compiler_params=_params("parallel"),
    )(hn, w_q, sub_keys)


def _take_top(arrays, order, count, sentinel):
    arrays = list(arrays)
    vals = [[] for _ in arrays]
    idxs = [[] for _ in arrays]
    for _ in range(count):
        for n, s in enumerate(arrays):
            m = jnp.max(s, axis=0, keepdims=True)
            pos = jnp.min(jnp.where(s == m, order, sentinel), axis=0, keepdims=True)
            vals[n].append(m)
            idxs[n].append(pos)
            arrays[n] = jnp.where(order == pos, NEG_INF, s)
    return vals, idxs


def _sort_network(n):
    pairs, p = [], 1
    while p < n:
        k = p
        while k >= 1:
            for j in range(k % p, n - k, 2 * k):
                for i in range(min(k, n - j - k)):
                    if (i + j) // (2 * p) == (i + j + k) // (2 * p):
                        pairs.append((i + j, i + j + k))
            k //= 2
        p *= 2
    return pairs


def _take_top_sorted(arrays, order, count, sentinel):
    tiles = arrays[0].shape[0] // 8
    assert all(a.shape[0] == 8 * tiles for a in arrays)
    val = [[a[8 * v:8 * v + 8] for v in range(tiles)] for a in arrays]
    idx = [[order[8 * v:8 * v + 8] for v in range(tiles)] for _ in arrays]
    network = [(i, j) for i, j in _sort_network(1 << (tiles - 1).bit_length()) if j < tiles]
    for i, j in network:
        for n in range(len(arrays)):
            vi, vj, ri, rj = val[n][i], val[n][j], idx[n][i], idx[n][j]
            swap = (vj > vi) | ((vj == vi) & (rj < ri))
            val[n][i], val[n][j] = jnp.where(swap, vj, vi), jnp.where(swap, vi, vj)
            idx[n][i], idx[n][j] = jnp.where(swap, rj, ri), jnp.where(swap, ri, rj)
    vals = [[] for _ in arrays]
    idxs = [[] for _ in arrays]
    for t in range(count):
        for n in range(len(arrays)):
            head, rank = val[n][0], idx[n][0]
            m = jnp.max(head, axis=0, keepdims=True)
            pos = jnp.min(jnp.where(head == m, rank, sentinel), axis=0, keepdims=True)
            vals[n].append(m)
            idxs[n].append(pos)
            popped = rank == pos
            for d in range(min(tiles, count - t - 1)):
                if d + 1 < tiles:
                    val[n][d] = jnp.where(popped, val[n][d + 1], val[n][d])
                    idx[n][d] = jnp.where(popped, idx[n][d + 1], idx[n][d])
                else:
                    val[n][d] = jnp.where(popped, NEG_INF, val[n][d])
                    idx[n][d] = jnp.where(popped, sentinel, idx[n][d])
    return vals, idxs


def _stack_rows(rows_list, krow):
    out = jnp.zeros(krow.shape, F32)
    for j, r in enumerate(rows_list):
        out = jnp.where(krow == float(j), r, out)
    return out


def _cand_layout():
    k = PEER_TOPK
    pieces = [("row_a", a, 16 if a == 0 else 8, 0, k // (a + 1)) for a in range(4)]
    pieces += [("col_b", 0, 16, 4, 16), ("col_b", 1, 8, 4, 8), ("col_b", 2, 8, 4, 5)]
    pos = []
    for kind, idx, rows, lo, hi in pieces:
        for r in range(rows):
            a, b = (idx, r) if kind == "row_a" else (r, idx)
            ok = lo <= r < hi and (a + 1) * (b + 1) <= k
            pos.append(a * k + b if ok else k * k)
    assert sorted(p for p in pos if p < k * k) == sorted(
        a * k + b for a in range(k) for b in range(k) if (a + 1) * (b + 1) <= k)
    return pieces, np.asarray(pos, np.float32)


def _route_kernel(s_ref, pos_ref, u_ref, v_ref, e1_ref, e2_ref, gate_ref, ub_ref, vb_ref):
    ub_ref[...] = u_ref[...].astype(ub_ref.dtype)
    vb_ref[...] = v_ref[...].astype(vb_ref.dtype)
    k = PEER_TOPK
    heads = s_ref.shape[0] // 2
    lanes = s_ref.shape[2]
    key_rank = lax.broadcasted_iota(jnp.int32, (N_KEYS, lanes), 0).astype(F32)
    vals, idxs = _take_top_sorted([s_ref[n] for n in range(2 * heads)], key_rank, k, float(N_KEYS))
    krow = lax.broadcasted_iota(jnp.int32, (k, lanes), 0).astype(F32)
    flat = pos_ref[...]
    cands, i1_all, i2_all = [], [], []
    for hd in range(heads):
        v1, v2 = vals[2 * hd], vals[2 * hd + 1]
        v1_all = _stack_rows(v1, krow)
        v2_all = _stack_rows(v2, krow)
        i1_all.append(_stack_rows(idxs[2 * hd], krow))
        i2_all.append(_stack_rows(idxs[2 * hd + 1], krow))
        parts = [v1[idx] + v2_all[:rows] if kind == "row_a" else v1_all[:rows] + v2[idx]
                 for kind, idx, rows, _, _ in _cand_layout()[0]]
        cands.append(jnp.where(flat < float(k * k), jnp.concatenate(parts, axis=0), NEG_INF))
    top_s, pos = _take_top_sorted(cands, flat, k, float(k * k))
    for hd in range(heads):
        e1, e2 = [], []
        for j in range(k):
            a = jnp.floor(pos[hd][j] * (1.0 / k))
            b = pos[hd][j] - a * k
            e1.append(jnp.sum(jnp.where(krow == a, i1_all[hd], 0.0), axis=0, keepdims=True))
            e2.append(jnp.sum(jnp.where(krow == b, i2_all[hd], 0.0), axis=0, keepdims=True))
        ex = jnp.exp(_stack_rows(top_s[hd], krow) - top_s[hd][0])
        rows = slice(hd * k, (hd + 1) * k)
        gate_ref[rows, :] = ex / jnp.sum(ex, axis=0, keepdims=True)
        e1_ref[rows, :] = _stack_rows(e1, krow)
        e2_ref[rows, :] = _stack_rows(e2, krow)


def _route(scores_t, tl, heads_per_step, peer_u, peer_v):
    t = scores_t.shape[2]
    kk = PEER_HEADS * PEER_TOPK
    n_head_steps = PEER_HEADS // heads_per_step
    steps = (t // tl) * n_head_steps
    n_blocks = 1 << (steps.bit_length() - 1)
    tab_rows = N_EXPERTS // n_blocks
    tab = pl.BlockSpec((tab_rows, D_MODEL), lambda i, h: (jnp.minimum(i * n_head_steps + h, n_blocks - 1), 0))
    out = pl.BlockSpec((heads_per_step * PEER_TOPK, tl), lambda i, h: (h, i))
    flat = jnp.asarray(np.repeat(_cand_layout()[1][:, None], tl, axis=1))
    return pl.pallas_call(
        _route_kernel,
        grid=(t // tl, n_head_steps),
        in_specs=[pl.BlockSpec((2 * heads_per_step, N_KEYS, tl), lambda i, h: (h, 0, i)),
                  pl.BlockSpec(flat.shape, lambda i, h: (0, 0)), tab, tab],
        out_specs=[out, out, out, tab, tab],
        out_shape=[jax.ShapeDtypeStruct((kk, t), F32)] * 3 + [jax.ShapeDtypeStruct((N_EXPERTS, D_MODEL), BF16)] * 2,
        compiler_params=_params("arbitrary", "arbitrary"),
    )(scores_t, flat, peer_u, peer_v)


def _expert_weight_kernel(e1_ref, e2_ref, gate_ref, g_ref, e1t_ref, e2t_ref, gt_ref):
    e1t_ref[...] = e1_ref[...].T
    e2t_ref[...] = e2_ref[...].T
    gt_ref[...] = gate_ref[...].T
    kk = e1_ref.shape[0]
    key = lax.broadcasted_iota(jnp.int32, (N_KEYS, kk), 0).astype(F32)
    nt = (((1,), (1,)), ((), ()))

    def body(t, carry):
        r1 = e1t_ref[pl.ds(t, 1), :]
        r2 = e2t_ref[pl.ds(t, 1), :]
        gr = gt_ref[pl.ds(t, 1), :]
        a_t = jnp.where(key == r1, 1.0, 0.0).astype(BF16)
        b_t = jnp.where(key == r2, gr, 0.0).astype(BF16)
        g_ref[t] = lax.dot_general(a_t, b_t, nt, preferred_element_type=F32).astype(g_ref.dtype)
        return carry

    lax.fori_loop(0, g_ref.shape[0], body, 0, unroll=32)


def _expert_weights(e1, e2, gate, tl):
    kk, t = e1.shape
    slot = pl.BlockSpec((kk, tl), lambda i: (0, i))
    return pl.pallas_call(
        _expert_weight_kernel,
        grid=(t // tl,),
        in_specs=[slot, slot, slot],
        out_specs=pl.BlockSpec((tl, N_KEYS, N_KEYS), lambda i: (i, 0, 0)),
        out_shape=jax.ShapeDtypeStruct((t, N_KEYS, N_KEYS), F32),
        scratch_shapes=[pltpu.VMEM((tl, kk), F32)] * 3,
        compiler_params=_params("parallel"),
    )(e1, e2, gate)


def _peer_kernel(hn_ref, u_ref, v_ref, g_ref, y_ref):
    e = pl.program_id(1)
    nt = (((1,), (1,)), ((), ()))
    hk = lax.dot_general(hn_ref[...], u_ref[...], nt, preferred_element_type=F32)
    g = jnp.swapaxes(g_ref[...], 0, 1)
    act = [(jax.nn.gelu(hk[:, a * N_KEYS:(a + 1) * N_KEYS]) * g[a]).astype(BF16) for a in range(g.shape[0])]
    out = jnp.dot(jnp.concatenate(act, axis=-1), v_ref[...], preferred_element_type=F32)

    @pl.when(e == 0)
    def _():
        y_ref[...] = out

    @pl.when(e > 0)
    def _():
        y_ref[...] += out


def _peer(hn, u, v, g, t, rows_out, tm, te):
    row = pl.BlockSpec((tm, D_MODEL), lambda i, e: (i, 0))
    tab = pl.BlockSpec((te, D_MODEL), lambda i, e: (e, 0))
    return pl.pallas_call(
        _peer_kernel,
        grid=(t // tm, N_EXPERTS // te),
        in_specs=[row, tab, tab, pl.BlockSpec((tm, te // N_KEYS, N_KEYS), lambda i, e: (i, e, 0))],
        out_specs=row,
        out_shape=jax.ShapeDtypeStruct((rows_out, D_MODEL), F32),
        compiler_params=_params("parallel", "arbitrary"),
    )(hn, u, v, g)


def _final_norm_prompt_kernel(h_ref, f_ref, g_ref, y_ref):
    y = [_rms(h_ref[rr % 4, rr // 4] + f_ref[rr % 4, rr // 4], g_ref[...]) for rr in range(8)]
    y_ref[...] = jnp.swapaxes(jnp.stack(y, axis=0), 0, 1)


def _final_norm_prompt(h, f, g, batch, seq):
    rows = 64
    h5 = h.reshape(batch + 1, 4, 4, SLAB_ROWS, D_MODEL)
    f5 = f.reshape(batch + 1, 4, 4, SLAB_ROWS, D_MODEL)
    blk = pl.BlockSpec((None, 4, 2, rows, D_MODEL), lambda b, hh, j: (b, 0, hh, j, 0))
    y = pl.pallas_call(
        _final_norm_prompt_kernel,
        grid=(batch, 2, SLAB_ROWS // rows),
        in_specs=[blk, blk, pl.BlockSpec((1, D_MODEL), lambda b, hh, j: (0, 0))],
        out_specs=pl.BlockSpec((None, rows, 8, D_MODEL), lambda b, hh, j: (b, j, hh, 0)),
        out_shape=jax.ShapeDtypeStruct((batch, SLAB_ROWS, SLABS, D_MODEL), F32),
        compiler_params=_params("parallel", "parallel", "parallel"),
    )(h5, f5, g.reshape(1, D_MODEL))
    return y.reshape(batch, seq, D_MODEL)


def _final_norm_rows_kernel(h_ref, f_ref, g_ref, y_ref):
    y_ref[...] = _rms(h_ref[...] + f_ref[...], g_ref[...])


def _final_norm_rows(h, f, g, row_block, rows):
    blk = pl.BlockSpec((rows, D_MODEL), lambda i: (row_block, 0))
    return pl.pallas_call(
        _final_norm_rows_kernel,
        grid=(1,),
        in_specs=[blk, blk, pl.BlockSpec((1, D_MODEL), lambda i: (0, 0))],
        out_specs=pl.BlockSpec((rows, D_MODEL), lambda i: (0, 0)),
        out_shape=jax.ShapeDtypeStruct((rows, D_MODEL), F32),
        compiler_params=_params("arbitrary"),
    )(h, f, g.reshape(1, D_MODEL))


def _kv_prompt_kernel(*refs):
    ins, outs = refs[:2 * N_DIL], refs[2 * N_DIL:]
    for g in range(N_DIL):
        for kv in range(2):
            src = ins[2 * g + kv]
            heads = [src[0, :, h * HEAD_DIM:(h + 1) * HEAD_DIM] for h in range(ATT_HEADS)]
            outs[g][:, kv, :, :] = jnp.swapaxes(jnp.stack(heads, axis=0), 0, 1)


def _kv_prompt(p3, batch, seq):
    in_specs, out_specs, out_shape = [], [], []
    residue = lambda s: (s % 4) * 4 + s // 4
    for g in range(N_DIL):
        steps = min(DIL_WINDOWS[g], seq) // SLABS
        last = SLAB_ROWS // steps - 1
        for kv in range(2):
            col = COL_ATT + 3 * g + 1 + kv
            in_specs.append(pl.BlockSpec((1, steps, D_MODEL),
                                         lambda b, s, last=last, col=col: (b * SLABS + s, last, col)))
        out_specs.append(pl.BlockSpec((None, steps, None, 2, ATT_HEADS, HEAD_DIM),
                                      lambda b, s: (b, 0, residue(s), 0, 0, 0)))
        out_shape.append(jax.ShapeDtypeStruct((batch, steps, SLABS, 2, ATT_HEADS, HEAD_DIM), F32))
    outs = pl.pallas_call(
        _kv_prompt_kernel,
        grid=(batch, SLABS),
        in_specs=in_specs,
        out_specs=out_specs,
        out_shape=out_shape,
        compiler_params=_params("parallel", "parallel"),
    )(*([p3] * len(in_specs)))
    return [o.reshape(1, batch, -1, 2, ATT_HEADS, HEAD_DIM) for o in outs]


def _row_tile(t, candidates):
    for c in candidates:
        if t % c == 0:
            return c
    raise ValueError(f"no row tile for {t} tokens")


def kernel(x_prompt, x_sample, cache_kv_w128, cache_kv_w512, cache_kv_w2048, norm_mix_g, w_in, sgu_norm_g, sgu_w, sgu_b, w_out, norm_ffn_g, peer_w_q, peer_sub_keys, peer_u, peer_v, norm_final_g):
    batch, seq, _ = x_prompt.shape
    db, ds, _ = x_sample.shape
    assert w_in.shape[0] == 1 and db * ds == CHUNK and seq == SLABS * SLAB_ROWS
    caches = (cache_kv_w128, cache_kv_w512, cache_kv_w2048)
    tp, ts = batch * seq, db * ds
    t = tp + ts
    n_slabs = t // SLAB_ROWS
    tm_big = _row_tile(t, (1040, 640, 128))
    tm_lane = _row_tile(t, (640, 128))

    xs = x_sample.reshape(ts, D_MODEL)
    xl, xn = _permute_norm(x_prompt, norm_mix_g[0])
    xl = lax.dynamic_update_slice(xl, xs, (tp, 0))
    xn = lax.dynamic_update_slice(xn, _rmsnorm_bf16(xs, norm_mix_g[0]), (tp, 0))

    p, att16 = _inproj(xn, w_in[0], t, tm_big, 1024)
    p3 = p.reshape(n_slabs, SLAB_ROWS, p.shape[1])
    att16_3 = att16.reshape(n_slabs, SLAB_ROWS, att16.shape[1])

    w_tril = sgu_w[0] * jnp.tril(jnp.ones((CHUNK, CHUNK), F32))
    tau = _block_steps(0)
    w_p = w_tril[:, tau][:, :, tau].astype(BF16)
    b_p = sgu_b[0][:, tau][..., None]
    w_s = jnp.einsum("bc,gis->gbics", jnp.eye(db, dtype=F32), w_tril[:, :ds, :ds]).reshape(SGU_GROUPS, ts, ts)
    b_s = jnp.tile(sgu_b[0][:, :ds], (1, db))[..., None]
    a_out = _sgu_prompt(p3, sgu_norm_g[0], w_p, b_p, batch).reshape(t, D_MODEL)
    a_s, vn_s = _sgu_sample(p3, sgu_norm_g[0], w_s.astype(BF16), b_s, tp // SLAB_ROWS)
    a_out = lax.dynamic_update_slice(a_out, a_s, (tp, 0))

    att_s = p[tp:, COL_ATT * D_MODEL:COL_GATE * D_MODEL].reshape(db, ds, N_DIL, 3, ATT_HEADS, HEAD_DIM)
    outs, lses = [], []
    o_s, lse_s = _cached_attention(att_s, [c[0] for c in caches])
    for g in range(N_DIL):
        if g == 0:
            o_p, lse_p = _band_attention(p3, COL_ATT, False, g, batch)
        else:
            o_p, lse_p = _band_attention(att16_3, 3 * (g - 1), True, g, batch)
        lse_g = jnp.pad(lse_s[g, ..., 0].reshape(ts, ATT_HEADS), ((0, 0), (0, 128 - ATT_HEADS)))
        o_g = o_s[g].reshape(ts, D_MODEL).astype(o_p.dtype)
        outs.append(lax.dynamic_update_slice(o_p.reshape(t, D_MODEL), o_g, (tp, 0)))
        lses.append(lax.dynamic_update_slice(lse_p.reshape(t, 128), lse_g, (tp, 0)))

    h, hn = _merge(a_out, att16, COL_GATE - COL_ATT16, outs, lses, xl, _to_bf16(w_out[0], 1024, D_MODEL),
                   norm_ffn_g[0], t, _row_tile(t, (320, 128)))

    scores_t = _peer_scores(hn, _to_bf16(peer_w_q[0], 1024, D_MODEL), peer_sub_keys[0].astype(BF16), tm_lane)
    e1, e2, gate, u_b16, v_b16 = _route(scores_t, 128, 2, peer_u[0], peer_v[0])
    g_dense = _expert_weights(e1, e2, gate, 128)
    f = _peer(hn, u_b16, v_b16, g_dense, t, h.shape[0], tm_lane, 1024)

    y_prompt = _final_norm_prompt(h, f, norm_final_g, batch, seq)
    y_sample = _final_norm_rows(h, f, norm_final_g, tp // ts, ts).reshape(db, ds, D_MODEL)
    kv_prompt = _kv_prompt(p3, batch, seq)
    kv_sample = [att_s[:, :, g, 1:3][None] for g in range(N_DIL)]
    sgu_v_sample = vn_s.reshape(1, db, ds, D_MODEL)
    return (y_prompt, y_sample, kv_prompt[0], kv_prompt[1], kv_prompt[2],
            kv_sample[0], kv_sample[1], kv_sample[2], sgu_v_sample)
```

```python
import functools

import numpy as np
import jax
import jax.numpy as jnp
from jax import lax
from jax.experimental import pallas as pl
from jax.experimental.pallas import tpu as pltpu

F32 = jnp.float32
BF16 = jnp.bfloat16

D_MODEL = 2048
HEAD_DIM = 128
ATT_HEADS = D_MODEL // HEAD_DIM
N_DIL = 3
DIL_WINDOWS = (128, 512, 2048)
DIL_RATES = (1, 4, 16)
ATT_BLOCK = 128
SLABS = DIL_RATES[-1]
SLAB_ROWS = 128
SGU_GROUPS = 8
SGU_GROUP_DIM = D_MODEL // SGU_GROUPS
CHUNK = 128
N_COL_BLOCKS = 13
COL_U, COL_V, COL_ATT, COL_GATE = 0, 1, 2, 11
COL_ATT16 = COL_ATT + 3
PEER_HEADS = 8
PEER_TOPK = 16
N_KEYS = 128
N_EXPERTS = N_KEYS * N_KEYS
PEER_HALF = 128
NORM_EPS = 1e-6
MASK_VALUE = -1e30
NEG_INF = float("-inf")
VMEM_LIMIT = 56 * 1024 * 1024

BLOCK_SHAPES = ((16, 8), (4, 32), (1, 128))


def _alibi_slopes():
    n = N_DIL * ATT_HEADS
    e = np.arange(1, n + 1, dtype=np.float32)
    return np.exp2(np.float32(-8.0) * e / np.float32(n)).astype(np.float32).reshape(N_DIL, ATT_HEADS)


def _block_steps(group):
    slabs, rows = BLOCK_SHAPES[group]
    n = np.arange(slabs * rows)
    s, j = n // rows, n % rows
    if group == 0:
        return j * 16 + (s % 4) * 4 + s // 4
    if group == 1:
        return j * 4 + s
    return j


def _params(*sem):
    return pltpu.CompilerParams(dimension_semantics=sem, vmem_limit_bytes=VMEM_LIMIT)


def _rms(x, g):
    return x * lax.rsqrt(jnp.mean(x * x, axis=-1, keepdims=True) + NORM_EPS) * g


def _permute_norm_kernel(x_ref, g_ref, xl_ref, xn_ref):
    xs = jnp.swapaxes(x_ref[...], 0, 1)
    for rr in range(8):
        x = xs[rr]
        xl_ref[rr % 4, rr // 4] = x
        xn_ref[rr % 4, rr // 4] = _rms(x, g_ref[...]).astype(xn_ref.dtype)


def _permute_norm(x_prompt, g):
    batch, seq, _ = x_prompt.shape
    x4 = x_prompt.reshape(batch, SLAB_ROWS, SLABS, D_MODEL)
    out = pl.BlockSpec((None, 4, 2, SLAB_ROWS, D_MODEL), lambda b, h: (b, 0, h, 0, 0))
    xl, xn = pl.pallas_call(
        _permute_norm_kernel,
        grid=(batch, 2),
        in_specs=[pl.BlockSpec((None, SLAB_ROWS, 8, D_MODEL), lambda b, h: (b, 0, h, 0)),
                  pl.BlockSpec((1, D_MODEL), lambda b, h: (0, 0))],
        out_specs=[out, out],
        out_shape=[jax.ShapeDtypeStruct((batch + 1, 4, 4, SLAB_ROWS, D_MODEL), F32),
                   jax.ShapeDtypeStruct((batch + 1, 4, 4, SLAB_ROWS, D_MODEL), BF16)],
        compiler_params=_params("parallel", "parallel"),
    )(x4, g.reshape(1, D_MODEL))
    rows = (batch + 1) * seq
    return xl.reshape(rows, D_MODEL), xn.reshape(rows, D_MODEL)


def _rmsnorm_kernel(x_ref, g_ref, o_ref):
    o_ref[...] = _rms(x_ref[...], g_ref[...]).astype(o_ref.dtype)


def _rmsnorm_bf16(x, g):
    return pl.pallas_call(
        _rmsnorm_kernel,
        out_shape=jax.ShapeDtypeStruct(x.shape, BF16),
    )(x, g.reshape(1, D_MODEL))


def _cast_kernel(x_ref, o_ref):
    o_ref[...] = x_ref[...].astype(o_ref.dtype)


def _to_bf16(w, rows, cols):
    r, c = w.shape
    return pl.pallas_call(
        _cast_kernel,
        grid=(r // rows, c // cols),
        in_specs=[pl.BlockSpec((rows, cols), lambda i, j: (i, j))],
        out_specs=pl.BlockSpec((rows, cols), lambda i, j: (i, j)),
        out_shape=jax.ShapeDtypeStruct((r, c), BF16),
        compiler_params=_params("parallel", "parallel"),
    )(w)


def _inproj_kernel(x_ref, w_ref, o_ref, att_ref, w16_ref, *, blocks_per_col):
    cb = pl.program_id(0) // blocks_per_col

    @pl.when(pl.program_id(1) == 0)
    def _():
        w16_ref[...] = w_ref[...].astype(w16_ref.dtype)

    p = jnp.dot(x_ref[...], w16_ref[...], preferred_element_type=F32)

    @pl.when(cb < COL_ATT)
    def _():
        o_ref[...] = jax.nn.gelu(p)

    @pl.when((cb >= COL_ATT) & (cb < COL_ATT16))
    def _():
        o_ref[...] = p

    @pl.when((cb >= COL_ATT16) & (cb < COL_GATE))
    def _():
        o_ref[...] = p
        is_q = (cb - COL_ATT) % 3 == 0
        att_ref[...] = (p * jnp.where(is_q, HEAD_DIM ** -0.5, 1.0)).astype(att_ref.dtype)

    @pl.when(cb >= COL_GATE)
    def _():
        att_ref[...] = jax.nn.sigmoid(p).astype(att_ref.dtype)


def _inproj(xn, w, t, tm, tn):
    n = w.shape[1]
    per_col = D_MODEL // tn
    n_rows = t // tm
    n_f32 = COL_GATE * per_col
    first_b16 = COL_ATT16 * per_col
    n_b16 = (N_COL_BLOCKS - COL_ATT16) * per_col
    f32_block = lambda j, i: (jnp.where(j < n_f32, i, n_rows - 1), jnp.minimum(j, n_f32 - 1))
    b16_block = lambda j, i: (jnp.where(j >= first_b16, i, 0), jnp.maximum(j - first_b16, 0))
    return pl.pallas_call(
        functools.partial(_inproj_kernel, blocks_per_col=per_col),
        grid=(n // tn, n_rows),
        in_specs=[pl.BlockSpec((tm, D_MODEL), lambda j, i: (i, 0)),
                  pl.BlockSpec((D_MODEL, tn), lambda j, i: (0, j))],
        out_specs=[pl.BlockSpec((tm, tn), f32_block),
                   pl.BlockSpec((tm, tn), b16_block)],
        out_shape=[jax.ShapeDtypeStruct((t, n_f32 * tn), F32),
                   jax.ShapeDtypeStruct((t, n_b16 * tn), BF16)],
        scratch_shapes=[pltpu.VMEM((D_MODEL, tn), BF16)],
        compiler_params=_params("arbitrary", "arbitrary"),
    )(xn, w)


def _sgu_kernel(u_ref, v_ref, g_ref, w_ref, b_ref, a_ref, vn_ref):
    v = v_ref[...].reshape(CHUNK, D_MODEL)
    u = u_ref[...].reshape(CHUNK, D_MODEL)
    vc = v - jnp.mean(v, axis=-1, keepdims=True)
    vn = vc * lax.rsqrt(jnp.mean(vc * vc, axis=-1, keepdims=True) + NORM_EPS) * g_ref[...]
    if vn_ref is not None:
        vn_ref[...] = vn
    cols = []
    for g in range(SGU_GROUPS):
        cs = slice(g * SGU_GROUP_DIM, (g + 1) * SGU_GROUP_DIM)
        mix = jnp.dot(w_ref[g], vn[:, cs].astype(BF16), preferred_element_type=F32) + b_ref[g]
        cols.append(u[:, cs] * mix)
    a_ref[...] = jnp.concatenate(cols, axis=-1).reshape(a_ref.shape)


def _sgu_prompt(p3, sgu_norm_g, w, b, batch):
    slabs, rows = BLOCK_SHAPES[0]
    n_chunks = SLAB_ROWS // rows
    blk = lambda col: pl.BlockSpec((slabs, rows, D_MODEL), lambda bb, c: (bb, c, col))
    return pl.pallas_call(
        lambda u, v, g, ww, bb, a: _sgu_kernel(u, v, g, ww, bb, a, None),
        grid=(batch, n_chunks),
        in_specs=[blk(COL_U), blk(COL_V),
                  pl.BlockSpec((1, D_MODEL), lambda bb, c: (0, 0)),
                  pl.BlockSpec((SGU_GROUPS, CHUNK, CHUNK), lambda bb, c: (0, 0, 0)),
                  pl.BlockSpec((SGU_GROUPS, CHUNK, 1), lambda bb, c: (0, 0, 0))],
        out_specs=blk(0),
        out_shape=jax.ShapeDtypeStruct((p3.shape[0], SLAB_ROWS, D_MODEL), F32),
        compiler_params=_params("parallel", "parallel"),
    )(p3, p3, sgu_norm_g.reshape(1, D_MODEL), w, b)


def _sgu_sample(p3, sgu_norm_g, w, b, slab):
    blk = lambda col: pl.BlockSpec((1, SLAB_ROWS, D_MODEL), lambda i: (slab, 0, col))
    return pl.pallas_call(
        _sgu_kernel,
        grid=(1,),
        in_specs=[blk(COL_U), blk(COL_V),
                  pl.BlockSpec((1, D_MODEL), lambda i: (0, 0)),
                  pl.BlockSpec((SGU_GROUPS, CHUNK, CHUNK), lambda i: (0, 0, 0)),
                  pl.BlockSpec((SGU_GROUPS, CHUNK, 1), lambda i: (0, 0, 0))],
        out_specs=[pl.BlockSpec((CHUNK, D_MODEL), lambda i: (0, 0)),
                   pl.BlockSpec((CHUNK, D_MODEL), lambda i: (0, 0))],
        out_shape=[jax.ShapeDtypeStruct((CHUNK, D_MODEL), F32), jax.ShapeDtypeStruct((CHUNK, D_MODEL), F32)],
        compiler_params=_params("arbitrary"),
    )(p3, p3, sgu_norm_g.reshape(1, D_MODEL), w, b)


def _band_block(bc_ref, bp_ref, q_ref, kc_ref, vc_ref, o_ref, lse_ref, kp_ref, vp_ref, prescaled):
    has_prev = kp_ref is not None
    n = ATT_BLOCK
    scale = HEAD_DIM ** -0.5
    nt = (((1,), (1,)), ((), ()))
    heads = range(ATT_HEADS)
    hs = [slice(h * HEAD_DIM, (h + 1) * HEAD_DIM) for h in heads]
    ld = lambda ref, h: ref[:, :, hs[h]].reshape(n, HEAD_DIM)
    q = [ld(q_ref, h) if prescaled else (ld(q_ref, h) * scale).astype(BF16) for h in heads]
    s_c = [lax.dot_general(q[h], ld(kc_ref, h).astype(BF16), nt, preferred_element_type=F32) + bc_ref[h]
           for h in heads]
    m = [jnp.max(s_c[h], axis=-1, keepdims=True) for h in heads]
    if has_prev:
        s_p = [lax.dot_general(q[h], ld(kp_ref, h).astype(BF16), nt, preferred_element_type=F32) + bp_ref[h]
               for h in heads]
        m = [jnp.maximum(m[h], jnp.max(s_p[h], axis=-1, keepdims=True)) for h in heads]
    p_c = [jnp.exp(s_c[h] - m[h]) for h in heads]
    l = [jnp.sum(p_c[h], axis=-1, keepdims=True) for h in heads]
    o = [jnp.dot(p_c[h].astype(BF16), ld(vc_ref, h).astype(BF16), preferred_element_type=F32) for h in heads]
    if has_prev:
        p_p = [jnp.exp(s_p[h] - m[h]) for h in heads]
        l = [l[h] + jnp.sum(p_p[h], axis=-1, keepdims=True) for h in heads]
        o = [o[h] + jnp.dot(p_p[h].astype(BF16), ld(vp_ref, h).astype(BF16), preferred_element_type=F32)
             for h in heads]
    lane = lax.broadcasted_iota(jnp.int32, (n, 128), 1)
    lse = jnp.zeros((n, 128), F32)
    for h in heads:
        lse = jnp.where(lane == h, m[h] + jnp.log(l[h]), lse)
    o_ref[...] = jnp.concatenate([o[h] / l[h] for h in heads], axis=-1).reshape(o_ref.shape).astype(o_ref.dtype)
    lse_ref[...] = lse.reshape(lse_ref.shape)
    if has_prev:
        kp_ref[...] = kc_ref[...]
        vp_ref[...] = vc_ref[...]


def _band_attn_kernel(*refs, blocks_per_stream, prescaled):
    refs = list(refs)
    k = pl.program_id(1)
    groups = range(len(blocks_per_stream))
    ins = []
    for g in groups:
        n_in = 5 if blocks_per_stream[g] > 1 else 4
        ins.append(refs[:n_in])
        refs = refs[n_in:]
    outs = [refs[2 * g:2 * g + 2] for g in groups]
    refs = refs[2 * len(blocks_per_stream):]
    for g in groups:
        if blocks_per_stream[g] > 1:
            bc_ref, bp_ref, q_ref, kc_ref, vc_ref = ins[g]
            kp_ref, vp_ref = refs[:2]
            refs = refs[2:]

            @pl.when(k % blocks_per_stream[g] == 0)
            def _(kp_ref=kp_ref, vp_ref=vp_ref):
                kp_ref[...] = jnp.zeros_like(kp_ref)
                vp_ref[...] = jnp.zeros_like(vp_ref)
        else:
            (bc_ref, q_ref, kc_ref, vc_ref), bp_ref, kp_ref, vp_ref = ins[g], None, None, None
        _band_block(bc_ref, bp_ref, q_ref, kc_ref, vc_ref, outs[g][0], outs[g][1], kp_ref, vp_ref, prescaled[g])


def _band_bias(group):
    steps = _block_steps(group)
    back = (steps[:, None] - steps[None, :]).astype(np.float32)
    coef = (_alibi_slopes()[group] * np.float32(DIL_RATES[group]))[:, None, None]
    cur = np.where(back >= 0, -(coef * back), np.float32(MASK_VALUE)).astype(np.float32)
    back_p = back + np.float32(ATT_BLOCK)
    prev = np.where(back_p <= ATT_BLOCK, -(coef * back_p), np.float32(MASK_VALUE)).astype(np.float32)
    return cur, np.stack([np.full_like(prev, MASK_VALUE), prev])


def _band_attention(srcs, col_q, prescaled, batch):
    n_slabs = srcs[0].shape[0]
    table = (ATT_HEADS, ATT_BLOCK, ATT_BLOCK)
    in_specs, args, out_specs, out_shape, scratch, blocks_per_stream = [], [], [], [], [], []
    for g in range(N_DIL):
        slabs, rows = BLOCK_SHAPES[g]
        streams = SLABS // slabs
        nb = SLAB_ROWS // rows
        blocks_per_stream.append(nb)
        bias_c, bias_p = _band_bias(g)

        def cur(col, width=D_MODEL, slabs=slabs, rows=rows, streams=streams, nb=nb):
            return pl.BlockSpec((slabs, rows, width), lambda b, k: (b * streams + k // nb, k % nb, col))

        in_specs.append(pl.BlockSpec(table, lambda b, k: (0, 0, 0)))
        args.append(jnp.asarray(bias_c))
        if nb > 1:
            in_specs.append(pl.BlockSpec((None,) + table, lambda b, k, nb=nb: (jnp.minimum(k % nb, 1), 0, 0, 0)))
            args.append(jnp.asarray(bias_p))
            scratch += [pltpu.VMEM((slabs, rows, D_MODEL), srcs[g].dtype)] * 2
        in_specs += [cur(col_q[g]), cur(col_q[g] + 1), cur(col_q[g] + 2)]
        args += [srcs[g]] * 3
        out_specs += [cur(0), cur(0, 128)]
        out_shape += [jax.ShapeDtypeStruct((n_slabs, SLAB_ROWS, D_MODEL), BF16 if prescaled[g] else F32),
                      jax.ShapeDtypeStruct((n_slabs, SLAB_ROWS, 128), F32)]
    res = pl.pallas_call(
        functools.partial(_band_attn_kernel, blocks_per_stream=tuple(blocks_per_stream), prescaled=tuple(prescaled)),
        grid=(batch, SLABS),
        in_specs=in_specs,
        out_specs=out_specs,
        out_shape=out_shape,
        scratch_shapes=scratch,
        compiler_params=_params("parallel", "arbitrary"),
    )(*args)
    return [(res[2 * g], res[2 * g + 1]) for g in range(N_DIL)]


def _cached_attn_kernel(new_ref, c0_ref, c1_ref, c2_ref, coef_ref, o_ref, lse_ref, *, dec_seq):
    caches = (c0_ref, c1_ref, c2_ref)
    shared = [d == 1 for d in DIL_RATES]
    pairs = [(g, i) for g in range(N_DIL) for i in range(dec_seq)]
    res = lambda g, i: 0 if shared[g] else i
    lane = lax.broadcasted_iota(jnp.int32, (ATT_HEADS, ATT_BLOCK), 1)
    scale = HEAD_DIM ** -0.5
    q = {(g, i): new_ref[i, g, 0] for g, i in pairs}

    def qk_body(r, carry):
        out = []
        for n, (g, i) in enumerate(pairs):
            col = jnp.sum(q[g, i] * caches[g][r, res(g, i), 0], axis=-1, keepdims=True)
            out.append(jnp.where(lane == r, col, carry[n]))
        return tuple(out)

    zero = jnp.zeros((ATT_HEADS, ATT_BLOCK), F32)
    s_all = lax.fori_loop(0, ATT_BLOCK, qk_body, (zero,) * len(pairs), unroll=16)

    ps, p_news, ls, ms = [], [], [], []
    for n, (g, i) in enumerate(pairs):
        coef = coef_ref[g]
        if shared[g]:
            back = (ATT_BLOCK + i - lane).astype(F32)
            s = jnp.where(lane >= i, s_all[n] * scale - coef * back, MASK_VALUE)
            s_new = [jnp.sum(q[g, i] * new_ref[i2, g, 1], axis=-1, keepdims=True) * scale
                     - coef[:, :1] * float(i - i2) for i2 in range(i + 1)]
        else:
            s = s_all[n] * scale - coef * (ATT_BLOCK - lane).astype(F32)
            s_new = [jnp.sum(q[g, i] * new_ref[i, g, 1], axis=-1, keepdims=True) * scale]
        m = jnp.max(s, axis=-1, keepdims=True)
        for sn in s_new:
            m = jnp.maximum(m, sn)
        p = jnp.exp(s - m)
        p_new = [jnp.exp(sn - m) for sn in s_new]
        l = jnp.sum(p, axis=-1, keepdims=True)
        for pn in p_new:
            l = l + pn
        ps.append(p)
        p_news.append(p_new)
        ls.append(l)
        ms.append(m)

    def pv_body(r, carry):
        out = []
        for n, (g, i) in enumerate(pairs):
            col = jnp.sum(jnp.where(lane == r, ps[n], 0.0), axis=-1, keepdims=True)
            out.append(carry[n] + col * caches[g][r, res(g, i), 1])
        return tuple(out)

    zero_o = jnp.zeros((ATT_HEADS, HEAD_DIM), F32)
    o_all = lax.fori_loop(0, ATT_BLOCK, pv_body, (zero_o,) * len(pairs), unroll=16)
    for n, (g, i) in enumerate(pairs):
        o = o_all[n]
        for i2, pn in enumerate(p_news[n]):
            o = o + pn * new_ref[i2 if shared[g] else i, g, 2]
        o_ref[g, i] = o / ls[n]
        lse_ref[g, i] = jnp.broadcast_to(ms[n] + jnp.log(ls[n]), (ATT_HEADS, 128))


def _cached_attention(att_s, caches):
    db, dec_seq = att_s.shape[:2]
    views, specs = [], []
    for g, cache in enumerate(caches):
        dil, window = DIL_RATES[g], DIL_WINDOWS[g]
        assert cache.shape[1] == window and window == dil * ATT_BLOCK
        assert dil == 1 or dec_seq <= dil
        nres = 1 if dil == 1 else dec_seq
        views.append(cache.reshape(db, ATT_BLOCK, dil, 2, ATT_HEADS, HEAD_DIM))
        specs.append(pl.BlockSpec((None, ATT_BLOCK, nres, 2, ATT_HEADS, HEAD_DIM), lambda b: (b, 0, 0, 0, 0, 0)))
    coef = np.stack([np.repeat((_alibi_slopes()[g] * DIL_RATES[g])[:, None], 128, axis=1) for g in range(N_DIL)])
    out = lambda w: pl.BlockSpec((N_DIL, None, dec_seq, ATT_HEADS, w), lambda b: (0, b, 0, 0, 0))
    return pl.pallas_call(
        functools.partial(_cached_attn_kernel, dec_seq=dec_seq),
        grid=(db,),
        in_specs=[pl.BlockSpec((None, dec_seq, N_DIL, 3, ATT_HEADS, HEAD_DIM), lambda b: (b, 0, 0, 0, 0, 0))]
        + specs + [pl.BlockSpec((N_DIL, ATT_HEADS, 128), lambda b: (0, 0, 0))],
        out_specs=[out(HEAD_DIM), out(128)],
        out_shape=[jax.ShapeDtypeStruct((N_DIL, db, dec_seq, ATT_HEADS, HEAD_DIM), F32),
                   jax.ShapeDtypeStruct((N_DIL, db, dec_seq, ATT_HEADS, 128), F32)],
        compiler_params=_params("parallel"),
    )(att_s, *views, jnp.asarray(coef.astype(np.float32)))


def _merge_kernel(a_ref, ga_ref, gb_ref, o0_ref, o1_ref, o2_ref, l0_ref, l1_ref, l2_ref,
                  x_ref, w_ref, g_ref, h_ref, hn_ref, merged_ref):
    l0, l1, l2 = l0_ref[...], l1_ref[...], l2_ref[...]
    mx = jnp.maximum(jnp.maximum(l0, l1), l2)
    e0, e1, e2 = jnp.exp(l0 - mx), jnp.exp(l1 - mx), jnp.exp(l2 - mx)
    den = e0 + e1 + e2
    w0, w1, w2 = e0 / den, e1 / den, e2 / den
    for h in range(ATT_HEADS):
        hs = slice(h * HEAD_DIM, (h + 1) * HEAD_DIM)
        b_out = (w0[:, h:h + 1] * o0_ref[:, hs] + w1[:, h:h + 1] * o1_ref[:, hs].astype(F32)
                 + w2[:, h:h + 1] * o2_ref[:, hs].astype(F32))
        merged = ga_ref[:, hs].astype(F32) * a_ref[:, hs] + gb_ref[:, hs].astype(F32) * b_out
        merged_ref[:, hs] = merged.astype(merged_ref.dtype)
    h_new = x_ref[...] + jnp.dot(merged_ref[...], w_ref[...], preferred_element_type=F32)
    h_ref[...] = h_new
    hn_ref[...] = _rms(h_new, g_ref[...]).astype(hn_ref.dtype)


def _merge(a_out, gates, gate_col, outs, lses, x, w_out, norm_g, t, tm):
    row = lambda col: pl.BlockSpec((tm, D_MODEL), lambda i: (i, col))
    lrow = pl.BlockSpec((tm, 128), lambda i: (i, 0))
    p = gates
    return pl.pallas_call(
        _merge_kernel,
        grid=(t // tm,),
        in_specs=[row(0), row(gate_col), row(gate_col + 1), row(0), row(0), row(0), lrow, lrow, lrow,
                  row(0), pl.BlockSpec((D_MODEL, D_MODEL), lambda i: (0, 0)),
                  pl.BlockSpec((1, D_MODEL), lambda i: (0, 0))],
        out_specs=[row(0), row(0)],
        out_shape=[jax.ShapeDtypeStruct((x.shape[0], D_MODEL), F32), jax.ShapeDtypeStruct((t, D_MODEL), BF16)],
        scratch_shapes=[pltpu.VMEM((tm, D_MODEL), BF16)],
        compiler_params=_params("parallel"),
    )(a_out, p, p, outs[0], outs[1], outs[2], lses[0], lses[1], lses[2], x, w_out, norm_g.reshape(1, D_MODEL))


def _peer_score_kernel(hn_ref, wq_ref, keys_ref, s_ref):
    q = jnp.dot(hn_ref[...], wq_ref[...], preferred_element_type=F32).astype(BF16)
    nt = (((1,), (1,)), ((), ()))
    for hc in range(2 * PEER_HEADS):
        cs = slice(hc * PEER_HALF, (hc + 1) * PEER_HALF)
        s_ref[hc] = lax.dot_general(keys_ref[hc % 2], q[:, cs], nt, preferred_element_type=F32)


def _peer_scores(hn, w_q, sub_keys, tm):
    t = hn.shape[0]
    return pl.pallas_call(
        _peer_score_kernel,
        grid=(t // tm,),
        in_specs=[pl.BlockSpec((tm, D_MODEL), lambda i: (i, 0)),
                  pl.BlockSpec((D_MODEL, 2 * PEER_HEADS * PEER_HALF), lambda i: (0, 0)),
                  pl.BlockSpec((2, N_KEYS, PEER_HALF), lambda i: (0, 0, 0))],
        out_specs=pl.BlockSpec((2 * PEER_HEADS, N_KEYS, tm), lambda i: (0, 0, i)),
        out_shape=jax.ShapeDtypeStruct((2 * PEER_HEADS, N_KEYS, t), F32),
        compiler_params=_params("parallel"),
    )(hn, w_q, sub_keys)


def _take_top(arrays, order, count, sentinel):
    arrays = list(arrays)
    vals = [[] for _ in arrays]
    idxs = [[] for _ in arrays]
    for _ in range(count):
        for n, s in enumerate(arrays):
            m = jnp.max(s, axis=0, keepdims=True)
            pos = jnp.min(jnp.where(s == m, order, sentinel), axis=0, keepdims=True)
            vals[n].append(m)
            idxs[n].append(pos)
            arrays[n] = jnp.where(order == pos, NEG_INF, s)
    return vals, idxs


def _sort_network(n):
    pairs, p = [], 1
    while p < n:
        k = p
        while k >= 1:
            for j in range(k % p, n - k, 2 * k):
                for i in range(min(k, n - j - k)):
                    if (i + j) // (2 * p) == (i + j + k) // (2 * p):
                        pairs.append((i + j, i + j + k))
            k //= 2
        p *= 2
    return pairs


def _take_top_sorted(arrays, order, count, sentinel):
    tiles = arrays[0].shape[0] // 8
    assert all(a.shape[0] == 8 * tiles for a in arrays)
    val = [[a[8 * v:8 * v + 8] for v in range(tiles)] for a in arrays]
    idx = [[order[8 * v:8 * v + 8] for v in range(tiles)] for _ in arrays]
    network = [(i, j) for i, j in _sort_network(1 << (tiles - 1).bit_length()) if j < tiles]
    for i, j in network:
        for n in range(len(arrays)):
            vi, vj, ri, rj = val[n][i], val[n][j], idx[n][i], idx[n][j]
            swap = (vj > vi) | ((vj == vi) & (rj < ri))
            val[n][i], val[n][j] = jnp.where(swap, vj, vi), jnp.where(swap, vi, vj)
            idx[n][i], idx[n][j] = jnp.where(swap, rj, ri), jnp.where(swap, ri, rj)
    vals = [[] for _ in arrays]
    idxs = [[] for _ in arrays]
    for t in range(count):
        for n in range(len(arrays)):
            head, rank = val[n][0], idx[n][0]
            m = jnp.max(head, axis=0, keepdims=True)
            pos = jnp.min(jnp.where(head == m, rank, sentinel), axis=0, keepdims=True)
            vals[n].append(m)
            idxs[n].append(pos)
            popped = rank == pos
            for d in range(min(tiles, count - t - 1)):
                if d + 1 < tiles:
                    val[n][d] = jnp.where(popped, val[n][d + 1], val[n][d])
                    idx[n][d] = jnp.where(popped, idx[n][d + 1], idx[n][d])
                else:
                    val[n][d] = jnp.where(popped, NEG_INF, val[n][d])
                    idx[n][d] = jnp.where(popped, sentinel, idx[n][d])
    return vals, idxs


def _stack_rows(rows_list, krow):
    out = jnp.zeros(krow.shape, F32)
    for j, r in enumerate(rows_list):
        out = jnp.where(krow == float(j), r, out)
    return out


def _cand_layout():
    k = PEER_TOPK
    pieces = [("row_a", a, 16 if a == 0 else 8, 0, k // (a + 1)) for a in range(4)]
    pieces += [("col_b", 0, 16, 4, 16), ("col_b", 1, 8, 4, 8), ("col_b", 2, 8, 4, 5)]
    pos = []
    for kind, idx, rows, lo, hi in pieces:
        for r in range(rows):
            a, b = (idx, r) if kind == "row_a" else (r, idx)
            ok = lo <= r < hi and (a + 1) * (b + 1) <= k
            pos.append(a * k + b if ok else k * k)
    assert sorted(p for p in pos if p < k * k) == sorted(
        a * k + b for a in range(k) for b in range(k) if (a + 1) * (b + 1) <= k)
    return pieces, np.asarray(pos, np.float32)


def _route_kernel(s_ref, pos_ref, u_ref, v_ref, e1_ref, e2_ref, gate_ref, ub_ref, vb_ref):
    ub_ref[...] = u_ref[...].astype(ub_ref.dtype)
    vb_ref[...] = v_ref[...].astype(vb_ref.dtype)
    k = PEER_TOPK
    heads = s_ref.shape[0] // 2
    lanes = s_ref.shape[2]
    key_rank = lax.broadcasted_iota(jnp.int32, (N_KEYS, lanes), 0).astype(F32)
    vals, idxs = _take_top_sorted([s_ref[n] for n in range(2 * heads)], key_rank, k, float(N_KEYS))
    krow = lax.broadcasted_iota(jnp.int32, (k, lanes), 0).astype(F32)
    flat = pos_ref[...]
    cands, i1_all, i2_all = [], [], []
    for hd in range(heads):
        v1, v2 = vals[2 * hd], vals[2 * hd + 1]
        v1_all = _stack_rows(v1, krow)
        v2_all = _stack_rows(v2, krow)
        i1_all.append(_stack_rows(idxs[2 * hd], krow))
        i2_all.append(_stack_rows(idxs[2 * hd + 1], krow))
        parts = [v1[idx] + v2_all[:rows] if kind == "row_a" else v1_all[:rows] + v2[idx]
                 for kind, idx, rows, _, _ in _cand_layout()[0]]
        cands.append(jnp.where(flat < float(k * k), jnp.concatenate(parts, axis=0), NEG_INF))
    top_s, pos = _take_top_sorted(cands, flat, k, float(k * k))
    for hd in range(heads):
        e1, e2 = [], []
        for j in range(k):
            a = jnp.floor(pos[hd][j] * (1.0 / k))
            b = pos[hd][j] - a * k
            e1.append(jnp.sum(jnp.where(krow == a, i1_all[hd], 0.0), axis=0, keepdims=True))
            e2.append(jnp.sum(jnp.where(krow == b, i2_all[hd], 0.0), axis=0, keepdims=True))
        ex = jnp.exp(_stack_rows(top_s[hd], krow) - top_s[hd][0])
        rows = slice(hd * k, (hd + 1) * k)
        gate_ref[rows, :] = ex / jnp.sum(ex, axis=0, keepdims=True)
        e1_ref[rows, :] = _stack_rows(e1, krow)
        e2_ref[rows, :] = _stack_rows(e2, krow)


def _route(scores_t, tl, heads_per_step, peer_u, peer_v):
    t = scores_t.shape[2]
    kk = PEER_HEADS * PEER_TOPK
    n_head_steps = PEER_HEADS // heads_per_step
    steps = (t // tl) * n_head_steps
    n_blocks = 1 << (steps.bit_length() - 1)
    tab_rows = N_EXPERTS // n_blocks
    tab = pl.BlockSpec((tab_rows, D_MODEL), lambda i, h: (jnp.minimum(i * n_head_steps + h, n_blocks - 1), 0))
    out = pl.BlockSpec((heads_per_step * PEER_TOPK, tl), lambda i, h: (h, i))
    flat = jnp.asarray(np.repeat(_cand_layout()[1][:, None], tl, axis=1))
    return pl.pallas_call(
        _route_kernel,
        grid=(t // tl, n_head_steps),
        in_specs=[pl.BlockSpec((2 * heads_per_step, N_KEYS, tl), lambda i, h: (h, 0, i)),
                  pl.BlockSpec(flat.shape, lambda i, h: (0, 0)), tab, tab],
        out_specs=[out, out, out, tab, tab],
        out_shape=[jax.ShapeDtypeStruct((kk, t), F32)] * 3 + [jax.ShapeDtypeStruct((N_EXPERTS, D_MODEL), BF16)] * 2,
        compiler_params=_params("arbitrary", "arbitrary"),
    )(scores_t, flat, peer_u, peer_v)


def _expert_weight_kernel(e1_ref, e2_ref, gate_ref, g_ref, e1t_ref, e2t_ref, gt_ref):
    e1t_ref[...] = e1_ref[...].T
    e2t_ref[...] = e2_ref[...].T
    gt_ref[...] = gate_ref[...].T
    kk = e1_ref.shape[0]
    key = lax.broadcasted_iota(jnp.int32, (N_KEYS, kk), 0).astype(F32)
    nt = (((1,), (1,)), ((), ()))

    def body(t, carry):
        r1 = e1t_ref[pl.ds(t, 1), :]
        r2 = e2t_ref[pl.ds(t, 1), :]
        gr = gt_ref[pl.ds(t, 1), :]
        a_t = jnp.where(key == r1, 1.0, 0.0).astype(BF16)
        b_t = jnp.where(key == r2, gr, 0.0).astype(BF16)
        g_ref[t] = lax.dot_general(a_t, b_t, nt, preferred_element_type=F32).astype(g_ref.dtype)
        return carry

    lax.fori_loop(0, g_ref.shape[0], body, 0, unroll=32)


def _expert_weights(e1, e2, gate, tl):
    kk, t = e1.shape
    slot = pl.BlockSpec((kk, tl), lambda i: (0, i))
    return pl.pallas_call(
        _expert_weight_kernel,
        grid=(t // tl,),
        in_specs=[slot, slot, slot],
        out_specs=pl.BlockSpec((tl, N_KEYS, N_KEYS), lambda i: (i, 0, 0)),
        out_shape=jax.ShapeDtypeStruct((t, N_KEYS, N_KEYS), F32),
        scratch_shapes=[pltpu.VMEM((tl, kk), F32)] * 3,
        compiler_params=_params("parallel"),
    )(e1, e2, gate)


def _peer_kernel(hn_ref, u_ref, v_ref, g_ref, y_ref):
    e = pl.program_id(1)
    nt = (((1,), (1,)), ((), ()))
    hk = lax.dot_general(hn_ref[...], u_ref[...], nt, preferred_element_type=F32)
    g = jnp.swapaxes(g_ref[...], 0, 1)
    act = [(jax.nn.gelu(hk[:, a * N_KEYS:(a + 1) * N_KEYS]) * g[a]).astype(BF16) for a in range(g.shape[0])]
    out = jnp.dot(jnp.concatenate(act, axis=-1), v_ref[...], preferred_element_type=F32)

    @pl.when(e == 0)
    def _():
        y_ref[...] = out

    @pl.when(e > 0)
    def _():
        y_ref[...] += out


def _peer(hn, u, v, g, t, rows_out, tm, te):
    row = pl.BlockSpec((tm, D_MODEL), lambda i, e: (i, 0))
    tab = pl.BlockSpec((te, D_MODEL), lambda i, e: (e, 0))
    return pl.pallas_call(
        _peer_kernel,
        grid=(t // tm, N_EXPERTS // te),
        in_specs=[row, tab, tab, pl.BlockSpec((tm, te // N_KEYS, N_KEYS), lambda i, e: (i, e, 0))],
        out_specs=row,
        out_shape=jax.ShapeDtypeStruct((rows_out, D_MODEL), F32),
        compiler_params=_params("parallel", "arbitrary"),
    )(hn, u, v, g)


def _final_norm_prompt_kernel(h_ref, f_ref, g_ref, y_ref):
    y = [_rms(h_ref[rr % 4, rr // 4] + f_ref[rr % 4, rr // 4], g_ref[...]) for rr in range(8)]
    y_ref[...] = jnp.swapaxes(jnp.stack(y, axis=0), 0, 1)


def _final_norm_prompt(h, f, g, batch, seq):
    rows = 64
    h5 = h.reshape(batch + 1, 4, 4, SLAB_ROWS, D_MODEL)
    f5 = f.reshape(batch + 1, 4, 4, SLAB_ROWS, D_MODEL)
    blk = pl.BlockSpec((None, 4, 2, rows, D_MODEL), lambda b, hh, j: (b, 0, hh, j, 0))
    y = pl.pallas_call(
        _final_norm_prompt_kernel,
        grid=(batch, 2, SLAB_ROWS // rows),
        in_specs=[blk, blk, pl.BlockSpec((1, D_MODEL), lambda b, hh, j: (0, 0))],
        out_specs=pl.BlockSpec((None, rows, 8, D_MODEL), lambda b, hh, j: (b, j, hh, 0)),
        out_shape=jax.ShapeDtypeStruct((batch, SLAB_ROWS, SLABS, D_MODEL), F32),
        compiler_params=_params("parallel", "parallel", "parallel"),
    )(h5, f5, g.reshape(1, D_MODEL))
    return y.reshape(batch, seq, D_MODEL)


def _final_norm_rows_kernel(h_ref, f_ref, g_ref, y_ref):
    y_ref[...] = _rms(h_ref[...] + f_ref[...], g_ref[...])


def _final_norm_rows(h, f, g, row_block, rows):
    blk = pl.BlockSpec((rows, D_MODEL), lambda i: (row_block, 0))
    return pl.pallas_call(
        _final_norm_rows_kernel,
        grid=(1,),
        in_specs=[blk, blk, pl.BlockSpec((1, D_MODEL), lambda i: (0, 0))],
        out_specs=pl.BlockSpec((rows, D_MODEL), lambda i: (0, 0)),
        out_shape=jax.ShapeDtypeStruct((rows, D_MODEL), F32),
        compiler_params=_params("arbitrary"),
    )(h, f, g.reshape(1, D_MODEL))


def _kv_prompt_kernel(*refs):
    ins, outs = refs[:2 * N_DIL], refs[2 * N_DIL:]
    for g in range(N_DIL):
        for kv in range(2):
            src = ins[2 * g + kv]
            heads = [src[0, :, h * HEAD_DIM:(h + 1) * HEAD_DIM] for h in range(ATT_HEADS)]
            outs[g][:, kv, :, :] = jnp.swapaxes(jnp.stack(heads, axis=0), 0, 1)


def _kv_prompt(p3, batch, seq):
    in_specs, out_specs, out_shape = [], [], []
    residue = lambda s: (s % 4) * 4 + s // 4
    for g in range(N_DIL):
        steps = min(DIL_WINDOWS[g], seq) // SLABS
        last = SLAB_ROWS // steps - 1
        for kv in range(2):
            col = COL_ATT + 3 * g + 1 + kv
            in_specs.append(pl.BlockSpec((1, steps, D_MODEL),
                                         lambda b, s, last=last, col=col: (b * SLABS + s, last, col)))
        out_specs.append(pl.BlockSpec((None, steps, None, 2, ATT_HEADS, HEAD_DIM),
                                      lambda b, s: (b, 0, residue(s), 0, 0, 0)))
        out_shape.append(jax.ShapeDtypeStruct((batch, steps, SLABS, 2, ATT_HEADS, HEAD_DIM), F32))
    outs = pl.pallas_call(
        _kv_prompt_kernel,
        grid=(batch, SLABS),
        in_specs=in_specs,
        out_specs=out_specs,
        out_shape=out_shape,
        compiler_params=_params("parallel", "parallel"),
    )(*([p3] * len(in_specs)))
    return [o.reshape(1, batch, -1, 2, ATT_HEADS, HEAD_DIM) for o in outs]


def _row_tile(t, candidates):
    for c in candidates:
        if t % c == 0:
            return c
    raise ValueError(f"no row tile for {t} tokens")


def kernel(x_prompt, x_sample, cache_kv_w128, cache_kv_w512, cache_kv_w2048, norm_mix_g, w_in, sgu_norm_g, sgu_w, sgu_b, w_out, norm_ffn_g, peer_w_q, peer_sub_keys, peer_u, peer_v, norm_final_g):
    batch, seq, _ = x_prompt.shape
    db, ds, _ = x_sample.shape
    assert w_in.shape[0] == 1 and db * ds == CHUNK and seq == SLABS * SLAB_ROWS
    caches = (cache_kv_w128, cache_kv_w512, cache_kv_w2048)
    tp, ts = batch * seq, db * ds
    t = tp + ts
    n_slabs = t // SLAB_ROWS
    tm_big = _row_tile(t, (1040, 640, 128))
    tm_lane = _row_tile(t, (640, 128))

    xs = x_sample.reshape(ts, D_MODEL)
    xl, xn = _permute_norm(x_prompt, norm_mix_g[0])
    xl = lax.dynamic_update_slice(xl, xs, (tp, 0))
    xn = lax.dynamic_update_slice(xn, _rmsnorm_bf16(xs, norm_mix_g[0]), (tp, 0))

    p, att16 = _inproj(xn, w_in[0], t, tm_big, 1024)
    p3 = p.reshape(n_slabs, SLAB_ROWS, p.shape[1])
    att16_3 = att16.reshape(n_slabs, SLAB_ROWS, att16.shape[1])

    w_tril = sgu_w[0] * jnp.tril(jnp.ones((CHUNK, CHUNK), F32))
    tau = _block_steps(0)
    w_p = w_tril[:, tau][:, :, tau].astype(BF16)
    b_p = sgu_b[0][:, tau][..., None]
    w_s = jnp.einsum("bc,gis->gbics", jnp.eye(db, dtype=F32), w_tril[:, :ds, :ds]).reshape(SGU_GROUPS, ts, ts)
    b_s = jnp.tile(sgu_b[0][:, :ds], (1, db))[..., None]
    a_out = _sgu_prompt(p3, sgu_norm_g[0], w_p, b_p, batch).reshape(t, D_MODEL)
    a_s, vn_s = _sgu_sample(p3, sgu_norm_g[0], w_s.astype(BF16), b_s, tp // SLAB_ROWS)
    a_out = lax.dynamic_update_slice(a_out, a_s, (tp, 0))

    att_s = p[tp:, COL_ATT * D_MODEL:COL_GATE * D_MODEL].reshape(db, ds, N_DIL, 3, ATT_HEADS, HEAD_DIM)
    outs, lses = [], []
    o_s, lse_s = _cached_attention(att_s, [c[0] for c in caches])
    band = _band_attention([p3, att16_3, att16_3], [COL_ATT, 0, 3], [False, True, True], batch)
    for g in range(N_DIL):
        o_p, lse_p = band[g]
        lse_g = jnp.pad(lse_s[g, ..., 0].reshape(ts, ATT_HEADS), ((0, 0), (0, 128 - ATT_HEADS)))
        o_g = o_s[g].reshape(ts, D_MODEL).astype(o_p.dtype)
        outs.append(lax.dynamic_update_slice(o_p.reshape(t, D_MODEL), o_g, (tp, 0)))
        lses.append(lax.dynamic_update_slice(lse_p.reshape(t, 128), lse_g, (tp, 0)))

    h, hn = _merge(a_out, att16, COL_GATE - COL_ATT16, outs, lses, xl, _to_bf16(w_out[0], 1024, D_MODEL),
                   norm_ffn_g[0], t, _row_tile(t, (320, 128)))

    scores_t = _peer_scores(hn, _to_bf16(peer_w_q[0], 1024, D_MODEL), peer_sub_keys[0].astype(BF16), tm_lane)
    e1, e2, gate, u_b16, v_b16 = _route(scores_t, 128, 4, peer_u[0], peer_v[0])
    g_dense = _expert_weights(e1, e2, gate, 128)
    f = _peer(hn, u_b16, v_b16, g_dense, t, h.shape[0], tm_lane, 1024)

    y_prompt = _final_norm_prompt(h, f, norm_final_g, batch, seq)
    y_sample = _final_norm_rows(h, f, norm_final_g, tp // ts, ts).reshape(db, ds, D_MODEL)
    kv_prompt = _kv_prompt(p3, batch, seq)
    kv_sample = [att_s[:, :, g, 1:3][None] for g in range(N_DIL)]
    sgu_v_sample = vn_s.reshape(1, db, ds, D_MODEL)
    return (y_prompt, y_sample, kv_prompt[0], kv_prompt[1], kv_prompt[2],
            kv_sample[0], kv_sample[1], kv_sample[2], sgu_v_sample)
```

```python
import functools

import numpy as np
import jax
import jax.numpy as jnp
from jax import lax
from jax.experimental import pallas as pl
from jax.experimental.pallas import tpu as pltpu

F32 = jnp.float32
BF16 = jnp.bfloat16

D_MODEL = 2048
HEAD_DIM = 128
ATT_HEADS = D_MODEL // HEAD_DIM
N_DIL = 3
DIL_WINDOWS = (128, 512, 2048)
DIL_RATES = (1, 4, 16)
ATT_BLOCK = 128
SLABS = DIL_RATES[-1]
SLAB_ROWS = 128
SGU_GROUPS = 8
SGU_GROUP_DIM = D_MODEL // SGU_GROUPS
CHUNK = 128
N_COL_BLOCKS = 13
COL_U, COL_V, COL_ATT, COL_GATE = 0, 1, 2, 11
COL_ATT16 = COL_ATT + 3
PEER_HEADS = 8
PEER_TOPK = 16
N_KEYS = 128
N_EXPERTS = N_KEYS * N_KEYS
PEER_HALF = 128
NORM_EPS = 1e-6
MASK_VALUE = -1e30
NEG_INF = float("-inf")
VMEM_LIMIT = 56 * 1024 * 1024

BLOCK_SHAPES = ((16, 8), (4, 32), (1, 128))


def _alibi_slopes():
    n = N_DIL * ATT_HEADS
    e = np.arange(1, n + 1, dtype=np.float32)
    return np.exp2(np.float32(-8.0) * e / np.float32(n)).astype(np.float32).reshape(N_DIL, ATT_HEADS)


def _block_steps(group):
    slabs, rows = BLOCK_SHAPES[group]
    n = np.arange(slabs * rows)
    s, j = n // rows, n % rows
    if group == 0:
        return j * 16 + (s % 4) * 4 + s // 4
    if group == 1:
        return j * 4 + s
    return j


def _params(*sem):
    return pltpu.CompilerParams(dimension_semantics=sem, vmem_limit_bytes=VMEM_LIMIT)


def _rms(x, g):
    return x * lax.rsqrt(jnp.mean(x * x, axis=-1, keepdims=True) + NORM_EPS) * g


def _permute_norm_kernel(x_ref, g_ref, xl_ref, xn_ref):
    xs = jnp.swapaxes(x_ref[...], 0, 1)
    for rr in range(8):
        x = xs[rr]
        xl_ref[rr % 4, rr // 4] = x
        xn_ref[rr % 4, rr // 4] = _rms(x, g_ref[...]).astype(xn_ref.dtype)


def _permute_norm(x_prompt, g):
    batch, seq, _ = x_prompt.shape
    x4 = x_prompt.reshape(batch, SLAB_ROWS, SLABS, D_MODEL)
    out = pl.BlockSpec((None, 4, 2, SLAB_ROWS, D_MODEL), lambda b, h: (b, 0, h, 0, 0))
    xl, xn = pl.pallas_call(
        _permute_norm_kernel,
        grid=(batch, 2),
        in_specs=[pl.BlockSpec((None, SLAB_ROWS, 8, D_MODEL), lambda b, h: (b, 0, h, 0)),
                  pl.BlockSpec((1, D_MODEL), lambda b, h: (0, 0))],
        out_specs=[out, out],
        out_shape=[jax.ShapeDtypeStruct((batch + 1, 4, 4, SLAB_ROWS, D_MODEL), F32),
                   jax.ShapeDtypeStruct((batch + 1, 4, 4, SLAB_ROWS, D_MODEL), BF16)],
        compiler_params=_params("parallel", "parallel"),
    )(x4, g.reshape(1, D_MODEL))
    rows = (batch + 1) * seq
    return xl.reshape(rows, D_MODEL), xn.reshape(rows, D_MODEL)


def _rmsnorm_kernel(x_ref, g_ref, o_ref):
    o_ref[...] = _rms(x_ref[...], g_ref[...]).astype(o_ref.dtype)


def _rmsnorm_bf16(x, g):
    return pl.pallas_call(
        _rmsnorm_kernel,
        out_shape=jax.ShapeDtypeStruct(x.shape, BF16),
    )(x, g.reshape(1, D_MODEL))


def _cast_kernel(x_ref, o_ref):
    o_ref[...] = x_ref[...].astype(o_ref.dtype)


def _to_bf16(w, rows, cols):
    r, c = w.shape
    return pl.pallas_call(
        _cast_kernel,
        grid=(r // rows, c // cols),
        in_specs=[pl.BlockSpec((rows, cols), lambda i, j: (i, j))],
        out_specs=pl.BlockSpec((rows, cols), lambda i, j: (i, j)),
        out_shape=jax.ShapeDtypeStruct((r, c), BF16),
        compiler_params=_params("parallel", "parallel"),
    )(w)


def _inproj_kernel(x_ref, w_ref, o_ref, att_ref, w16_ref, *, blocks_per_col):
    cb = pl.program_id(0) // blocks_per_col

    @pl.when(pl.program_id(1) == 0)
    def _():
        w16_ref[...] = w_ref[...].astype(w16_ref.dtype)

    p = jnp.dot(x_ref[...], w16_ref[...], preferred_element_type=F32)

    @pl.when(cb < COL_ATT)
    def _():
        o_ref[...] = jax.nn.gelu(p)

    @pl.when((cb >= COL_ATT) & (cb < COL_ATT16))
    def _():
        o_ref[...] = p

    @pl.when((cb >= COL_ATT16) & (cb < COL_GATE))
    def _():
        o_ref[...] = p
        is_q = (cb - COL_ATT) % 3 == 0
        att_ref[...] = (p * jnp.where(is_q, HEAD_DIM ** -0.5, 1.0)).astype(att_ref.dtype)

    @pl.when(cb >= COL_GATE)
    def _():
        att_ref[...] = jax.nn.sigmoid(p).astype(att_ref.dtype)


def _inproj(xn, w, t, tm, tn):
    n = w.shape[1]
    per_col = D_MODEL // tn
    n_rows = t // tm
    n_f32 = COL_GATE * per_col
    first_b16 = COL_ATT16 * per_col
    n_b16 = (N_COL_BLOCKS - COL_ATT16) * per_col
    f32_block = lambda j, i: (jnp.where(j < n_f32, i, n_rows - 1), jnp.minimum(j, n_f32 - 1))
    b16_block = lambda j, i: (jnp.where(j >= first_b16, i, 0), jnp.maximum(j - first_b16, 0))
    return pl.pallas_call(
        functools.partial(_inproj_kernel, blocks_per_col=per_col),
        grid=(n // tn, n_rows),
        in_specs=[pl.BlockSpec((tm, D_MODEL), lambda j, i: (i, 0)),
                  pl.BlockSpec((D_MODEL, tn), lambda j, i: (0, j))],
        out_specs=[pl.BlockSpec((tm, tn), f32_block),
                   pl.BlockSpec((tm, tn), b16_block)],
        out_shape=[jax.ShapeDtypeStruct((t, n_f32 * tn), F32),
                   jax.ShapeDtypeStruct((t, n_b16 * tn), BF16)],
        scratch_shapes=[pltpu.VMEM((D_MODEL, tn), BF16)],
        compiler_params=_params("arbitrary", "arbitrary"),
    )(xn, w)


def _sgu_chunk(u, v, g_ref, w_ref, b_ref):
    vc = v - jnp.mean(v, axis=-1, keepdims=True)
    vn = vc * lax.rsqrt(jnp.mean(vc * vc, axis=-1, keepdims=True) + NORM_EPS) * g_ref[...]
    cols = []
    for g in range(SGU_GROUPS):
        cs = slice(g * SGU_GROUP_DIM, (g + 1) * SGU_GROUP_DIM)
        mix = jnp.dot(w_ref[g], vn[:, cs].astype(BF16), preferred_element_type=F32) + b_ref[g]
        cols.append(u[:, cs] * mix)
    return jnp.concatenate(cols, axis=-1), vn


def _sgu_kernel(u_ref, v_ref, g_ref, w_ref, b_ref, a_ref, vn_ref):
    a, vn = _sgu_chunk(u_ref[...].reshape(CHUNK, D_MODEL), v_ref[...].reshape(CHUNK, D_MODEL), g_ref, w_ref, b_ref)
    a_ref[...] = a.reshape(a_ref.shape)
    vn_ref[...] = vn


def _sgu_prompt_kernel(u_ref, v_ref, g_ref, w_ref, b_ref, a_ref, *, rows):
    for c in range(u_ref.shape[1] // rows):
        rs = slice(c * rows, (c + 1) * rows)
        a, _ = _sgu_chunk(u_ref[:, rs, :].reshape(CHUNK, D_MODEL), v_ref[:, rs, :].reshape(CHUNK, D_MODEL),
                          g_ref, w_ref, b_ref)
        a_ref[:, rs, :] = a.reshape(a_ref.shape[0], rows, D_MODEL)


def _sgu_prompt(p3, sgu_norm_g, w, b, batch, chunks_per_step=2):
    slabs, rows = BLOCK_SHAPES[0]
    n_chunks = SLAB_ROWS // (rows * chunks_per_step)
    blk = lambda col: pl.BlockSpec((slabs, rows * chunks_per_step, D_MODEL), lambda bb, c: (bb, c, col))
    return pl.pallas_call(
        functools.partial(_sgu_prompt_kernel, rows=rows),
        grid=(batch, n_chunks),
        in_specs=[blk(COL_U), blk(COL_V),
                  pl.BlockSpec((1, D_MODEL), lambda bb, c: (0, 0)),
                  pl.BlockSpec((SGU_GROUPS, CHUNK, CHUNK), lambda bb, c: (0, 0, 0)),
                  pl.BlockSpec((SGU_GROUPS, CHUNK, 1), lambda bb, c: (0, 0, 0))],
        out_specs=blk(0),
        out_shape=jax.ShapeDtypeStruct((p3.shape[0], SLAB_ROWS, D_MODEL), F32),
        compiler_params=_params("parallel", "parallel"),
    )(p3, p3, sgu_norm_g.reshape(1, D_MODEL), w, b)


def _sgu_sample(p3, sgu_norm_g, w, b, slab):
    blk = lambda col: pl.BlockSpec((1, SLAB_ROWS, D_MODEL), lambda i: (slab, 0, col))
    return pl.pallas_call(
        _sgu_kernel,
        grid=(1,),
        in_specs=[blk(COL_U), blk(COL_V),
                  pl.BlockSpec((1, D_MODEL), lambda i: (0, 0)),
                  pl.BlockSpec((SGU_GROUPS, CHUNK, CHUNK), lambda i: (0, 0, 0)),
                  pl.BlockSpec((SGU_GROUPS, CHUNK, 1), lambda i: (0, 0, 0))],
        out_specs=[pl.BlockSpec((CHUNK, D_MODEL), lambda i: (0, 0)),
                   pl.BlockSpec((CHUNK, D_MODEL), lambda i: (0, 0))],
        out_shape=[jax.ShapeDtypeStruct((CHUNK, D_MODEL), F32), jax.ShapeDtypeStruct((CHUNK, D_MODEL), F32)],
        compiler_params=_params("arbitrary"),
    )(p3, p3, sgu_norm_g.reshape(1, D_MODEL), w, b)


def _band_block(bc_ref, bp_ref, q_ref, kc_ref, vc_ref, o_ref, lse_ref, kp_ref, vp_ref, prescaled):
    has_prev = kp_ref is not None
    n = ATT_BLOCK
    scale = HEAD_DIM ** -0.5
    nt = (((1,), (1,)), ((), ()))
    heads = range(ATT_HEADS)
    hs = [slice(h * HEAD_DIM, (h + 1) * HEAD_DIM) for h in heads]
    ld = lambda ref, h: ref[:, :, hs[h]].reshape(n, HEAD_DIM)
    q = [ld(q_ref, h) if prescaled else (ld(q_ref, h) * scale).astype(BF16) for h in heads]
    s_c = [lax.dot_general(q[h], ld(kc_ref, h).astype(BF16), nt, preferred_element_type=F32) + bc_ref[h]
           for h in heads]
    m = [jnp.max(s_c[h], axis=-1, keepdims=True) for h in heads]
    if has_prev:
        s_p = [lax.dot_general(q[h], ld(kp_ref, h).astype(BF16), nt, preferred_element_type=F32) + bp_ref[h]
               for h in heads]
        m = [jnp.maximum(m[h], jnp.max(s_p[h], axis=-1, keepdims=True)) for h in heads]
    p_c = [jnp.exp(s_c[h] - m[h]) for h in heads]
    l = [jnp.sum(p_c[h], axis=-1, keepdims=True) for h in heads]
    o = [jnp.dot(p_c[h].astype(BF16), ld(vc_ref, h).astype(BF16), preferred_element_type=F32) for h in heads]
    if has_prev:
        p_p = [jnp.exp(s_p[h] - m[h]) for h in heads]
        l = [l[h] + jnp.sum(p_p[h], axis=-1, keepdims=True) for h in heads]
        o = [o[h] + jnp.dot(p_p[h].astype(BF16), ld(vp_ref, h).astype(BF16), preferred_element_type=F32)
             for h in heads]
    lane = lax.broadcasted_iota(jnp.int32, (n, 128), 1)
    lse = jnp.zeros((n, 128), F32)
    for h in heads:
        lse = jnp.where(lane == h, m[h] + jnp.log(l[h]), lse)
    o_ref[...] = jnp.concatenate([o[h] / l[h] for h in heads], axis=-1).reshape(o_ref.shape).astype(o_ref.dtype)
    lse_ref[...] = lse.reshape(lse_ref.shape)
    if has_prev:
        kp_ref[...] = kc_ref[...]
        vp_ref[...] = vc_ref[...]


def _band_attn_kernel(*refs, blocks_per_stream, prescaled):
    refs = list(refs)
    k = pl.program_id(1)
    groups = range(len(blocks_per_stream))
    ins = []
    for g in groups:
        n_in = 5 if blocks_per_stream[g] > 1 else 4
        ins.append(refs[:n_in])
        refs = refs[n_in:]
    outs = [refs[2 * g:2 * g + 2] for g in groups]
    refs = refs[2 * len(blocks_per_stream):]
    for g in groups:
        if blocks_per_stream[g] > 1:
            bc_ref, bp_ref, q_ref, kc_ref, vc_ref = ins[g]
            kp_ref, vp_ref = refs[:2]
            refs = refs[2:]

            @pl.when(k % blocks_per_stream[g] == 0)
            def _(kp_ref=kp_ref, vp_ref=vp_ref):
                kp_ref[...] = jnp.zeros_like(kp_ref)
                vp_ref[...] = jnp.zeros_like(vp_ref)
        else:
            (bc_ref, q_ref, kc_ref, vc_ref), bp_ref, kp_ref, vp_ref = ins[g], None, None, None
        _band_block(bc_ref, bp_ref, q_ref, kc_ref, vc_ref, outs[g][0], outs[g][1], kp_ref, vp_ref, prescaled[g])


def _band_bias(group):
    steps = _block_steps(group)
    back = (steps[:, None] - steps[None, :]).astype(np.float32)
    coef = (_alibi_slopes()[group] * np.float32(DIL_RATES[group]))[:, None, None]
    cur = np.where(back >= 0, -(coef * back), np.float32(MASK_VALUE)).astype(np.float32)
    back_p = back + np.float32(ATT_BLOCK)
    prev = np.where(back_p <= ATT_BLOCK, -(coef * back_p), np.float32(MASK_VALUE)).astype(np.float32)
    return cur, np.stack([np.full_like(prev, MASK_VALUE), prev])


def _band_attention(srcs, col_q, prescaled, batch):
    n_slabs = srcs[0].shape[0]
    table = (ATT_HEADS, ATT_BLOCK, ATT_BLOCK)
    in_specs, args, out_specs, out_shape, scratch, blocks_per_stream = [], [], [], [], [], []
    for g in range(N_DIL):
        slabs, rows = BLOCK_SHAPES[g]
        streams = SLABS // slabs
        nb = SLAB_ROWS // rows
        blocks_per_stream.append(nb)
        bias_c, bias_p = _band_bias(g)

        def cur(col, width=D_MODEL, slabs=slabs, rows=rows, streams=streams, nb=nb):
            return pl.BlockSpec((slabs, rows, width), lambda b, k: (b * streams + k // nb, k % nb, col))

        in_specs.append(pl.BlockSpec(table, lambda b, k: (0, 0, 0)))
        args.append(jnp.asarray(bias_c))
        if nb > 1:
            in_specs.append(pl.BlockSpec((None,) + table, lambda b, k, nb=nb: (jnp.minimum(k % nb, 1), 0, 0, 0)))
            args.append(jnp.asarray(bias_p))
            scratch += [pltpu.VMEM((slabs, rows, D_MODEL), srcs[g].dtype)] * 2
        in_specs += [cur(col_q[g]), cur(col_q[g] + 1), cur(col_q[g] + 2)]
        args += [srcs[g]] * 3
        out_specs += [cur(0), cur(0, 128)]
        out_shape += [jax.ShapeDtypeStruct((n_slabs, SLAB_ROWS, D_MODEL), BF16 if prescaled[g] else F32),
                      jax.ShapeDtypeStruct((n_slabs, SLAB_ROWS, 128), F32)]
    res = pl.pallas_call(
        functools.partial(_band_attn_kernel, blocks_per_stream=tuple(blocks_per_stream), prescaled=tuple(prescaled)),
        grid=(batch, SLABS),
        in_specs=in_specs,
        out_specs=out_specs,
        out_shape=out_shape,
        scratch_shapes=scratch,
        compiler_params=_params("parallel", "arbitrary"),
    )(*args)
    return [(res[2 * g], res[2 * g + 1]) for g in range(N_DIL)]


def _cached_attn_kernel(new_ref, c0_ref, c1_ref, c2_ref, coef_ref, o_ref, lse_ref, *, dec_seq):
    caches = (c0_ref, c1_ref, c2_ref)
    shared = [d == 1 for d in DIL_RATES]
    pairs = [(g, i) for g in range(N_DIL) for i in range(dec_seq)]
    res = lambda g, i: 0 if shared[g] else i
    lane = lax.broadcasted_iota(jnp.int32, (ATT_HEADS, ATT_BLOCK), 1)
    scale = HEAD_DIM ** -0.5
    q = {(g, i): new_ref[i, g, 0] for g, i in pairs}

    def qk_body(r, carry):
        out = []
        for n, (g, i) in enumerate(pairs):
            col = jnp.sum(q[g, i] * caches[g][r, res(g, i), 0], axis=-1, keepdims=True)
            out.append(jnp.where(lane == r, col, carry[n]))
        return tuple(out)

    zero = jnp.zeros((ATT_HEADS, ATT_BLOCK), F32)
    s_all = lax.fori_loop(0, ATT_BLOCK, qk_body, (zero,) * len(pairs), unroll=16)

    ps, p_news, ls, ms = [], [], [], []
    for n, (g, i) in enumerate(pairs):
        coef = coef_ref[g]
        if shared[g]:
            back = (ATT_BLOCK + i - lane).astype(F32)
            s = jnp.where(lane >= i, s_all[n] * scale - coef * back, MASK_VALUE)
            s_new = [jnp.sum(q[g, i] * new_ref[i2, g, 1], axis=-1, keepdims=True) * scale
                     - coef[:, :1] * float(i - i2) for i2 in range(i + 1)]
        else:
            s = s_all[n] * scale - coef * (ATT_BLOCK - lane).astype(F32)
            s_new = [jnp.sum(q[g, i] * new_ref[i, g, 1], axis=-1, keepdims=True) * scale]
        m = jnp.max(s, axis=-1, keepdims=True)
        for sn in s_new:
            m = jnp.maximum(m, sn)
        p = jnp.exp(s - m)
        p_new = [jnp.exp(sn - m) for sn in s_new]
        l = jnp.sum(p, axis=-1, keepdims=True)
        for pn in p_new:
            l = l + pn
        ps.append(p)
        p_news.append(p_new)
        ls.append(l)
        ms.append(m)

    def pv_body(r, carry):
        out = []
        for n, (g, i) in enumerate(pairs):
            col = jnp.sum(jnp.where(lane == r, ps[n], 0.0), axis=-1, keepdims=True)
            out.append(carry[n] + col * caches[g][r, res(g, i), 1])
        return tuple(out)

    zero_o = jnp.zeros((ATT_HEADS, HEAD_DIM), F32)
    o_all = lax.fori_loop(0, ATT_BLOCK, pv_body, (zero_o,) * len(pairs), unroll=16)
    for n, (g, i) in enumerate(pairs):
        o = o_all[n]
        for i2, pn in enumerate(p_news[n]):
            o = o + pn * new_ref[i2 if shared[g] else i, g, 2]
        o_ref[g, i] = o / ls[n]
        lse_ref[g, i] = jnp.broadcast_to(ms[n] + jnp.log(ls[n]), (ATT_HEADS, 128))


def _cached_attention(att_s, caches):
    db, dec_seq = att_s.shape[:2]
    views, specs = [], []
    for g, cache in enumerate(caches):
        dil, window = DIL_RATES[g], DIL_WINDOWS[g]
        assert cache.shape[1] == window and window == dil * ATT_BLOCK
        assert dil == 1 or dec_seq <= dil
        nres = 1 if dil == 1 else dec_seq
        views.append(cache.reshape(db, ATT_BLOCK, dil, 2, ATT_HEADS, HEAD_DIM))
        specs.append(pl.BlockSpec((None, ATT_BLOCK, nres, 2, ATT_HEADS, HEAD_DIM), lambda b: (b, 0, 0, 0, 0, 0)))
    coef = np.stack([np.repeat((_alibi_slopes()[g] * DIL_RATES[g])[:, None], 128, axis=1) for g in range(N_DIL)])
    out = lambda w: pl.BlockSpec((N_DIL, None, dec_seq, ATT_HEADS, w), lambda b: (0, b, 0, 0, 0))
    return pl.pallas_call(
        functools.partial(_cached_attn_kernel, dec_seq=dec_seq),
        grid=(db,),
        in_specs=[pl.BlockSpec((None, dec_seq, N_DIL, 3, ATT_HEADS, HEAD_DIM), lambda b: (b, 0, 0, 0, 0, 0))]
        + specs + [pl.BlockSpec((N_DIL, ATT_HEADS, 128), lambda b: (0, 0, 0))],
        out_specs=[out(HEAD_DIM), out(128)],
        out_shape=[jax.ShapeDtypeStruct((N_DIL, db, dec_seq, ATT_HEADS, HEAD_DIM), F32),
                   jax.ShapeDtypeStruct((N_DIL, db, dec_seq, ATT_HEADS, 128), F32)],
        compiler_params=_params("parallel"),
    )(att_s, *views, jnp.asarray(coef.astype(np.float32)))


def _merge_kernel(a_ref, ga_ref, gb_ref, o0_ref, o1_ref, o2_ref, l0_ref, l1_ref, l2_ref,
                  x_ref, w_ref, g_ref, h_ref, hn_ref, merged_ref):
    l0, l1, l2 = l0_ref[...], l1_ref[...], l2_ref[...]
    mx = jnp.maximum(jnp.maximum(l0, l1), l2)
    e0, e1, e2 = jnp.exp(l0 - mx), jnp.exp(l1 - mx), jnp.exp(l2 - mx)
    den = e0 + e1 + e2
    w0, w1, w2 = e0 / den, e1 / den, e2 / den
    for h in range(ATT_HEADS):
        hs = slice(h * HEAD_DIM, (h + 1) * HEAD_DIM)
        b_out = (w0[:, h:h + 1] * o0_ref[:, hs] + w1[:, h:h + 1] * o1_ref[:, hs].astype(F32)
                 + w2[:, h:h + 1] * o2_ref[:, hs].astype(F32))
        merged = ga_ref[:, hs].astype(F32) * a_ref[:, hs] + gb_ref[:, hs].astype(F32) * b_out
        merged_ref[:, hs] = merged.astype(merged_ref.dtype)
    h_new = x_ref[...] + jnp.dot(merged_ref[...], w_ref[...], preferred_element_type=F32)
    h_ref[...] = h_new
    hn_ref[...] = _rms(h_new, g_ref[...]).astype(hn_ref.dtype)


def _merge(a_out, gates, gate_col, outs, lses, x, w_out, norm_g, t, tm):
    row = lambda col: pl.BlockSpec((tm, D_MODEL), lambda i: (i, col))
    lrow = pl.BlockSpec((tm, 128), lambda i: (i, 0))
    p = gates
    return pl.pallas_call(
        _merge_kernel,
        grid=(t // tm,),
        in_specs=[row(0), row(gate_col), row(gate_col + 1), row(0), row(0), row(0), lrow, lrow, lrow,
                  row(0), pl.BlockSpec((D_MODEL, D_MODEL), lambda i: (0, 0)),
                  pl.BlockSpec((1, D_MODEL), lambda i: (0, 0))],
        out_specs=[row(0), row(0)],
        out_shape=[jax.ShapeDtypeStruct((x.shape[0], D_MODEL), F32), jax.ShapeDtypeStruct((t, D_MODEL), BF16)],
        scratch_shapes=[pltpu.VMEM((tm, D_MODEL), BF16)],
        compiler_params=_params("parallel"),
    )(a_out, p, p, outs[0], outs[1], outs[2], lses[0], lses[1], lses[2], x, w_out, norm_g.reshape(1, D_MODEL))


def _peer_score_kernel(hn_ref, wq_ref, keys_ref, s_ref):
    q = jnp.dot(hn_ref[...], wq_ref[...], preferred_element_type=F32).astype(BF16)
    nt = (((1,), (1,)), ((), ()))
    for hc in range(2 * PEER_HEADS):
        cs = slice(hc * PEER_HALF, (hc + 1) * PEER_HALF)
        s_ref[hc] = lax.dot_general(keys_ref[hc % 2], q[:, cs], nt, preferred_element_type=F32)


def _peer_scores(hn, w_q, sub_keys, tm):
    t = hn.shape[0]
    return pl.pallas_call(
        _peer_score_kernel,
        grid=(t // tm,),
        in_specs=[pl.BlockSpec((tm, D_MODEL), lambda i: (i, 0)),
                  pl.BlockSpec((D_MODEL, 2 * PEER_HEADS * PEER_HALF), lambda i: (0, 0)),
                  pl.BlockSpec((2, N_KEYS, PEER_HALF), lambda i: (0, 0, 0))],
        out_specs=pl.BlockSpec((2 * PEER_HEADS, N_KEYS, tm), lambda i: (0, 0, i)),
        out_shape=jax.ShapeDtypeStruct((2 * PEER_HEADS, N_KEYS, t), F32),
        compiler_params=_params("parallel"),
    )(hn, w_q, sub_keys)


def _take_top(arrays, order, count, sentinel):
    arrays = list(arrays)
    vals = [[] for _ in arrays]
    idxs = [[] for _ in arrays]
    for _ in range(count):
        for n, s in enumerate(arrays):
            m = jnp.max(s, axis=0, keepdims=True)
            pos = jnp.min(jnp.where(s == m, order, sentinel), axis=0, keepdims=True)
            vals[n].append(m)
            idxs[n].append(pos)
            arrays[n] = jnp.where(order == pos, NEG_INF, s)
    return vals, idxs


def _sort_network(n):
    pairs, p = [], 1
    while p < n:
        k = p
        while k >= 1:
            for j in range(k % p, n - k, 2 * k):
                for i in range(min(k, n - j - k)):
                    if (i + j) // (2 * p) == (i + j + k) // (2 * p):
                        pairs.append((i + j, i + j + k))
            k //= 2
        p *= 2
    return pairs


def _take_top_sorted(arrays, order, count, sentinel):
    tiles = arrays[0].shape[0] // 8
    assert all(a.shape[0] == 8 * tiles for a in arrays)
    val = [[a[8 * v:8 * v + 8] for v in range(tiles)] for a in arrays]
    idx = [[order[8 * v:8 * v + 8] for v in range(tiles)] for _ in arrays]
    network = [(i, j) for i, j in _sort_network(1 << (tiles - 1).bit_length()) if j < tiles]
    for i, j in network:
        for n in range(len(arrays)):
            vi, vj, ri, rj = val[n][i], val[n][j], idx[n][i], idx[n][j]
            swap = (vj > vi) | ((vj == vi) & (rj < ri))
            val[n][i], val[n][j] = jnp.where(swap, vj, vi), jnp.where(swap, vi, vj)
            idx[n][i], idx[n][j] = jnp.where(swap, rj, ri), jnp.where(swap, ri, rj)
    vals = [[] for _ in arrays]
    idxs = [[] for _ in arrays]
    for t in range(count):
        for n in range(len(arrays)):
            head, rank = val[n][0], idx[n][0]
            m = jnp.max(head, axis=0, keepdims=True)
            pos = jnp.min(jnp.where(head == m, rank, sentinel), axis=0, keepdims=True)
            vals[n].append(m)
            idxs[n].append(pos)
            popped = rank == pos
            for d in range(min(tiles, count - t - 1)):
                if d + 1 < tiles:
                    val[n][d] = jnp.where(popped, val[n][d + 1], val[n][d])
                    idx[n][d] = jnp.where(popped, idx[n][d + 1], idx[n][d])
                else:
                    val[n][d] = jnp.where(popped, NEG_INF, val[n][d])
                    idx[n][d] = jnp.where(popped, sentinel, idx[n][d])
    return vals, idxs


def _stack_rows(rows_list, krow):
    out = jnp.zeros(krow.shape, F32)
    for j, r in enumerate(rows_list):
        out = jnp.where(krow == float(j), r, out)
    return out


def _cand_layout():
    k = PEER_TOPK
    pieces = [("row_a", a, 16 if a == 0 else 8, 0, k // (a + 1)) for a in range(4)]
    pieces += [("col_b", 0, 16, 4, 16), ("col_b", 1, 8, 4, 8), ("col_b", 2, 8, 4, 5)]
    pos = []
    for kind, idx, rows, lo, hi in pieces:
        for r in range(rows):
            a, b = (idx, r) if kind == "row_a" else (r, idx)
            ok = lo <= r < hi and (a + 1) * (b + 1) <= k
            pos.append(a * k + b if ok else k * k)
    assert sorted(p for p in pos if p < k * k) == sorted(
        a * k + b for a in range(k) for b in range(k) if (a + 1) * (b + 1) <= k)
    return pieces, np.asarray(pos, np.float32)


def _route_kernel(s_ref, pos_ref, u_ref, v_ref, e1_ref, e2_ref, gate_ref, ub_ref, vb_ref):
    ub_ref[...] = u_ref[...].astype(ub_ref.dtype)
    vb_ref[...] = v_ref[...].astype(vb_ref.dtype)
    k = PEER_TOPK
    heads = s_ref.shape[0] // 2
    lanes = s_ref.shape[2]
    key_rank = lax.broadcasted_iota(jnp.int32, (N_KEYS, lanes), 0).astype(F32)
    vals, idxs = _take_top_sorted([s_ref[n] for n in range(2 * heads)], key_rank, k, float(N_KEYS))
    krow = lax.broadcasted_iota(jnp.int32, (k, lanes), 0).astype(F32)
    flat = pos_ref[...]
    cands, i1_all, i2_all = [], [], []
    for hd in range(heads):
        v1, v2 = vals[2 * hd], vals[2 * hd + 1]
        v1_all = _stack_rows(v1, krow)
        v2_all = _stack_rows(v2, krow)
        i1_all.append(_stack_rows(idxs[2 * hd], krow))
        i2_all.append(_stack_rows(idxs[2 * hd + 1], krow))
        parts = [v1[idx] + v2_all[:rows] if kind == "row_a" else v1_all[:rows] + v2[idx]
                 for kind, idx, rows, _, _ in _cand_layout()[0]]
        cands.append(jnp.where(flat < float(k * k), jnp.concatenate(parts, axis=0), NEG_INF))
    top_s, pos = _take_top_sorted(cands, flat, k, float(k * k))
    for hd in range(heads):
        e1, e2 = [], []
        for j in range(k):
            a = jnp.floor(pos[hd][j] * (1.0 / k))
            b = pos[hd][j] - a * k
            e1.append(jnp.sum(jnp.where(krow == a, i1_all[hd], 0.0), axis=0, keepdims=True))
            e2.append(jnp.sum(jnp.where(krow == b, i2_all[hd], 0.0), axis=0, keepdims=True))
        ex = jnp.exp(_stack_rows(top_s[hd], krow) - top_s[hd][0])
        rows = slice(hd * k, (hd + 1) * k)
        gate_ref[rows, :] = ex / jnp.sum(ex, axis=0, keepdims=True)
        e1_ref[rows, :] = _stack_rows(e1, krow)
        e2_ref[rows, :] = _stack_rows(e2, krow)


def _route(scores_t, tl, heads_per_step, peer_u, peer_v):
    t = scores_t.shape[2]
    kk = PEER_HEADS * PEER_TOPK
    n_head_steps = PEER_HEADS // heads_per_step
    steps = (t // tl) * n_head_steps
    n_blocks = 1 << (steps.bit_length() - 1)
    tab_rows = N_EXPERTS // n_blocks
    tab = pl.BlockSpec((tab_rows, D_MODEL), lambda i, h: (jnp.minimum(i * n_head_steps + h, n_blocks - 1), 0))
    out = pl.BlockSpec((heads_per_step * PEER_TOPK, tl), lambda i, h: (h, i))
    flat = jnp.asarray(np.repeat(_cand_layout()[1][:, None], tl, axis=1))
    return pl.pallas_call(
        _route_kernel,
        grid=(t // tl, n_head_steps),
        in_specs=[pl.BlockSpec((2 * heads_per_step, N_KEYS, tl), lambda i, h: (h, 0, i)),
                  pl.BlockSpec(flat.shape, lambda i, h: (0, 0)), tab, tab],
        out_specs=[out, out, out, tab, tab],
        out_shape=[jax.ShapeDtypeStruct((kk, t), F32)] * 3 + [jax.ShapeDtypeStruct((N_EXPERTS, D_MODEL), BF16)] * 2,
        compiler_params=_params("arbitrary", "arbitrary"),
    )(scores_t, flat, peer_u, peer_v)


def _expert_weight_kernel(e1_ref, e2_ref, gate_ref, g_ref, e1t_ref, e2t_ref, gt_ref):
    e1t_ref[...] = e1_ref[...].T
    e2t_ref[...] = e2_ref[...].T
    gt_ref[...] = gate_ref[...].T
    kk = e1_ref.shape[0]
    key = lax.broadcasted_iota(jnp.int32, (N_KEYS, kk), 0).astype(F32)
    nt = (((1,), (1,)), ((), ()))

    def body(t, carry):
        r1 = e1t_ref[pl.ds(t, 1), :]
        r2 = e2t_ref[pl.ds(t, 1), :]
        gr = gt_ref[pl.ds(t, 1), :]
        a_t = jnp.where(key == r1, 1.0, 0.0).astype(BF16)
        b_t = jnp.where(key == r2, gr, 0.0).astype(BF16)
        g_ref[t] = lax.dot_general(a_t, b_t, nt, preferred_element_type=F32).astype(g_ref.dtype)
        return carry

    lax.fori_loop(0, g_ref.shape[0], body, 0, unroll=32)


def _expert_weights(e1, e2, gate, tl):
    kk, t = e1.shape
    slot = pl.BlockSpec((kk, tl), lambda i: (0, i))
    return pl.pallas_call(
        _expert_weight_kernel,
        grid=(t // tl,),
        in_specs=[slot, slot, slot],
        out_specs=pl.BlockSpec((tl, N_KEYS, N_KEYS), lambda i: (i, 0, 0)),
        out_shape=jax.ShapeDtypeStruct((t, N_KEYS, N_KEYS), F32),
        scratch_shapes=[pltpu.VMEM((tl, kk), F32)] * 3,
        compiler_params=_params("parallel"),
    )(e1, e2, gate)


def _peer_kernel(hn_ref, u_ref, v_ref, g_ref, y_ref):
    e = pl.program_id(1)
    nt = (((1,), (1,)), ((), ()))
    hk = lax.dot_general(hn_ref[...], u_ref[...], nt, preferred_element_type=F32)
    g = jnp.swapaxes(g_ref[...], 0, 1)
    act = [(jax.nn.gelu(hk[:, a * N_KEYS:(a + 1) * N_KEYS]) * g[a]).astype(BF16) for a in range(g.shape[0])]
    out = jnp.dot(jnp.concatenate(act, axis=-1), v_ref[...], preferred_element_type=F32)

    @pl.when(e == 0)
    def _():
        y_ref[...] = out

    @pl.when(e > 0)
    def _():
        y_ref[...] += out


def _peer(hn, u, v, g, t, rows_out, tm, te):
    row = pl.BlockSpec((tm, D_MODEL), lambda i, e: (i, 0), pipeline_mode=pl.Buffered(1))
    tab = pl.BlockSpec((te, D_MODEL), lambda i, e: (e, 0))
    return pl.pallas_call(
        _peer_kernel,
        grid=(t // tm, N_EXPERTS // te),
        in_specs=[row, tab, tab, pl.BlockSpec((tm, te // N_KEYS, N_KEYS), lambda i, e: (i, e, 0))],
        out_specs=row,
        out_shape=jax.ShapeDtypeStruct((rows_out, D_MODEL), F32),
        compiler_params=_params("parallel", "arbitrary"),
    )(hn, u, v, g)


def _final_norm_prompt_kernel(h_ref, f_ref, g_ref, y_ref):
    y = [_rms(h_ref[rr % 4, rr // 4] + f_ref[rr % 4, rr // 4], g_ref[...]) for rr in range(8)]
    y_ref[...] = jnp.swapaxes(jnp.stack(y, axis=0), 0, 1)


def _final_norm_prompt(h, f, g, batch, seq):
    rows = 64
    h5 = h.reshape(batch + 1, 4, 4, SLAB_ROWS, D_MODEL)
    f5 = f.reshape(batch + 1, 4, 4, SLAB_ROWS, D_MODEL)
    blk = pl.BlockSpec((None, 4, 2, rows, D_MODEL), lambda b, hh, j: (b, 0, hh, j, 0))
    y = pl.pallas_call(
        _final_norm_prompt_kernel,
        grid=(batch, 2, SLAB_ROWS // rows),
        in_specs=[blk, blk, pl.BlockSpec((1, D_MODEL), lambda b, hh, j: (0, 0))],
        out_specs=pl.BlockSpec((None, rows, 8, D_MODEL), lambda b, hh, j: (b, j, hh, 0)),
        out_shape=jax.ShapeDtypeStruct((batch, SLAB_ROWS, SLABS, D_MODEL), F32),
        compiler_params=_params("parallel", "parallel", "parallel"),
    )(h5, f5, g.reshape(1, D_MODEL))
    return y.reshape(batch, seq, D_MODEL)


def _final_norm_rows_kernel(h_ref, f_ref, g_ref, y_ref):
    y_ref[...] = _rms(h_ref[...] + f_ref[...], g_ref[...])


def _final_norm_rows(h, f, g, row_block, rows):
    blk = pl.BlockSpec((rows, D_MODEL), lambda i: (row_block, 0))
    return pl.pallas_call(
        _final_norm_rows_kernel,
        grid=(1,),
        in_specs=[blk, blk, pl.BlockSpec((1, D_MODEL), lambda i: (0, 0))],
        out_specs=pl.BlockSpec((rows, D_MODEL), lambda i: (0, 0)),
        out_shape=jax.ShapeDtypeStruct((rows, D_MODEL), F32),
        compiler_params=_params("arbitrary"),
    )(h, f, g.reshape(1, D_MODEL))


def _kv_prompt_kernel(*refs):
    ins, outs = refs[:2 * N_DIL], refs[2 * N_DIL:]
    for g in range(N_DIL):
        for kv in range(2):
            src = ins[2 * g + kv]
            heads = [src[0, :, h * HEAD_DIM:(h + 1) * HEAD_DIM] for h in range(ATT_HEADS)]
            outs[g][:, kv, :, :] = jnp.swapaxes(jnp.stack(heads, axis=0), 0, 1)


def _kv_prompt(p3, batch, seq):
    in_specs, out_specs, out_shape = [], [], []
    residue = lambda s: (s % 4) * 4 + s // 4
    for g in range(N_DIL):
        steps = min(DIL_WINDOWS[g], seq) // SLABS
        last = SLAB_ROWS // steps - 1
        for kv in range(2):
            col = COL_ATT + 3 * g + 1 + kv
            in_specs.append(pl.BlockSpec((1, steps, D_MODEL),
                                         lambda b, s, last=last, col=col: (b * SLABS + s, last, col)))
        out_specs.append(pl.BlockSpec((None, steps, None, 2, ATT_HEADS, HEAD_DIM),
                                      lambda b, s: (b, 0, residue(s), 0, 0, 0)))
        out_shape.append(jax.ShapeDtypeStruct((batch, steps, SLABS, 2, ATT_HEADS, HEAD_DIM), F32))
    outs = pl.pallas_call(
        _kv_prompt_kernel,
        grid=(batch, SLABS),
        in_specs=in_specs,
        out_specs=out_specs,
        out_shape=out_shape,
        compiler_params=_params("parallel", "parallel"),
    )(*([p3] * len(in_specs)))
    return [o.reshape(1, batch, -1, 2, ATT_HEADS, HEAD_DIM) for o in outs]


def _row_tile(t, candidates):
    for c in candidates:
        if t % c == 0:
            return c
    raise ValueError(f"no row tile for {t} tokens")


def kernel(x_prompt, x_sample, cache_kv_w128, cache_kv_w512, cache_kv_w2048, norm_mix_g, w_in, sgu_norm_g, sgu_w, sgu_b, w_out, norm_ffn_g, peer_w_q, peer_sub_keys, peer_u, peer_v, norm_final_g):
    batch, seq, _ = x_prompt.shape
    db, ds, _ = x_sample.shape
    assert w_in.shape[0] == 1 and db * ds == CHUNK and seq == SLABS * SLAB_ROWS
    caches = (cache_kv_w128, cache_kv_w512, cache_kv_w2048)
    tp, ts = batch * seq, db * ds
    t = tp + ts
    n_slabs = t // SLAB_ROWS
    tm_big = _row_tile(t, (1040, 640, 128))
    tm_lane = _row_tile(t, (640, 128))

    xs = x_sample.reshape(ts, D_MODEL)
    xl, xn = _permute_norm(x_prompt, norm_mix_g[0])
    xl = lax.dynamic_update_slice(xl, xs, (tp, 0))
    xn = lax.dynamic_update_slice(xn, _rmsnorm_bf16(xs, norm_mix_g[0]), (tp, 0))

    p, att16 = _inproj(xn, w_in[0], t, tm_big, 1024)
    p3 = p.reshape(n_slabs, SLAB_ROWS, p.shape[1])
    att16_3 = att16.reshape(n_slabs, SLAB_ROWS, att16.shape[1])

    w_tril = sgu_w[0] * jnp.tril(jnp.ones((CHUNK, CHUNK), F32))
    tau = _block_steps(0)
    w_p = w_tril[:, tau][:, :, tau].astype(BF16)
    b_p = sgu_b[0][:, tau][..., None]
    w_s = jnp.einsum("bc,gis->gbics", jnp.eye(db, dtype=F32), w_tril[:, :ds, :ds]).reshape(SGU_GROUPS, ts, ts)
    b_s = jnp.tile(sgu_b[0][:, :ds], (1, db))[..., None]
    a_out = _sgu_prompt(p3, sgu_norm_g[0], w_p, b_p, batch).reshape(t, D_MODEL)
    a_s, vn_s = _sgu_sample(p3, sgu_norm_g[0], w_s.astype(BF16), b_s, tp // SLAB_ROWS)
    a_out = lax.dynamic_update_slice(a_out, a_s, (tp, 0))

    att_s = p[tp:, COL_ATT * D_MODEL:COL_GATE * D_MODEL].reshape(db, ds, N_DIL, 3, ATT_HEADS, HEAD_DIM)
    outs, lses = [], []
    o_s, lse_s = _cached_attention(att_s, [c[0] for c in caches])
    band = _band_attention([p3, att16_3, att16_3], [COL_ATT, 0, 3], [False, True, True], batch)
    for g in range(N_DIL):
        o_p, lse_p = band[g]
        lse_g = jnp.pad(lse_s[g, ..., 0].reshape(ts, ATT_HEADS), ((0, 0), (0, 128 - ATT_HEADS)))
        o_g = o_s[g].reshape(ts, D_MODEL).astype(o_p.dtype)
        outs.append(lax.dynamic_update_slice(o_p.reshape(t, D_MODEL), o_g, (tp, 0)))
        lses.append(lax.dynamic_update_slice(lse_p.reshape(t, 128), lse_g, (tp, 0)))

    h, hn = _merge(a_out, att16, COL_GATE - COL_ATT16, outs, lses, xl, _to_bf16(w_out[0], 1024, D_MODEL),
                   norm_ffn_g[0], t, _row_tile(t, (320, 128)))

    scores_t = _peer_scores(hn, _to_bf16(peer_w_q[0], 1024, D_MODEL), peer_sub_keys[0].astype(BF16), tm_lane)
    e1, e2, gate, u_b16, v_b16 = _route(scores_t, 128, 4, peer_u[0], peer_v[0])
    g_dense = _expert_weights(e1, e2, gate, 128)
    f = _peer(hn, u_b16, v_b16, g_dense, t, h.shape[0], tm_big, 1024)

    y_prompt = _final_norm_prompt(h, f, norm_final_g, batch, seq)
    y_sample = _final_norm_rows(h, f, norm_final_g, tp // ts, ts).reshape(db, ds, D_MODEL)
    kv_prompt = _kv_prompt(p3, batch, seq)
    kv_sample = [att_s[:, :, g, 1:3][None] for g in range(N_DIL)]
    sgu_v_sample = vn_s.reshape(1, db, ds, D_MODEL)
    return (y_prompt, y_sample, kv_prompt[0], kv_prompt[1], kv_prompt[2],
            kv_sample[0], kv_sample[1], kv_sample[2], sgu_v_sample)
```

```python
import functools

import numpy as np
import jax
import jax.numpy as jnp
from jax import lax
from jax.experimental import pallas as pl
from jax.experimental.pallas import tpu as pltpu

F32 = jnp.float32
BF16 = jnp.bfloat16

D_MODEL = 2048
HEAD_DIM = 128
ATT_HEADS = D_MODEL // HEAD_DIM
N_DIL = 3
DIL_WINDOWS = (128, 512, 2048)
DIL_RATES = (1, 4, 16)
ATT_BLOCK = 128
SLABS = DIL_RATES[-1]
SLAB_ROWS = 128
SGU_GROUPS = 8
SGU_GROUP_DIM = D_MODEL // SGU_GROUPS
CHUNK = 128
N_COL_BLOCKS = 13
COL_U, COL_V, COL_ATT, COL_GATE = 0, 1, 2, 11
COL_ATT16 = COL_ATT + 3
PEER_HEADS = 8
PEER_TOPK = 16
N_KEYS = 128
N_EXPERTS = N_KEYS * N_KEYS
PEER_HALF = 128
NORM_EPS = 1e-6
MASK_VALUE = -1e30
NEG_INF = float("-inf")
VMEM_LIMIT = 56 * 1024 * 1024

BLOCK_SHAPES = ((16, 8), (4, 32), (1, 128))


def _alibi_slopes():
    n = N_DIL * ATT_HEADS
    e = np.arange(1, n + 1, dtype=np.float32)
    return np.exp2(np.float32(-8.0) * e / np.float32(n)).astype(np.float32).reshape(N_DIL, ATT_HEADS)


def _block_steps(group):
    slabs, rows = BLOCK_SHAPES[group]
    n = np.arange(slabs * rows)
    s, j = n // rows, n % rows
    if group == 0:
        return j * 16 + (s % 4) * 4 + s // 4
    if group == 1:
        return j * 4 + s
    return j


def _params(*sem):
    return pltpu.CompilerParams(dimension_semantics=sem, vmem_limit_bytes=VMEM_LIMIT)


def _rms(x, g):
    return x * lax.rsqrt(jnp.mean(x * x, axis=-1, keepdims=True) + NORM_EPS) * g


def _permute_norm_kernel(x_ref, g_ref, xl_ref, xn_ref):
    xs = jnp.swapaxes(x_ref[...], 0, 1)
    for rr in range(8):
        x = xs[rr]
        xl_ref[rr % 4, rr // 4] = x
        xn_ref[rr % 4, rr // 4] = _rms(x, g_ref[...]).astype(xn_ref.dtype)


def _permute_norm(x_prompt, g):
    batch, seq, _ = x_prompt.shape
    x4 = x_prompt.reshape(batch, SLAB_ROWS, SLABS, D_MODEL)
    out = pl.BlockSpec((None, 4, 2, SLAB_ROWS, D_MODEL), lambda b, h: (b, 0, h, 0, 0))
    xl, xn = pl.pallas_call(
        _permute_norm_kernel,
        grid=(batch, 2),
        in_specs=[pl.BlockSpec((None, SLAB_ROWS, 8, D_MODEL), lambda b, h: (b, 0, h, 0)),
                  pl.BlockSpec((1, D_MODEL), lambda b, h: (0, 0))],
        out_specs=[out, out],
        out_shape=[jax.ShapeDtypeStruct((batch + 1, 4, 4, SLAB_ROWS, D_MODEL), F32),
                   jax.ShapeDtypeStruct((batch + 1, 4, 4, SLAB_ROWS, D_MODEL), BF16)],
        compiler_params=_params("parallel", "parallel"),
    )(x4, g.reshape(1, D_MODEL))
    rows = (batch + 1) * seq
    return xl.reshape(rows, D_MODEL), xn.reshape(rows, D_MODEL)


def _rmsnorm_kernel(x_ref, g_ref, o_ref):
    o_ref[...] = _rms(x_ref[...], g_ref[...]).astype(o_ref.dtype)


def _rmsnorm_bf16(x, g):
    return pl.pallas_call(
        _rmsnorm_kernel,
        out_shape=jax.ShapeDtypeStruct(x.shape, BF16),
    )(x, g.reshape(1, D_MODEL))


def _cast_kernel(x_ref, o_ref):
    o_ref[...] = x_ref[...].astype(o_ref.dtype)


def _to_bf16(w, rows, cols):
    r, c = w.shape
    return pl.pallas_call(
        _cast_kernel,
        grid=(r // rows, c // cols),
        in_specs=[pl.BlockSpec((rows, cols), lambda i, j: (i, j))],
        out_specs=pl.BlockSpec((rows, cols), lambda i, j: (i, j)),
        out_shape=jax.ShapeDtypeStruct((r, c), BF16),
        compiler_params=_params("parallel", "parallel"),
    )(w)


def _inproj_kernel(x_ref, w_ref, o_ref, att_ref, w16_ref, *, blocks_per_col):
    cb = pl.program_id(0) // blocks_per_col

    @pl.when(pl.program_id(1) == 0)
    def _():
        w16_ref[...] = w_ref[...].astype(w16_ref.dtype)

    p = jnp.dot(x_ref[...], w16_ref[...], preferred_element_type=F32)

    @pl.when(cb < COL_ATT)
    def _():
        o_ref[...] = jax.nn.gelu(p)

    @pl.when((cb >= COL_ATT) & (cb < COL_ATT16))
    def _():
        o_ref[...] = p

    @pl.when((cb >= COL_ATT16) & (cb < COL_GATE))
    def _():
        o_ref[...] = p
        is_q = (cb - COL_ATT) % 3 == 0
        att_ref[...] = (p * jnp.where(is_q, HEAD_DIM ** -0.5, 1.0)).astype(att_ref.dtype)

    @pl.when(cb >= COL_GATE)
    def _():
        att_ref[...] = jax.nn.sigmoid(p).astype(att_ref.dtype)


def _inproj(xn, w, t, tm, tn):
    n = w.shape[1]
    per_col = D_MODEL // tn
    n_rows = t // tm
    n_f32 = COL_GATE * per_col
    first_b16 = COL_ATT16 * per_col
    n_b16 = (N_COL_BLOCKS - COL_ATT16) * per_col
    f32_block = lambda j, i: (jnp.where(j < n_f32, i, n_rows - 1), jnp.minimum(j, n_f32 - 1))
    b16_block = lambda j, i: (jnp.where(j >= first_b16, i, 0), jnp.maximum(j - first_b16, 0))
    return pl.pallas_call(
        functools.partial(_inproj_kernel, blocks_per_col=per_col),
        grid=(n // tn, n_rows),
        in_specs=[pl.BlockSpec((tm, D_MODEL), lambda j, i: (i, 0)),
                  pl.BlockSpec((D_MODEL, tn), lambda j, i: (0, j))],
        out_specs=[pl.BlockSpec((tm, tn), f32_block),
                   pl.BlockSpec((tm, tn), b16_block)],
        out_shape=[jax.ShapeDtypeStruct((t, n_f32 * tn), F32),
                   jax.ShapeDtypeStruct((t, n_b16 * tn), BF16)],
        scratch_shapes=[pltpu.VMEM((D_MODEL, tn), BF16)],
        compiler_params=_params("arbitrary", "arbitrary"),
    )(xn, w)


def _sgu_chunk(u, v, g_ref, w_ref, b_ref):
    vc = v - jnp.mean(v, axis=-1, keepdims=True)
    vn = vc * lax.rsqrt(jnp.mean(vc * vc, axis=-1, keepdims=True) + NORM_EPS) * g_ref[...]
    cols = []
    for g in range(SGU_GROUPS):
        cs = slice(g * SGU_GROUP_DIM, (g + 1) * SGU_GROUP_DIM)
        mix = jnp.dot(w_ref[g], vn[:, cs].astype(BF16), preferred_element_type=F32) + b_ref[g]
        cols.append(u[:, cs] * mix)
    return jnp.concatenate(cols, axis=-1), vn


def _sgu_kernel(u_ref, v_ref, g_ref, w_ref, b_ref, a_ref, vn_ref):
    a, vn = _sgu_chunk(u_ref[...].reshape(CHUNK, D_MODEL), v_ref[...].reshape(CHUNK, D_MODEL), g_ref, w_ref, b_ref)
    a_ref[...] = a.reshape(a_ref.shape)
    vn_ref[...] = vn


def _sgu_prompt_kernel(u_ref, v_ref, g_ref, w_ref, b_ref, a_ref, *, rows):
    for c in range(u_ref.shape[1] // rows):
        rs = slice(c * rows, (c + 1) * rows)
        a, _ = _sgu_chunk(u_ref[:, rs, :].reshape(CHUNK, D_MODEL), v_ref[:, rs, :].reshape(CHUNK, D_MODEL),
                          g_ref, w_ref, b_ref)
        a_ref[:, rs, :] = a.reshape(a_ref.shape[0], rows, D_MODEL)


def _sgu_prompt(p3, sgu_norm_g, w, b, batch, chunks_per_step=4):
    slabs, rows = BLOCK_SHAPES[0]
    n_chunks = SLAB_ROWS // (rows * chunks_per_step)
    blk = lambda col: pl.BlockSpec((slabs, rows * chunks_per_step, D_MODEL), lambda bb, c: (bb, c, col))
    return pl.pallas_call(
        functools.partial(_sgu_prompt_kernel, rows=rows),
        grid=(batch, n_chunks),
        in_specs=[blk(COL_U), blk(COL_V),
                  pl.BlockSpec((1, D_MODEL), lambda bb, c: (0, 0)),
                  pl.BlockSpec((SGU_GROUPS, CHUNK, CHUNK), lambda bb, c: (0, 0, 0)),
                  pl.BlockSpec((SGU_GROUPS, CHUNK, 1), lambda bb, c: (0, 0, 0))],
        out_specs=blk(0),
        out_shape=jax.ShapeDtypeStruct((p3.shape[0], SLAB_ROWS, D_MODEL), F32),
        compiler_params=_params("parallel", "parallel"),
    )(p3, p3, sgu_norm_g.reshape(1, D_MODEL), w, b)


def _sgu_sample(p3, sgu_norm_g, w, b, slab):
    blk = lambda col: pl.BlockSpec((1, SLAB_ROWS, D_MODEL), lambda i: (slab, 0, col))
    return pl.pallas_call(
        _sgu_kernel,
        grid=(1,),
        in_specs=[blk(COL_U), blk(COL_V),
                  pl.BlockSpec((1, D_MODEL), lambda i: (0, 0)),
                  pl.BlockSpec((SGU_GROUPS, CHUNK, CHUNK), lambda i: (0, 0, 0)),
                  pl.BlockSpec((SGU_GROUPS, CHUNK, 1), lambda i: (0, 0, 0))],
        out_specs=[pl.BlockSpec((CHUNK, D_MODEL), lambda i: (0, 0)),
                   pl.BlockSpec((CHUNK, D_MODEL), lambda i: (0, 0))],
        out_shape=[jax.ShapeDtypeStruct((CHUNK, D_MODEL), F32), jax.ShapeDtypeStruct((CHUNK, D_MODEL), F32)],
        compiler_params=_params("arbitrary"),
    )(p3, p3, sgu_norm_g.reshape(1, D_MODEL), w, b)


def _band_block(bc_ref, bp_ref, q_ref, kc_ref, vc_ref, o_ref, lse_ref, kp_ref, vp_ref, prescaled):
    has_prev = kp_ref is not None
    n = ATT_BLOCK
    scale = HEAD_DIM ** -0.5
    nt = (((1,), (1,)), ((), ()))
    heads = range(ATT_HEADS)
    hs = [slice(h * HEAD_DIM, (h + 1) * HEAD_DIM) for h in heads]
    ld = lambda ref, h: ref[:, :, hs[h]].reshape(n, HEAD_DIM)
    q = [ld(q_ref, h) if prescaled else (ld(q_ref, h) * scale).astype(BF16) for h in heads]
    s_c = [lax.dot_general(q[h], ld(kc_ref, h).astype(BF16), nt, preferred_element_type=F32) + bc_ref[h]
           for h in heads]
    m = [jnp.max(s_c[h], axis=-1, keepdims=True) for h in heads]
    if has_prev:
        s_p = [lax.dot_general(q[h], ld(kp_ref, h).astype(BF16), nt, preferred_element_type=F32) + bp_ref[h]
               for h in heads]
        m = [jnp.maximum(m[h], jnp.max(s_p[h], axis=-1, keepdims=True)) for h in heads]
    p_c = [jnp.exp(s_c[h] - m[h]) for h in heads]
    l = [jnp.sum(p_c[h], axis=-1, keepdims=True) for h in heads]
    o = [jnp.dot(p_c[h].astype(BF16), ld(vc_ref, h).astype(BF16), preferred_element_type=F32) for h in heads]
    if has_prev:
        p_p = [jnp.exp(s_p[h] - m[h]) for h in heads]
        l = [l[h] + jnp.sum(p_p[h], axis=-1, keepdims=True) for h in heads]
        o = [o[h] + jnp.dot(p_p[h].astype(BF16), ld(vp_ref, h).astype(BF16), preferred_element_type=F32)
             for h in heads]
    lane = lax.broadcasted_iota(jnp.int32, (n, 128), 1)
    lse = jnp.zeros((n, 128), F32)
    for h in heads:
        lse = jnp.where(lane == h, m[h] + jnp.log(l[h]), lse)
    o_ref[...] = jnp.concatenate([o[h] / l[h] for h in heads], axis=-1).reshape(o_ref.shape).astype(o_ref.dtype)
    lse_ref[...] = lse.reshape(lse_ref.shape)
    if has_prev:
        kp_ref[...] = kc_ref[...]
        vp_ref[...] = vc_ref[...]


def _band_attn_kernel(*refs, blocks_per_stream, prescaled):
    refs = list(refs)
    k = pl.program_id(1)
    groups = range(len(blocks_per_stream))
    ins = []
    for g in groups:
        n_in = 5 if blocks_per_stream[g] > 1 else 4
        ins.append(refs[:n_in])
        refs = refs[n_in:]
    outs = [refs[2 * g:2 * g + 2] for g in groups]
    refs = refs[2 * len(blocks_per_stream):]
    for g in groups:
        if blocks_per_stream[g] > 1:
            bc_ref, bp_ref, q_ref, kc_ref, vc_ref = ins[g]
            kp_ref, vp_ref = refs[:2]
            refs = refs[2:]

            @pl.when(k % blocks_per_stream[g] == 0)
            def _(kp_ref=kp_ref, vp_ref=vp_ref):
                kp_ref[...] = jnp.zeros_like(kp_ref)
                vp_ref[...] = jnp.zeros_like(vp_ref)
        else:
            (bc_ref, q_ref, kc_ref, vc_ref), bp_ref, kp_ref, vp_ref = ins[g], None, None, None
        _band_block(bc_ref, bp_ref, q_ref, kc_ref, vc_ref, outs[g][0], outs[g][1], kp_ref, vp_ref, prescaled[g])


def _band_bias(group):
    steps = _block_steps(group)
    back = (steps[:, None] - steps[None, :]).astype(np.float32)
    coef = (_alibi_slopes()[group] * np.float32(DIL_RATES[group]))[:, None, None]
    cur = np.where(back >= 0, -(coef * back), np.float32(MASK_VALUE)).astype(np.float32)
    back_p = back + np.float32(ATT_BLOCK)
    prev = np.where(back_p <= ATT_BLOCK, -(coef * back_p), np.float32(MASK_VALUE)).astype(np.float32)
    return cur, np.stack([np.full_like(prev, MASK_VALUE), prev])


def _band_attention(srcs, col_q, prescaled, batch):
    n_slabs = srcs[0].shape[0]
    table = (ATT_HEADS, ATT_BLOCK, ATT_BLOCK)
    in_specs, args, out_specs, out_shape, scratch, blocks_per_stream = [], [], [], [], [], []
    for g in range(N_DIL):
        slabs, rows = BLOCK_SHAPES[g]
        streams = SLABS // slabs
        nb = SLAB_ROWS // rows
        blocks_per_stream.append(nb)
        bias_c, bias_p = _band_bias(g)

        def cur(col, width=D_MODEL, slabs=slabs, rows=rows, streams=streams, nb=nb):
            return pl.BlockSpec((slabs, rows, width), lambda b, k: (b * streams + k // nb, k % nb, col))

        in_specs.append(pl.BlockSpec(table, lambda b, k: (0, 0, 0)))
        args.append(jnp.asarray(bias_c))
        if nb > 1:
            in_specs.append(pl.BlockSpec((None,) + table, lambda b, k, nb=nb: (jnp.minimum(k % nb, 1), 0, 0, 0)))
            args.append(jnp.asarray(bias_p))
            scratch += [pltpu.VMEM((slabs, rows, D_MODEL), srcs[g].dtype)] * 2
        in_specs += [cur(col_q[g]), cur(col_q[g] + 1), cur(col_q[g] + 2)]
        args += [srcs[g]] * 3
        out_specs += [cur(0), cur(0, 128)]
        out_shape += [jax.ShapeDtypeStruct((n_slabs, SLAB_ROWS, D_MODEL), BF16 if prescaled[g] else F32),
                      jax.ShapeDtypeStruct((n_slabs, SLAB_ROWS, 128), F32)]
    res = pl.pallas_call(
        functools.partial(_band_attn_kernel, blocks_per_stream=tuple(blocks_per_stream), prescaled=tuple(prescaled)),
        grid=(batch, SLABS),
        in_specs=in_specs,
        out_specs=out_specs,
        out_shape=out_shape,
        scratch_shapes=scratch,
        compiler_params=_params("parallel", "arbitrary"),
    )(*args)
    return [(res[2 * g], res[2 * g + 1]) for g in range(N_DIL)]


def _cached_attn_kernel(new_ref, c0_ref, c1_ref, c2_ref, coef_ref, o_ref, lse_ref, *, dec_seq):
    caches = (c0_ref, c1_ref, c2_ref)
    shared = [d == 1 for d in DIL_RATES]
    pairs = [(g, i) for g in range(N_DIL) for i in range(dec_seq)]
    res = lambda g, i: 0 if shared[g] else i
    lane = lax.broadcasted_iota(jnp.int32, (ATT_HEADS, ATT_BLOCK), 1)
    scale = HEAD_DIM ** -0.5
    q = {(g, i): new_ref[i, g, 0] for g, i in pairs}

    def qk_body(r, carry):
        out = []
        for n, (g, i) in enumerate(pairs):
            col = jnp.sum(q[g, i] * caches[g][r, res(g, i), 0], axis=-1, keepdims=True)
            out.append(jnp.where(lane == r, col, carry[n]))
        return tuple(out)

    zero = jnp.zeros((ATT_HEADS, ATT_BLOCK), F32)
    s_all = lax.fori_loop(0, ATT_BLOCK, qk_body, (zero,) * len(pairs), unroll=32)

    ps, p_news, ls, ms = [], [], [], []
    for n, (g, i) in enumerate(pairs):
        coef = coef_ref[g]
        if shared[g]:
            back = (ATT_BLOCK + i - lane).astype(F32)
            s = jnp.where(lane >= i, s_all[n] * scale - coef * back, MASK_VALUE)
            s_new = [jnp.sum(q[g, i] * new_ref[i2, g, 1], axis=-1, keepdims=True) * scale
                     - coef[:, :1] * float(i - i2) for i2 in range(i + 1)]
        else:
            s = s_all[n] * scale - coef * (ATT_BLOCK - lane).astype(F32)
            s_new = [jnp.sum(q[g, i] * new_ref[i, g, 1], axis=-1, keepdims=True) * scale]
        m = jnp.max(s, axis=-1, keepdims=True)
        for sn in s_new:
            m = jnp.maximum(m, sn)
        p = jnp.exp(s - m)
        p_new = [jnp.exp(sn - m) for sn in s_new]
        l = jnp.sum(p, axis=-1, keepdims=True)
        for pn in p_new:
            l = l + pn
        ps.append(p)
        p_news.append(p_new)
        ls.append(l)
        ms.append(m)

    def pv_body(r, carry):
        out = []
        for n, (g, i) in enumerate(pairs):
            col = jnp.sum(jnp.where(lane == r, ps[n], 0.0), axis=-1, keepdims=True)
            out.append(carry[n] + col * caches[g][r, res(g, i), 1])
        return tuple(out)

    zero_o = jnp.zeros((ATT_HEADS, HEAD_DIM), F32)
    o_all = lax.fori_loop(0, ATT_BLOCK, pv_body, (zero_o,) * len(pairs), unroll=32)
    for n, (g, i) in enumerate(pairs):
        o = o_all[n]
        for i2, pn in enumerate(p_news[n]):
            o = o + pn * new_ref[i2 if shared[g] else i, g, 2]
        o_ref[g, i] = o / ls[n]
        lse_ref[g, i] = jnp.broadcast_to(ms[n] + jnp.log(ls[n]), (ATT_HEADS, 128))


def _cached_attention(att_s, caches):
    db, dec_seq = att_s.shape[:2]
    views, specs = [], []
    for g, cache in enumerate(caches):
        dil, window = DIL_RATES[g], DIL_WINDOWS[g]
        assert cache.shape[1] == window and window == dil * ATT_BLOCK
        assert dil == 1 or dec_seq <= dil
        nres = 1 if dil == 1 else dec_seq
        views.append(cache.reshape(db, ATT_BLOCK, dil, 2, ATT_HEADS, HEAD_DIM))
        specs.append(pl.BlockSpec((None, ATT_BLOCK, nres, 2, ATT_HEADS, HEAD_DIM), lambda b: (b, 0, 0, 0, 0, 0)))
    coef = np.stack([np.repeat((_alibi_slopes()[g] * DIL_RATES[g])[:, None], 128, axis=1) for g in range(N_DIL)])
    out = lambda w: pl.BlockSpec((N_DIL, None, dec_seq, ATT_HEADS, w), lambda b: (0, b, 0, 0, 0))
    return pl.pallas_call(
        functools.partial(_cached_attn_kernel, dec_seq=dec_seq),
        grid=(db,),
        in_specs=[pl.BlockSpec((None, dec_seq, N_DIL, 3, ATT_HEADS, HEAD_DIM), lambda b: (b, 0, 0, 0, 0, 0))]
        + specs + [pl.BlockSpec((N_DIL, ATT_HEADS, 128), lambda b: (0, 0, 0))],
        out_specs=[out(HEAD_DIM), out(128)],
        out_shape=[jax.ShapeDtypeStruct((N_DIL, db, dec_seq, ATT_HEADS, HEAD_DIM), F32),
                   jax.ShapeDtypeStruct((N_DIL, db, dec_seq, ATT_HEADS, 128), F32)],
        compiler_params=_params("parallel"),
    )(att_s, *views, jnp.asarray(coef.astype(np.float32)))


def _merge_kernel(a_ref, ga_ref, gb_ref, o0_ref, o1_ref, o2_ref, l0_ref, l1_ref, l2_ref,
                  x_ref, w_ref, g_ref, h_ref, hn_ref, merged_ref):
    l0, l1, l2 = l0_ref[...], l1_ref[...], l2_ref[...]
    mx = jnp.maximum(jnp.maximum(l0, l1), l2)
    e0, e1, e2 = jnp.exp(l0 - mx), jnp.exp(l1 - mx), jnp.exp(l2 - mx)
    den = e0 + e1 + e2
    w0, w1, w2 = e0 / den, e1 / den, e2 / den
    for h in range(ATT_HEADS):
        hs = slice(h * HEAD_DIM, (h + 1) * HEAD_DIM)
        b_out = (w0[:, h:h + 1] * o0_ref[:, hs] + w1[:, h:h + 1] * o1_ref[:, hs].astype(F32)
                 + w2[:, h:h + 1] * o2_ref[:, hs].astype(F32))
        merged = ga_ref[:, hs].astype(F32) * a_ref[:, hs] + gb_ref[:, hs].astype(F32) * b_out
        merged_ref[:, hs] = merged.astype(merged_ref.dtype)
    h_new = x_ref[...] + jnp.dot(merged_ref[...], w_ref[...], preferred_element_type=F32)
    h_ref[...] = h_new
    hn_ref[...] = _rms(h_new, g_ref[...]).astype(hn_ref.dtype)


def _merge(a_out, gates, gate_col, outs, lses, x, w_out, norm_g, t, tm):
    row = lambda col: pl.BlockSpec((tm, D_MODEL), lambda i: (i, col))
    lrow = pl.BlockSpec((tm, 128), lambda i: (i, 0))
    p = gates
    return pl.pallas_call(
        _merge_kernel,
        grid=(t // tm,),
        in_specs=[row(0), row(gate_col), row(gate_col + 1), row(0), row(0), row(0), lrow, lrow, lrow,
                  row(0), pl.BlockSpec((D_MODEL, D_MODEL), lambda i: (0, 0)),
                  pl.BlockSpec((1, D_MODEL), lambda i: (0, 0))],
        out_specs=[row(0), row(0)],
        out_shape=[jax.ShapeDtypeStruct((x.shape[0], D_MODEL), F32), jax.ShapeDtypeStruct((t, D_MODEL), BF16)],
        scratch_shapes=[pltpu.VMEM((tm, D_MODEL), BF16)],
        compiler_params=_params("parallel"),
    )(a_out, p, p, outs[0], outs[1], outs[2], lses[0], lses[1], lses[2], x, w_out, norm_g.reshape(1, D_MODEL))


def _peer_score_kernel(hn_ref, wq_ref, keys_ref, s_ref):
    q = jnp.dot(hn_ref[...], wq_ref[...], preferred_element_type=F32).astype(BF16)
    nt = (((1,), (1,)), ((), ()))
    for hc in range(2 * PEER_HEADS):
        cs = slice(hc * PEER_HALF, (hc + 1) * PEER_HALF)
        s_ref[hc] = lax.dot_general(keys_ref[hc % 2], q[:, cs], nt, preferred_element_type=F32)


def _peer_scores(hn, w_q, sub_keys, tm):
    t = hn.shape[0]
    return pl.pallas_call(
        _peer_score_kernel,
        grid=(t // tm,),
        in_specs=[pl.BlockSpec((tm, D_MODEL), lambda i: (i, 0)),
                  pl.BlockSpec((D_MODEL, 2 * PEER_HEADS * PEER_HALF), lambda i: (0, 0)),
                  pl.BlockSpec((2, N_KEYS, PEER_HALF), lambda i: (0, 0, 0))],
        out_specs=pl.BlockSpec((2 * PEER_HEADS, N_KEYS, tm), lambda i: (0, 0, i)),
        out_shape=jax.ShapeDtypeStruct((2 * PEER_HEADS, N_KEYS, t), F32),
        compiler_params=_params("parallel"),
    )(hn, w_q, sub_keys)


def _take_top(arrays, order, count, sentinel):
    arrays = list(arrays)
    vals = [[] for _ in arrays]
    idxs = [[] for _ in arrays]
    for _ in range(count):
        for n, s in enumerate(arrays):
            m = jnp.max(s, axis=0, keepdims=True)
            pos = jnp.min(jnp.where(s == m, order, sentinel), axis=0, keepdims=True)
            vals[n].append(m)
            idxs[n].append(pos)
            arrays[n] = jnp.where(order == pos, NEG_INF, s)
    return vals, idxs


def _sort_network(n):
    pairs, p = [], 1
    while p < n:
        k = p
        while k >= 1:
            for j in range(k % p, n - k, 2 * k):
                for i in range(min(k, n - j - k)):
                    if (i + j) // (2 * p) == (i + j + k) // (2 * p):
                        pairs.append((i + j, i + j + k))
            k //= 2
        p *= 2
    return pairs


def _take_top_sorted(arrays, order, count, sentinel):
    tiles = arrays[0].shape[0] // 8
    assert all(a.shape[0] == 8 * tiles for a in arrays)
    val = [[a[8 * v:8 * v + 8] for v in range(tiles)] for a in arrays]
    idx = [[order[8 * v:8 * v + 8] for v in range(tiles)] for _ in arrays]
    network = [(i, j) for i, j in _sort_network(1 << (tiles - 1).bit_length()) if j < tiles]
    for i, j in network:
        for n in range(len(arrays)):
            vi, vj, ri, rj = val[n][i], val[n][j], idx[n][i], idx[n][j]
            swap = (vj > vi) | ((vj == vi) & (rj < ri))
            val[n][i], val[n][j] = jnp.where(swap, vj, vi), jnp.where(swap, vi, vj)
            idx[n][i], idx[n][j] = jnp.where(swap, rj, ri), jnp.where(swap, ri, rj)
    vals = [[] for _ in arrays]
    idxs = [[] for _ in arrays]
    for t in range(count):
        for n in range(len(arrays)):
            head, rank = val[n][0], idx[n][0]
            m = jnp.max(head, axis=0, keepdims=True)
            pos = jnp.min(jnp.where(head == m, rank, sentinel), axis=0, keepdims=True)
            vals[n].append(m)
            idxs[n].append(pos)
            popped = rank == pos
            for d in range(min(tiles, count - t - 1)):
                if d + 1 < tiles:
                    val[n][d] = jnp.where(popped, val[n][d + 1], val[n][d])
                    idx[n][d] = jnp.where(popped, idx[n][d + 1], idx[n][d])
                else:
                    val[n][d] = jnp.where(popped, NEG_INF, val[n][d])
                    idx[n][d] = jnp.where(popped, sentinel, idx[n][d])
    return vals, idxs


def _stack_rows(rows_list, krow):
    out = jnp.zeros(krow.shape, F32)
    for j, r in enumerate(rows_list):
        out = jnp.where(krow == float(j), r, out)
    return out


def _cand_layout():
    k = PEER_TOPK
    pieces = [("row_a", a, 16 if a == 0 else 8, 0, k // (a + 1)) for a in range(4)]
    pieces += [("col_b", 0, 16, 4, 16), ("col_b", 1, 8, 4, 8), ("col_b", 2, 8, 4, 5)]
    pos = []
    for kind, idx, rows, lo, hi in pieces:
        for r in range(rows):
            a, b = (idx, r) if kind == "row_a" else (r, idx)
            ok = lo <= r < hi and (a + 1) * (b + 1) <= k
            pos.append(a * k + b if ok else k * k)
    assert sorted(p for p in pos if p < k * k) == sorted(
        a * k + b for a in range(k) for b in range(k) if (a + 1) * (b + 1) <= k)
    return pieces, np.asarray(pos, np.float32)


def _route_kernel(s_ref, pos_ref, u_ref, v_ref, e1_ref, e2_ref, gate_ref, ub_ref, vb_ref):
    ub_ref[...] = u_ref[...].astype(ub_ref.dtype)
    vb_ref[...] = v_ref[...].astype(vb_ref.dtype)
    k = PEER_TOPK
    heads = s_ref.shape[0] // 2
    lanes = s_ref.shape[2]
    key_rank = lax.broadcasted_iota(jnp.int32, (N_KEYS, lanes), 0).astype(F32)
    vals, idxs = _take_top_sorted([s_ref[n] for n in range(2 * heads)], key_rank, k, float(N_KEYS))
    krow = lax.broadcasted_iota(jnp.int32, (k, lanes), 0).astype(F32)
    flat = pos_ref[...]
    cands, i1_all, i2_all = [], [], []
    for hd in range(heads):
        v1, v2 = vals[2 * hd], vals[2 * hd + 1]
        v1_all = _stack_rows(v1, krow)
        v2_all = _stack_rows(v2, krow)
        i1_all.append(_stack_rows(idxs[2 * hd], krow))
        i2_all.append(_stack_rows(idxs[2 * hd + 1], krow))
        parts = [v1[idx] + v2_all[:rows] if kind == "row_a" else v1_all[:rows] + v2[idx]
                 for kind, idx, rows, _, _ in _cand_layout()[0]]
        cands.append(jnp.where(flat < float(k * k), jnp.concatenate(parts, axis=0), NEG_INF))
    top_s, pos = _take_top_sorted(cands, flat, k, float(k * k))
    for hd in range(heads):
        e1, e2 = [], []
        for j in range(k):
            a = jnp.floor(pos[hd][j] * (1.0 / k))
            b = pos[hd][j] - a * k
            e1.append(jnp.sum(jnp.where(krow == a, i1_all[hd], 0.0), axis=0, keepdims=True))
            e2.append(jnp.sum(jnp.where(krow == b, i2_all[hd], 0.0), axis=0, keepdims=True))
        ex = jnp.exp(_stack_rows(top_s[hd], krow) - top_s[hd][0])
        rows = slice(hd * k, (hd + 1) * k)
        gate_ref[rows, :] = ex / jnp.sum(ex, axis=0, keepdims=True)
        e1_ref[rows, :] = _stack_rows(e1, krow)
        e2_ref[rows, :] = _stack_rows(e2, krow)


def _route(scores_t, tl, heads_per_step, peer_u, peer_v):
    t = scores_t.shape[2]
    kk = PEER_HEADS * PEER_TOPK
    n_head_steps = PEER_HEADS // heads_per_step
    steps = (t // tl) * n_head_steps
    n_blocks = 1 << (steps.bit_length() - 1)
    tab_rows = N_EXPERTS // n_blocks
    tab = pl.BlockSpec((tab_rows, D_MODEL), lambda i, h: (jnp.minimum(i * n_head_steps + h, n_blocks - 1), 0))
    out = pl.BlockSpec((heads_per_step * PEER_TOPK, tl), lambda i, h: (h, i))
    flat = jnp.asarray(np.repeat(_cand_layout()[1][:, None], tl, axis=1))
    return pl.pallas_call(
        _route_kernel,
        grid=(t // tl, n_head_steps),
        in_specs=[pl.BlockSpec((2 * heads_per_step, N_KEYS, tl), lambda i, h: (h, 0, i)),
                  pl.BlockSpec(flat.shape, lambda i, h: (0, 0)), tab, tab],
        out_specs=[out, out, out, tab, tab],
        out_shape=[jax.ShapeDtypeStruct((kk, t), F32)] * 3 + [jax.ShapeDtypeStruct((N_EXPERTS, D_MODEL), BF16)] * 2,
        compiler_params=_params("arbitrary", "arbitrary"),
    )(scores_t, flat, peer_u, peer_v)


def _expert_weight_kernel(e1_ref, e2_ref, gate_ref, g_ref, e1t_ref, e2t_ref, gt_ref):
    e1t_ref[...] = e1_ref[...].T
    e2t_ref[...] = e2_ref[...].T
    gt_ref[...] = gate_ref[...].T
    kk = e1_ref.shape[0]
    key = lax.broadcasted_iota(jnp.int32, (N_KEYS, kk), 0).astype(F32)
    nt = (((1,), (1,)), ((), ()))

    def body(t, carry):
        r1 = e1t_ref[pl.ds(t, 1), :]
        r2 = e2t_ref[pl.ds(t, 1), :]
        gr = gt_ref[pl.ds(t, 1), :]
        a_t = jnp.where(key == r1, 1.0, 0.0).astype(BF16)
        b_t = jnp.where(key == r2, gr, 0.0).astype(BF16)
        g_ref[t] = lax.dot_general(a_t, b_t, nt, preferred_element_type=F32).astype(g_ref.dtype)
        return carry

    lax.fori_loop(0, g_ref.shape[0], body, 0, unroll=32)


def _expert_weights(e1, e2, gate, tl):
    kk, t = e1.shape
    slot = pl.BlockSpec((kk, tl), lambda i: (0, i))
    return pl.pallas_call(
        _expert_weight_kernel,
        grid=(t // tl,),
        in_specs=[slot, slot, slot],
        out_specs=pl.BlockSpec((tl, N_KEYS, N_KEYS), lambda i: (i, 0, 0)),
        out_shape=jax.ShapeDtypeStruct((t, N_KEYS, N_KEYS), F32),
        scratch_shapes=[pltpu.VMEM((tl, kk), F32)] * 3,
        compiler_params=_params("parallel"),
    )(e1, e2, gate)


def _peer_kernel(hn_ref, u_ref, v_ref, g_ref, y_ref):
    e = pl.program_id(1)
    nt = (((1,), (1,)), ((), ()))
    hk = lax.dot_general(hn_ref[...], u_ref[...], nt, preferred_element_type=F32)
    g = jnp.swapaxes(g_ref[...], 0, 1)
    act = [(jax.nn.gelu(hk[:, a * N_KEYS:(a + 1) * N_KEYS]) * g[a]).astype(BF16) for a in range(g.shape[0])]
    out = jnp.dot(jnp.concatenate(act, axis=-1), v_ref[...], preferred_element_type=F32)

    @pl.when(e == 0)
    def _():
        y_ref[...] = out

    @pl.when(e > 0)
    def _():
        y_ref[...] += out


def _peer(hn, u, v, g, t, rows_out, tm, te):
    row = pl.BlockSpec((tm, D_MODEL), lambda i, e: (i, 0), pipeline_mode=pl.Buffered(1))
    tab = pl.BlockSpec((te, D_MODEL), lambda i, e: (e, 0))
    return pl.pallas_call(
        _peer_kernel,
        grid=(t // tm, N_EXPERTS // te),
        in_specs=[row, tab, tab, pl.BlockSpec((tm, te // N_KEYS, N_KEYS), lambda i, e: (i, e, 0))],
        out_specs=row,
        out_shape=jax.ShapeDtypeStruct((rows_out, D_MODEL), F32),
        compiler_params=_params("parallel", "arbitrary"),
    )(hn, u, v, g)


def _final_norm_prompt_kernel(h_ref, f_ref, g_ref, y_ref):
    y = [_rms(h_ref[rr % 4, rr // 4] + f_ref[rr % 4, rr // 4], g_ref[...]) for rr in range(8)]
    y_ref[...] = jnp.swapaxes(jnp.stack(y, axis=0), 0, 1)


def _final_norm_prompt(h, f, g, batch, seq):
    rows = 64
    h5 = h.reshape(batch + 1, 4, 4, SLAB_ROWS, D_MODEL)
    f5 = f.reshape(batch + 1, 4, 4, SLAB_ROWS, D_MODEL)
    blk = pl.BlockSpec((None, 4, 2, rows, D_MODEL), lambda b, hh, j: (b, 0, hh, j, 0))
    y = pl.pallas_call(
        _final_norm_prompt_kernel,
        grid=(batch, 2, SLAB_ROWS // rows),
        in_specs=[blk, blk, pl.BlockSpec((1, D_MODEL), lambda b, hh, j: (0, 0))],
        out_specs=pl.BlockSpec((None, rows, 8, D_MODEL), lambda b, hh, j: (b, j, hh, 0)),
        out_shape=jax.ShapeDtypeStruct((batch, SLAB_ROWS, SLABS, D_MODEL), F32),
        compiler_params=_params("parallel", "parallel", "parallel"),
    )(h5, f5, g.reshape(1, D_MODEL))
    return y.reshape(batch, seq, D_MODEL)


def _final_norm_rows_kernel(h_ref, f_ref, g_ref, y_ref):
    y_ref[...] = _rms(h_ref[...] + f_ref[...], g_ref[...])


def _final_norm_rows(h, f, g, row_block, rows):
    blk = pl.BlockSpec((rows, D_MODEL), lambda i: (row_block, 0))
    return pl.pallas_call(
        _final_norm_rows_kernel,
        grid=(1,),
        in_specs=[blk, blk, pl.BlockSpec((1, D_MODEL), lambda i: (0, 0))],
        out_specs=pl.BlockSpec((rows, D_MODEL), lambda i: (0, 0)),
        out_shape=jax.ShapeDtypeStruct((rows, D_MODEL), F32),
        compiler_params=_params("arbitrary"),
    )(h, f, g.reshape(1, D_MODEL))


def _kv_prompt_kernel(*refs):
    ins, outs = refs[:2 * N_DIL], refs[2 * N_DIL:]
    for g in range(N_DIL):
        for kv in range(2):
            src = ins[2 * g + kv]
            heads = [src[0, :, h * HEAD_DIM:(h + 1) * HEAD_DIM] for h in range(ATT_HEADS)]
            outs[g][:, kv, :, :] = jnp.swapaxes(jnp.stack(heads, axis=0), 0, 1)


def _kv_prompt(p3, batch, seq):
    in_specs, out_specs, out_shape = [], [], []
    residue = lambda s: (s % 4) * 4 + s // 4
    for g in range(N_DIL):
        steps = min(DIL_WINDOWS[g], seq) // SLABS
        last = SLAB_ROWS // steps - 1
        for kv in range(2):
            col = COL_ATT + 3 * g + 1 + kv
            in_specs.append(pl.BlockSpec((1, steps, D_MODEL),
                                         lambda b, s, last=last, col=col: (b * SLABS + s, last, col)))
        out_specs.append(pl.BlockSpec((None, steps, None, 2, ATT_HEADS, HEAD_DIM),
                                      lambda b, s: (b, 0, residue(s), 0, 0, 0)))
        out_shape.append(jax.ShapeDtypeStruct((batch, steps, SLABS, 2, ATT_HEADS, HEAD_DIM), F32))
    outs = pl.pallas_call(
        _kv_prompt_kernel,
        grid=(batch, SLABS),
        in_specs=in_specs,
        out_specs=out_specs,
        out_shape=out_shape,
        compiler_params=_params("parallel", "parallel"),
    )(*([p3] * len(in_specs)))
    return [o.reshape(1, batch, -1, 2, ATT_HEADS, HEAD_DIM) for o in outs]


def _row_tile(t, candidates):
    for c in candidates:
        if t % c == 0:
            return c
    raise ValueError(f"no row tile for {t} tokens")


def kernel(x_prompt, x_sample, cache_kv_w128, cache_kv_w512, cache_kv_w2048, norm_mix_g, w_in, sgu_norm_g, sgu_w, sgu_b, w_out, norm_ffn_g, peer_w_q, peer_sub_keys, peer_u, peer_v, norm_final_g):
    batch, seq, _ = x_prompt.shape
    db, ds, _ = x_sample.shape
    assert w_in.shape[0] == 1 and db * ds == CHUNK and seq == SLABS * SLAB_ROWS
    caches = (cache_kv_w128, cache_kv_w512, cache_kv_w2048)
    tp, ts = batch * seq, db * ds
    t = tp + ts
    n_slabs = t // SLAB_ROWS
    tm_big = _row_tile(t, (1040, 640, 128))
    tm_lane = _row_tile(t, (640, 128))

    xs = x_sample.reshape(ts, D_MODEL)
    xl, xn = _permute_norm(x_prompt, norm_mix_g[0])
    xl = lax.dynamic_update_slice(xl, xs, (tp, 0))
    xn = lax.dynamic_update_slice(xn, _rmsnorm_bf16(xs, norm_mix_g[0]), (tp, 0))

    p, att16 = _inproj(xn, w_in[0], t, tm_big, 1024)
    p3 = p.reshape(n_slabs, SLAB_ROWS, p.shape[1])
    att16_3 = att16.reshape(n_slabs, SLAB_ROWS, att16.shape[1])

    w_tril = sgu_w[0] * jnp.tril(jnp.ones((CHUNK, CHUNK), F32))
    tau = _block_steps(0)
    w_p = w_tril[:, tau][:, :, tau].astype(BF16)
    b_p = sgu_b[0][:, tau][..., None]
    w_s = jnp.einsum("bc,gis->gbics", jnp.eye(db, dtype=F32), w_tril[:, :ds, :ds]).reshape(SGU_GROUPS, ts, ts)
    b_s = jnp.tile(sgu_b[0][:, :ds], (1, db))[..., None]
    a_out = _sgu_prompt(p3, sgu_norm_g[0], w_p, b_p, batch).reshape(t, D_MODEL)
    a_s, vn_s = _sgu_sample(p3, sgu_norm_g[0], w_s.astype(BF16), b_s, tp // SLAB_ROWS)
    a_out = lax.dynamic_update_slice(a_out, a_s, (tp, 0))

    att_s = p[tp:, COL_ATT * D_MODEL:COL_GATE * D_MODEL].reshape(db, ds, N_DIL, 3, ATT_HEADS, HEAD_DIM)
    outs, lses = [], []
    o_s, lse_s = _cached_attention(att_s, [c[0] for c in caches])
    band = _band_attention([p3, att16_3, att16_3], [COL_ATT, 0, 3], [False, True, True], batch)
    for g in range(N_DIL):
        o_p, lse_p = band[g]
        lse_g = jnp.pad(lse_s[g, ..., 0].reshape(ts, ATT_HEADS), ((0, 0), (0, 128 - ATT_HEADS)))
        o_g = o_s[g].reshape(ts, D_MODEL).astype(o_p.dtype)
        outs.append(lax.dynamic_update_slice(o_p.reshape(t, D_MODEL), o_g, (tp, 0)))
        lses.append(lax.dynamic_update_slice(lse_p.reshape(t, 128), lse_g, (tp, 0)))

    h, hn = _merge(a_out, att16, COL_GATE - COL_ATT16, outs, lses, xl, _to_bf16(w_out[0], 1024, D_MODEL),
                   norm_ffn_g[0], t, _row_tile(t, (320, 128)))

    scores_t = _peer_scores(hn, _to_bf16(peer_w_q[0], 1024, D_MODEL), peer_sub_keys[0].astype(BF16), tm_lane)
    e1, e2, gate, u_b16, v_b16 = _route(scores_t, 128, 4, peer_u[0], peer_v[0])
    g_dense = _expert_weights(e1, e2, gate, 128)
    f = _peer(hn, u_b16, v_b16, g_dense, t, h.shape[0], tm_big, 1024)

    y_prompt = _final_norm_prompt(h, f, norm_final_g, batch, seq)
    y_sample = _final_norm_rows(h, f, norm_final_g, tp // ts, ts).reshape(db, ds, D_MODEL)
    kv_prompt = _kv_prompt(p3, batch, seq)
    kv_sample = [att_s[:, :, g, 1:3][None] for g in range(N_DIL)]
    sgu_v_sample = vn_s.reshape(1, db, ds, D_MODEL)
    return (y_prompt, y_sample, kv_prompt[0], kv_prompt[1], kv_prompt[2],
            kv_sample[0], kv_sample[1], kv_sample[2], sgu_v_sample)
```

```python
import functools

import numpy as np
import jax
import jax.numpy as jnp
from jax import lax
from jax.experimental import pallas as pl
from jax.experimental.pallas import tpu as pltpu

F32 = jnp.float32
BF16 = jnp.bfloat16

D_MODEL = 2048
HEAD_DIM = 128
ATT_HEADS = D_MODEL // HEAD_DIM
N_DIL = 3
DIL_WINDOWS = (128, 512, 2048)
DIL_RATES = (1, 4, 16)
ATT_BLOCK = 128
SLABS = DIL_RATES[-1]
SLAB_ROWS = 128
SGU_GROUPS = 8
SGU_GROUP_DIM = D_MODEL // SGU_GROUPS
CHUNK = 128
N_COL_BLOCKS = 13
COL_U, COL_V, COL_ATT, COL_GATE = 0, 1, 2, 11
COL_ATT16 = COL_ATT + 3
PEER_HEADS = 8
PEER_TOPK = 16
N_KEYS = 128
N_EXPERTS = N_KEYS * N_KEYS
PEER_HALF = 128
NORM_EPS = 1e-6
MASK_VALUE = -1e30
NEG_INF = float("-inf")
VMEM_LIMIT = 56 * 1024 * 1024

BLOCK_SHAPES = ((16, 8), (4, 32), (1, 128))


def _alibi_slopes():
    n = N_DIL * ATT_HEADS
    e = np.arange(1, n + 1, dtype=np.float32)
    return np.exp2(np.float32(-8.0) * e / np.float32(n)).astype(np.float32).reshape(N_DIL, ATT_HEADS)


def _block_steps(group):
    slabs, rows = BLOCK_SHAPES[group]
    n = np.arange(slabs * rows)
    s, j = n // rows, n % rows
    if group == 0:
        return j * 16 + (s % 4) * 4 + s // 4
    if group == 1:
        return j * 4 + s
    return j


def _params(*sem):
    return pltpu.CompilerParams(dimension_semantics=sem, vmem_limit_bytes=VMEM_LIMIT)


def _rms(x, g):
    return x * lax.rsqrt(jnp.mean(x * x, axis=-1, keepdims=True) + NORM_EPS) * g


def _permute_norm_kernel(x_ref, g_ref, xl_ref, xn_ref):
    xs = jnp.swapaxes(x_ref[...], 0, 1)
    for rr in range(8):
        x = xs[rr]
        xl_ref[rr % 4, rr // 4] = x
        xn_ref[rr % 4, rr // 4] = _rms(x, g_ref[...]).astype(xn_ref.dtype)


def _permute_norm(x_prompt, g):
    batch, seq, _ = x_prompt.shape
    x4 = x_prompt.reshape(batch, SLAB_ROWS, SLABS, D_MODEL)
    out = pl.BlockSpec((None, 4, 2, SLAB_ROWS, D_MODEL), lambda b, h: (b, 0, h, 0, 0))
    xl, xn = pl.pallas_call(
        _permute_norm_kernel,
        grid=(batch, 2),
        in_specs=[pl.BlockSpec((None, SLAB_ROWS, 8, D_MODEL), lambda b, h: (b, 0, h, 0)),
                  pl.BlockSpec((1, D_MODEL), lambda b, h: (0, 0))],
        out_specs=[out, out],
        out_shape=[jax.ShapeDtypeStruct((batch + 1, 4, 4, SLAB_ROWS, D_MODEL), F32),
                   jax.ShapeDtypeStruct((batch + 1, 4, 4, SLAB_ROWS, D_MODEL), BF16)],
        compiler_params=_params("parallel", "parallel"),
    )(x4, g.reshape(1, D_MODEL))
    rows = (batch + 1) * seq
    return xl.reshape(rows, D_MODEL), xn.reshape(rows, D_MODEL)


def _rmsnorm_kernel(x_ref, g_ref, o_ref):
    o_ref[...] = _rms(x_ref[...], g_ref[...]).astype(o_ref.dtype)


def _rmsnorm_bf16(x, g):
    return pl.pallas_call(
        _rmsnorm_kernel,
        out_shape=jax.ShapeDtypeStruct(x.shape, BF16),
    )(x, g.reshape(1, D_MODEL))


def _cast_kernel(x_ref, o_ref):
    o_ref[...] = x_ref[...].astype(o_ref.dtype)


def _to_bf16(w, rows, cols):
    r, c = w.shape
    return pl.pallas_call(
        _cast_kernel,
        grid=(r // rows, c // cols),
        in_specs=[pl.BlockSpec((rows, cols), lambda i, j: (i, j))],
        out_specs=pl.BlockSpec((rows, cols), lambda i, j: (i, j)),
        out_shape=jax.ShapeDtypeStruct((r, c), BF16),
        compiler_params=_params("parallel", "parallel"),
    )(w)


def _inproj_kernel(x_ref, w_ref, o_ref, att_ref, w16_ref, *, blocks_per_col):
    cb = pl.program_id(0) // blocks_per_col

    @pl.when(pl.program_id(1) == 0)
    def _():
        w16_ref[...] = w_ref[...].astype(w16_ref.dtype)

    p = jnp.dot(x_ref[...], w16_ref[...], preferred_element_type=F32)

    @pl.when(cb < COL_ATT)
    def _():
        o_ref[...] = jax.nn.gelu(p)

    @pl.when((cb >= COL_ATT) & (cb < COL_ATT16))
    def _():
        o_ref[...] = p

    @pl.when((cb >= COL_ATT16) & (cb < COL_GATE))
    def _():
        o_ref[...] = p
        is_q = (cb - COL_ATT) % 3 == 0
        att_ref[...] = (p * jnp.where(is_q, HEAD_DIM ** -0.5, 1.0)).astype(att_ref.dtype)

    @pl.when(cb >= COL_GATE)
    def _():
        att_ref[...] = jax.nn.sigmoid(p).astype(att_ref.dtype)


def _inproj(xn, w, t, tm, tn):
    n = w.shape[1]
    per_col = D_MODEL // tn
    n_rows = t // tm
    n_f32 = COL_GATE * per_col
    first_b16 = COL_ATT16 * per_col
    n_b16 = (N_COL_BLOCKS - COL_ATT16) * per_col
    f32_block = lambda j, i: (jnp.where(j < n_f32, i, n_rows - 1), jnp.minimum(j, n_f32 - 1))
    b16_block = lambda j, i: (jnp.where(j >= first_b16, i, 0), jnp.maximum(j - first_b16, 0))
    return pl.pallas_call(
        functools.partial(_inproj_kernel, blocks_per_col=per_col),
        grid=(n // tn, n_rows),
        in_specs=[pl.BlockSpec((tm, D_MODEL), lambda j, i: (i, 0)),
                  pl.BlockSpec((D_MODEL, tn), lambda j, i: (0, j))],
        out_specs=[pl.BlockSpec((tm, tn), f32_block),
                   pl.BlockSpec((tm, tn), b16_block)],
        out_shape=[jax.ShapeDtypeStruct((t, n_f32 * tn), F32),
                   jax.ShapeDtypeStruct((t, n_b16 * tn), BF16)],
        scratch_shapes=[pltpu.VMEM((D_MODEL, tn), BF16)],
        compiler_params=_params("arbitrary", "arbitrary"),
    )(xn, w)


def _sgu_chunk(u, v, g_ref, w_ref, b_ref):
    vc = v - jnp.mean(v, axis=-1, keepdims=True)
    vn = vc * lax.rsqrt(jnp.mean(vc * vc, axis=-1, keepdims=True) + NORM_EPS) * g_ref[...]
    cols = []
    for g in range(SGU_GROUPS):
        cs = slice(g * SGU_GROUP_DIM, (g + 1) * SGU_GROUP_DIM)
        mix = jnp.dot(w_ref[g], vn[:, cs].astype(BF16), preferred_element_type=F32) + b_ref[g]
        cols.append(u[:, cs] * mix)
    return jnp.concatenate(cols, axis=-1), vn


def _sgu_kernel(u_ref, v_ref, g_ref, w_ref, b_ref, a_ref, vn_ref):
    a, vn = _sgu_chunk(u_ref[...].reshape(CHUNK, D_MODEL), v_ref[...].reshape(CHUNK, D_MODEL), g_ref, w_ref, b_ref)
    a_ref[...] = a.reshape(a_ref.shape)
    vn_ref[...] = vn


def _sgu_prompt_kernel(u_ref, v_ref, g_ref, w_ref, b_ref, a_ref, *, rows):
    for c in range(u_ref.shape[1] // rows):
        rs = slice(c * rows, (c + 1) * rows)
        a, _ = _sgu_chunk(u_ref[:, rs, :].reshape(CHUNK, D_MODEL), v_ref[:, rs, :].reshape(CHUNK, D_MODEL),
                          g_ref, w_ref, b_ref)
        a_ref[:, rs, :] = a.reshape(a_ref.shape[0], rows, D_MODEL)


def _sgu_prompt(p3, sgu_norm_g, w, b, batch, chunks_per_step=4):
    slabs, rows = BLOCK_SHAPES[0]
    n_chunks = SLAB_ROWS // (rows * chunks_per_step)
    blk = lambda col: pl.BlockSpec((slabs, rows * chunks_per_step, D_MODEL), lambda bb, c: (bb, c, col))
    return pl.pallas_call(
        functools.partial(_sgu_prompt_kernel, rows=rows),
        grid=(batch, n_chunks),
        in_specs=[blk(COL_U), blk(COL_V),
                  pl.BlockSpec((1, D_MODEL), lambda bb, c: (0, 0)),
                  pl.BlockSpec((SGU_GROUPS, CHUNK, CHUNK), lambda bb, c: (0, 0, 0)),
                  pl.BlockSpec((SGU_GROUPS, CHUNK, 1), lambda bb, c: (0, 0, 0))],
        out_specs=blk(0),
        out_shape=jax.ShapeDtypeStruct((p3.shape[0], SLAB_ROWS, D_MODEL), F32),
        compiler_params=_params("parallel", "parallel"),
    )(p3, p3, sgu_norm_g.reshape(1, D_MODEL), w, b)


def _sgu_sample(p3, sgu_norm_g, w, b, slab):
    blk = lambda col: pl.BlockSpec((1, SLAB_ROWS, D_MODEL), lambda i: (slab, 0, col))
    return pl.pallas_call(
        _sgu_kernel,
        grid=(1,),
        in_specs=[blk(COL_U), blk(COL_V),
                  pl.BlockSpec((1, D_MODEL), lambda i: (0, 0)),
                  pl.BlockSpec((SGU_GROUPS, CHUNK, CHUNK), lambda i: (0, 0, 0)),
                  pl.BlockSpec((SGU_GROUPS, CHUNK, 1), lambda i: (0, 0, 0))],
        out_specs=[pl.BlockSpec((CHUNK, D_MODEL), lambda i: (0, 0)),
                   pl.BlockSpec((CHUNK, D_MODEL), lambda i: (0, 0))],
        out_shape=[jax.ShapeDtypeStruct((CHUNK, D_MODEL), F32), jax.ShapeDtypeStruct((CHUNK, D_MODEL), F32)],
        compiler_params=_params("arbitrary"),
    )(p3, p3, sgu_norm_g.reshape(1, D_MODEL), w, b)


def _band_block(bc_ref, bp_ref, q_ref, kc_ref, vc_ref, o_ref, lse_ref, kp_ref, vp_ref, prescaled):
    has_prev = kp_ref is not None
    n = ATT_BLOCK
    scale = HEAD_DIM ** -0.5
    nt = (((1,), (1,)), ((), ()))
    heads = range(ATT_HEADS)
    hs = [slice(h * HEAD_DIM, (h + 1) * HEAD_DIM) for h in heads]
    ld = lambda ref, h: ref[:, :, hs[h]].reshape(n, HEAD_DIM)
    q = [ld(q_ref, h) if prescaled else (ld(q_ref, h) * scale).astype(BF16) for h in heads]
    s_c = [lax.dot_general(q[h], ld(kc_ref, h).astype(BF16), nt, preferred_element_type=F32) + bc_ref[h]
           for h in heads]
    m = [jnp.max(s_c[h], axis=-1, keepdims=True) for h in heads]
    if has_prev:
        s_p = [lax.dot_general(q[h], ld(kp_ref, h).astype(BF16), nt, preferred_element_type=F32) + bp_ref[h]
               for h in heads]
        m = [jnp.maximum(m[h], jnp.max(s_p[h], axis=-1, keepdims=True)) for h in heads]
    p_c = [jnp.exp(s_c[h] - m[h]) for h in heads]
    l = [jnp.sum(p_c[h], axis=-1, keepdims=True) for h in heads]
    o = [jnp.dot(p_c[h].astype(BF16), ld(vc_ref, h).astype(BF16), preferred_element_type=F32) for h in heads]
    if has_prev:
        p_p = [jnp.exp(s_p[h] - m[h]) for h in heads]
        l = [l[h] + jnp.sum(p_p[h], axis=-1, keepdims=True) for h in heads]
        o = [o[h] + jnp.dot(p_p[h].astype(BF16), ld(vp_ref, h).astype(BF16), preferred_element_type=F32)
             for h in heads]
    lane = lax.broadcasted_iota(jnp.int32, (n, 128), 1)
    lse = jnp.zeros((n, 128), F32)
    for h in heads:
        lse = jnp.where(lane == h, m[h] + jnp.log(l[h]), lse)
    o_ref[...] = jnp.concatenate([o[h] / l[h] for h in heads], axis=-1).reshape(o_ref.shape).astype(o_ref.dtype)
    lse_ref[...] = lse.reshape(lse_ref.shape)
    if has_prev:
        kp_ref[...] = kc_ref[...]
        vp_ref[...] = vc_ref[...]


def _band_attn_kernel(*refs, blocks_per_stream, prescaled):
    refs = list(refs)
    k = pl.program_id(1)
    groups = range(len(blocks_per_stream))
    ins = []
    for g in groups:
        n_in = 5 if blocks_per_stream[g] > 1 else 4
        ins.append(refs[:n_in])
        refs = refs[n_in:]
    outs = [refs[2 * g:2 * g + 2] for g in groups]
    refs = refs[2 * len(blocks_per_stream):]
    for g in groups:
        if blocks_per_stream[g] > 1:
            bc_ref, bp_ref, q_ref, kc_ref, vc_ref = ins[g]
            kp_ref, vp_ref = refs[:2]
            refs = refs[2:]

            @pl.when(k % blocks_per_stream[g] == 0)
            def _(kp_ref=kp_ref, vp_ref=vp_ref):
                kp_ref[...] = jnp.zeros_like(kp_ref)
                vp_ref[...] = jnp.zeros_like(vp_ref)
        else:
            (bc_ref, q_ref, kc_ref, vc_ref), bp_ref, kp_ref, vp_ref = ins[g], None, None, None
        _band_block(bc_ref, bp_ref, q_ref, kc_ref, vc_ref, outs[g][0], outs[g][1], kp_ref, vp_ref, prescaled[g])


def _band_bias(group):
    steps = _block_steps(group)
    back = (steps[:, None] - steps[None, :]).astype(np.float32)
    coef = (_alibi_slopes()[group] * np.float32(DIL_RATES[group]))[:, None, None]
    cur = np.where(back >= 0, -(coef * back), np.float32(MASK_VALUE)).astype(np.float32)
    back_p = back + np.float32(ATT_BLOCK)
    prev = np.where(back_p <= ATT_BLOCK, -(coef * back_p), np.float32(MASK_VALUE)).astype(np.float32)
    return cur, np.stack([np.full_like(prev, MASK_VALUE), prev])


def _band_attention(srcs, col_q, prescaled, batch):
    n_slabs = srcs[0].shape[0]
    table = (ATT_HEADS, ATT_BLOCK, ATT_BLOCK)
    in_specs, args, out_specs, out_shape, scratch, blocks_per_stream = [], [], [], [], [], []
    for g in range(N_DIL):
        slabs, rows = BLOCK_SHAPES[g]
        streams = SLABS // slabs
        nb = SLAB_ROWS // rows
        blocks_per_stream.append(nb)
        bias_c, bias_p = _band_bias(g)

        def cur(col, width=D_MODEL, slabs=slabs, rows=rows, streams=streams, nb=nb):
            return pl.BlockSpec((slabs, rows, width), lambda b, k: (b * streams + k // nb, k % nb, col))

        in_specs.append(pl.BlockSpec(table, lambda b, k: (0, 0, 0)))
        args.append(jnp.asarray(bias_c))
        if nb > 1:
            in_specs.append(pl.BlockSpec((None,) + table, lambda b, k, nb=nb: (jnp.minimum(k % nb, 1), 0, 0, 0)))
            args.append(jnp.asarray(bias_p))
            scratch += [pltpu.VMEM((slabs, rows, D_MODEL), srcs[g].dtype)] * 2
        in_specs += [cur(col_q[g]), cur(col_q[g] + 1), cur(col_q[g] + 2)]
        args += [srcs[g]] * 3
        out_specs += [cur(0), cur(0, 128)]
        out_shape += [jax.ShapeDtypeStruct((n_slabs, SLAB_ROWS, D_MODEL), BF16 if prescaled[g] else F32),
                      jax.ShapeDtypeStruct((n_slabs, SLAB_ROWS, 128), F32)]
    res = pl.pallas_call(
        functools.partial(_band_attn_kernel, blocks_per_stream=tuple(blocks_per_stream), prescaled=tuple(prescaled)),
        grid=(batch, SLABS),
        in_specs=in_specs,
        out_specs=out_specs,
        out_shape=out_shape,
        scratch_shapes=scratch,
        compiler_params=_params("parallel", "arbitrary"),
    )(*args)
    return [(res[2 * g], res[2 * g + 1]) for g in range(N_DIL)]


def _cached_attn_kernel(new_ref, c0_ref, c1_ref, c2_ref, coef_ref, o_ref, lse_ref, *, dec_seq):
    caches = (c0_ref, c1_ref, c2_ref)
    shared = [d == 1 for d in DIL_RATES]
    pairs = [(g, i) for g in range(N_DIL) for i in range(dec_seq)]
    res = lambda g, i: 0 if shared[g] else i
    lane = lax.broadcasted_iota(jnp.int32, (ATT_HEADS, ATT_BLOCK), 1)
    scale = HEAD_DIM ** -0.5
    q = {(g, i): new_ref[i, g, 0] for g, i in pairs}

    def qk_body(r, carry):
        out = []
        for n, (g, i) in enumerate(pairs):
            col = jnp.sum(q[g, i] * caches[g][r, res(g, i), 0], axis=-1, keepdims=True)
            out.append(jnp.where(lane == r, col, carry[n]))
        return tuple(out)

    zero = jnp.zeros((ATT_HEADS, ATT_BLOCK), F32)
    s_all = lax.fori_loop(0, ATT_BLOCK, qk_body, (zero,) * len(pairs), unroll=32)

    ps, p_news, ls, ms = [], [], [], []
    for n, (g, i) in enumerate(pairs):
        coef = coef_ref[g]
        if shared[g]:
            back = (ATT_BLOCK + i - lane).astype(F32)
            s = jnp.where(lane >= i, s_all[n] * scale - coef * back, MASK_VALUE)
            s_new = [jnp.sum(q[g, i] * new_ref[i2, g, 1], axis=-1, keepdims=True) * scale
                     - coef[:, :1] * float(i - i2) for i2 in range(i + 1)]
        else:
            s = s_all[n] * scale - coef * (ATT_BLOCK - lane).astype(F32)
            s_new = [jnp.sum(q[g, i] * new_ref[i, g, 1], axis=-1, keepdims=True) * scale]
        m = jnp.max(s, axis=-1, keepdims=True)
        for sn in s_new:
            m = jnp.maximum(m, sn)
        p = jnp.exp(s - m)
        p_new = [jnp.exp(sn - m) for sn in s_new]
        l = jnp.sum(p, axis=-1, keepdims=True)
        for pn in p_new:
            l = l + pn
        ps.append(p)
        p_news.append(p_new)
        ls.append(l)
        ms.append(m)

    def pv_body(r, carry):
        out = []
        for n, (g, i) in enumerate(pairs):
            col = jnp.sum(jnp.where(lane == r, ps[n], 0.0), axis=-1, keepdims=True)
            out.append(carry[n] + col * caches[g][r, res(g, i), 1])
        return tuple(out)

    zero_o = jnp.zeros((ATT_HEADS, HEAD_DIM), F32)
    o_all = lax.fori_loop(0, ATT_BLOCK, pv_body, (zero_o,) * len(pairs), unroll=32)
    for n, (g, i) in enumerate(pairs):
        o = o_all[n]
        for i2, pn in enumerate(p_news[n]):
            o = o + pn * new_ref[i2 if shared[g] else i, g, 2]
        o_ref[g, i] = o / ls[n]
        lse_ref[g, i] = jnp.broadcast_to(ms[n] + jnp.log(ls[n]), (ATT_HEADS, 128))


def _cached_attention(att_s, caches):
    db, dec_seq = att_s.shape[:2]
    views, specs = [], []
    for g, cache in enumerate(caches):
        dil, window = DIL_RATES[g], DIL_WINDOWS[g]
        assert cache.shape[1] == window and window == dil * ATT_BLOCK
        assert dil == 1 or dec_seq <= dil
        nres = 1 if dil == 1 else dec_seq
        views.append(cache.reshape(db, ATT_BLOCK, dil, 2, ATT_HEADS, HEAD_DIM))
        specs.append(pl.BlockSpec((None, ATT_BLOCK, nres, 2, ATT_HEADS, HEAD_DIM), lambda b: (b, 0, 0, 0, 0, 0)))
    coef = np.stack([np.repeat((_alibi_slopes()[g] * DIL_RATES[g])[:, None], 128, axis=1) for g in range(N_DIL)])
    out = lambda w: pl.BlockSpec((N_DIL, None, dec_seq, ATT_HEADS, w), lambda b: (0, b, 0, 0, 0))
    return pl.pallas_call(
        functools.partial(_cached_attn_kernel, dec_seq=dec_seq),
        grid=(db,),
        in_specs=[pl.BlockSpec((None, dec_seq, N_DIL, 3, ATT_HEADS, HEAD_DIM), lambda b: (b, 0, 0, 0, 0, 0))]
        + specs + [pl.BlockSpec((N_DIL, ATT_HEADS, 128), lambda b: (0, 0, 0))],
        out_specs=[out(HEAD_DIM), out(128)],
        out_shape=[jax.ShapeDtypeStruct((N_DIL, db, dec_seq, ATT_HEADS, HEAD_DIM), F32),
                   jax.ShapeDtypeStruct((N_DIL, db, dec_seq, ATT_HEADS, 128), F32)],
        compiler_params=_params("parallel"),
    )(att_s, *views, jnp.asarray(coef.astype(np.float32)))


def _merge_kernel(a_ref, ga_ref, gb_ref, o0_ref, o1_ref, o2_ref, l0_ref, l1_ref, l2_ref,
                  x_ref, w_ref, g_ref, h_ref, hn_ref, merged_ref):
    l0, l1, l2 = l0_ref[...], l1_ref[...], l2_ref[...]
    mx = jnp.maximum(jnp.maximum(l0, l1), l2)
    e0, e1, e2 = jnp.exp(l0 - mx), jnp.exp(l1 - mx), jnp.exp(l2 - mx)
    den = e0 + e1 + e2
    w0, w1, w2 = e0 / den, e1 / den, e2 / den
    for h in range(ATT_HEADS):
        hs = slice(h * HEAD_DIM, (h + 1) * HEAD_DIM)
        b_out = (w0[:, h:h + 1] * o0_ref[:, hs] + w1[:, h:h + 1] * o1_ref[:, hs].astype(F32)
                 + w2[:, h:h + 1] * o2_ref[:, hs].astype(F32))
        merged = ga_ref[:, hs].astype(F32) * a_ref[:, hs] + gb_ref[:, hs].astype(F32) * b_out
        merged_ref[:, hs] = merged.astype(merged_ref.dtype)
    h_new = x_ref[...] + jnp.dot(merged_ref[...], w_ref[...], preferred_element_type=F32)
    h_ref[...] = h_new
    hn_ref[...] = _rms(h_new, g_ref[...]).astype(hn_ref.dtype)


def _merge(a_out, gates, gate_col, outs, lses, x, w_out, norm_g, t, tm):
    row = lambda col: pl.BlockSpec((tm, D_MODEL), lambda i: (i, col))
    lrow = pl.BlockSpec((tm, 128), lambda i: (i, 0))
    p = gates
    return pl.pallas_call(
        _merge_kernel,
        grid=(t // tm,),
        in_specs=[row(0), row(gate_col), row(gate_col + 1), row(0), row(0), row(0), lrow, lrow, lrow,
                  row(0), pl.BlockSpec((D_MODEL, D_MODEL), lambda i: (0, 0)),
                  pl.BlockSpec((1, D_MODEL), lambda i: (0, 0))],
        out_specs=[row(0), row(0)],
        out_shape=[jax.ShapeDtypeStruct((x.shape[0], D_MODEL), F32), jax.ShapeDtypeStruct((t, D_MODEL), BF16)],
        scratch_shapes=[pltpu.VMEM((tm, D_MODEL), BF16)],
        compiler_params=_params("parallel"),
    )(a_out, p, p, outs[0], outs[1], outs[2], lses[0], lses[1], lses[2], x, w_out, norm_g.reshape(1, D_MODEL))


def _peer_score_kernel(hn_ref, wq_ref, keys_ref, s_ref):
    q = jnp.dot(hn_ref[...], wq_ref[...], preferred_element_type=F32).astype(BF16)
    nt = (((1,), (1,)), ((), ()))
    for hc in range(2 * PEER_HEADS):
        cs = slice(hc * PEER_HALF, (hc + 1) * PEER_HALF)
        s_ref[hc] = lax.dot_general(keys_ref[hc % 2], q[:, cs], nt, preferred_element_type=F32)


def _peer_scores(hn, w_q, sub_keys, tm):
    t = hn.shape[0]
    return pl.pallas_call(
        _peer_score_kernel,
        grid=(t // tm,),
        in_specs=[pl.BlockSpec((tm, D_MODEL), lambda i: (i, 0)),
                  pl.BlockSpec((D_MODEL, 2 * PEER_HEADS * PEER_HALF), lambda i: (0, 0)),
                  pl.BlockSpec((2, N_KEYS, PEER_HALF), lambda i: (0, 0, 0))],
        out_specs=pl.BlockSpec((2 * PEER_HEADS, N_KEYS, tm), lambda i: (0, 0, i)),
        out_shape=jax.ShapeDtypeStruct((2 * PEER_HEADS, N_KEYS, t), F32),
        compiler_params=_params("parallel"),
    )(hn, w_q, sub_keys)


def _take_top(arrays, order, count, sentinel):
    arrays = list(arrays)
    vals = [[] for _ in arrays]
    idxs = [[] for _ in arrays]
    for _ in range(count):
        for n, s in enumerate(arrays):
            m = jnp.max(s, axis=0, keepdims=True)
            pos = jnp.min(jnp.where(s == m, order, sentinel), axis=0, keepdims=True)
            vals[n].append(m)
            idxs[n].append(pos)
            arrays[n] = jnp.where(order == pos, NEG_INF, s)
    return vals, idxs


def _sort_network(n):
    pairs, p = [], 1
    while p < n:
        k = p
        while k >= 1:
            for j in range(k % p, n - k, 2 * k):
                for i in range(min(k, n - j - k)):
                    if (i + j) // (2 * p) == (i + j + k) // (2 * p):
                        pairs.append((i + j, i + j + k))
            k //= 2
        p *= 2
    return pairs


def _take_top_sorted(arrays, order, count, sentinel):
    tiles = arrays[0].shape[0] // 8
    assert all(a.shape[0] == 8 * tiles for a in arrays)
    val = [[a[8 * v:8 * v + 8] for v in range(tiles)] for a in arrays]
    idx = [[order[8 * v:8 * v + 8] for v in range(tiles)] for _ in arrays]
    network = [(i, j) for i, j in _sort_network(1 << (tiles - 1).bit_length()) if j < tiles]
    for i, j in network:
        for n in range(len(arrays)):
            vi, vj, ri, rj = val[n][i], val[n][j], idx[n][i], idx[n][j]
            swap = (vj > vi) | ((vj == vi) & (rj < ri))
            val[n][i], val[n][j] = jnp.where(swap, vj, vi), jnp.where(swap, vi, vj)
            idx[n][i], idx[n][j] = jnp.where(swap, rj, ri), jnp.where(swap, ri, rj)
    vals = [[] for _ in arrays]
    idxs = [[] for _ in arrays]
    for t in range(count):
        for n in range(len(arrays)):
            head, rank = val[n][0], idx[n][0]
            m = jnp.max(head, axis=0, keepdims=True)
            pos = jnp.min(jnp.where(head == m, rank, sentinel), axis=0, keepdims=True)
            vals[n].append(m)
            idxs[n].append(pos)
            popped = rank == pos
            for d in range(min(tiles, count - t - 1)):
                if d + 1 < tiles:
                    val[n][d] = jnp.where(popped, val[n][d + 1], val[n][d])
                    idx[n][d] = jnp.where(popped, idx[n][d + 1], idx[n][d])
                else:
                    val[n][d] = jnp.where(popped, NEG_INF, val[n][d])
                    idx[n][d] = jnp.where(popped, sentinel, idx[n][d])
    return vals, idxs


def _stack_rows(rows_list, krow):
    out = jnp.zeros(krow.shape, F32)
    for j, r in enumerate(rows_list):
        out = jnp.where(krow == float(j), r, out)
    return out


def _cand_layout():
    k = PEER_TOPK
    pieces = [("row_a", a, 16 if a == 0 else 8, 0, k // (a + 1)) for a in range(4)]
    pieces += [("col_b", 0, 16, 4, 16), ("col_b", 1, 8, 4, 8), ("col_b", 2, 8, 4, 5)]
    pos = []
    for kind, idx, rows, lo, hi in pieces:
        for r in range(rows):
            a, b = (idx, r) if kind == "row_a" else (r, idx)
            ok = lo <= r < hi and (a + 1) * (b + 1) <= k
            pos.append(a * k + b if ok else k * k)
    assert sorted(p for p in pos if p < k * k) == sorted(
        a * k + b for a in range(k) for b in range(k) if (a + 1) * (b + 1) <= k)
    return pieces, np.asarray(pos, np.float32)


def _route_kernel(s_ref, pos_ref, u_ref, v_ref, e1_ref, e2_ref, gate_ref, ub_ref, vb_ref):
    ub_ref[...] = u_ref[...].astype(ub_ref.dtype)
    vb_ref[...] = v_ref[...].astype(vb_ref.dtype)
    k = PEER_TOPK
    heads = s_ref.shape[0] // 2
    lanes = s_ref.shape[2]
    key_rank = lax.broadcasted_iota(jnp.int32, (N_KEYS, lanes), 0).astype(F32)
    vals, idxs = _take_top_sorted([s_ref[n] for n in range(2 * heads)], key_rank, k, float(N_KEYS))
    krow = lax.broadcasted_iota(jnp.int32, (k, lanes), 0).astype(F32)
    flat = pos_ref[...]
    cands, i1_all, i2_all = [], [], []
    for hd in range(heads):
        v1, v2 = vals[2 * hd], vals[2 * hd + 1]
        v1_all = _stack_rows(v1, krow)
        v2_all = _stack_rows(v2, krow)
        i1_all.append(_stack_rows(idxs[2 * hd], krow))
        i2_all.append(_stack_rows(idxs[2 * hd + 1], krow))
        parts = [v1[idx] + v2_all[:rows] if kind == "row_a" else v1_all[:rows] + v2[idx]
                 for kind, idx, rows, _, _ in _cand_layout()[0]]
        cands.append(jnp.where(flat < float(k * k), jnp.concatenate(parts, axis=0), NEG_INF))
    top_s, pos = _take_top_sorted(cands, flat, k, float(k * k))
    for hd in range(heads):
        e1, e2 = [], []
        for j in range(k):
            a = jnp.floor(pos[hd][j] * (1.0 / k))
            b = pos[hd][j] - a * k
            e1.append(jnp.sum(jnp.where(krow == a, i1_all[hd], 0.0), axis=0, keepdims=True))
            e2.append(jnp.sum(jnp.where(krow == b, i2_all[hd], 0.0), axis=0, keepdims=True))
        ex = jnp.exp(_stack_rows(top_s[hd], krow) - top_s[hd][0])
        rows = slice(hd * k, (hd + 1) * k)
        gate_ref[rows, :] = ex / jnp.sum(ex, axis=0, keepdims=True)
        e1_ref[rows, :] = _stack_rows(e1, krow)
        e2_ref[rows, :] = _stack_rows(e2, krow)


def _route(scores_t, tl, heads_per_step, peer_u, peer_v):
    t = scores_t.shape[2]
    kk = PEER_HEADS * PEER_TOPK
    n_head_steps = PEER_HEADS // heads_per_step
    steps = (t // tl) * n_head_steps
    n_blocks = 1 << (steps.bit_length() - 1)
    tab_rows = N_EXPERTS // n_blocks
    tab = pl.BlockSpec((tab_rows, D_MODEL), lambda i, h: (jnp.minimum(i * n_head_steps + h, n_blocks - 1), 0))
    out = pl.BlockSpec((heads_per_step * PEER_TOPK, tl), lambda i, h: (h, i))
    flat = jnp.asarray(np.repeat(_cand_layout()[1][:, None], tl, axis=1))
    return pl.pallas_call(
        _route_kernel,
        grid=(t // tl, n_head_steps),
        in_specs=[pl.BlockSpec((2 * heads_per_step, N_KEYS, tl), lambda i, h: (h, 0, i)),
                  pl.BlockSpec(flat.shape, lambda i, h: (0, 0)), tab, tab],
        out_specs=[out, out, out, tab, tab],
        out_shape=[jax.ShapeDtypeStruct((kk, t), F32)] * 3 + [jax.ShapeDtypeStruct((N_EXPERTS, D_MODEL), BF16)] * 2,
        compiler_params=_params("arbitrary", "arbitrary"),
    )(scores_t, flat, peer_u, peer_v)


def _expert_weight_kernel(e1_ref, e2_ref, gate_ref, g_ref, e1t_ref, e2t_ref, gt_ref):
    e1t_ref[...] = e1_ref[...].T
    e2t_ref[...] = e2_ref[...].T
    gt_ref[...] = gate_ref[...].T
    kk = e1_ref.shape[0]
    key = lax.broadcasted_iota(jnp.int32, (N_KEYS, kk), 0).astype(F32)
    nt = (((1,), (1,)), ((), ()))

    def body(t, carry):
        r1 = e1t_ref[pl.ds(t, 1), :]
        r2 = e2t_ref[pl.ds(t, 1), :]
        gr = gt_ref[pl.ds(t, 1), :]
        a_t = jnp.where(key == r1, 1.0, 0.0).astype(BF16)
        b_t = jnp.where(key == r2, gr, 0.0).astype(BF16)
        g_ref[t] = lax.dot_general(a_t, b_t, nt, preferred_element_type=F32).astype(g_ref.dtype)
        return carry

    lax.fori_loop(0, g_ref.shape[0], body, 0, unroll=32)


def _expert_weights(e1, e2, gate, tl):
    kk, t = e1.shape
    slot = pl.BlockSpec((kk, tl), lambda i: (0, i))
    return pl.pallas_call(
        _expert_weight_kernel,
        grid=(t // tl,),
        in_specs=[slot, slot, slot],
        out_specs=pl.BlockSpec((tl, N_KEYS, N_KEYS), lambda i: (i, 0, 0)),
        out_shape=jax.ShapeDtypeStruct((t, N_KEYS, N_KEYS), F32),
        scratch_shapes=[pltpu.VMEM((tl, kk), F32)] * 3,
        compiler_params=_params("parallel"),
    )(e1, e2, gate)


def _peer_kernel(hn_ref, u_ref, v_ref, g_ref, y_ref):
    e = pl.program_id(1)
    nt = (((1,), (1,)), ((), ()))
    hk = lax.dot_general(hn_ref[...], u_ref[...], nt, preferred_element_type=F32)
    g = jnp.swapaxes(g_ref[...], 0, 1)
    act = [(jax.nn.gelu(hk[:, a * N_KEYS:(a + 1) * N_KEYS]) * g[a]).astype(BF16) for a in range(g.shape[0])]
    out = jnp.dot(jnp.concatenate(act, axis=-1), v_ref[...], preferred_element_type=F32)

    @pl.when(e == 0)
    def _():
        y_ref[...] = out

    @pl.when(e > 0)
    def _():
        y_ref[...] += out


def _peer(hn, u, v, g, t, rows_out, tm, te):
    row = pl.BlockSpec((tm, D_MODEL), lambda i, e: (i, 0), pipeline_mode=pl.Buffered(1))
    tab = pl.BlockSpec((te, D_MODEL), lambda i, e: (e, 0))
    return pl.pallas_call(
        _peer_kernel,
        grid=(t // tm, N_EXPERTS // te),
        in_specs=[row, tab, tab, pl.BlockSpec((tm, te // N_KEYS, N_KEYS), lambda i, e: (i, e, 0))],
        out_specs=row,
        out_shape=jax.ShapeDtypeStruct((rows_out, D_MODEL), F32),
        compiler_params=_params("parallel", "arbitrary"),
    )(hn, u, v, g)


def _final_norm_prompt_kernel(h_ref, f_ref, g_ref, y_ref):
    y = [_rms(h_ref[rr % 4, rr // 4] + f_ref[rr % 4, rr // 4], g_ref[...]) for rr in range(8)]
    y_ref[...] = jnp.swapaxes(jnp.stack(y, axis=0), 0, 1)


def _final_norm_prompt(h, f, g, batch, seq):
    rows = 64
    h5 = h.reshape(batch + 1, 4, 4, SLAB_ROWS, D_MODEL)
    f5 = f.reshape(batch + 1, 4, 4, SLAB_ROWS, D_MODEL)
    blk = pl.BlockSpec((None, 4, 2, rows, D_MODEL), lambda b, hh, j: (b, 0, hh, j, 0))
    y = pl.pallas_call(
        _final_norm_prompt_kernel,
        grid=(batch, 2, SLAB_ROWS // rows),
        in_specs=[blk, blk, pl.BlockSpec((1, D_MODEL), lambda b, hh, j: (0, 0))],
        out_specs=pl.BlockSpec((None, rows, 8, D_MODEL), lambda b, hh, j: (b, j, hh, 0)),
        out_shape=jax.ShapeDtypeStruct((batch, SLAB_ROWS, SLABS, D_MODEL), F32),
        compiler_params=_params("parallel", "parallel", "parallel"),
    )(h5, f5, g.reshape(1, D_MODEL))
    return y.reshape(batch, seq, D_MODEL)


def _final_norm_rows_kernel(h_ref, f_ref, g_ref, y_ref):
    y_ref[...] = _rms(h_ref[...] + f_ref[...], g_ref[...])


def _final_norm_rows(h, f, g, row_block, rows):
    blk = pl.BlockSpec((rows, D_MODEL), lambda i: (row_block, 0))
    return pl.pallas_call(
        _final_norm_rows_kernel,
        grid=(1,),
        in_specs=[blk, blk, pl.BlockSpec((1, D_MODEL), lambda i: (0, 0))],
        out_specs=pl.BlockSpec((rows, D_MODEL), lambda i: (0, 0)),
        out_shape=jax.ShapeDtypeStruct((rows, D_MODEL), F32),
        compiler_params=_params("arbitrary"),
    )(h, f, g.reshape(1, D_MODEL))


def _kv_prompt_kernel(*refs):
    ins, outs = refs[:2 * N_DIL], refs[2 * N_DIL:]
    for g in range(N_DIL):
        for kv in range(2):
            src = ins[2 * g + kv]
            heads = [src[0, :, h * HEAD_DIM:(h + 1) * HEAD_DIM] for h in range(ATT_HEADS)]
            outs[g][:, kv, :, :] = jnp.swapaxes(jnp.stack(heads, axis=0), 0, 1)


def _kv_prompt(p3, batch, seq):
    in_specs, out_specs, out_shape = [], [], []
    residue = lambda s: (s % 4) * 4 + s // 4
    for g in range(N_DIL):
        steps = min(DIL_WINDOWS[g], seq) // SLABS
        last = SLAB_ROWS // steps - 1
        for kv in range(2):
            col = COL_ATT + 3 * g + 1 + kv
            in_specs.append(pl.BlockSpec((1, steps, D_MODEL),
                                         lambda b, s, last=last, col=col: (b * SLABS + s, last, col)))
        out_specs.append(pl.BlockSpec((None, steps, None, 2, ATT_HEADS, HEAD_DIM),
                                      lambda b, s: (b, 0, residue(s), 0, 0, 0)))
        out_shape.append(jax.ShapeDtypeStruct((batch, steps, SLABS, 2, ATT_HEADS, HEAD_DIM), F32))
    outs = pl.pallas_call(
        _kv_prompt_kernel,
        grid=(batch, SLABS),
        in_specs=in_specs,
        out_specs=out_specs,
        out_shape=out_shape,
        compiler_params=_params("parallel", "parallel"),
    )(*([p3] * len(in_specs)))
    return [o.reshape(1, batch, -1, 2, ATT_HEADS, HEAD_DIM) for o in outs]


def _row_tile(t, candidates):
    for c in candidates:
        if t % c == 0:
            return c
    raise ValueError(f"no row tile for {t} tokens")


def kernel(x_prompt, x_sample, cache_kv_w128, cache_kv_w512, cache_kv_w2048, norm_mix_g, w_in, sgu_norm_g, sgu_w, sgu_b, w_out, norm_ffn_g, peer_w_q, peer_sub_keys, peer_u, peer_v, norm_final_g):
    batch, seq, _ = x_prompt.shape
    db, ds, _ = x_sample.shape
    assert w_in.shape[0] == 1 and db * ds == CHUNK and seq == SLABS * SLAB_ROWS
    caches = (cache_kv_w128, cache_kv_w512, cache_kv_w2048)
    tp, ts = batch * seq, db * ds
    t = tp + ts
    n_slabs = t // SLAB_ROWS
    tm_big = _row_tile(t, (1040, 640, 128))
    tm_lane = _row_tile(t, (640, 128))

    xs = x_sample.reshape(ts, D_MODEL)
    xl, xn = _permute_norm(x_prompt, norm_mix_g[0])
    xl = lax.dynamic_update_slice(xl, xs, (tp, 0))
    xn = lax.dynamic_update_slice(xn, _rmsnorm_bf16(xs, norm_mix_g[0]), (tp, 0))

    p, att16 = _inproj(xn, w_in[0], t, tm_big, 1024)
    p3 = p.reshape(n_slabs, SLAB_ROWS, p.shape[1])
    att16_3 = att16.reshape(n_slabs, SLAB_ROWS, att16.shape[1])

    w_tril = sgu_w[0] * jnp.tril(jnp.ones((CHUNK, CHUNK), F32))
    tau = _block_steps(0)
    w_p = w_tril[:, tau][:, :, tau].astype(BF16)
    b_p = sgu_b[0][:, tau][..., None]
    w_s = jnp.einsum("bc,gis->gbics", jnp.eye(db, dtype=F32), w_tril[:, :ds, :ds]).reshape(SGU_GROUPS, ts, ts)
    b_s = jnp.tile(sgu_b[0][:, :ds], (1, db))[..., None]
    a_out = _sgu_prompt(p3, sgu_norm_g[0], w_p, b_p, batch).reshape(t, D_MODEL)
    a_s, vn_s = _sgu_sample(p3, sgu_norm_g[0], w_s.astype(BF16), b_s, tp // SLAB_ROWS)
    a_out = lax.dynamic_update_slice(a_out, a_s, (tp, 0))

    att_s = p[tp:, COL_ATT * D_MODEL:COL_GATE * D_MODEL].reshape(db, ds, N_DIL, 3, ATT_HEADS, HEAD_DIM)
    outs, lses = [], []
    o_s, lse_s = _cached_attention(att_s, [c[0] for c in caches])
    band = _band_attention([p3, att16_3, att16_3], [COL_ATT, 0, 3], [False, True, True], batch)
    for g in range(N_DIL):
        o_p, lse_p = band[g]
        lse_g = jnp.pad(lse_s[g, ..., 0].reshape(ts, ATT_HEADS), ((0, 0), (0, 128 - ATT_HEADS)))
        o_g = o_s[g].reshape(ts, D_MODEL).astype(o_p.dtype)
        outs.append(lax.dynamic_update_slice(o_p.reshape(t, D_MODEL), o_g, (tp, 0)))
        lses.append(lax.dynamic_update_slice(lse_p.reshape(t, 128), lse_g, (tp, 0)))

    h, hn = _merge(a_out, att16, COL_GATE - COL_ATT16, outs, lses, xl, _to_bf16(w_out[0], 1024, D_MODEL),
                   norm_ffn_g[0], t, _row_tile(t, (320, 128)))

    scores_t = _peer_scores(hn, _to_bf16(peer_w_q[0], 1024, D_MODEL), peer_sub_keys[0].astype(BF16), tm_lane)
    e1, e2, gate, u_b16, v_b16 = _route(scores_t, 128, 8, peer_u[0], peer_v[0])
    g_dense = _expert_weights(e1, e2, gate, 128)
    f = _peer(hn, u_b16, v_b16, g_dense, t, h.shape[0], tm_big, 1024)

    y_prompt = _final_norm_prompt(h, f, norm_final_g, batch, seq)
    y_sample = _final_norm_rows(h, f, norm_final_g, tp // ts, ts).reshape(db, ds, D_MODEL)
    kv_prompt = _kv_prompt(p3, batch, seq)
    kv_sample = [att_s[:, :, g, 1:3][None] for g in range(N_DIL)]
    sgu_v_sample = vn_s.reshape(1, db, ds, D_MODEL)
    return (y_prompt, y_sample, kv_prompt[0], kv_prompt[1], kv_prompt[2],
            kv_sample[0], kv_sample[1], kv_sample[2], sgu_v_sample)
```

```python
import functools

import numpy as np
import jax
import jax.numpy as jnp
from jax import lax
from jax.experimental import pallas as pl
from jax.experimental.pallas import tpu as pltpu

F32 = jnp.float32
BF16 = jnp.bfloat16

D_MODEL = 2048
HEAD_DIM = 128
ATT_HEADS = D_MODEL // HEAD_DIM
N_DIL = 3
DIL_WINDOWS = (128, 512, 2048)
DIL_RATES = (1, 4, 16)
ATT_BLOCK = 128
SLABS = DIL_RATES[-1]
SLAB_ROWS = 128
SGU_GROUPS = 8
SGU_GROUP_DIM = D_MODEL // SGU_GROUPS
CHUNK = 128
N_COL_BLOCKS = 13
COL_U, COL_V, COL_ATT, COL_GATE = 0, 1, 2, 11
COL_ATT16 = COL_ATT + 3
PEER_HEADS = 8
PEER_TOPK = 16
N_KEYS = 128
N_EXPERTS = N_KEYS * N_KEYS
PEER_HALF = 128
NORM_EPS = 1e-6
MASK_VALUE = -1e30
NEG_INF = float("-inf")
VMEM_LIMIT = 56 * 1024 * 1024

BLOCK_SHAPES = ((16, 8), (4, 32), (1, 128))


def _alibi_slopes():
    n = N_DIL * ATT_HEADS
    e = np.arange(1, n + 1, dtype=np.float32)
    return np.exp2(np.float32(-8.0) * e / np.float32(n)).astype(np.float32).reshape(N_DIL, ATT_HEADS)


def _block_steps(group):
    slabs, rows = BLOCK_SHAPES[group]
    n = np.arange(slabs * rows)
    s, j = n // rows, n % rows
    if group == 0:
        return j * 16 + (s % 4) * 4 + s // 4
    if group == 1:
        return j * 4 + s
    return j


def _params(*sem):
    return pltpu.CompilerParams(dimension_semantics=sem, vmem_limit_bytes=VMEM_LIMIT)


def _rms(x, g):
    return x * lax.rsqrt(jnp.mean(x * x, axis=-1, keepdims=True) + NORM_EPS) * g


def _permute_norm_kernel(x_ref, g_ref, xl_ref, xn_ref):
    xs = jnp.swapaxes(x_ref[...], 0, 1)
    for rr in range(8):
        x = xs[rr]
        xl_ref[rr % 4, rr // 4] = x
        xn_ref[rr % 4, rr // 4] = _rms(x, g_ref[...]).astype(xn_ref.dtype)


def _permute_norm(x_prompt, g):
    batch, seq, _ = x_prompt.shape
    x4 = x_prompt.reshape(batch, SLAB_ROWS, SLABS, D_MODEL)
    out = pl.BlockSpec((None, 4, 2, SLAB_ROWS, D_MODEL), lambda b, h: (b, 0, h, 0, 0))
    xl, xn = pl.pallas_call(
        _permute_norm_kernel,
        grid=(batch, 2),
        in_specs=[pl.BlockSpec((None, SLAB_ROWS, 8, D_MODEL), lambda b, h: (b, 0, h, 0)),
                  pl.BlockSpec((1, D_MODEL), lambda b, h: (0, 0))],
        out_specs=[out, out],
        out_shape=[jax.ShapeDtypeStruct((batch + 1, 4, 4, SLAB_ROWS, D_MODEL), F32),
                   jax.ShapeDtypeStruct((batch + 1, 4, 4, SLAB_ROWS, D_MODEL), BF16)],
        compiler_params=_params("parallel", "parallel"),
    )(x4, g.reshape(1, D_MODEL))
    rows = (batch + 1) * seq
    return xl.reshape(rows, D_MODEL), xn.reshape(rows, D_MODEL)


def _rmsnorm_kernel(x_ref, g_ref, o_ref):
    o_ref[...] = _rms(x_ref[...], g_ref[...]).astype(o_ref.dtype)


def _rmsnorm_bf16(x, g):
    return pl.pallas_call(
        _rmsnorm_kernel,
        out_shape=jax.ShapeDtypeStruct(x.shape, BF16),
    )(x, g.reshape(1, D_MODEL))


def _cast_kernel(x_ref, o_ref):
    o_ref[...] = x_ref[...].astype(o_ref.dtype)


def _to_bf16(w, rows, cols):
    r, c = w.shape
    return pl.pallas_call(
        _cast_kernel,
        grid=(r // rows, c // cols),
        in_specs=[pl.BlockSpec((rows, cols), lambda i, j: (i, j))],
        out_specs=pl.BlockSpec((rows, cols), lambda i, j: (i, j)),
        out_shape=jax.ShapeDtypeStruct((r, c), BF16),
        compiler_params=_params("parallel", "parallel"),
    )(w)


def _inproj_kernel(x_ref, w_ref, o_ref, att_ref, w16_ref, *, blocks_per_col):
    cb = pl.program_id(0) // blocks_per_col

    @pl.when(pl.program_id(1) == 0)
    def _():
        w16_ref[...] = w_ref[...].astype(w16_ref.dtype)

    p = jnp.dot(x_ref[...], w16_ref[...], preferred_element_type=F32)

    @pl.when(cb < COL_ATT)
    def _():
        o_ref[...] = jax.nn.gelu(p)

    @pl.when((cb >= COL_ATT) & (cb < COL_ATT16))
    def _():
        o_ref[...] = p

    @pl.when((cb >= COL_ATT16) & (cb < COL_GATE))
    def _():
        o_ref[...] = p
        is_q = (cb - COL_ATT) % 3 == 0
        att_ref[...] = (p * jnp.where(is_q, HEAD_DIM ** -0.5, 1.0)).astype(att_ref.dtype)

    @pl.when(cb >= COL_GATE)
    def _():
        att_ref[...] = jax.nn.sigmoid(p).astype(att_ref.dtype)


def _inproj(xn, w, t, tm, tn):
    n = w.shape[1]
    per_col = D_MODEL // tn
    n_rows = t // tm
    n_f32 = COL_GATE * per_col
    first_b16 = COL_ATT16 * per_col
    n_b16 = (N_COL_BLOCKS - COL_ATT16) * per_col
    f32_block = lambda j, i: (jnp.where(j < n_f32, i, n_rows - 1), jnp.minimum(j, n_f32 - 1))
    b16_block = lambda j, i: (jnp.where(j >= first_b16, i, 0), jnp.maximum(j - first_b16, 0))
    return pl.pallas_call(
        functools.partial(_inproj_kernel, blocks_per_col=per_col),
        grid=(n // tn, n_rows),
        in_specs=[pl.BlockSpec((tm, D_MODEL), lambda j, i: (i, 0)),
                  pl.BlockSpec((D_MODEL, tn), lambda j, i: (0, j))],
        out_specs=[pl.BlockSpec((tm, tn), f32_block),
                   pl.BlockSpec((tm, tn), b16_block)],
        out_shape=[jax.ShapeDtypeStruct((t, n_f32 * tn), F32),
                   jax.ShapeDtypeStruct((t, n_b16 * tn), BF16)],
        scratch_shapes=[pltpu.VMEM((D_MODEL, tn), BF16)],
        compiler_params=_params("arbitrary", "arbitrary"),
    )(xn, w)


def _sgu_chunk(u, v, g_ref, w_ref, b_ref):
    vc = v - jnp.mean(v, axis=-1, keepdims=True)
    vn = vc * lax.rsqrt(jnp.mean(vc * vc, axis=-1, keepdims=True) + NORM_EPS) * g_ref[...]
    cols = []
    for g in range(SGU_GROUPS):
        cs = slice(g * SGU_GROUP_DIM, (g + 1) * SGU_GROUP_DIM)
        mix = jnp.dot(w_ref[g], vn[:, cs].astype(BF16), preferred_element_type=F32) + b_ref[g]
        cols.append(u[:, cs] * mix)
    return jnp.concatenate(cols, axis=-1), vn


def _sgu_kernel(u_ref, v_ref, g_ref, w_ref, b_ref, a_ref, vn_ref):
    a, vn = _sgu_chunk(u_ref[...].reshape(CHUNK, D_MODEL), v_ref[...].reshape(CHUNK, D_MODEL), g_ref, w_ref, b_ref)
    a_ref[...] = a.reshape(a_ref.shape)
    vn_ref[...] = vn


def _sgu_prompt_kernel(u_ref, v_ref, g_ref, w_ref, b_ref, a_ref, *, rows):
    for c in range(u_ref.shape[1] // rows):
        rs = slice(c * rows, (c + 1) * rows)
        a, _ = _sgu_chunk(u_ref[:, rs, :].reshape(CHUNK, D_MODEL), v_ref[:, rs, :].reshape(CHUNK, D_MODEL),
                          g_ref, w_ref, b_ref)
        a_ref[:, rs, :] = a.reshape(a_ref.shape[0], rows, D_MODEL)


def _sgu_prompt(p3, sgu_norm_g, w, b, batch, chunks_per_step=4):
    slabs, rows = BLOCK_SHAPES[0]
    n_chunks = SLAB_ROWS // (rows * chunks_per_step)
    blk = lambda col: pl.BlockSpec((slabs, rows * chunks_per_step, D_MODEL), lambda bb, c: (bb, c, col))
    return pl.pallas_call(
        functools.partial(_sgu_prompt_kernel, rows=rows),
        grid=(batch, n_chunks),
        in_specs=[blk(COL_U), blk(COL_V),
                  pl.BlockSpec((1, D_MODEL), lambda bb, c: (0, 0)),
                  pl.BlockSpec((SGU_GROUPS, CHUNK, CHUNK), lambda bb, c: (0, 0, 0)),
                  pl.BlockSpec((SGU_GROUPS, CHUNK, 1), lambda bb, c: (0, 0, 0))],
        out_specs=blk(0),
        out_shape=jax.ShapeDtypeStruct((p3.shape[0], SLAB_ROWS, D_MODEL), F32),
        compiler_params=_params("parallel", "parallel"),
    )(p3, p3, sgu_norm_g.reshape(1, D_MODEL), w, b)


def _sgu_sample(p3, sgu_norm_g, w, b, slab):
    blk = lambda col: pl.BlockSpec((1, SLAB_ROWS, D_MODEL), lambda i: (slab, 0, col))
    return pl.pallas_call(
        _sgu_kernel,
        grid=(1,),
        in_specs=[blk(COL_U), blk(COL_V),
                  pl.BlockSpec((1, D_MODEL), lambda i: (0, 0)),
                  pl.BlockSpec((SGU_GROUPS, CHUNK, CHUNK), lambda i: (0, 0, 0)),
                  pl.BlockSpec((SGU_GROUPS, CHUNK, 1), lambda i: (0, 0, 0))],
        out_specs=[pl.BlockSpec((CHUNK, D_MODEL), lambda i: (0, 0)),
                   pl.BlockSpec((CHUNK, D_MODEL), lambda i: (0, 0))],
        out_shape=[jax.ShapeDtypeStruct((CHUNK, D_MODEL), F32), jax.ShapeDtypeStruct((CHUNK, D_MODEL), F32)],
        compiler_params=_params("arbitrary"),
    )(p3, p3, sgu_norm_g.reshape(1, D_MODEL), w, b)


def _band_block(bc_ref, bp_ref, q_ref, kc_ref, vc_ref, o_ref, lse_ref, kp_ref, vp_ref, prescaled):
    has_prev = kp_ref is not None
    n = ATT_BLOCK
    scale = HEAD_DIM ** -0.5
    nt = (((1,), (1,)), ((), ()))
    heads = range(ATT_HEADS)
    hs = [slice(h * HEAD_DIM, (h + 1) * HEAD_DIM) for h in heads]
    ld = lambda ref, h: ref[:, :, hs[h]].reshape(n, HEAD_DIM)
    q = [ld(q_ref, h) if prescaled else (ld(q_ref, h) * scale).astype(BF16) for h in heads]
    if has_prev:
        keys = [jnp.concatenate([ld(kp_ref, h).astype(BF16), ld(kc_ref, h).astype(BF16)], axis=0) for h in heads]
        vals = [jnp.concatenate([ld(vp_ref, h).astype(BF16), ld(vc_ref, h).astype(BF16)], axis=0) for h in heads]
        bias = [jnp.concatenate([bp_ref[h], bc_ref[h]], axis=1) for h in heads]
    else:
        keys = [ld(kc_ref, h).astype(BF16) for h in heads]
        vals = [ld(vc_ref, h).astype(BF16) for h in heads]
        bias = [bc_ref[h] for h in heads]
    s = [lax.dot_general(q[h], keys[h], nt, preferred_element_type=F32) + bias[h] for h in heads]
    m = [jnp.max(s[h], axis=-1, keepdims=True) for h in heads]
    p = [jnp.exp(s[h] - m[h]) for h in heads]
    l = [jnp.sum(p[h], axis=-1, keepdims=True) for h in heads]
    o = [jnp.dot(p[h].astype(BF16), vals[h], preferred_element_type=F32) for h in heads]
    lane = lax.broadcasted_iota(jnp.int32, (n, 128), 1)
    lse = jnp.zeros((n, 128), F32)
    for h in heads:
        lse = jnp.where(lane == h, m[h] + jnp.log(l[h]), lse)
    o_ref[...] = jnp.concatenate([o[h] / l[h] for h in heads], axis=-1).reshape(o_ref.shape).astype(o_ref.dtype)
    lse_ref[...] = lse.reshape(lse_ref.shape)
    if has_prev:
        kp_ref[...] = kc_ref[...]
        vp_ref[...] = vc_ref[...]


def _band_attn_kernel(*refs, blocks_per_stream, prescaled):
    refs = list(refs)
    k = pl.program_id(1)
    groups = range(len(blocks_per_stream))
    ins = []
    for g in groups:
        n_in = 5 if blocks_per_stream[g] > 1 else 4
        ins.append(refs[:n_in])
        refs = refs[n_in:]
    outs = [refs[2 * g:2 * g + 2] for g in groups]
    refs = refs[2 * len(blocks_per_stream):]
    for g in groups:
        if blocks_per_stream[g] > 1:
            bc_ref, bp_ref, q_ref, kc_ref, vc_ref = ins[g]
            kp_ref, vp_ref = refs[:2]
            refs = refs[2:]

            @pl.when(k % blocks_per_stream[g] == 0)
            def _(kp_ref=kp_ref, vp_ref=vp_ref):
                kp_ref[...] = jnp.zeros_like(kp_ref)
                vp_ref[...] = jnp.zeros_like(vp_ref)
        else:
            (bc_ref, q_ref, kc_ref, vc_ref), bp_ref, kp_ref, vp_ref = ins[g], None, None, None
        _band_block(bc_ref, bp_ref, q_ref, kc_ref, vc_ref, outs[g][0], outs[g][1], kp_ref, vp_ref, prescaled[g])


def _band_bias(group):
    steps = _block_steps(group)
    back = (steps[:, None] - steps[None, :]).astype(np.float32)
    coef = (_alibi_slopes()[group] * np.float32(DIL_RATES[group]))[:, None, None]
    cur = np.where(back >= 0, -(coef * back), np.float32(MASK_VALUE)).astype(np.float32)
    back_p = back + np.float32(ATT_BLOCK)
    prev = np.where(back_p <= ATT_BLOCK, -(coef * back_p), np.float32(MASK_VALUE)).astype(np.float32)
    return cur, np.stack([np.full_like(prev, MASK_VALUE), prev])


def _band_attention(srcs, col_q, prescaled, batch):
    n_slabs = srcs[0].shape[0]
    table = (ATT_HEADS, ATT_BLOCK, ATT_BLOCK)
    in_specs, args, out_specs, out_shape, scratch, blocks_per_stream = [], [], [], [], [], []
    for g in range(N_DIL):
        slabs, rows = BLOCK_SHAPES[g]
        streams = SLABS // slabs
        nb = SLAB_ROWS // rows
        blocks_per_stream.append(nb)
        bias_c, bias_p = _band_bias(g)

        def cur(col, width=D_MODEL, slabs=slabs, rows=rows, streams=streams, nb=nb):
            return pl.BlockSpec((slabs, rows, width), lambda b, k: (b * streams + k // nb, k % nb, col))

        in_specs.append(pl.BlockSpec(table, lambda b, k: (0, 0, 0)))
        args.append(jnp.asarray(bias_c))
        if nb > 1:
            in_specs.append(pl.BlockSpec((None,) + table, lambda b, k, nb=nb: (jnp.minimum(k % nb, 1), 0, 0, 0)))
            args.append(jnp.asarray(bias_p))
            scratch += [pltpu.VMEM((slabs, rows, D_MODEL), srcs[g].dtype)] * 2
        in_specs += [cur(col_q[g]), cur(col_q[g] + 1), cur(col_q[g] + 2)]
        args += [srcs[g]] * 3
        out_specs += [cur(0), cur(0, 128)]
        out_shape += [jax.ShapeDtypeStruct((n_slabs, SLAB_ROWS, D_MODEL), BF16 if prescaled[g] else F32),
                      jax.ShapeDtypeStruct((n_slabs, SLAB_ROWS, 128), F32)]
    res = pl.pallas_call(
        functools.partial(_band_attn_kernel, blocks_per_stream=tuple(blocks_per_stream), prescaled=tuple(prescaled)),
        grid=(batch, SLABS),
        in_specs=in_specs,
        out_specs=out_specs,
        out_shape=out_shape,
        scratch_shapes=scratch,
        compiler_params=_params("parallel", "arbitrary"),
    )(*args)
    return [(res[2 * g], res[2 * g + 1]) for g in range(N_DIL)]


def _cached_attn_kernel(new_ref, c0_ref, c1_ref, c2_ref, coef_ref, o_ref, lse_ref, *, dec_seq):
    caches = (c0_ref, c1_ref, c2_ref)
    shared = [d == 1 for d in DIL_RATES]
    pairs = [(g, i) for g in range(N_DIL) for i in range(dec_seq)]
    res = lambda g, i: 0 if shared[g] else i
    lane = lax.broadcasted_iota(jnp.int32, (ATT_HEADS, ATT_BLOCK), 1)
    scale = HEAD_DIM ** -0.5
    q = {(g, i): new_ref[i, g, 0] for g, i in pairs}

    def qk_body(r, carry):
        out = []
        for n, (g, i) in enumerate(pairs):
            col = jnp.sum(q[g, i] * caches[g][r, res(g, i), 0], axis=-1, keepdims=True)
            out.append(jnp.where(lane == r, col, carry[n]))
        return tuple(out)

    zero = jnp.zeros((ATT_HEADS, ATT_BLOCK), F32)
    s_all = lax.fori_loop(0, ATT_BLOCK, qk_body, (zero,) * len(pairs), unroll=32)

    ps, p_news, ls, ms = [], [], [], []
    for n, (g, i) in enumerate(pairs):
        coef = coef_ref[g]
        if shared[g]:
            back = (ATT_BLOCK + i - lane).astype(F32)
            s = jnp.where(lane >= i, s_all[n] * scale - coef * back, MASK_VALUE)
            s_new = [jnp.sum(q[g, i] * new_ref[i2, g, 1], axis=-1, keepdims=True) * scale
                     - coef[:, :1] * float(i - i2) for i2 in range(i + 1)]
        else:
            s = s_all[n] * scale - coef * (ATT_BLOCK - lane).astype(F32)
            s_new = [jnp.sum(q[g, i] * new_ref[i, g, 1], axis=-1, keepdims=True) * scale]
        m = jnp.max(s, axis=-1, keepdims=True)
        for sn in s_new:
            m = jnp.maximum(m, sn)
        p = jnp.exp(s - m)
        p_new = [jnp.exp(sn - m) for sn in s_new]
        l = jnp.sum(p, axis=-1, keepdims=True)
        for pn in p_new:
            l = l + pn
        ps.append(p)
        p_news.append(p_new)
        ls.append(l)
        ms.append(m)

    def pv_body(r, carry):
        out = []
        for n, (g, i) in enumerate(pairs):
            col = jnp.sum(jnp.where(lane == r, ps[n], 0.0), axis=-1, keepdims=True)
            out.append(carry[n] + col * caches[g][r, res(g, i), 1])
        return tuple(out)

    zero_o = jnp.zeros((ATT_HEADS, HEAD_DIM), F32)
    o_all = lax.fori_loop(0, ATT_BLOCK, pv_body, (zero_o,) * len(pairs), unroll=32)
    for n, (g, i) in enumerate(pairs):
        o = o_all[n]
        for i2, pn in enumerate(p_news[n]):
            o = o + pn * new_ref[i2 if shared[g] else i, g, 2]
        o_ref[g, i] = o / ls[n]
        lse_ref[g, i] = jnp.broadcast_to(ms[n] + jnp.log(ls[n]), (ATT_HEADS, 128))


def _cached_attention(att_s, caches):
    db, dec_seq = att_s.shape[:2]
    views, specs = [], []
    for g, cache in enumerate(caches):
        dil, window = DIL_RATES[g], DIL_WINDOWS[g]
        assert cache.shape[1] == window and window == dil * ATT_BLOCK
        assert dil == 1 or dec_seq <= dil
        nres = 1 if dil == 1 else dec_seq
        views.append(cache.reshape(db, ATT_BLOCK, dil, 2, ATT_HEADS, HEAD_DIM))
        specs.append(pl.BlockSpec((None, ATT_BLOCK, nres, 2, ATT_HEADS, HEAD_DIM), lambda b: (b, 0, 0, 0, 0, 0)))
    coef = np.stack([np.repeat((_alibi_slopes()[g] * DIL_RATES[g])[:, None], 128, axis=1) for g in range(N_DIL)])
    out = lambda w: pl.BlockSpec((N_DIL, None, dec_seq, ATT_HEADS, w), lambda b: (0, b, 0, 0, 0))
    return pl.pallas_call(
        functools.partial(_cached_attn_kernel, dec_seq=dec_seq),
        grid=(db,),
        in_specs=[pl.BlockSpec((None, dec_seq, N_DIL, 3, ATT_HEADS, HEAD_DIM), lambda b: (b, 0, 0, 0, 0, 0))]
        + specs + [pl.BlockSpec((N_DIL, ATT_HEADS, 128), lambda b: (0, 0, 0))],
        out_specs=[out(HEAD_DIM), out(128)],
        out_shape=[jax.ShapeDtypeStruct((N_DIL, db, dec_seq, ATT_HEADS, HEAD_DIM), F32),
                   jax.ShapeDtypeStruct((N_DIL, db, dec_seq, ATT_HEADS, 128), F32)],
        compiler_params=_params("parallel"),
    )(att_s, *views, jnp.asarray(coef.astype(np.float32)))


def _merge_kernel(a_ref, ga_ref, gb_ref, o0_ref, o1_ref, o2_ref, l0_ref, l1_ref, l2_ref,
                  x_ref, w_ref, g_ref, h_ref, hn_ref, merged_ref):
    l0, l1, l2 = l0_ref[...], l1_ref[...], l2_ref[...]
    mx = jnp.maximum(jnp.maximum(l0, l1), l2)
    e0, e1, e2 = jnp.exp(l0 - mx), jnp.exp(l1 - mx), jnp.exp(l2 - mx)
    den = e0 + e1 + e2
    w0, w1, w2 = e0 / den, e1 / den, e2 / den
    for h in range(ATT_HEADS):
        hs = slice(h * HEAD_DIM, (h + 1) * HEAD_DIM)
        b_out = (w0[:, h:h + 1] * o0_ref[:, hs] + w1[:, h:h + 1] * o1_ref[:, hs].astype(F32)
                 + w2[:, h:h + 1] * o2_ref[:, hs].astype(F32))
        merged = ga_ref[:, hs].astype(F32) * a_ref[:, hs] + gb_ref[:, hs].astype(F32) * b_out
        merged_ref[:, hs] = merged.astype(merged_ref.dtype)
    h_new = x_ref[...] + jnp.dot(merged_ref[...], w_ref[...], preferred_element_type=F32)
    h_ref[...] = h_new
    hn_ref[...] = _rms(h_new, g_ref[...]).astype(hn_ref.dtype)


def _merge(a_out, gates, gate_col, outs, lses, x, w_out, norm_g, t, tm):
    row = lambda col: pl.BlockSpec((tm, D_MODEL), lambda i: (i, col))
    lrow = pl.BlockSpec((tm, 128), lambda i: (i, 0))
    p = gates
    return pl.pallas_call(
        _merge_kernel,
        grid=(t // tm,),
        in_specs=[row(0), row(gate_col), row(gate_col + 1), row(0), row(0), row(0), lrow, lrow, lrow,
                  row(0), pl.BlockSpec((D_MODEL, D_MODEL), lambda i: (0, 0)),
                  pl.BlockSpec((1, D_MODEL), lambda i: (0, 0))],
        out_specs=[row(0), row(0)],
        out_shape=[jax.ShapeDtypeStruct((x.shape[0], D_MODEL), F32), jax.ShapeDtypeStruct((t, D_MODEL), BF16)],
        scratch_shapes=[pltpu.VMEM((tm, D_MODEL), BF16)],
        compiler_params=_params("parallel"),
    )(a_out, p, p, outs[0], outs[1], outs[2], lses[0], lses[1], lses[2], x, w_out, norm_g.reshape(1, D_MODEL))


def _peer_score_kernel(hn_ref, wq_ref, keys_ref, s_ref):
    q = jnp.dot(hn_ref[...], wq_ref[...], preferred_element_type=F32).astype(BF16)
    nt = (((1,), (1,)), ((), ()))
    for hc in range(2 * PEER_HEADS):
        cs = slice(hc * PEER_HALF, (hc + 1) * PEER_HALF)
        s_ref[hc] = lax.dot_general(keys_ref[hc % 2], q[:, cs], nt, preferred_element_type=F32)


def _peer_scores(hn, w_q, sub_keys, tm):
    t = hn.shape[0]
    return pl.pallas_call(
        _peer_score_kernel,
        grid=(t // tm,),
        in_specs=[pl.BlockSpec((tm, D_MODEL), lambda i: (i, 0)),
                  pl.BlockSpec((D_MODEL, 2 * PEER_HEADS * PEER_HALF), lambda i: (0, 0)),
                  pl.BlockSpec((2, N_KEYS, PEER_HALF), lambda i: (0, 0, 0))],
        out_specs=pl.BlockSpec((2 * PEER_HEADS, N_KEYS, tm), lambda i: (0, 0, i)),
        out_shape=jax.ShapeDtypeStruct((2 * PEER_HEADS, N_KEYS, t), F32),
        compiler_params=_params("parallel"),
    )(hn, w_q, sub_keys)


def _take_top(arrays, order, count, sentinel):
    arrays = list(arrays)
    vals = [[] for _ in arrays]
    idxs = [[] for _ in arrays]
    for _ in range(count):
        for n, s in enumerate(arrays):
            m = jnp.max(s, axis=0, keepdims=True)
            pos = jnp.min(jnp.where(s == m, order, sentinel), axis=0, keepdims=True)
            vals[n].append(m)
            idxs[n].append(pos)
            arrays[n] = jnp.where(order == pos, NEG_INF, s)
    return vals, idxs


def _sort_network(n):
    pairs, p = [], 1
    while p < n:
        k = p
        while k >= 1:
            for j in range(k % p, n - k, 2 * k):
                for i in range(min(k, n - j - k)):
                    if (i + j) // (2 * p) == (i + j + k) // (2 * p):
                        pairs.append((i + j, i + j + k))
            k //= 2
        p *= 2
    return pairs


def _take_top_sorted(arrays, order, count, sentinel):
    tiles = arrays[0].shape[0] // 8
    assert all(a.shape[0] == 8 * tiles for a in arrays)
    val = [[a[8 * v:8 * v + 8] for v in range(tiles)] for a in arrays]
    idx = [[order[8 * v:8 * v + 8] for v in range(tiles)] for _ in arrays]
    network = [(i, j) for i, j in _sort_network(1 << (tiles - 1).bit_length()) if j < tiles]
    for i, j in network:
        for n in range(len(arrays)):
            vi, vj, ri, rj = val[n][i], val[n][j], idx[n][i], idx[n][j]
            swap = (vj > vi) | ((vj == vi) & (rj < ri))
            val[n][i], val[n][j] = jnp.where(swap, vj, vi), jnp.where(swap, vi, vj)
            idx[n][i], idx[n][j] = jnp.where(swap, rj, ri), jnp.where(swap, ri, rj)
    vals = [[] for _ in arrays]
    idxs = [[] for _ in arrays]
    for t in range(count):
        for n in range(len(arrays)):
            head, rank = val[n][0], idx[n][0]
            m = jnp.max(head, axis=0, keepdims=True)
            pos = jnp.min(jnp.where(head == m, rank, sentinel), axis=0, keepdims=True)
            vals[n].append(m)
            idxs[n].append(pos)
            popped = rank == pos
            for d in range(min(tiles, count - t - 1)):
                if d + 1 < tiles:
                    val[n][d] = jnp.where(popped, val[n][d + 1], val[n][d])
                    idx[n][d] = jnp.where(popped, idx[n][d + 1], idx[n][d])
                else:
                    val[n][d] = jnp.where(popped, NEG_INF, val[n][d])
                    idx[n][d] = jnp.where(popped, sentinel, idx[n][d])
    return vals, idxs


def _stack_rows(rows_list, krow):
    out = jnp.zeros(krow.shape, F32)
    for j, r in enumerate(rows_list):
        out = jnp.where(krow == float(j), r, out)
    return out


def _cand_layout():
    k = PEER_TOPK
    pieces = [("row_a", a, 16 if a == 0 else 8, 0, k // (a + 1)) for a in range(4)]
    pieces += [("col_b", 0, 16, 4, 16), ("col_b", 1, 8, 4, 8), ("col_b", 2, 8, 4, 5)]
    pos = []
    for kind, idx, rows, lo, hi in pieces:
        for r in range(rows):
            a, b = (idx, r) if kind == "row_a" else (r, idx)
            ok = lo <= r < hi and (a + 1) * (b + 1) <= k
            pos.append(a * k + b if ok else k * k)
    assert sorted(p for p in pos if p < k * k) == sorted(
        a * k + b for a in range(k) for b in range(k) if (a + 1) * (b + 1) <= k)
    return pieces, np.asarray(pos, np.float32)


def _route_kernel(s_ref, pos_ref, u_ref, v_ref, e1_ref, e2_ref, gate_ref, ub_ref, vb_ref):
    ub_ref[...] = u_ref[...].astype(ub_ref.dtype)
    vb_ref[...] = v_ref[...].astype(vb_ref.dtype)
    k = PEER_TOPK
    heads = s_ref.shape[0] // 2
    lanes = s_ref.shape[2]
    key_rank = lax.broadcasted_iota(jnp.int32, (N_KEYS, lanes), 0).astype(F32)
    vals, idxs = _take_top_sorted([s_ref[n] for n in range(2 * heads)], key_rank, k, float(N_KEYS))
    krow = lax.broadcasted_iota(jnp.int32, (k, lanes), 0).astype(F32)
    flat = pos_ref[...]
    cands, i1_all, i2_all = [], [], []
    for hd in range(heads):
        v1, v2 = vals[2 * hd], vals[2 * hd + 1]
        v1_all = _stack_rows(v1, krow)
        v2_all = _stack_rows(v2, krow)
        i1_all.append(_stack_rows(idxs[2 * hd], krow))
        i2_all.append(_stack_rows(idxs[2 * hd + 1], krow))
        parts = [v1[idx] + v2_all[:rows] if kind == "row_a" else v1_all[:rows] + v2[idx]
                 for kind, idx, rows, _, _ in _cand_layout()[0]]
        cands.append(jnp.where(flat < float(k * k), jnp.concatenate(parts, axis=0), NEG_INF))
    top_s, pos = _take_top_sorted(cands, flat, k, float(k * k))
    for hd in range(heads):
        e1, e2 = [], []
        for j in range(k):
            a = jnp.floor(pos[hd][j] * (1.0 / k))
            b = pos[hd][j] - a * k
            e1.append(jnp.sum(jnp.where(krow == a, i1_all[hd], 0.0), axis=0, keepdims=True))
            e2.append(jnp.sum(jnp.where(krow == b, i2_all[hd], 0.0), axis=0, keepdims=True))
        ex = jnp.exp(_stack_rows(top_s[hd], krow) - top_s[hd][0])
        rows = slice(hd * k, (hd + 1) * k)
        gate_ref[rows, :] = ex / jnp.sum(ex, axis=0, keepdims=True)
        e1_ref[rows, :] = _stack_rows(e1, krow)
        e2_ref[rows, :] = _stack_rows(e2, krow)


def _route(scores_t, tl, heads_per_step, peer_u, peer_v):
    t = scores_t.shape[2]
    kk = PEER_HEADS * PEER_TOPK
    n_head_steps = PEER_HEADS // heads_per_step
    steps = (t // tl) * n_head_steps
    n_blocks = 1 << (steps.bit_length() - 1)
    tab_rows = N_EXPERTS // n_blocks
    tab = pl.BlockSpec((tab_rows, D_MODEL), lambda i, h: (jnp.minimum(i * n_head_steps + h, n_blocks - 1), 0))
    out = pl.BlockSpec((heads_per_step * PEER_TOPK, tl), lambda i, h: (h, i))
    flat = jnp.asarray(np.repeat(_cand_layout()[1][:, None], tl, axis=1))
    return pl.pallas_call(
        _route_kernel,
        grid=(t // tl, n_head_steps),
        in_specs=[pl.BlockSpec((2 * heads_per_step, N_KEYS, tl), lambda i, h: (h, 0, i)),
                  pl.BlockSpec(flat.shape, lambda i, h: (0, 0)), tab, tab],
        out_specs=[out, out, out, tab, tab],
        out_shape=[jax.ShapeDtypeStruct((kk, t), F32)] * 3 + [jax.ShapeDtypeStruct((N_EXPERTS, D_MODEL), BF16)] * 2,
        compiler_params=_params("arbitrary", "arbitrary"),
    )(scores_t, flat, peer_u, peer_v)


def _expert_weight_kernel(e1_ref, e2_ref, gate_ref, g_ref, e1t_ref, e2t_ref, gt_ref):
    e1t_ref[...] = e1_ref[...].T
    e2t_ref[...] = e2_ref[...].T
    gt_ref[...] = gate_ref[...].T
    kk = e1_ref.shape[0]
    key = lax.broadcasted_iota(jnp.int32, (N_KEYS, kk), 0).astype(F32)
    nt = (((1,), (1,)), ((), ()))

    def body(t, carry):
        r1 = e1t_ref[pl.ds(t, 1), :]
        r2 = e2t_ref[pl.ds(t, 1), :]
        gr = gt_ref[pl.ds(t, 1), :]
        a_t = jnp.where(key == r1, 1.0, 0.0).astype(BF16)
        b_t = jnp.where(key == r2, gr, 0.0).astype(BF16)
        g_ref[t] = lax.dot_general(a_t, b_t, nt, preferred_element_type=F32).astype(g_ref.dtype)
        return carry

    lax.fori_loop(0, g_ref.shape[0], body, 0, unroll=32)


def _expert_weights(e1, e2, gate, tl):
    kk, t = e1.shape
    slot = pl.BlockSpec((kk, tl), lambda i: (0, i))
    return pl.pallas_call(
        _expert_weight_kernel,
        grid=(t // tl,),
        in_specs=[slot, slot, slot],
        out_specs=pl.BlockSpec((tl, N_KEYS, N_KEYS), lambda i: (i, 0, 0)),
        out_shape=jax.ShapeDtypeStruct((t, N_KEYS, N_KEYS), F32),
        scratch_shapes=[pltpu.VMEM((tl, kk), F32)] * 3,
        compiler_params=_params("parallel"),
    )(e1, e2, gate)


def _peer_kernel(hn_ref, u_ref, v_ref, g_ref, y_ref):
    e = pl.program_id(1)
    nt = (((1,), (1,)), ((), ()))
    hk = lax.dot_general(hn_ref[...], u_ref[...], nt, preferred_element_type=F32)
    g = jnp.swapaxes(g_ref[...], 0, 1)
    act = [(jax.nn.gelu(hk[:, a * N_KEYS:(a + 1) * N_KEYS]) * g[a]).astype(BF16) for a in range(g.shape[0])]
    out = jnp.dot(jnp.concatenate(act, axis=-1), v_ref[...], preferred_element_type=F32)

    @pl.when(e == 0)
    def _():
        y_ref[...] = out

    @pl.when(e > 0)
    def _():
        y_ref[...] += out


def _peer(hn, u, v, g, t, rows_out, tm, te):
    row = pl.BlockSpec((tm, D_MODEL), lambda i, e: (i, 0), pipeline_mode=pl.Buffered(1))
    tab = pl.BlockSpec((te, D_MODEL), lambda i, e: (e, 0))
    return pl.pallas_call(
        _peer_kernel,
        grid=(t // tm, N_EXPERTS // te),
        in_specs=[row, tab, tab, pl.BlockSpec((tm, te // N_KEYS, N_KEYS), lambda i, e: (i, e, 0))],
        out_specs=row,
        out_shape=jax.ShapeDtypeStruct((rows_out, D_MODEL), F32),
        compiler_params=_params("parallel", "arbitrary"),
    )(hn, u, v, g)


def _final_norm_prompt_kernel(h_ref, f_ref, g_ref, y_ref):
    y = [_rms(h_ref[rr % 4, rr // 4] + f_ref[rr % 4, rr // 4], g_ref[...]) for rr in range(8)]
    y_ref[...] = jnp.swapaxes(jnp.stack(y, axis=0), 0, 1)


def _final_norm_prompt(h, f, g, batch, seq):
    rows = 64
    h5 = h.reshape(batch + 1, 4, 4, SLAB_ROWS, D_MODEL)
    f5 = f.reshape(batch + 1, 4, 4, SLAB_ROWS, D_MODEL)
    blk = pl.BlockSpec((None, 4, 2, rows, D_MODEL), lambda b, hh, j: (b, 0, hh, j, 0))
    y = pl.pallas_call(
        _final_norm_prompt_kernel,
        grid=(batch, 2, SLAB_ROWS // rows),
        in_specs=[blk, blk, pl.BlockSpec((1, D_MODEL), lambda b, hh, j: (0, 0))],
        out_specs=pl.BlockSpec((None, rows, 8, D_MODEL), lambda b, hh, j: (b, j, hh, 0)),
        out_shape=jax.ShapeDtypeStruct((batch, SLAB_ROWS, SLABS, D_MODEL), F32),
        compiler_params=_params("parallel", "parallel", "parallel"),
    )(h5, f5, g.reshape(1, D_MODEL))
    return y.reshape(batch, seq, D_MODEL)


def _final_norm_rows_kernel(h_ref, f_ref, g_ref, y_ref):
    y_ref[...] = _rms(h_ref[...] + f_ref[...], g_ref[...])


def _final_norm_rows(h, f, g, row_block, rows):
    blk = pl.BlockSpec((rows, D_MODEL), lambda i: (row_block, 0))
    return pl.pallas_call(
        _final_norm_rows_kernel,
        grid=(1,),
        in_specs=[blk, blk, pl.BlockSpec((1, D_MODEL), lambda i: (0, 0))],
        out_specs=pl.BlockSpec((rows, D_MODEL), lambda i: (0, 0)),
        out_shape=jax.ShapeDtypeStruct((rows, D_MODEL), F32),
        compiler_params=_params("arbitrary"),
    )(h, f, g.reshape(1, D_MODEL))


def _kv_prompt_kernel(*refs):
    ins, outs = refs[:2 * N_DIL], refs[2 * N_DIL:]
    for g in range(N_DIL):
        for kv in range(2):
            src = ins[2 * g + kv]
            heads = [src[0, :, h * HEAD_DIM:(h + 1) * HEAD_DIM] for h in range(ATT_HEADS)]
            outs[g][:, kv, :, :] = jnp.swapaxes(jnp.stack(heads, axis=0), 0, 1)


def _kv_prompt(p3, batch, seq):
    in_specs, out_specs, out_shape = [], [], []
    residue = lambda s: (s % 4) * 4 + s // 4
    for g in range(N_DIL):
        steps = min(DIL_WINDOWS[g], seq) // SLABS
        last = SLAB_ROWS // steps - 1
        for kv in range(2):
            col = COL_ATT + 3 * g + 1 + kv
            in_specs.append(pl.BlockSpec((1, steps, D_MODEL),
                                         lambda b, s, last=last, col=col: (b * SLABS + s, last, col)))
        out_specs.append(pl.BlockSpec((None, steps, None, 2, ATT_HEADS, HEAD_DIM),
                                      lambda b, s: (b, 0, residue(s), 0, 0, 0)))
        out_shape.append(jax.ShapeDtypeStruct((batch, steps, SLABS, 2, ATT_HEADS, HEAD_DIM), F32))
    outs = pl.pallas_call(
        _kv_prompt_kernel,
        grid=(batch, SLABS),
        in_specs=in_specs,
        out_specs=out_specs,
        out_shape=out_shape,
        compiler_params=_params("parallel", "parallel"),
    )(*([p3] * len(in_specs)))
    return [o.reshape(1, batch, -1, 2, ATT_HEADS, HEAD_DIM) for o in outs]


def _row_tile(t, candidates):
    for c in candidates:
        if t % c == 0:
            return c
    raise ValueError(f"no row tile for {t} tokens")


def kernel(x_prompt, x_sample, cache_kv_w128, cache_kv_w512, cache_kv_w2048, norm_mix_g, w_in, sgu_norm_g, sgu_w, sgu_b, w_out, norm_ffn_g, peer_w_q, peer_sub_keys, peer_u, peer_v, norm_final_g):
    batch, seq, _ = x_prompt.shape
    db, ds, _ = x_sample.shape
    assert w_in.shape[0] == 1 and db * ds == CHUNK and seq == SLABS * SLAB_ROWS
    caches = (cache_kv_w128, cache_kv_w512, cache_kv_w2048)
    tp, ts = batch * seq, db * ds
    t = tp + ts
    n_slabs = t // SLAB_ROWS
    tm_big = _row_tile(t, (1040, 640, 128))
    tm_lane = _row_tile(t, (640, 128))

    xs = x_sample.reshape(ts, D_MODEL)
    xl, xn = _permute_norm(x_prompt, norm_mix_g[0])
    xl = lax.dynamic_update_slice(xl, xs, (tp, 0))
    xn = lax.dynamic_update_slice(xn, _rmsnorm_bf16(xs, norm_mix_g[0]), (tp, 0))

    p, att16 = _inproj(xn, w_in[0], t, tm_big, 1024)
    p3 = p.reshape(n_slabs, SLAB_ROWS, p.shape[1])
    att16_3 = att16.reshape(n_slabs, SLAB_ROWS, att16.shape[1])

    w_tril = sgu_w[0] * jnp.tril(jnp.ones((CHUNK, CHUNK), F32))
    tau = _block_steps(0)
    w_p = w_tril[:, tau][:, :, tau].astype(BF16)
    b_p = sgu_b[0][:, tau][..., None]
    w_s = jnp.einsum("bc,gis->gbics", jnp.eye(db, dtype=F32), w_tril[:, :ds, :ds]).reshape(SGU_GROUPS, ts, ts)
    b_s = jnp.tile(sgu_b[0][:, :ds], (1, db))[..., None]
    a_out = _sgu_prompt(p3, sgu_norm_g[0], w_p, b_p, batch).reshape(t, D_MODEL)
    a_s, vn_s = _sgu_sample(p3, sgu_norm_g[0], w_s.astype(BF16), b_s, tp // SLAB_ROWS)
    a_out = lax.dynamic_update_slice(a_out, a_s, (tp, 0))

    att_s = p[tp:, COL_ATT * D_MODEL:COL_GATE * D_MODEL].reshape(db, ds, N_DIL, 3, ATT_HEADS, HEAD_DIM)
    outs, lses = [], []
    o_s, lse_s = _cached_attention(att_s, [c[0] for c in caches])
    band = _band_attention([p3, att16_3, att16_3], [COL_ATT, 0, 3], [False, True, True], batch)
    for g in range(N_DIL):
        o_p, lse_p = band[g]
        lse_g = jnp.pad(lse_s[g, ..., 0].reshape(ts, ATT_HEADS), ((0, 0), (0, 128 - ATT_HEADS)))
        o_g = o_s[g].reshape(ts, D_MODEL).astype(o_p.dtype)
        outs.append(lax.dynamic_update_slice(o_p.reshape(t, D_MODEL), o_g, (tp, 0)))
        lses.append(lax.dynamic_update_slice(lse_p.reshape(t, 128), lse_g, (tp, 0)))

    h, hn = _merge(a_out, att16, COL_GATE - COL_ATT16, outs, lses, xl, _to_bf16(w_out[0], 1024, D_MODEL),
                   norm_ffn_g[0], t, _row_tile(t, (320, 128)))

    scores_t = _peer_scores(hn, _to_bf16(peer_w_q[0], 1024, D_MODEL), peer_sub_keys[0].astype(BF16), tm_lane)
    e1, e2, gate, u_b16, v_b16 = _route(scores_t, 128, 8, peer_u[0], peer_v[0])
    g_dense = _expert_weights(e1, e2, gate, 128)
    f = _peer(hn, u_b16, v_b16, g_dense, t, h.shape[0], tm_big, 1024)

    y_prompt = _final_norm_prompt(h, f, norm_final_g, batch, seq)
    y_sample = _final_norm_rows(h, f, norm_final_g, tp // ts, ts).reshape(db, ds, D_MODEL)
    kv_prompt = _kv_prompt(p3, batch, seq)
    kv_sample = [att_s[:, :, g, 1:3][None] for g in range(N_DIL)]
    sgu_v_sample = vn_s.reshape(1, db, ds, D_MODEL)
    return (y_prompt, y_sample, kv_prompt[0], kv_prompt[1], kv_prompt[2],
            kv_sample[0], kv_sample[1], kv_sample[2], sgu_v_sample)
```

```python
import functools

import numpy as np
import jax
import jax.numpy as jnp
from jax import lax
from jax.experimental import pallas as pl
from jax.experimental.pallas import tpu as pltpu

F32 = jnp.float32
BF16 = jnp.bfloat16

D_MODEL = 2048
HEAD_DIM = 128
ATT_HEADS = D_MODEL // HEAD_DIM
N_DIL = 3
DIL_WINDOWS = (128, 512, 2048)
DIL_RATES = (1, 4, 16)
ATT_BLOCK = 128
SLABS = DIL_RATES[-1]
SLAB_ROWS = 128
SGU_GROUPS = 8
SGU_GROUP_DIM = D_MODEL // SGU_GROUPS
CHUNK = 128
N_COL_BLOCKS = 13
COL_U, COL_V, COL_ATT, COL_GATE = 0, 1, 2, 11
COL_ATT16 = COL_ATT + 3
PEER_HEADS = 8
PEER_TOPK = 16
N_KEYS = 128
N_EXPERTS = N_KEYS * N_KEYS
PEER_HALF = 128
NORM_EPS = 1e-6
MASK_VALUE = -1e30
NEG_INF = float("-inf")
VMEM_LIMIT = 56 * 1024 * 1024

BLOCK_SHAPES = ((16, 8), (4, 32), (1, 128))


def _alibi_slopes():
    n = N_DIL * ATT_HEADS
    e = np.arange(1, n + 1, dtype=np.float32)
    return np.exp2(np.float32(-8.0) * e / np.float32(n)).astype(np.float32).reshape(N_DIL, ATT_HEADS)


def _block_steps(group):
    slabs, rows = BLOCK_SHAPES[group]
    n = np.arange(slabs * rows)
    s, j = n // rows, n % rows
    if group == 0:
        return j * 16 + (s % 4) * 4 + s // 4
    if group == 1:
        return j * 4 + s
    return j


def _params(*sem):
    return pltpu.CompilerParams(dimension_semantics=sem, vmem_limit_bytes=VMEM_LIMIT)


def _rms(x, g):
    return x * lax.rsqrt(jnp.mean(x * x, axis=-1, keepdims=True) + NORM_EPS) * g


def _permute_norm_kernel(x_ref, g_ref, xl_ref, xn_ref):
    xs = jnp.swapaxes(x_ref[...], 0, 1)
    for rr in range(8):
        x = xs[rr]
        xl_ref[rr % 4, rr // 4] = x
        xn_ref[rr % 4, rr // 4] = _rms(x, g_ref[...]).astype(xn_ref.dtype)


def _permute_norm(x_prompt, g):
    batch, seq, _ = x_prompt.shape
    x4 = x_prompt.reshape(batch, SLAB_ROWS, SLABS, D_MODEL)
    out = pl.BlockSpec((None, 4, 2, SLAB_ROWS, D_MODEL), lambda b, h: (b, 0, h, 0, 0))
    xl, xn = pl.pallas_call(
        _permute_norm_kernel,
        grid=(batch, 2),
        in_specs=[pl.BlockSpec((None, SLAB_ROWS, 8, D_MODEL), lambda b, h: (b, 0, h, 0)),
                  pl.BlockSpec((1, D_MODEL), lambda b, h: (0, 0))],
        out_specs=[out, out],
        out_shape=[jax.ShapeDtypeStruct((batch + 1, 4, 4, SLAB_ROWS, D_MODEL), F32),
                   jax.ShapeDtypeStruct((batch + 1, 4, 4, SLAB_ROWS, D_MODEL), BF16)],
        compiler_params=_params("parallel", "parallel"),
    )(x4, g.reshape(1, D_MODEL))
    rows = (batch + 1) * seq
    return xl.reshape(rows, D_MODEL), xn.reshape(rows, D_MODEL)


def _rmsnorm_kernel(x_ref, g_ref, o_ref):
    o_ref[...] = _rms(x_ref[...], g_ref[...]).astype(o_ref.dtype)


def _rmsnorm_bf16(x, g):
    return pl.pallas_call(
        _rmsnorm_kernel,
        out_shape=jax.ShapeDtypeStruct(x.shape, BF16),
    )(x, g.reshape(1, D_MODEL))


def _cast_kernel(x_ref, o_ref):
    o_ref[...] = x_ref[...].astype(o_ref.dtype)


def _to_bf16(w, rows, cols):
    r, c = w.shape
    return pl.pallas_call(
        _cast_kernel,
        grid=(r // rows, c // cols),
        in_specs=[pl.BlockSpec((rows, cols), lambda i, j: (i, j))],
        out_specs=pl.BlockSpec((rows, cols), lambda i, j: (i, j)),
        out_shape=jax.ShapeDtypeStruct((r, c), BF16),
        compiler_params=_params("parallel", "parallel"),
    )(w)


def _inproj_kernel(x_ref, w_ref, o_ref, att_ref, w16_ref, *, blocks_per_col):
    cb = pl.program_id(0) // blocks_per_col

    @pl.when(pl.program_id(1) == 0)
    def _():
        w16_ref[...] = w_ref[...].astype(w16_ref.dtype)

    p = jnp.dot(x_ref[...], w16_ref[...], preferred_element_type=F32)

    @pl.when(cb < COL_ATT)
    def _():
        o_ref[...] = jax.nn.gelu(p)

    @pl.when((cb >= COL_ATT) & (cb < COL_ATT16))
    def _():
        o_ref[...] = p

    @pl.when((cb >= COL_ATT16) & (cb < COL_GATE))
    def _():
        o_ref[...] = p
        is_q = (cb - COL_ATT) % 3 == 0
        att_ref[...] = (p * jnp.where(is_q, HEAD_DIM ** -0.5, 1.0)).astype(att_ref.dtype)

    @pl.when(cb >= COL_GATE)
    def _():
        att_ref[...] = jax.nn.sigmoid(p).astype(att_ref.dtype)


def _inproj(xn, w, t, tm, tn):
    n = w.shape[1]
    per_col = D_MODEL // tn
    n_rows = t // tm
    n_f32 = COL_GATE * per_col
    first_b16 = COL_ATT16 * per_col
    n_b16 = (N_COL_BLOCKS - COL_ATT16) * per_col
    f32_block = lambda j, i: (jnp.where(j < n_f32, i, n_rows - 1), jnp.minimum(j, n_f32 - 1))
    b16_block = lambda j, i: (jnp.where(j >= first_b16, i, 0), jnp.maximum(j - first_b16, 0))
    return pl.pallas_call(
        functools.partial(_inproj_kernel, blocks_per_col=per_col),
        grid=(n // tn, n_rows),
        in_specs=[pl.BlockSpec((tm, D_MODEL), lambda j, i: (i, 0)),
                  pl.BlockSpec((D_MODEL, tn), lambda j, i: (0, j))],
        out_specs=[pl.BlockSpec((tm, tn), f32_block),
                   pl.BlockSpec((tm, tn), b16_block)],
        out_shape=[jax.ShapeDtypeStruct((t, n_f32 * tn), F32),
                   jax.ShapeDtypeStruct((t, n_b16 * tn), BF16)],
        scratch_shapes=[pltpu.VMEM((D_MODEL, tn), BF16)],
        compiler_params=_params("arbitrary", "arbitrary"),
    )(xn, w)


def _sgu_chunk(u, v, g_ref, w_ref, b_ref):
    vc = v - jnp.mean(v, axis=-1, keepdims=True)
    vn = vc * lax.rsqrt(jnp.mean(vc * vc, axis=-1, keepdims=True) + NORM_EPS) * g_ref[...]
    cols = []
    for g in range(SGU_GROUPS):
        cs = slice(g * SGU_GROUP_DIM, (g + 1) * SGU_GROUP_DIM)
        mix = jnp.dot(w_ref[g], vn[:, cs].astype(BF16), preferred_element_type=F32) + b_ref[g]
        cols.append(u[:, cs] * mix)
    return jnp.concatenate(cols, axis=-1), vn


def _sgu_kernel(u_ref, v_ref, g_ref, w_ref, b_ref, a_ref, vn_ref):
    a, vn = _sgu_chunk(u_ref[...].reshape(CHUNK, D_MODEL), v_ref[...].reshape(CHUNK, D_MODEL), g_ref, w_ref, b_ref)
    a_ref[...] = a.reshape(a_ref.shape)
    vn_ref[...] = vn


def _sgu_prompt_kernel(u_ref, v_ref, g_ref, w_ref, b_ref, a_ref, *, rows):
    for c in range(u_ref.shape[1] // rows):
        rs = slice(c * rows, (c + 1) * rows)
        a, _ = _sgu_chunk(u_ref[:, rs, :].reshape(CHUNK, D_MODEL), v_ref[:, rs, :].reshape(CHUNK, D_MODEL),
                          g_ref, w_ref, b_ref)
        a_ref[:, rs, :] = a.reshape(a_ref.shape[0], rows, D_MODEL)


def _sgu_prompt(p3, sgu_norm_g, w, b, batch, chunks_per_step=4):
    slabs, rows = BLOCK_SHAPES[0]
    n_chunks = SLAB_ROWS // (rows * chunks_per_step)
    blk = lambda col: pl.BlockSpec((slabs, rows * chunks_per_step, D_MODEL), lambda bb, c: (bb, c, col))
    return pl.pallas_call(
        functools.partial(_sgu_prompt_kernel, rows=rows),
        grid=(batch, n_chunks),
        in_specs=[blk(COL_U), blk(COL_V),
                  pl.BlockSpec((1, D_MODEL), lambda bb, c: (0, 0)),
                  pl.BlockSpec((SGU_GROUPS, CHUNK, CHUNK), lambda bb, c: (0, 0, 0)),
                  pl.BlockSpec((SGU_GROUPS, CHUNK, 1), lambda bb, c: (0, 0, 0))],
        out_specs=blk(0),
        out_shape=jax.ShapeDtypeStruct((p3.shape[0], SLAB_ROWS, D_MODEL), F32),
        compiler_params=_params("parallel", "parallel"),
    )(p3, p3, sgu_norm_g.reshape(1, D_MODEL), w, b)


def _sgu_sample(p3, sgu_norm_g, w, b, slab):
    blk = lambda col: pl.BlockSpec((1, SLAB_ROWS, D_MODEL), lambda i: (slab, 0, col))
    return pl.pallas_call(
        _sgu_kernel,
        grid=(1,),
        in_specs=[blk(COL_U), blk(COL_V),
                  pl.BlockSpec((1, D_MODEL), lambda i: (0, 0)),
                  pl.BlockSpec((SGU_GROUPS, CHUNK, CHUNK), lambda i: (0, 0, 0)),
                  pl.BlockSpec((SGU_GROUPS, CHUNK, 1), lambda i: (0, 0, 0))],
        out_specs=[pl.BlockSpec((CHUNK, D_MODEL), lambda i: (0, 0)),
                   pl.BlockSpec((CHUNK, D_MODEL), lambda i: (0, 0))],
        out_shape=[jax.ShapeDtypeStruct((CHUNK, D_MODEL), F32), jax.ShapeDtypeStruct((CHUNK, D_MODEL), F32)],
        compiler_params=_params("arbitrary"),
    )(p3, p3, sgu_norm_g.reshape(1, D_MODEL), w, b)


def _band_block(bc_ref, bp_ref, q_ref, kc_ref, vc_ref, o_ref, lse_ref, kp_ref, vp_ref, prescaled):
    has_prev = kp_ref is not None
    n = ATT_BLOCK
    scale = HEAD_DIM ** -0.5
    nt = (((1,), (1,)), ((), ()))
    heads = range(ATT_HEADS)
    hs = [slice(h * HEAD_DIM, (h + 1) * HEAD_DIM) for h in heads]
    ld = lambda ref, h: ref[:, :, hs[h]].reshape(n, HEAD_DIM)
    q = [ld(q_ref, h) if prescaled else (ld(q_ref, h) * scale).astype(BF16) for h in heads]
    if has_prev:
        keys = [jnp.concatenate([kp_ref[:, hs[h]], ld(kc_ref, h).astype(BF16)], axis=0) for h in heads]
        vals = [jnp.concatenate([vp_ref[:, hs[h]], ld(vc_ref, h).astype(BF16)], axis=0) for h in heads]
        bias = [jnp.concatenate([bp_ref[h], bc_ref[h]], axis=1) for h in heads]
    else:
        keys = [ld(kc_ref, h).astype(BF16) for h in heads]
        vals = [ld(vc_ref, h).astype(BF16) for h in heads]
        bias = [bc_ref[h] for h in heads]
    s = [lax.dot_general(q[h], keys[h], nt, preferred_element_type=F32) + bias[h] for h in heads]
    m = [jnp.max(s[h], axis=-1, keepdims=True) for h in heads]
    p = [jnp.exp(s[h] - m[h]) for h in heads]
    l = [jnp.sum(p[h], axis=-1, keepdims=True) for h in heads]
    o = [jnp.dot(p[h].astype(BF16), vals[h], preferred_element_type=F32) for h in heads]
    lane = lax.broadcasted_iota(jnp.int32, (n, 128), 1)
    lse = jnp.zeros((n, 128), F32)
    for h in heads:
        lse = jnp.where(lane == h, m[h] + jnp.log(l[h]), lse)
    o_ref[...] = jnp.concatenate([o[h] / l[h] for h in heads], axis=-1).reshape(o_ref.shape).astype(o_ref.dtype)
    lse_ref[...] = lse.reshape(lse_ref.shape)
    if has_prev:
        kp_ref[...] = kc_ref[...].reshape(n, D_MODEL).astype(BF16)
        vp_ref[...] = vc_ref[...].reshape(n, D_MODEL).astype(BF16)


def _band_attn_kernel(*refs, blocks_per_stream, prescaled):
    refs = list(refs)
    k = pl.program_id(1)
    groups = range(len(blocks_per_stream))
    ins = []
    for g in groups:
        n_in = 5 if blocks_per_stream[g] > 1 else 4
        ins.append(refs[:n_in])
        refs = refs[n_in:]
    outs = [refs[2 * g:2 * g + 2] for g in groups]
    refs = refs[2 * len(blocks_per_stream):]
    for g in groups:
        if blocks_per_stream[g] > 1:
            bc_ref, bp_ref, q_ref, kc_ref, vc_ref = ins[g]
            kp_ref, vp_ref = refs[:2]
            refs = refs[2:]

            @pl.when(k % blocks_per_stream[g] == 0)
            def _(kp_ref=kp_ref, vp_ref=vp_ref):
                kp_ref[...] = jnp.zeros_like(kp_ref)
                vp_ref[...] = jnp.zeros_like(vp_ref)
        else:
            (bc_ref, q_ref, kc_ref, vc_ref), bp_ref, kp_ref, vp_ref = ins[g], None, None, None
        _band_block(bc_ref, bp_ref, q_ref, kc_ref, vc_ref, outs[g][0], outs[g][1], kp_ref, vp_ref, prescaled[g])


def _band_bias(group):
    steps = _block_steps(group)
    back = (steps[:, None] - steps[None, :]).astype(np.float32)
    coef = (_alibi_slopes()[group] * np.float32(DIL_RATES[group]))[:, None, None]
    cur = np.where(back >= 0, -(coef * back), np.float32(MASK_VALUE)).astype(np.float32)
    back_p = back + np.float32(ATT_BLOCK)
    prev = np.where(back_p <= ATT_BLOCK, -(coef * back_p), np.float32(MASK_VALUE)).astype(np.float32)
    return cur, np.stack([np.full_like(prev, MASK_VALUE), prev])


def _band_attention(srcs, col_q, prescaled, batch):
    n_slabs = srcs[0].shape[0]
    table = (ATT_HEADS, ATT_BLOCK, ATT_BLOCK)
    in_specs, args, out_specs, out_shape, scratch, blocks_per_stream = [], [], [], [], [], []
    for g in range(N_DIL):
        slabs, rows = BLOCK_SHAPES[g]
        streams = SLABS // slabs
        nb = SLAB_ROWS // rows
        blocks_per_stream.append(nb)
        bias_c, bias_p = _band_bias(g)

        def cur(col, width=D_MODEL, slabs=slabs, rows=rows, streams=streams, nb=nb):
            return pl.BlockSpec((slabs, rows, width), lambda b, k: (b * streams + k // nb, k % nb, col))

        in_specs.append(pl.BlockSpec(table, lambda b, k: (0, 0, 0)))
        args.append(jnp.asarray(bias_c))
        if nb > 1:
            in_specs.append(pl.BlockSpec((None,) + table, lambda b, k, nb=nb: (jnp.minimum(k % nb, 1), 0, 0, 0)))
            args.append(jnp.asarray(bias_p))
            scratch += [pltpu.VMEM((ATT_BLOCK, D_MODEL), BF16)] * 2
        in_specs += [cur(col_q[g]), cur(col_q[g] + 1), cur(col_q[g] + 2)]
        args += [srcs[g]] * 3
        out_specs += [cur(0), cur(0, 128)]
        out_shape += [jax.ShapeDtypeStruct((n_slabs, SLAB_ROWS, D_MODEL), BF16 if prescaled[g] else F32),
                      jax.ShapeDtypeStruct((n_slabs, SLAB_ROWS, 128), F32)]
    res = pl.pallas_call(
        functools.partial(_band_attn_kernel, blocks_per_stream=tuple(blocks_per_stream), prescaled=tuple(prescaled)),
        grid=(batch, SLABS),
        in_specs=in_specs,
        out_specs=out_specs,
        out_shape=out_shape,
        scratch_shapes=scratch,
        compiler_params=_params("parallel", "arbitrary"),
    )(*args)
    return [(res[2 * g], res[2 * g + 1]) for g in range(N_DIL)]


def _cached_attn_kernel(new_ref, c0_ref, c1_ref, c2_ref, coef_ref, o_ref, lse_ref, *, dec_seq):
    caches = (c0_ref, c1_ref, c2_ref)
    shared = [d == 1 for d in DIL_RATES]
    pairs = [(g, i) for g in range(N_DIL) for i in range(dec_seq)]
    res = lambda g, i: 0 if shared[g] else i
    lane = lax.broadcasted_iota(jnp.int32, (ATT_HEADS, ATT_BLOCK), 1)
    scale = HEAD_DIM ** -0.5
    q = {(g, i): new_ref[i, g, 0] for g, i in pairs}

    def qk_body(r, carry):
        out = []
        for n, (g, i) in enumerate(pairs):
            col = jnp.sum(q[g, i] * caches[g][r, res(g, i), 0], axis=-1, keepdims=True)
            out.append(jnp.where(lane == r, col, carry[n]))
        return tuple(out)

    zero = jnp.zeros((ATT_HEADS, ATT_BLOCK), F32)
    s_all = lax.fori_loop(0, ATT_BLOCK, qk_body, (zero,) * len(pairs), unroll=32)

    ps, p_news, ls, ms = [], [], [], []
    for n, (g, i) in enumerate(pairs):
        coef = coef_ref[g]
        if shared[g]:
            back = (ATT_BLOCK + i - lane).astype(F32)
            s = jnp.where(lane >= i, s_all[n] * scale - coef * back, MASK_VALUE)
            s_new = [jnp.sum(q[g, i] * new_ref[i2, g, 1], axis=-1, keepdims=True) * scale
                     - coef[:, :1] * float(i - i2) for i2 in range(i + 1)]
        else:
            s = s_all[n] * scale - coef * (ATT_BLOCK - lane).astype(F32)
            s_new = [jnp.sum(q[g, i] * new_ref[i, g, 1], axis=-1, keepdims=True) * scale]
        m = jnp.max(s, axis=-1, keepdims=True)
        for sn in s_new:
            m = jnp.maximum(m, sn)
        p = jnp.exp(s - m)
        p_new = [jnp.exp(sn - m) for sn in s_new]
        l = jnp.sum(p, axis=-1, keepdims=True)
        for pn in p_new:
            l = l + pn
        ps.append(p)
        p_news.append(p_new)
        ls.append(l)
        ms.append(m)

    def pv_body(r, carry):
        out = []
        for n, (g, i) in enumerate(pairs):
            col = jnp.sum(jnp.where(lane == r, ps[n], 0.0), axis=-1, keepdims=True)
            out.append(carry[n] + col * caches[g][r, res(g, i), 1])
        return tuple(out)

    zero_o = jnp.zeros((ATT_HEADS, HEAD_DIM), F32)
    o_all = lax.fori_loop(0, ATT_BLOCK, pv_body, (zero_o,) * len(pairs), unroll=32)
    for n, (g, i) in enumerate(pairs):
        o = o_all[n]
        for i2, pn in enumerate(p_news[n]):
            o = o + pn * new_ref[i2 if shared[g] else i, g, 2]
        o_ref[g, i] = o / ls[n]
        lse_ref[g, i] = jnp.broadcast_to(ms[n] + jnp.log(ls[n]), (ATT_HEADS, 128))


def _cached_attention(att_s, caches):
    db, dec_seq = att_s.shape[:2]
    views, specs = [], []
    for g, cache in enumerate(caches):
        dil, window = DIL_RATES[g], DIL_WINDOWS[g]
        assert cache.shape[1] == window and window == dil * ATT_BLOCK
        assert dil == 1 or dec_seq <= dil
        nres = 1 if dil == 1 else dec_seq
        views.append(cache.reshape(db, ATT_BLOCK, dil, 2, ATT_HEADS, HEAD_DIM))
        specs.append(pl.BlockSpec((None, ATT_BLOCK, nres, 2, ATT_HEADS, HEAD_DIM), lambda b: (b, 0, 0, 0, 0, 0)))
    coef = np.stack([np.repeat((_alibi_slopes()[g] * DIL_RATES[g])[:, None], 128, axis=1) for g in range(N_DIL)])
    out = lambda w: pl.BlockSpec((N_DIL, None, dec_seq, ATT_HEADS, w), lambda b: (0, b, 0, 0, 0))
    return pl.pallas_call(
        functools.partial(_cached_attn_kernel, dec_seq=dec_seq),
        grid=(db,),
        in_specs=[pl.BlockSpec((None, dec_seq, N_DIL, 3, ATT_HEADS, HEAD_DIM), lambda b: (b, 0, 0, 0, 0, 0))]
        + specs + [pl.BlockSpec((N_DIL, ATT_HEADS, 128), lambda b: (0, 0, 0))],
        out_specs=[out(HEAD_DIM), out(128)],
        out_shape=[jax.ShapeDtypeStruct((N_DIL, db, dec_seq, ATT_HEADS, HEAD_DIM), F32),
                   jax.ShapeDtypeStruct((N_DIL, db, dec_seq, ATT_HEADS, 128), F32)],
        compiler_params=_params("parallel"),
    )(att_s, *views, jnp.asarray(coef.astype(np.float32)))


def _merge_kernel(a_ref, ga_ref, gb_ref, o0_ref, o1_ref, o2_ref, l0_ref, l1_ref, l2_ref,
                  x_ref, w_ref, g_ref, h_ref, hn_ref, merged_ref):
    l0, l1, l2 = l0_ref[...], l1_ref[...], l2_ref[...]
    mx = jnp.maximum(jnp.maximum(l0, l1), l2)
    e0, e1, e2 = jnp.exp(l0 - mx), jnp.exp(l1 - mx), jnp.exp(l2 - mx)
    den = e0 + e1 + e2
    w0, w1, w2 = e0 / den, e1 / den, e2 / den
    for h in range(ATT_HEADS):
        hs = slice(h * HEAD_DIM, (h + 1) * HEAD_DIM)
        b_out = (w0[:, h:h + 1] * o0_ref[:, hs] + w1[:, h:h + 1] * o1_ref[:, hs].astype(F32)
                 + w2[:, h:h + 1] * o2_ref[:, hs].astype(F32))
        merged = ga_ref[:, hs].astype(F32) * a_ref[:, hs] + gb_ref[:, hs].astype(F32) * b_out
        merged_ref[:, hs] = merged.astype(merged_ref.dtype)
    h_new = x_ref[...] + jnp.dot(merged_ref[...], w_ref[...], preferred_element_type=F32)
    h_ref[...] = h_new
    hn_ref[...] = _rms(h_new, g_ref[...]).astype(hn_ref.dtype)


def _merge(a_out, gates, gate_col, outs, lses, x, w_out, norm_g, t, tm):
    row = lambda col: pl.BlockSpec((tm, D_MODEL), lambda i: (i, col))
    lrow = pl.BlockSpec((tm, 128), lambda i: (i, 0))
    p = gates
    return pl.pallas_call(
        _merge_kernel,
        grid=(t // tm,),
        in_specs=[row(0), row(gate_col), row(gate_col + 1), row(0), row(0), row(0), lrow, lrow, lrow,
                  row(0), pl.BlockSpec((D_MODEL, D_MODEL), lambda i: (0, 0)),
                  pl.BlockSpec((1, D_MODEL), lambda i: (0, 0))],
        out_specs=[row(0), row(0)],
        out_shape=[jax.ShapeDtypeStruct((x.shape[0], D_MODEL), F32), jax.ShapeDtypeStruct((t, D_MODEL), BF16)],
        scratch_shapes=[pltpu.VMEM((tm, D_MODEL), BF16)],
        compiler_params=_params("parallel"),
    )(a_out, p, p, outs[0], outs[1], outs[2], lses[0], lses[1], lses[2], x, w_out, norm_g.reshape(1, D_MODEL))


def _peer_score_kernel(hn_ref, wq_ref, keys_ref, s_ref):
    q = jnp.dot(hn_ref[...], wq_ref[...], preferred_element_type=F32).astype(BF16)
    nt = (((1,), (1,)), ((), ()))
    for hc in range(2 * PEER_HEADS):
        cs = slice(hc * PEER_HALF, (hc + 1) * PEER_HALF)
        s_ref[hc] = lax.dot_general(keys_ref[hc % 2], q[:, cs], nt, preferred_element_type=F32)


def _peer_scores(hn, w_q, sub_keys, tm):
    t = hn.shape[0]
    return pl.pallas_call(
        _peer_score_kernel,
        grid=(t // tm,),
        in_specs=[pl.BlockSpec((tm, D_MODEL), lambda i: (i, 0)),
                  pl.BlockSpec((D_MODEL, 2 * PEER_HEADS * PEER_HALF), lambda i: (0, 0)),
                  pl.BlockSpec((2, N_KEYS, PEER_HALF), lambda i: (0, 0, 0))],
        out_specs=pl.BlockSpec((2 * PEER_HEADS, N_KEYS, tm), lambda i: (0, 0, i)),
        out_shape=jax.ShapeDtypeStruct((2 * PEER_HEADS, N_KEYS, t), F32),
        compiler_params=_params("parallel"),
    )(hn, w_q, sub_keys)


def _take_top(arrays, order, count, sentinel):
    arrays = list(arrays)
    vals = [[] for _ in arrays]
    idxs = [[] for _ in arrays]
    for _ in range(count):
        for n, s in enumerate(arrays):
            m = jnp.max(s, axis=0, keepdims=True)
            pos = jnp.min(jnp.where(s == m, order, sentinel), axis=0, keepdims=True)
            vals[n].append(m)
            idxs[n].append(pos)
            arrays[n] = jnp.where(order == pos, NEG_INF, s)
    return vals, idxs


def _sort_network(n):
    pairs, p = [], 1
    while p < n:
        k = p
        while k >= 1:
            for j in range(k % p, n - k, 2 * k):
                for i in range(min(k, n - j - k)):
                    if (i + j) // (2 * p) == (i + j + k) // (2 * p):
                        pairs.append((i + j, i + j + k))
            k //= 2
        p *= 2
    return pairs


def _take_top_sorted(arrays, order, count, sentinel):
    tiles = arrays[0].shape[0] // 8
    assert all(a.shape[0] == 8 * tiles for a in arrays)
    val = [[a[8 * v:8 * v + 8] for v in range(tiles)] for a in arrays]
    idx = [[order[8 * v:8 * v + 8] for v in range(tiles)] for _ in arrays]
    network = [(i, j) for i, j in _sort_network(1 << (tiles - 1).bit_length()) if j < tiles]
    for i, j in network:
        for n in range(len(arrays)):
            vi, vj, ri, rj = val[n][i], val[n][j], idx[n][i], idx[n][j]
            swap = (vj > vi) | ((vj == vi) & (rj < ri))
            val[n][i], val[n][j] = jnp.where(swap, vj, vi), jnp.where(swap, vi, vj)
            idx[n][i], idx[n][j] = jnp.where(swap, rj, ri), jnp.where(swap, ri, rj)
    vals = [[] for _ in arrays]
    idxs = [[] for _ in arrays]
    for t in range(count):
        for n in range(len(arrays)):
            head, rank = val[n][0], idx[n][0]
            m = jnp.max(head, axis=0, keepdims=True)
            pos = jnp.min(jnp.where(head == m, rank, sentinel), axis=0, keepdims=True)
            vals[n].append(m)
            idxs[n].append(pos)
            popped = rank == pos
            for d in range(min(tiles, count - t - 1)):
                if d + 1 < tiles:
                    val[n][d] = jnp.where(popped, val[n][d + 1], val[n][d])
                    idx[n][d] = jnp.where(popped, idx[n][d + 1], idx[n][d])
                else:
                    val[n][d] = jnp.where(popped, NEG_INF, val[n][d])
                    idx[n][d] = jnp.where(popped, sentinel, idx[n][d])
    return vals, idxs


def _stack_rows(rows_list, krow):
    out = jnp.zeros(krow.shape, F32)
    for j, r in enumerate(rows_list):
        out = jnp.where(krow == float(j), r, out)
    return out


def _cand_layout():
    k = PEER_TOPK
    pieces = [("row_a", a, 16 if a == 0 else 8, 0, k // (a + 1)) for a in range(4)]
    pieces += [("col_b", 0, 16, 4, 16), ("col_b", 1, 8, 4, 8), ("col_b", 2, 8, 4, 5)]
    pos = []
    for kind, idx, rows, lo, hi in pieces:
        for r in range(rows):
            a, b = (idx, r) if kind == "row_a" else (r, idx)
            ok = lo <= r < hi and (a + 1) * (b + 1) <= k
            pos.append(a * k + b if ok else k * k)
    assert sorted(p for p in pos if p < k * k) == sorted(
        a * k + b for a in range(k) for b in range(k) if (a + 1) * (b + 1) <= k)
    return pieces, np.asarray(pos, np.float32)


def _route_kernel(s_ref, pos_ref, u_ref, v_ref, e1_ref, e2_ref, gate_ref, ub_ref, vb_ref):
    ub_ref[...] = u_ref[...].astype(ub_ref.dtype)
    vb_ref[...] = v_ref[...].astype(vb_ref.dtype)
    k = PEER_TOPK
    heads = s_ref.shape[0] // 2
    lanes = s_ref.shape[2]
    key_rank = lax.broadcasted_iota(jnp.int32, (N_KEYS, lanes), 0).astype(F32)
    vals, idxs = _take_top_sorted([s_ref[n] for n in range(2 * heads)], key_rank, k, float(N_KEYS))
    krow = lax.broadcasted_iota(jnp.int32, (k, lanes), 0).astype(F32)
    flat = pos_ref[...]
    cands, i1_all, i2_all = [], [], []
    for hd in range(heads):
        v1, v2 = vals[2 * hd], vals[2 * hd + 1]
        v1_all = _stack_rows(v1, krow)
        v2_all = _stack_rows(v2, krow)
        i1_all.append(_stack_rows(idxs[2 * hd], krow))
        i2_all.append(_stack_rows(idxs[2 * hd + 1], krow))
        parts = [v1[idx] + v2_all[:rows] if kind == "row_a" else v1_all[:rows] + v2[idx]
                 for kind, idx, rows, _, _ in _cand_layout()[0]]
        cands.append(jnp.where(flat < float(k * k), jnp.concatenate(parts, axis=0), NEG_INF))
    top_s, pos = _take_top_sorted(cands, flat, k, float(k * k))
    for hd in range(heads):
        e1, e2 = [], []
        for j in range(k):
            a = jnp.floor(pos[hd][j] * (1.0 / k))
            b = pos[hd][j] - a * k
            e1.append(jnp.sum(jnp.where(krow == a, i1_all[hd], 0.0), axis=0, keepdims=True))
            e2.append(jnp.sum(jnp.where(krow == b, i2_all[hd], 0.0), axis=0, keepdims=True))
        ex = jnp.exp(_stack_rows(top_s[hd], krow) - top_s[hd][0])
        rows = slice(hd * k, (hd + 1) * k)
        gate_ref[rows, :] = ex / jnp.sum(ex, axis=0, keepdims=True)
        e1_ref[rows, :] = _stack_rows(e1, krow)
        e2_ref[rows, :] = _stack_rows(e2, krow)


def _route(scores_t, tl, heads_per_step, peer_u, peer_v):
    t = scores_t.shape[2]
    kk = PEER_HEADS * PEER_TOPK
    n_head_steps = PEER_HEADS // heads_per_step
    steps = (t // tl) * n_head_steps
    n_blocks = 1 << (steps.bit_length() - 1)
    tab_rows = N_EXPERTS // n_blocks
    tab = pl.BlockSpec((tab_rows, D_MODEL), lambda i, h: (jnp.minimum(i * n_head_steps + h, n_blocks - 1), 0))
    out = pl.BlockSpec((heads_per_step * PEER_TOPK, tl), lambda i, h: (h, i))
    flat = jnp.asarray(np.repeat(_cand_layout()[1][:, None], tl, axis=1))
    return pl.pallas_call(
        _route_kernel,
        grid=(t // tl, n_head_steps),
        in_specs=[pl.BlockSpec((2 * heads_per_step, N_KEYS, tl), lambda i, h: (h, 0, i)),
                  pl.BlockSpec(flat.shape, lambda i, h: (0, 0)), tab, tab],
        out_specs=[out, out, out, tab, tab],
        out_shape=[jax.ShapeDtypeStruct((kk, t), F32)] * 3 + [jax.ShapeDtypeStruct((N_EXPERTS, D_MODEL), BF16)] * 2,
        compiler_params=_params("arbitrary", "arbitrary"),
    )(scores_t, flat, peer_u, peer_v)


def _expert_weight_kernel(e1_ref, e2_ref, gate_ref, g_ref, e1t_ref, e2t_ref, gt_ref):
    e1t_ref[...] = e1_ref[...].T
    e2t_ref[...] = e2_ref[...].T
    gt_ref[...] = gate_ref[...].T
    kk = e1_ref.shape[0]
    key = lax.broadcasted_iota(jnp.int32, (N_KEYS, kk), 0).astype(F32)
    nt = (((1,), (1,)), ((), ()))

    def body(t, carry):
        r1 = e1t_ref[pl.ds(t, 1), :]
        r2 = e2t_ref[pl.ds(t, 1), :]
        gr = gt_ref[pl.ds(t, 1), :]
        a_t = jnp.where(key == r1, 1.0, 0.0).astype(BF16)
        b_t = jnp.where(key == r2, gr, 0.0).astype(BF16)
        g_ref[t] = lax.dot_general(a_t, b_t, nt, preferred_element_type=F32).astype(g_ref.dtype)
        return carry

    lax.fori_loop(0, g_ref.shape[0], body, 0, unroll=32)


def _expert_weights(e1, e2, gate, tl):
    kk, t = e1.shape
    slot = pl.BlockSpec((kk, tl), lambda i: (0, i))
    return pl.pallas_call(
        _expert_weight_kernel,
        grid=(t // tl,),
        in_specs=[slot, slot, slot],
        out_specs=pl.BlockSpec((tl, N_KEYS, N_KEYS), lambda i: (i, 0, 0)),
        out_shape=jax.ShapeDtypeStruct((t, N_KEYS, N_KEYS), F32),
        scratch_shapes=[pltpu.VMEM((tl, kk), F32)] * 3,
        compiler_params=_params("parallel"),
    )(e1, e2, gate)


def _peer_kernel(hn_ref, u_ref, v_ref, g_ref, y_ref):
    e = pl.program_id(1)
    nt = (((1,), (1,)), ((), ()))
    hk = lax.dot_general(hn_ref[...], u_ref[...], nt, preferred_element_type=F32)
    g = jnp.swapaxes(g_ref[...], 0, 1)
    act = [(jax.nn.gelu(hk[:, a * N_KEYS:(a + 1) * N_KEYS]) * g[a]).astype(BF16) for a in range(g.shape[0])]
    out = jnp.dot(jnp.concatenate(act, axis=-1), v_ref[...], preferred_element_type=F32)

    @pl.when(e == 0)
    def _():
        y_ref[...] = out

    @pl.when(e > 0)
    def _():
        y_ref[...] += out


def _peer(hn, u, v, g, t, rows_out, tm, te):
    row = pl.BlockSpec((tm, D_MODEL), lambda i, e: (i, 0), pipeline_mode=pl.Buffered(1))
    tab = pl.BlockSpec((te, D_MODEL), lambda i, e: (e, 0))
    return pl.pallas_call(
        _peer_kernel,
        grid=(t // tm, N_EXPERTS // te),
        in_specs=[row, tab, tab, pl.BlockSpec((tm, te // N_KEYS, N_KEYS), lambda i, e: (i, e, 0))],
        out_specs=row,
        out_shape=jax.ShapeDtypeStruct((rows_out, D_MODEL), F32),
        compiler_params=_params("parallel", "arbitrary"),
    )(hn, u, v, g)


def _final_norm_prompt_kernel(h_ref, f_ref, g_ref, y_ref):
    y = [_rms(h_ref[rr % 4, rr // 4] + f_ref[rr % 4, rr // 4], g_ref[...]) for rr in range(8)]
    y_ref[...] = jnp.swapaxes(jnp.stack(y, axis=0), 0, 1)


def _final_norm_prompt(h, f, g, batch, seq):
    rows = 64
    h5 = h.reshape(batch + 1, 4, 4, SLAB_ROWS, D_MODEL)
    f5 = f.reshape(batch + 1, 4, 4, SLAB_ROWS, D_MODEL)
    blk = pl.BlockSpec((None, 4, 2, rows, D_MODEL), lambda b, hh, j: (b, 0, hh, j, 0))
    y = pl.pallas_call(
        _final_norm_prompt_kernel,
        grid=(batch, 2, SLAB_ROWS // rows),
        in_specs=[blk, blk, pl.BlockSpec((1, D_MODEL), lambda b, hh, j: (0, 0))],
        out_specs=pl.BlockSpec((None, rows, 8, D_MODEL), lambda b, hh, j: (b, j, hh, 0)),
        out_shape=jax.ShapeDtypeStruct((batch, SLAB_ROWS, SLABS, D_MODEL), F32),
        compiler_params=_params("parallel", "parallel", "parallel"),
    )(h5, f5, g.reshape(1, D_MODEL))
    return y.reshape(batch, seq, D_MODEL)


def _final_norm_rows_kernel(h_ref, f_ref, g_ref, y_ref):
    y_ref[...] = _rms(h_ref[...] + f_ref[...], g_ref[...])


def _final_norm_rows(h, f, g, row_block, rows):
    blk = pl.BlockSpec((rows, D_MODEL), lambda i: (row_block, 0))
    return pl.pallas_call(
        _final_norm_rows_kernel,
        grid=(1,),
        in_specs=[blk, blk, pl.BlockSpec((1, D_MODEL), lambda i: (0, 0))],
        out_specs=pl.BlockSpec((rows, D_MODEL), lambda i: (0, 0)),
        out_shape=jax.ShapeDtypeStruct((rows, D_MODEL), F32),
        compiler_params=_params("arbitrary"),
    )(h, f, g.reshape(1, D_MODEL))


def _kv_prompt_kernel(*refs):
    ins, outs = refs[:2 * N_DIL], refs[2 * N_DIL:]
    for g in range(N_DIL):
        for kv in range(2):
            src = ins[2 * g + kv]
            heads = [src[0, :, h * HEAD_DIM:(h + 1) * HEAD_DIM] for h in range(ATT_HEADS)]
            outs[g][:, kv, :, :] = jnp.swapaxes(jnp.stack(heads, axis=0), 0, 1)


def _kv_prompt(p3, batch, seq):
    in_specs, out_specs, out_shape = [], [], []
    residue = lambda s: (s % 4) * 4 + s // 4
    for g in range(N_DIL):
        steps = min(DIL_WINDOWS[g], seq) // SLABS
        last = SLAB_ROWS // steps - 1
        for kv in range(2):
            col = COL_ATT + 3 * g + 1 + kv
            in_specs.append(pl.BlockSpec((1, steps, D_MODEL),
                                         lambda b, s, last=last, col=col: (b * SLABS + s, last, col)))
        out_specs.append(pl.BlockSpec((None, steps, None, 2, ATT_HEADS, HEAD_DIM),
                                      lambda b, s: (b, 0, residue(s), 0, 0, 0)))
        out_shape.append(jax.ShapeDtypeStruct((batch, steps, SLABS, 2, ATT_HEADS, HEAD_DIM), F32))
    outs = pl.pallas_call(
        _kv_prompt_kernel,
        grid=(batch, SLABS),
        in_specs=in_specs,
        out_specs=out_specs,
        out_shape=out_shape,
        compiler_params=_params("parallel", "parallel"),
    )(*([p3] * len(in_specs)))
    return [o.reshape(1, batch, -1, 2, ATT_HEADS, HEAD_DIM) for o in outs]


def _row_tile(t, candidates):
    for c in candidates:
        if t % c == 0:
            return c
    raise ValueError(f"no row tile for {t} tokens")


def kernel(x_prompt, x_sample, cache_kv_w128, cache_kv_w512, cache_kv_w2048, norm_mix_g, w_in, sgu_norm_g, sgu_w, sgu_b, w_out, norm_ffn_g, peer_w_q, peer_sub_keys, peer_u, peer_v, norm_final_g):
    batch, seq, _ = x_prompt.shape
    db, ds, _ = x_sample.shape
    assert w_in.shape[0] == 1 and db * ds == CHUNK and seq == SLABS * SLAB_ROWS
    caches = (cache_kv_w128, cache_kv_w512, cache_kv_w2048)
    tp, ts = batch * seq, db * ds
    t = tp + ts
    n_slabs = t // SLAB_ROWS
    tm_big = _row_tile(t, (1040, 640, 128))
    tm_lane = _row_tile(t, (640, 128))

    xs = x_sample.reshape(ts, D_MODEL)
    xl, xn = _permute_norm(x_prompt, norm_mix_g[0])
    xl = lax.dynamic_update_slice(xl, xs, (tp, 0))
    xn = lax.dynamic_update_slice(xn, _rmsnorm_bf16(xs, norm_mix_g[0]), (tp, 0))

    p, att16 = _inproj(xn, w_in[0], t, tm_big, 1024)
    p3 = p.reshape(n_slabs, SLAB_ROWS, p.shape[1])
    att16_3 = att16.reshape(n_slabs, SLAB_ROWS, att16.shape[1])

    w_tril = sgu_w[0] * jnp.tril(jnp.ones((CHUNK, CHUNK), F32))
    tau = _block_steps(0)
    w_p = w_tril[:, tau][:, :, tau].astype(BF16)
    b_p = sgu_b[0][:, tau][..., None]
    w_s = jnp.einsum("bc,gis->gbics", jnp.eye(db, dtype=F32), w_tril[:, :ds, :ds]).reshape(SGU_GROUPS, ts, ts)
    b_s = jnp.tile(sgu_b[0][:, :ds], (1, db))[..., None]
    a_out = _sgu_prompt(p3, sgu_norm_g[0], w_p, b_p, batch).reshape(t, D_MODEL)
    a_s, vn_s = _sgu_sample(p3, sgu_norm_g[0], w_s.astype(BF16), b_s, tp // SLAB_ROWS)
    a_out = lax.dynamic_update_slice(a_out, a_s, (tp, 0))

    att_s = p[tp:, COL_ATT * D_MODEL:COL_GATE * D_MODEL].reshape(db, ds, N_DIL, 3, ATT_HEADS, HEAD_DIM)
    outs, lses = [], []
    o_s, lse_s = _cached_attention(att_s, [c[0] for c in caches])
    band = _band_attention([p3, att16_3, att16_3], [COL_ATT, 0, 3], [False, True, True], batch)
    for g in range(N_DIL):
        o_p, lse_p = band[g]
        lse_g = jnp.pad(lse_s[g, ..., 0].reshape(ts, ATT_HEADS), ((0, 0), (0, 128 - ATT_HEADS)))
        o_g = o_s[g].reshape(ts, D_MODEL).astype(o_p.dtype)
        outs.append(lax.dynamic_update_slice(o_p.reshape(t, D_MODEL), o_g, (tp, 0)))
        lses.append(lax.dynamic_update_slice(lse_p.reshape(t, 128), lse_g, (tp, 0)))

    h, hn = _merge(a_out, att16, COL_GATE - COL_ATT16, outs, lses, xl, _to_bf16(w_out[0], 1024, D_MODEL),
                   norm_ffn_g[0], t, _row_tile(t, (320, 128)))

    scores_t = _peer_scores(hn, _to_bf16(peer_w_q[0], 1024, D_MODEL), peer_sub_keys[0].astype(BF16), tm_lane)
    e1, e2, gate, u_b16, v_b16 = _route(scores_t, 128, 8, peer_u[0], peer_v[0])
    g_dense = _expert_weights(e1, e2, gate, 128)
    f = _peer(hn, u_b16, v_b16, g_dense, t, h.shape[0], tm_big, 1024)

    y_prompt = _final_norm_prompt(h, f, norm_final_g, batch, seq)
    y_sample = _final_norm_rows(h, f, norm_final_g, tp // ts, ts).reshape(db, ds, D_MODEL)
    kv_prompt = _kv_prompt(p3, batch, seq)
    kv_sample = [att_s[:, :, g, 1:3][None] for g in range(N_DIL)]
    sgu_v_sample = vn_s.reshape(1, db, ds, D_MODEL)
    return (y_prompt, y_sample, kv_prompt[0], kv_prompt[1], kv_prompt[2],
            kv_sample[0], kv_sample[1], kv_sample[2], sgu_v_sample)
```
